```python
import math
import jax, jax.numpy as jnp
from jax import lax
import numpy as np

D_MODEL = 1024
BATCH = 2
SEQ = 8192
DEPTH = 2

HEAD_DIM = 64
ROPE_DIM = HEAD_DIM // 4
ROPE_THETA = 500000.0
Q_BLOCK = 128
NSA_HEADS = (D_MODEL // 2) // HEAD_DIM
NSA_KV_GROUPS = 2
CMP_BLOCK = 32
CMP_STRIDE = 16
CMP_HIDDEN = 4 * HEAD_DIM
SLC_BLOCK = 64
N_SELECT = 16
WINDOW = 512
SB_HEADS = (D_MODEL // 2) // HEAD_DIM
DIFF_HEADS = D_MODEL // (2 * HEAD_DIM)
D_FF = ((8 * D_MODEL // 3 + 255) // 256) * 256
CONV_WIDTH = 3
N_EVEN = (DEPTH + 1) // 2
N_ODD = DEPTH // 2
NSA_Q_W = NSA_HEADS * HEAD_DIM
NSA_KV_W = NSA_KV_GROUPS * HEAD_DIM
SB_W = SB_HEADS * HEAD_DIM
HYB_IN_SIZES = (NSA_Q_W,) + (NSA_KV_W,) * 6 + (3 * NSA_HEADS,) + (SB_W,) * 3
HYB_IN_W = sum(HYB_IN_SIZES)
HYB_OUT_W = NSA_Q_W + SB_W
DIFF_W = 2 * DIFF_HEADS * HEAD_DIM
RMS_EPS = 1e-6
NEG_INF = -1e30
FORCE = 1e9

kernel_name = 'hybrid_nsa_stickbreak_diffattn_convffn'


def rms_norm(x, g):
    xf = x.astype(jnp.float32)
    y = xf * lax.rsqrt(jnp.mean(xf * xf, axis=-1, keepdims=True) + RMS_EPS) * g.astype(jnp.float32)
    return y.astype(x.dtype)


def modulate(h, shift, scale):
    return h * (1.0 + scale[:, None, :]) + shift[:, None, :]


def rope_partial(x, pos):
    inv = ROPE_THETA ** (-jnp.arange(0, ROPE_DIM, 2, dtype=jnp.float32) / ROPE_DIM)
    ang = pos.astype(jnp.float32)[..., None] * inv
    cos = jnp.cos(ang)[:, :, None, :]
    sin = jnp.sin(ang)[:, :, None, :]
    half = ROPE_DIM // 2
    xf = x.astype(jnp.float32)
    x1, x2 = xf[..., :half], xf[..., half:ROPE_DIM]
    out = jnp.concatenate([x1 * cos - x2 * sin, x2 * cos + x1 * sin, xf[..., ROPE_DIM:]], axis=-1)
    return out.astype(x.dtype)


def masked_softmax(s, mask):
    p = jax.nn.softmax(jnp.where(mask, s, NEG_INF), axis=-1)
    return jnp.where(mask, p, 0.0)


def split_cols(a, sizes):
    out, start = [], 0
    for s in sizes:
        out.append(a[..., start:start + s])
        start += s
    return out


def compress(kv, pos_emb, w1, w2):
    B, T, G, dk = kv.shape
    nc = (T - CMP_BLOCK) // CMP_STRIDE + 1
    idx = (jnp.arange(nc) * CMP_STRIDE)[:, None] + jnp.arange(CMP_BLOCK)[None, :]
    blk = kv[:, idx] + pos_emb[None, None, :, None, :]
    blk = jnp.moveaxis(blk, 3, 2).reshape(B, nc, G, CMP_BLOCK * dk)
    return jax.nn.silu(blk @ w1) @ w2


def nsa_attention(q, k_c, v_c, k_s, v_s, k_w, v_w, gates):
    B, T, H, dk = q.shape
    G = NSA_KV_GROUPS
    R = H // G
    nc = k_c.shape[1]
    ns = T // SLC_BLOCK
    n_sel = min(N_SELECT, ns)
    scale = dk ** -0.5
    qg = q.reshape(B, T, G, R, dk)
    gg = gates.reshape(B, T, G, R, 3)
    cmp_start = jnp.arange(nc) * CMP_STRIDE
    cmp_end = cmp_start + CMP_BLOCK - 1
    slc_start = jnp.arange(ns) * SLC_BLOCK
    overlap = ((cmp_start[:, None] < slc_start[None, :] + SLC_BLOCK)
               & (cmp_start[:, None] + CMP_BLOCK > slc_start[None, :])).astype(jnp.float32)
    ks_blocks = k_s.reshape(B, ns, SLC_BLOCK, G, dk).transpose(0, 3, 1, 2, 4)
    vs_blocks = v_s.reshape(B, ns, SLC_BLOCK, G, dk).transpose(0, 3, 1, 2, 4)
    kw_pad = jnp.pad(k_w, ((0, 0), (WINDOW, 0), (0, 0), (0, 0)))
    vw_pad = jnp.pad(v_w, ((0, 0), (WINDOW, 0), (0, 0), (0, 0)))
    b_ix = jnp.arange(B)[:, None, None, None]
    g_ix = jnp.arange(G)[None, :, None, None]
    blk_ids = jnp.arange(ns)

    def block(qb):
        q0 = qb * Q_BLOCK
        t = q0 + jnp.arange(Q_BLOCK)
        qc = lax.dynamic_slice_in_dim(qg, q0, Q_BLOCK, axis=1)
        s_c = jnp.einsum('bqgrd,bcgd->bgrqc', qc, k_c).astype(jnp.float32) * scale
        p_c = masked_softmax(s_c, cmp_end[None, :] <= t[:, None])
        o_c = jnp.einsum('bgrqc,bcgd->bqgrd', p_c.astype(v_c.dtype), v_c)
        imp = jnp.einsum('bgrqc,cn->bgqn', p_c, overlap)
        cur = t // SLC_BLOCK
        forced = (blk_ids[None, :] == 0) | (blk_ids[None, :] == cur[:, None]) | (blk_ids[None, :] == cur[:, None] - 1)
        valid = blk_ids[None, :] <= cur[:, None]
        score = jnp.where(forced, FORCE, jnp.where(valid, imp, -FORCE))
        _, sel = lax.top_k(score, n_sel)
        k_sel = ks_blocks[b_ix, g_ix, sel]
        v_sel = vs_blocks[b_ix, g_ix, sel]
        tok = sel[..., None] * SLC_BLOCK + jnp.arange(SLC_BLOCK)
        m_s = (tok <= t[None, None, :, None, None]).reshape(B, G, 1, Q_BLOCK, n_sel * SLC_BLOCK)
        s_s = jnp.einsum('bqgrd,bgqnkd->bgrqnk', qc, k_sel).astype(jnp.float32) * scale
        p_s = masked_softmax(s_s.reshape(B, G, R, Q_BLOCK, n_sel * SLC_BLOCK), m_s)
        p_s = p_s.reshape(B, G, R, Q_BLOCK, n_sel, SLC_BLOCK)
        o_s = jnp.einsum('bgrqnk,bgqnkd->bqgrd', p_s.astype(v_sel.dtype), v_sel)
        kw = lax.dynamic_slice_in_dim(kw_pad, q0, Q_BLOCK + WINDOW, axis=1)
        vw = lax.dynamic_slice_in_dim(vw_pad, q0, Q_BLOCK + WINDOW, axis=1)
        kp = q0 - WINDOW + jnp.arange(Q_BLOCK + WINDOW)
        m_w = (kp[None, :] <= t[:, None]) & (kp[None, :] > t[:, None] - WINDOW) & (kp[None, :] >= 0)
        s_w = jnp.einsum('bqgrd,bkgd->bgrqk', qc, kw).astype(jnp.float32) * scale
        p_w = masked_softmax(s_w, m_w)
        o_w = jnp.einsum('bgrqk,bkgd->bqgrd', p_w.astype(vw.dtype), vw)
        g = jax.nn.sigmoid(lax.dynamic_slice_in_dim(gg, q0, Q_BLOCK, axis=1).astype(jnp.float32))
        o = g[..., 0:1] * o_c + g[..., 1:2] * o_s + g[..., 2:3] * o_w
        return o.astype(q.dtype).reshape(B, Q_BLOCK, H * dk)

    out = lax.map(block, jnp.arange(T // Q_BLOCK))
    return jnp.moveaxis(out, 0, 1).reshape(B, T, H * dk)


def stick_breaking_attention(q, k, v):
    B, T, H, dk = q.shape
    scale = dk ** -0.5
    key_pos = jnp.arange(T)

    def block(qb):
        q0 = qb * Q_BLOCK
        t = q0 + jnp.arange(Q_BLOCK)
        qc = lax.dynamic_slice_in_dim(q, q0, Q_BLOCK, axis=1)
        z = jnp.einsum('bqhd,bkhd->bhqk', qc, k).astype(jnp.float32) * scale
        m = key_pos[None, :] < t[:, None]
        log_keep = jnp.where(m, jax.nn.log_sigmoid(-z), 0.0)
        log_later = lax.cumsum(log_keep, axis=3, reverse=True) - log_keep
        a = jnp.where(m, jnp.exp(jax.nn.log_sigmoid(z) + log_later), 0.0)
        return jnp.einsum('bhqk,bkhd->bqhd', a.astype(v.dtype), v)

    out = lax.map(block, jnp.arange(T // Q_BLOCK))
    return jnp.moveaxis(out, 0, 1).reshape(B, T, H * dk)


def diff_attention(q1, q2, k1, k2, v, lam):
    B, T, H, dk = q1.shape
    scale = dk ** -0.5
    key_pos = jnp.arange(T)

    def block(qb):
        q0 = qb * Q_BLOCK
        t = q0 + jnp.arange(Q_BLOCK)
        m = key_pos[None, :] <= t[:, None]
        q1c = lax.dynamic_slice_in_dim(q1, q0, Q_BLOCK, axis=1)
        q2c = lax.dynamic_slice_in_dim(q2, q0, Q_BLOCK, axis=1)
        p1 = masked_softmax(jnp.einsum('bqhd,bkhd->bhqk', q1c, k1).astype(jnp.float32) * scale, m)
        p2 = masked_softmax(jnp.einsum('bqhd,bkhd->bhqk', q2c, k2).astype(jnp.float32) * scale, m)
        w = p1 - lam * p2
        return jnp.einsum('bhqk,bkhe->bqhe', w.astype(v.dtype), v)

    out = lax.map(block, jnp.arange(T // Q_BLOCK))
    return jnp.moveaxis(out, 0, 1).reshape(B, T, H, 2 * dk)


def hybrid_nsa_sb_mixer(h, positions, w_in, pos_k, pos_v, ck_w1, ck_w2, cv_w1, cv_w2, w_out):
    B, T, _ = h.shape
    G = NSA_KV_GROUPS
    (q_n, kc, vc, ks, vs, kw, vw, gl, q_s, k_s, v_s) = split_cols(h @ w_in, HYB_IN_SIZES)
    hd = lambda a, n: a.reshape(B, T, n, HEAD_DIM)
    q_n = rope_partial(hd(q_n, NSA_HEADS), positions)
    ks = rope_partial(hd(ks, G), positions)
    kw = rope_partial(hd(kw, G), positions)
    k_cmp = compress(hd(kc, G), pos_k, ck_w1, ck_w2)
    v_cmp = compress(hd(vc, G), pos_v, cv_w1, cv_w2)
    nc = k_cmp.shape[1]
    cmp_end = jnp.arange(nc) * CMP_STRIDE + CMP_BLOCK - 1
    k_cmp = rope_partial(k_cmp, positions[:, cmp_end])
    o_nsa = nsa_attention(q_n, k_cmp, v_cmp, ks, hd(vs, G), kw, hd(vw, G),
                          gl.reshape(B, T, NSA_HEADS, 3))
    o_sb = stick_breaking_attention(hd(q_s, SB_HEADS), hd(k_s, SB_HEADS), hd(v_s, SB_HEADS))
    return jnp.concatenate([o_nsa, o_sb], axis=-1) @ w_out


def diff_mixer(h, positions, w_qkv, lq1, lk1, lq2, lk2, subln, w_out, layer_idx):
    B, T, _ = h.shape
    H = DIFF_HEADS
    q, k, v = jnp.split(h @ w_qkv, 3, axis=-1)
    q = rope_partial(q.reshape(B, T, 2 * H, HEAD_DIM), positions).reshape(B, T, H, 2, HEAD_DIM)
    k = rope_partial(k.reshape(B, T, 2 * H, HEAD_DIM), positions).reshape(B, T, H, 2, HEAD_DIM)
    v = v.reshape(B, T, H, 2 * HEAD_DIM)
    lambda_init = 0.8 - 0.6 * math.exp(-0.3 * layer_idx)
    lam = (jnp.exp(jnp.sum(lq1.astype(jnp.float32) * lk1.astype(jnp.float32)))
           - jnp.exp(jnp.sum(lq2.astype(jnp.float32) * lk2.astype(jnp.float32))) + lambda_init)
    o = diff_attention(q[..., 0, :], q[..., 1, :], k[..., 0, :], k[..., 1, :], v, lam)
    o = rms_norm(o, subln) * (1.0 - lambda_init)
    return o.reshape(B, T, DIFF_W) @ w_out


def conv_ffn(h, w_gate, w_up, conv_w, conv_b, w_down):
    g = h @ w_gate
    g = lax.conv_general_dilated(g, conv_w[:, None, :].astype(g.dtype), window_strides=(1,),
                                 padding=[(CONV_WIDTH - 1, 0)],
                                 dimension_numbers=('NWC', 'WIO', 'NWC'),
                                 feature_group_count=g.shape[-1]) + conv_b
    return (jax.nn.silu(g) * (h @ w_up)) @ w_down


def setup_inputs(seed: int = 0) -> dict:
    key = jax.random.key(seed)
    ks = iter(jax.random.split(key, 32))
    D = D_MODEL

    def nrm(shape, scale):
        return scale * jax.random.normal(next(ks), shape, jnp.float32)

    def gain(shape):
        return 1.0 + 0.05 * jax.random.normal(next(ks), shape, jnp.float32)

    x = nrm((BATCH, SEQ, D), 1.0)
    c = nrm((BATCH, D), 1.0)
    positions = (jax.random.randint(next(ks), (BATCH, 1), 0, 1024, jnp.int32)
                 + jnp.arange(SEQ, dtype=jnp.int32)[None, :])
    return {
        'x': x, 'c': c, 'positions': positions,
        'mod_w': nrm((DEPTH, D, 6 * D), D ** -0.5),
        'mod_b': nrm((DEPTH, 6 * D), 0.02),
        'norm_mix': gain((DEPTH, D)),
        'norm_ffn': gain((DEPTH, D)),
        'ffn_w_gate': nrm((DEPTH, D, D_FF), D ** -0.5),
        'ffn_w_up': nrm((DEPTH, D, D_FF), D ** -0.5),
        'ffn_conv_w': nrm((DEPTH, CONV_WIDTH, D_FF), CONV_WIDTH ** -0.5),
        'ffn_conv_b': nrm((DEPTH, D_FF), 0.02),
        'ffn_w_down': nrm((DEPTH, D_FF, D), D_FF ** -0.5),
        'hyb_w_in': nrm((N_EVEN, D, HYB_IN_W), D ** -0.5),
        'nsa_pos_k': nrm((N_EVEN, CMP_BLOCK, HEAD_DIM), 0.1),
        'nsa_pos_v': nrm((N_EVEN, CMP_BLOCK, HEAD_DIM), 0.1),
        'nsa_ck_w1': nrm((N_EVEN, CMP_BLOCK * HEAD_DIM, CMP_HIDDEN), (CMP_BLOCK * HEAD_DIM) ** -0.5),
        'nsa_ck_w2': nrm((N_EVEN, CMP_HIDDEN, HEAD_DIM), CMP_HIDDEN ** -0.5),
        'nsa_cv_w1': nrm((N_EVEN, CMP_BLOCK * HEAD_DIM, CMP_HIDDEN), (CMP_BLOCK * HEAD_DIM) ** -0.5),
        'nsa_cv_w2': nrm((N_EVEN, CMP_HIDDEN, HEAD_DIM), CMP_HIDDEN ** -0.5),
        'hyb_w_out': nrm((N_EVEN, HYB_OUT_W, D), HYB_OUT_W ** -0.5),
        'diff_w_qkv': nrm((N_ODD, D, 3 * DIFF_W), D ** -0.5),
        'diff_lq1': nrm((N_ODD, HEAD_DIM), 0.1),
        'diff_lk1': nrm((N_ODD, HEAD_DIM), 0.1),
        'diff_lq2': nrm((N_ODD, HEAD_DIM), 0.1),
        'diff_lk2': nrm((N_ODD, HEAD_DIM), 0.1),
        'diff_subln': gain((N_ODD, 2 * HEAD_DIM)),
        'diff_w_out': nrm((N_ODD, DIFF_W, D), DIFF_W ** -0.5),
        'norm_f': gain((D,)),
    }


def reference(x, c, positions, mod_w, mod_b, norm_mix, norm_ffn, ffn_w_gate, ffn_w_up,
              ffn_conv_w, ffn_conv_b, ffn_w_down, hyb_w_in, nsa_pos_k, nsa_pos_v,
              nsa_ck_w1, nsa_ck_w2, nsa_cv_w1, nsa_cv_w2, hyb_w_out, diff_w_qkv,
              diff_lq1, diff_lk1, diff_lq2, diff_lk2, diff_subln, diff_w_out, norm_f):
    cond = jax.nn.silu(c)
    for i in range(DEPTH):
        mod = cond @ mod_w[i] + mod_b[i]
        sh_m, sc_m, g_m, sh_f, sc_f, g_f = jnp.split(mod, 6, axis=-1)
        h = modulate(rms_norm(x, norm_mix[i]), sh_m, sc_m)
        if i % 2 == 0:
            j = i // 2
            y = hybrid_nsa_sb_mixer(h, positions, hyb_w_in[j], nsa_pos_k[j], nsa_pos_v[j],
                                    nsa_ck_w1[j], nsa_ck_w2[j], nsa_cv_w1[j], nsa_cv_w2[j],
                                    hyb_w_out[j])
        else:
            j = i // 2
            y = diff_mixer(h, positions, diff_w_qkv[j], diff_lq1[j], diff_lk1[j], diff_lq2[j],
                           diff_lk2[j], diff_subln[j], diff_w_out[j], i)
        x = x + g_m[:, None, :] * y
        h = modulate(rms_norm(x, norm_ffn[i]), sh_f, sc_f)
        x = x + g_f[:, None, :] * conv_ffn(h, ffn_w_gate[i], ffn_w_up[i], ffn_conv_w[i],
                                           ffn_conv_b[i], ffn_w_down[i])
    return rms_norm(x, norm_f)
```

```python
import functools
import math
from typing import NamedTuple

import jax
import jax.numpy as jnp
from jax import lax
from jax.experimental import pallas as pl
from jax.experimental.pallas import tpu as pltpu

F32 = jnp.float32
BF16 = jnp.bfloat16
I32 = jnp.int32

HEAD_DIM = 64
ROPE_DIM = HEAD_DIM // 4
ROPE_THETA = 500000.0
NSA_HEADS = 8
NSA_KV_GROUPS = 2
NSA_REP = NSA_HEADS // NSA_KV_GROUPS
CMP_BLOCK = 32
CMP_STRIDE = 16
CMP_HIDDEN = 4 * HEAD_DIM
SLC_BLOCK = 64
N_SELECT = 16
WINDOW = 512
SB_HEADS = 8
DIFF_HEADS = 8
CONV_WIDTH = 3
RMS_EPS = 1e-6
NEG_INF = -1e30
FORCE = 1e9

LANES = 128
Q_TILE = 128
KEY_TILE = 256
PROJ_ROWS = 512
FFN_ROWS = 1024
FFN_COLS = 256
HALO_ROWS = 16
VMEM_LIMIT = 56 * 1024 * 1024

_NT = (((1,), (1,)), ((), ()))


def _params(*sem):
    return pltpu.CompilerParams(dimension_semantics=sem, vmem_limit_bytes=VMEM_LIMIT)


def _sigmoid(v):
    return 1.0 / (1.0 + jnp.exp(-v))


def _iota(shape, axis):
    return lax.broadcasted_iota(I32, shape, axis)


def _mod_kernel(c_ref, w_ref, b_ref, o_ref):
    c = c_ref[...]
    cond = c * _sigmoid(c)
    o_ref[0] = jnp.dot(cond, w_ref[0], preferred_element_type=F32,
                       precision=lax.Precision.HIGHEST) + b_ref[0]


def _adaln_mod(c, mod_w, mod_b):
    depth, d, n = mod_w.shape
    bsz = c.shape[0]
    tn = n // 4
    return pl.pallas_call(
        _mod_kernel,
        grid=(depth, n // tn),
        in_specs=[pl.BlockSpec((bsz, d), lambda i, j: (0, 0)),
                  pl.BlockSpec((1, d, tn), lambda i, j: (i, 0, j)),
                  pl.BlockSpec((1, 1, tn), lambda i, j: (i, 0, j))],
        out_specs=pl.BlockSpec((1, bsz, tn), lambda i, j: (i, 0, j)),
        out_shape=jax.ShapeDtypeStruct((depth, bsz, n), F32),
        compiler_params=_params("arbitrary", "arbitrary"),
        name="adaln_mod",
    )(c, mod_w, mod_b.reshape(depth, 1, n))


def _rope_kernel(pos_ref, inv_ref, sgn_ref, cos_ref, sin_ref):
    ang = pos_ref[0].astype(F32) * inv_ref[...]
    cos_ref[0] = jnp.cos(ang)
    sin_ref[0] = jnp.sin(ang) * sgn_ref[...]


def _rope_tables(positions):
    bsz, t = positions.shape
    inv = ROPE_THETA ** (-jnp.arange(0, ROPE_DIM, 2, dtype=F32) / ROPE_DIM)
    half = ROPE_DIM // 2
    per_head_inv = jnp.concatenate([inv, inv, jnp.zeros((HEAD_DIM - ROPE_DIM,), F32)])
    per_head_sgn = jnp.concatenate([-jnp.ones((half,), F32), jnp.ones((half,), F32),
                                    jnp.zeros((HEAD_DIM - ROPE_DIM,), F32)])
    inv_row = jnp.tile(per_head_inv, LANES // HEAD_DIM)[None, :]
    sgn_row = jnp.tile(per_head_sgn, LANES // HEAD_DIM)[None, :]
    rows = min(t, 1024)
    tab = jax.ShapeDtypeStruct((bsz, t, LANES), F32)
    row_spec = pl.BlockSpec((1, LANES), lambda b, i: (0, 0))
    return pl.pallas_call(
        _rope_kernel,
        grid=(bsz, t // rows),
        in_specs=[pl.BlockSpec((1, rows, 1), lambda b, i: (b, i, 0)), row_spec, row_spec],
        out_specs=[pl.BlockSpec((1, rows, LANES), lambda b, i: (b, i, 0))] * 2,
        out_shape=[tab, tab],
        compiler_params=_params("arbitrary", "arbitrary"),
        name="rope_tables",
    )(positions.reshape(bsz, t, 1), inv_row, sgn_row)


def _rope_chunk(y, cosv, sinv, first_half):
    ahead = pltpu.roll(y, LANES - ROPE_DIM // 2, 1)
    behind = pltpu.roll(y, ROPE_DIM // 2, 1)
    return y * cosv + jnp.where(first_half, ahead, behind) * sinv


class Seg(NamedTuple):
    start: int
    width: int
    rope: bool
    scale: float
    sigmoid: bool
    head_width: int
    dtype: object


def _norm_modulate(x, g, shift, scale):
    ms = jnp.mean(x * x, axis=-1, keepdims=True)
    y = x * lax.rsqrt(ms + RMS_EPS) * g
    return y * (1.0 + scale) + shift


def _proj_kernel(x_ref, g_ref, sh_ref, sc_ref, w_ref, cos_ref, sin_ref, *out_refs, segs):
    hb = _norm_modulate(x_ref[0], g_ref[...], sh_ref[0], sc_ref[0]).astype(BF16)
    cosv = cos_ref[0]
    sinv = sin_ref[0]
    first_half = (_iota(cosv.shape, 1) & (HEAD_DIM - 1)) < (ROPE_DIM // 2)
    for seg, o_ref in zip(segs, out_refs):
        y = jnp.dot(hb, w_ref[:, seg.start:seg.start + seg.width], preferred_element_type=F32)
        for ch in range(seg.width // LANES):
            yc = y[:, ch * LANES:(ch + 1) * LANES]
            if seg.rope:
                yc = _rope_chunk(yc, cosv, sinv, first_half)
            if seg.scale != 1.0:
                yc = yc * seg.scale
            if seg.sigmoid:
                yc = _sigmoid(yc)
            yc = yc.astype(seg.dtype)
            if seg.head_width == 0:
                o_ref[0, :, ch * LANES:(ch + 1) * LANES] = yc
            elif seg.head_width == LANES:
                o_ref[0, ch] = yc
            else:
                o_ref[0, 2 * ch] = yc[:, :HEAD_DIM]
                o_ref[0, 2 * ch + 1] = yc[:, HEAD_DIM:]


def _project(x, g, shift, scale, w, cos_t, sin_t, segs, name):
    bsz, t, d = x.shape
    rows = min(t, PROJ_ROWS)
    n = w.shape[1]
    out_shapes, out_specs = [], []
    for seg in segs:
        if seg.head_width == 0:
            out_shapes.append(jax.ShapeDtypeStruct((bsz, t, seg.width), seg.dtype))
            out_specs.append(pl.BlockSpec((1, rows, seg.width), lambda b, i: (b, i, 0)))
        else:
            nh = seg.width // seg.head_width
            out_shapes.append(jax.ShapeDtypeStruct((bsz, nh, t, seg.head_width), seg.dtype))
            out_specs.append(pl.BlockSpec((1, nh, rows, seg.head_width), lambda b, i: (b, 0, i, 0)))
    vec = pl.BlockSpec((1, 1, d), lambda b, i: (b, 0, 0))
    tab = pl.BlockSpec((1, rows, LANES), lambda b, i: (b, i, 0))
    return pl.pallas_call(
        functools.partial(_proj_kernel, segs=tuple(segs)),
        grid=(bsz, t // rows),
        in_specs=[pl.BlockSpec((1, rows, d), lambda b, i: (b, i, 0)),
                  pl.BlockSpec((1, d), lambda b, i: (0, 0)),
                  vec, vec,
                  pl.BlockSpec((d, n), lambda b, i: (0, 0)),
                  tab, tab],
        out_specs=out_specs,
        out_shape=out_shapes,
        compiler_params=_params("arbitrary", "arbitrary"),
        name=name,
    )(x, g.reshape(1, d), shift, scale, w, cos_t, sin_t)


def _compress_kernel(r_ref, pa_ref, pb_ref, wa_ref, wb_ref, w2_ref, cos_ref, sin_ref, o_ref, *, rope):
    r = r_ref[0]
    ncp = r.shape[0]
    a = jnp.dot((r + pa_ref[...]).astype(BF16), wa_ref[...], preferred_element_type=F32)
    b = jnp.dot((r + pb_ref[...]).astype(BF16), wb_ref[...], preferred_element_type=F32)
    hid = a + pltpu.roll(b, ncp - 1, 0)
    hid = hid * _sigmoid(hid)
    y = jnp.dot(hid.astype(BF16), w2_ref[...], preferred_element_type=F32)
    if rope:
        first_half = (_iota(y.shape, 1) & (HEAD_DIM - 1)) < (ROPE_DIM // 2)
        y = _rope_chunk(y, cos_ref[0], sin_ref[0], first_half)
    y = y.astype(o_ref.dtype)
    for g in range(NSA_KV_GROUPS):
        o_ref[0, g] = y[:, g * HEAD_DIM:(g + 1) * HEAD_DIM]


def _compress(kv, pos_emb, w1, w2, cos_c, sin_c, rope, name):
    bsz, t, _ = kv.shape
    ncp = t // CMP_STRIDE
    kwid = CMP_STRIDE * NSA_KV_GROUPS * HEAD_DIM
    r = kv.reshape(bsz, ncp, kwid)
    per = CMP_BLOCK // CMP_STRIDE
    w1r = w1.reshape(per, CMP_STRIDE, HEAD_DIM, CMP_HIDDEN)
    zeros = jnp.zeros_like(w1r)
    grp0 = jnp.concatenate([w1r, zeros], axis=-1)
    grp1 = jnp.concatenate([zeros, w1r], axis=-1)
    wbig = jnp.stack([grp0, grp1], axis=2).reshape(per, kwid, NSA_KV_GROUPS * CMP_HIDDEN).astype(BF16)
    posr = pos_emb.reshape(per, CMP_STRIDE, 1, HEAD_DIM)
    posbig = jnp.broadcast_to(posr, (per, CMP_STRIDE, NSA_KV_GROUPS, HEAD_DIM)).reshape(per, 1, kwid)
    z2 = jnp.zeros_like(w2)
    w2big = jnp.concatenate([jnp.concatenate([w2, z2], axis=1),
                             jnp.concatenate([z2, w2], axis=1)], axis=0).astype(BF16)
    const = lambda shape: pl.BlockSpec(shape, lambda b: (0,) * len(shape))
    tab = pl.BlockSpec((1, ncp, LANES), lambda b: (b, 0, 0))
    return pl.pallas_call(
        functools.partial(_compress_kernel, rope=rope),
        grid=(bsz,),
        in_specs=[pl.BlockSpec((1, ncp, kwid), lambda b: (b, 0, 0)),
                  const((1, kwid)), const((1, kwid)),
                  const((kwid, NSA_KV_GROUPS * CMP_HIDDEN)), const((kwid, NSA_KV_GROUPS * CMP_HIDDEN)),
                  const((NSA_KV_GROUPS * CMP_HIDDEN, LANES)), tab, tab],
        out_specs=pl.BlockSpec((1, NSA_KV_GROUPS, ncp, HEAD_DIM), lambda b: (b, 0, 0, 0)),
        out_shape=jax.ShapeDtypeStruct((bsz, NSA_KV_GROUPS, ncp, HEAD_DIM), BF16),
        compiler_params=_params("arbitrary"),
        name=name,
    )(r, posbig[0], posbig[1], wbig[0], wbig[1], w2big, cos_c, sin_c)


def _softmax_rows(s, mask):
    sm = jnp.where(mask, s, NEG_INF)
    mx = jnp.max(sm, axis=1, keepdims=True)
    e = jnp.where(mask, jnp.exp(sm - mx), 0.0)
    l = jnp.sum(e, axis=1, keepdims=True)
    return e * (1.0 / jnp.where(l == 0.0, 1.0, l))


def _online_step(s, mask, v, m_old, l_old, acc_old):
    sm = jnp.where(mask, s, NEG_INF)
    m_new = jnp.maximum(m_old, jnp.max(sm, axis=1, keepdims=True))
    alpha = jnp.exp(m_old - m_new)
    p = jnp.where(mask, jnp.exp(sm - m_new), 0.0)
    l_new = alpha * l_old + jnp.sum(p, axis=1, keepdims=True)
    acc_new = alpha * acc_old + jnp.dot(p.astype(BF16), v, preferred_element_type=F32)
    return m_new, l_new, acc_new


def _split3(v):
    hi = v.astype(BF16)
    r1 = v - hi.astype(F32)
    mid = r1.astype(BF16)
    lo = (r1 - mid.astype(F32)).astype(BF16)
    return hi, mid, lo


def _nsa_kernel(q_ref, kc_ref, vc_ref, ks_ref, vs_ref, kw_ref, vw_ref, g_ref, ovt_ref, o_ref,
                score_scr, *, n_sel):
    qn = q_ref.shape[2]
    ncp = kc_ref.shape[2]
    ns = ovt_ref.shape[0]
    q0 = pl.program_id(2) * qn
    reps = range(NSA_REP)

    kc = kc_ref[0, 0]
    vc = vc_ref[0, 0]
    t_c = q0 + _iota((qn, ncp), 0)
    mask_c = (_iota((qn, ncp), 1) * CMP_STRIDE + (CMP_BLOCK - 1)) <= t_c
    o_cmp, p_sum = [], None
    for r in reps:
        s = lax.dot_general(q_ref[0, r], kc, _NT, preferred_element_type=F32)
        p = _softmax_rows(s, mask_c)
        o_cmp.append(jnp.dot(p.astype(BF16), vc, preferred_element_type=F32))
        p_sum = p if p_sum is None else p_sum + p
    ovt = ovt_ref[...]
    imp_t = None
    for term in _split3(p_sum):
        part = lax.dot_general(ovt, term, _NT, preferred_element_type=F32)
        imp_t = part if imp_t is None else imp_t + part

    blk = _iota((ns, qn), 0)
    cur = (q0 + _iota((ns, qn), 1)) >> int(math.log2(SLC_BLOCK))
    forced = (blk == 0) | (blk == cur) | (blk == cur - 1)
    score = jnp.where(forced, FORCE, jnp.where(blk <= cur, imp_t, -FORCE))
    score_scr[...] = score

    def rank_body(m, cnt):
        other = score_scr[pl.ds(m, 1), :]
        ahead = (other > score) | ((other == score) & (blk > m))
        return cnt + jnp.where(ahead, 1.0, 0.0)

    cnt = lax.fori_loop(0, ns, rank_body, jnp.zeros((ns, qn), F32))
    sel = jnp.where(cnt < float(n_sel), 1.0, 0.0).T.astype(BF16)

    kt_n = KEY_TILE
    n_tiles = (q0 + qn + kt_n - 1) // kt_n
    t_s = q0 + _iota((qn, kt_n), 0)

    def sel_body(kt, carry):
        k0 = pl.multiple_of(kt * kt_n, kt_n)
        k = ks_ref[0, 0, pl.ds(k0, kt_n), :]
        v = vs_ref[0, 0, pl.ds(k0, kt_n), :]
        expand = jnp.where(((k0 + _iota((ns, kt_n), 1)) >> int(math.log2(SLC_BLOCK))) == _iota((ns, kt_n), 0),
                           1.0, 0.0).astype(BF16)
        picked = jnp.dot(sel, expand, preferred_element_type=F32)
        mask = (picked > 0.5) & ((k0 + _iota((qn, kt_n), 1)) <= t_s)
        out = []
        for r in reps:
            s = lax.dot_general(q_ref[0, r], k, _NT, preferred_element_type=F32)
            out.append(_online_step(s, mask, v, *carry[r]))
        return tuple(out)

    init = tuple((jnp.full((qn, 1), NEG_INF, F32), jnp.zeros((qn, 1), F32), jnp.zeros((qn, HEAD_DIM), F32))
                 for _ in reps)
    fin = lax.fori_loop(0, n_tiles, sel_body, init)
    o_sel = [acc * (1.0 / l) for (_, l, acc) in fin]

    wk = WINDOW + qn
    kstart = pl.multiple_of(jnp.maximum(q0 - WINDOW, 0), qn)
    kw = kw_ref[0, 0, pl.ds(kstart, wk), :]
    vw = vw_ref[0, 0, pl.ds(kstart, wk), :]
    kp = kstart + _iota((qn, wk), 1)
    t_w = q0 + _iota((qn, wk), 0)
    mask_w = (kp <= t_w) & (kp > t_w - WINDOW)
    o_win = []
    for r in reps:
        s = lax.dot_general(q_ref[0, r], kw, _NT, preferred_element_type=F32)
        p = _softmax_rows(s, mask_w)
        o_win.append(jnp.dot(p.astype(BF16), vw, preferred_element_type=F32))

    gates = g_ref[0]
    outs = []
    for r in reps:
        outs.append(gates[:, 3 * r:3 * r + 1] * o_cmp[r] + gates[:, 3 * r + 1:3 * r + 2] * o_sel[r]
                    + gates[:, 3 * r + 2:3 * r + 3] * o_win[r])
    o_ref[0] = jnp.concatenate(outs, axis=1).astype(o_ref.dtype)


def _nsa_attention(q, k_cmp, v_cmp, k_slc, v_slc, k_win, v_win, gates):
    bsz, _, t, _ = q.shape
    ncp = k_cmp.shape[2]
    ns = t // SLC_BLOCK
    qn = Q_TILE
    cmp_start = jnp.arange(ncp) * CMP_STRIDE
    slc_start = jnp.arange(ns) * SLC_BLOCK
    real = jnp.arange(ncp) < (t - CMP_BLOCK) // CMP_STRIDE + 1
    overlap_t = ((cmp_start[None, :] < slc_start[:, None] + SLC_BLOCK)
                 & (cmp_start[None, :] + CMP_BLOCK > slc_start[:, None]) & real[None, :]).astype(BF16)
    grp = lambda n: pl.BlockSpec((1, 1, n, HEAD_DIM), lambda b, g, i: (b, g, 0, 0))
    return pl.pallas_call(
        functools.partial(_nsa_kernel, n_sel=min(N_SELECT, ns)),
        grid=(bsz, NSA_KV_GROUPS, t // qn),
        in_specs=[pl.BlockSpec((1, NSA_REP, qn, HEAD_DIM), lambda b, g, i: (b, g, i, 0)),
                  grp(ncp), grp(ncp), grp(t), grp(t), grp(t), grp(t),
                  pl.BlockSpec((1, qn, LANES), lambda b, g, i: (b, i, g)),
                  pl.BlockSpec((ns, ncp), lambda b, g, i: (0, 0))],
        out_specs=pl.BlockSpec((1, qn, NSA_REP * HEAD_DIM), lambda b, g, i: (b, i, g)),
        out_shape=jax.ShapeDtypeStruct((bsz, t, NSA_HEADS * HEAD_DIM), BF16),
        scratch_shapes=[pltpu.VMEM((ns, qn), F32)],
        compiler_params=_params("arbitrary", "arbitrary", "arbitrary"),
        name="nsa_attention",
    )(q, k_cmp, v_cmp, k_slc, v_slc, k_win, v_win, gates, overlap_t)


SB_HEADS_PER_STEP = LANES // HEAD_DIM


def _sb_kernel(q_ref, k_ref, v_ref, o_ref):
    qn = q_ref.shape[2]
    kt_n = KEY_TILE
    q0 = pl.program_id(2) * qn
    n_tiles = (q0 + qn + kt_n - 1) // kt_n
    t_i = q0 + _iota((qn, kt_n), 0)
    later = jnp.where(_iota((kt_n, kt_n), 0) > _iota((kt_n, kt_n), 1), 1.0, 0.0).astype(BF16)
    outs = []
    for h in range(SB_HEADS_PER_STEP):
        q = q_ref[0, h]

        def body(i, carry, h=h, q=q):
            tail, acc = carry
            k0 = pl.multiple_of((n_tiles - 1 - i) * kt_n, kt_n)
            k = k_ref[0, h, pl.ds(k0, kt_n), :]
            v = v_ref[0, h, pl.ds(k0, kt_n), :]
            z = lax.dot_general(q, k, _NT, preferred_element_type=F32)
            mask = (k0 + _iota((qn, kt_n), 1)) < t_i
            soft = jnp.log1p(jnp.exp(-jnp.abs(z)))
            log_keep = jnp.where(mask, -(jnp.maximum(z, 0.0) + soft), 0.0)
            log_beta = -(jnp.maximum(-z, 0.0) + soft)
            within = None
            for term in _split3(log_keep):
                part = jnp.dot(term, later, preferred_element_type=F32)
                within = part if within is None else within + part
            a = jnp.where(mask, jnp.exp(log_beta + within + tail), 0.0)
            acc = acc + jnp.dot(a.astype(BF16), v, preferred_element_type=F32)
            tail = tail + jnp.sum(log_keep, axis=1, keepdims=True)
            return tail, acc

        _, acc = lax.fori_loop(0, n_tiles, body, (jnp.zeros((qn, 1), F32), jnp.zeros((qn, HEAD_DIM), F32)))
        outs.append(acc)
    o_ref[0] = jnp.concatenate(outs, axis=1).astype(o_ref.dtype)


def _sb_attention(q, k, v):
    bsz, nh, t, _ = q.shape
    qn = Q_TILE
    hps = SB_HEADS_PER_STEP
    kv = pl.BlockSpec((1, hps, t, HEAD_DIM), lambda b, h, i: (b, h, 0, 0))
    return pl.pallas_call(
        _sb_kernel,
        grid=(bsz, nh // hps, t // qn),
        in_specs=[pl.BlockSpec((1, hps, qn, HEAD_DIM), lambda b, h, i: (b, h, i, 0)), kv, kv],
        out_specs=pl.BlockSpec((1, qn, LANES), lambda b, h, i: (b, i, h)),
        out_shape=jax.ShapeDtypeStruct((bsz, t, nh * HEAD_DIM), BF16),
        compiler_params=_params("arbitrary", "arbitrary", "arbitrary"),
        name="stickbreak_attention",
    )(q, k, v)


def _diff_kernel(q_ref, k_ref, v_ref, lq1_ref, lk1_ref, lq2_ref, lk2_ref, sub_ref, o_ref, *, lambda_init):
    qn = q_ref.shape[2]
    kt_n = KEY_TILE
    q0 = pl.program_id(2) * qn
    n_tiles = (q0 + qn + kt_n - 1) // kt_n
    t_i = q0 + _iota((qn, kt_n), 0)
    lam = (jnp.exp(jnp.sum(lq1_ref[...] * lk1_ref[...], axis=1, keepdims=True))
           - jnp.exp(jnp.sum(lq2_ref[...] * lk2_ref[...], axis=1, keepdims=True)) + lambda_init)

    def body(kt, carry):
        k0 = pl.multiple_of(kt * kt_n, kt_n)
        v = v_ref[0, 0, pl.ds(k0, kt_n), :]
        mask = (k0 + _iota((qn, kt_n), 1)) <= t_i
        out = []
        for j in range(2):
            s = lax.dot_general(q_ref[0, j], k_ref[0, j, pl.ds(k0, kt_n), :], _NT, preferred_element_type=F32)
            out.append(_online_step(s, mask, v, *carry[j]))
        return tuple(out)

    vd = v_ref.shape[3]
    init = tuple((jnp.full((qn, 1), NEG_INF, F32), jnp.zeros((qn, 1), F32), jnp.zeros((qn, vd), F32))
                 for _ in range(2))
    (_, l1, a1), (_, l2, a2) = lax.fori_loop(0, n_tiles, body, init)
    o = a1 * (1.0 / l1) - lam * (a2 * (1.0 / l2))
    ms = jnp.mean(o * o, axis=-1, keepdims=True)
    o = o * lax.rsqrt(ms + RMS_EPS) * sub_ref[...] * (1.0 - lambda_init)
    o_ref[0] = o.astype(o_ref.dtype)


def _diff_attention(q, k, v, lq1, lk1, lq2, lk2, subln, lambda_init):
    bsz, nh, t, vd = v.shape
    qn = Q_TILE
    row = lambda a: a.reshape(1, -1)
    small = lambda n: pl.BlockSpec((1, n), lambda b, h, i: (0, 0))
    return pl.pallas_call(
        functools.partial(_diff_kernel, lambda_init=lambda_init),
        grid=(bsz, nh, t // qn),
        in_specs=[pl.BlockSpec((1, 2, qn, HEAD_DIM), lambda b, h, i: (b, h, i, 0)),
                  pl.BlockSpec((1, 2, t, HEAD_DIM), lambda b, h, i: (b, h, 0, 0)),
                  pl.BlockSpec((1, 1, t, vd), lambda b, h, i: (b, h, 0, 0)),
                  small(HEAD_DIM), small(HEAD_DIM), small(HEAD_DIM), small(HEAD_DIM), small(vd)],
        out_specs=pl.BlockSpec((1, qn, vd), lambda b, h, i: (b, i, h)),
        out_shape=jax.ShapeDtypeStruct((bsz, t, nh * vd), BF16),
        compiler_params=_params("arbitrary", "arbitrary", "arbitrary"),
        name="diff_attention",
    )(q, k, v, row(lq1), row(lk1), row(lq2), row(lk2), row(subln))


def _outproj_kernel(*refs, n_in):
    a_refs, w_refs = refs[:n_in], refs[n_in:2 * n_in]
    x_ref, gate_ref, o_ref = refs[2 * n_in:]
    y = None
    for a_ref, w_ref in zip(a_refs, w_refs):
        part = jnp.dot(a_ref[0], w_ref[...], preferred_element_type=F32)
        y = part if y is None else y + part
    o_ref[0] = x_ref[0] + gate_ref[0] * y


def _out_project(acts, weights, x, gate, name):
    bsz, t, d = x.shape
    rows = min(t, PROJ_ROWS)
    tile = lambda w: pl.BlockSpec((1, rows, w), lambda b, i: (b, i, 0))
    return pl.pallas_call(
        functools.partial(_outproj_kernel, n_in=len(acts)),
        grid=(bsz, t // rows),
        in_specs=([tile(a.shape[2]) for a in acts]
                  + [pl.BlockSpec(w.shape, lambda b, i: (0, 0)) for w in weights]
                  + [tile(d), pl.BlockSpec((1, 1, d), lambda b, i: (b, 0, 0))]),
        out_specs=tile(d),
        out_shape=jax.ShapeDtypeStruct((bsz, t, d), F32),
        compiler_params=_params("arbitrary", "arbitrary"),
        name=name,
    )(*acts, *weights, x, gate)


def _ffn_kernel(x_ref, halo_ref, g_ref, sh_ref, sc_ref, gate_ref, wg_ref, wu_ref, cw_ref, cb_ref, wd_ref,
                nf_ref, o_ref, h_scr, halo_scr, acc_scr, *, final_norm):
    ti = pl.program_id(1)
    j = pl.program_id(2)

    @pl.when(j == 0)
    def _():
        h_scr[...] = _norm_modulate(x_ref[0], g_ref[...], sh_ref[0], sc_ref[0]).astype(BF16)
        halo_scr[...] = _norm_modulate(halo_ref[0], g_ref[...], sh_ref[0], sc_ref[0]).astype(BF16)
        acc_scr[...] = jnp.zeros_like(acc_scr)

    h = h_scr[...]
    wg = wg_ref[...]
    g = jnp.dot(h, wg, preferred_element_type=F32)
    g_prev = jnp.dot(halo_scr[...], wg, preferred_element_type=F32)
    g_prev = g_prev * jnp.where(ti > 0, 1.0, 0.0)
    rows = g.shape[0]
    row = _iota(g.shape, 0)
    last = g_prev[HALO_ROWS - 1:HALO_ROWS]
    last2 = g_prev[HALO_ROWS - 2:HALO_ROWS - 1]
    g1 = jnp.where(row == 0, last, pltpu.roll(g, 1, 0))
    g2 = jnp.where(row == 0, last2, jnp.where(row == 1, last, pltpu.roll(g, 2, 0)))
    cw = cw_ref[...]
    conv = cw[0:1] * g2 + cw[1:2] * g1 + cw[2:3] * g + cb_ref[...]
    a = conv * _sigmoid(conv) * jnp.dot(h, wu_ref[...], preferred_element_type=F32)
    acc_scr[...] += jnp.dot(a.astype(BF16), wd_ref[...], preferred_element_type=F32)

    @pl.when(j == pl.num_programs(2) - 1)
    def _():
        y = x_ref[0] + gate_ref[0] * acc_scr[...]
        if final_norm:
            ms = jnp.mean(y * y, axis=-1, keepdims=True)
            y = y * lax.rsqrt(ms + RMS_EPS) * nf_ref[...]
        o_ref[0] = y


def _conv_ffn(x, g, shift, scale, gate, w_gate, w_up, conv_w, conv_b, w_down, norm_f, final_norm, name):
    bsz, t, d = x.shape
    f = w_gate.shape[1]
    rows = min(t, FFN_ROWS)
    cols = FFN_COLS
    halo_blocks = rows // HALO_ROWS
    vec = pl.BlockSpec((1, 1, d), lambda b, i, j: (b, 0, 0))
    drow = pl.BlockSpec((1, d), lambda b, i, j: (0, 0))
    return pl.pallas_call(
        functools.partial(_ffn_kernel, final_norm=final_norm),
        grid=(bsz, t // rows, f // cols),
        in_specs=[pl.BlockSpec((1, rows, d), lambda b, i, j: (b, i, 0)),
                  pl.BlockSpec((1, HALO_ROWS, d), lambda b, i, j: (b, jnp.maximum(i * halo_blocks - 1, 0), 0)),
                  drow, vec, vec, vec,
                  pl.BlockSpec((d, cols), lambda b, i, j: (0, j)),
                  pl.BlockSpec((d, cols), lambda b, i, j: (0, j)),
                  pl.BlockSpec((CONV_WIDTH, cols), lambda b, i, j: (0, j)),
                  pl.BlockSpec((1, cols), lambda b, i, j: (0, j)),
                  pl.BlockSpec((cols, d), lambda b, i, j: (j, 0)),
                  drow],
        out_specs=pl.BlockSpec((1, rows, d), lambda b, i, j: (b, i, 0)),
        out_shape=jax.ShapeDtypeStruct((bsz, t, d), F32),
        scratch_shapes=[pltpu.VMEM((rows, d), BF16), pltpu.VMEM((HALO_ROWS, d), BF16),
                        pltpu.VMEM((rows, d), F32)],
        compiler_params=_params("arbitrary", "arbitrary", "arbitrary"),
        name=name,
    )(x, x, g.reshape(1, d), shift, scale, gate, w_gate.astype(BF16), w_up.astype(BF16), conv_w,
      conv_b.reshape(1, f), w_down.astype(BF16), norm_f.reshape(1, d))


def _pack_hybrid_weight(w_in):
    d = w_in.shape[0]
    qw, kvw, sbw = NSA_HEADS * HEAD_DIM, NSA_KV_GROUPS * HEAD_DIM, SB_HEADS * HEAD_DIM
    sizes = (qw,) + (kvw,) * 6 + (3 * NSA_HEADS,) + (sbw,) * 3
    parts, start = [], 0
    for s in sizes:
        parts.append(w_in[:, start:start + s])
        start += s
    q_n, kc, vc, ks, vs, kw, vw, gl, q_s, k_s, v_s = parts
    per_group = 3 * NSA_REP
    pad = jnp.zeros((d, LANES - per_group), w_in.dtype)
    gl_pad = jnp.concatenate([gl[:, :per_group], pad, gl[:, per_group:], pad], axis=1)
    return jnp.concatenate([q_n, ks, kw, vs, vw, kc, vc, gl_pad, q_s, k_s, v_s], axis=1).astype(BF16)


def _hybrid_segments():
    qw, kvw, sbw = NSA_HEADS * HEAD_DIM, NSA_KV_GROUPS * HEAD_DIM, SB_HEADS * HEAD_DIM
    inv_sqrt = HEAD_DIM ** -0.5
    layout = [(qw, True, inv_sqrt, False, HEAD_DIM, BF16),
              (kvw, True, 1.0, False, HEAD_DIM, BF16),
              (kvw, True, 1.0, False, HEAD_DIM, BF16),
              (kvw, False, 1.0, False, HEAD_DIM, BF16),
              (kvw, False, 1.0, False, HEAD_DIM, BF16),
              (kvw, False, 1.0, False, 0, F32),
              (kvw, False, 1.0, False, 0, F32),
              (NSA_KV_GROUPS * LANES, False, 1.0, True, 0, F32),
              (sbw, False, inv_sqrt, False, HEAD_DIM, BF16),
              (sbw, False, 1.0, False, HEAD_DIM, BF16),
              (sbw, False, 1.0, False, HEAD_DIM, BF16)]
    segs, start = [], 0
    for width, rope, scale, sig, hw, dt in layout:
        segs.append(Seg(start, width, rope, scale, sig, hw, dt))
        start += width
    return segs


def _hybrid_layer(x, mods, positions, cos_t, sin_t, norm_g, w_in, pos_k, pos_v, ck_w1, ck_w2, cv_w1, cv_w2, w_out):
    shift, scale, gate = mods
    bsz, t, _ = x.shape
    (q_n, k_slc, k_win, v_slc, v_win, kc, vc, gates, q_s, k_s, v_s) = _project(
        x, norm_g, shift, scale, _pack_hybrid_weight(w_in), cos_t, sin_t, _hybrid_segments(), "hybrid_in_proj")
    ncp = t // CMP_STRIDE
    end_rows = jnp.minimum(jnp.arange(ncp) * CMP_STRIDE + CMP_BLOCK - 1, t - 1)
    cos_c, sin_c = cos_t[:, end_rows], sin_t[:, end_rows]
    k_cmp = _compress(kc, pos_k, ck_w1, ck_w2, cos_c, sin_c, True, "compress_k")
    v_cmp = _compress(vc, pos_v, cv_w1, cv_w2, cos_c, sin_c, False, "compress_v")
    o_nsa = _nsa_attention(q_n, k_cmp, v_cmp, k_slc, v_slc, k_win, v_win, gates)
    o_sb = _sb_attention(q_s, k_s, v_s)
    nsa_w = NSA_HEADS * HEAD_DIM
    w_out_b = w_out.astype(BF16)
    return _out_project([o_nsa, o_sb], [w_out_b[:nsa_w], w_out_b[nsa_w:]], x, gate, "hybrid_out_proj")


def _diff_layer(x, mods, cos_t, sin_t, norm_g, w_qkv, lq1, lk1, lq2, lk2, subln, w_out, layer_idx):
    shift, scale, gate = mods
    dw = 2 * DIFF_HEADS * HEAD_DIM
    inv_sqrt = HEAD_DIM ** -0.5
    segs = [Seg(0, dw, True, inv_sqrt, False, HEAD_DIM, BF16),
            Seg(dw, dw, True, 1.0, False, HEAD_DIM, BF16),
            Seg(2 * dw, dw, False, 1.0, False, 2 * HEAD_DIM, BF16)]
    q, k, v = _project(x, norm_g, shift, scale, w_qkv.astype(BF16), cos_t, sin_t, segs, "diff_in_proj")
    lambda_init = 0.8 - 0.6 * math.exp(-0.3 * layer_idx)
    o = _diff_attention(q, k, v, lq1, lk1, lq2, lk2, subln, lambda_init)
    return _out_project([o], [w_out.astype(BF16)], x, gate, "diff_out_proj")


def kernel(x, c, positions, mod_w, mod_b, norm_mix, norm_ffn, ffn_w_gate, ffn_w_up, ffn_conv_w, ffn_conv_b, ffn_w_down, hyb_w_in, nsa_pos_k, nsa_pos_v, nsa_ck_w1, nsa_ck_w2, nsa_cv_w1, nsa_cv_w2, hyb_w_out, diff_w_qkv, diff_lq1, diff_lk1, diff_lq2, diff_lk2, diff_subln, diff_w_out, norm_f):
    bsz, t, d = x.shape
    depth = mod_w.shape[0]
    mod = _adaln_mod(c, mod_w, mod_b)
    cos_t, sin_t = _rope_tables(positions)
    for i in range(depth):
        sh_m, sc_m, g_m, sh_f, sc_f, g_f = (mod[i, :, k * d:(k + 1) * d].reshape(bsz, 1, d) for k in range(6))
        j = i // 2
        if i % 2 == 0:
            x = _hybrid_layer(x, (sh_m, sc_m, g_m), positions, cos_t, sin_t, norm_mix[i], hyb_w_in[j],
                              nsa_pos_k[j], nsa_pos_v[j], nsa_ck_w1[j], nsa_ck_w2[j], nsa_cv_w1[j],
                              nsa_cv_w2[j], hyb_w_out[j])
        else:
            x = _diff_layer(x, (sh_m, sc_m, g_m), cos_t, sin_t, norm_mix[i], diff_w_qkv[j], diff_lq1[j],
                            diff_lk1[j], diff_lq2[j], diff_lk2[j], diff_subln[j], diff_w_out[j], i)
        x = _conv_ffn(x, norm_ffn[i], sh_f, sc_f, g_f, ffn_w_gate[i], ffn_w_up[i], ffn_conv_w[i],
                      ffn_conv_b[i], ffn_w_down[i], norm_f, i == depth - 1, "conv_ffn_%d" % i)
    return x
```

```python
import functools
import math
from typing import NamedTuple

import jax
import jax.numpy as jnp
from jax import lax
from jax.experimental import pallas as pl
from jax.experimental.pallas import tpu as pltpu

F32 = jnp.float32
BF16 = jnp.bfloat16
I32 = jnp.int32

HEAD_DIM = 64
ROPE_DIM = HEAD_DIM // 4
ROPE_HALF = ROPE_DIM // 2
ROPE_THETA = 500000.0
NSA_HEADS = 8
NSA_KV_GROUPS = 2
NSA_REP = NSA_HEADS // NSA_KV_GROUPS
CMP_BLOCK = 32
CMP_STRIDE = 16
CMP_HIDDEN = 4 * HEAD_DIM
SLC_BLOCK = 64
SLC_SHIFT = 6
N_SELECT = 16
WINDOW = 512
SB_HEADS = 8
DIFF_HEADS = 8
CONV_WIDTH = 3
RMS_EPS = 1e-6
NEG_INF = -1e30
FORCE = 1e9

LANES = 128
Q_TILE = 512
KEY_TILE = 512
GATE_ROWS = 16
PROJ_ROWS = 512
FFN_ROWS = 1024
FFN_COLS = 256
HALO_ROWS = 16
VMEM_LIMIT = 56 * 1024 * 1024

_NT = (((1,), (1,)), ((), ()))
_TN = (((0,), (0,)), ((), ()))


def _params(*sem):
    return pltpu.CompilerParams(dimension_semantics=sem, vmem_limit_bytes=VMEM_LIMIT)


def _sigmoid(v):
    return 1.0 / (1.0 + jnp.exp(-v))


def _iota(shape, axis):
    return lax.broadcasted_iota(I32, shape, axis)


def _dot(a, b):
    return jnp.dot(a, b, preferred_element_type=F32)


def _mod_kernel(c_ref, w_ref, b_ref, o_ref):
    c = c_ref[...]
    cond = c * _sigmoid(c)
    o_ref[0] = jnp.dot(cond, w_ref[0], preferred_element_type=F32,
                       precision=lax.Precision.HIGHEST) + b_ref[0]


def _adaln_mod(c, mod_w, mod_b):
    depth, d, n = mod_w.shape
    bsz = c.shape[0]
    tn = n // 4
    return pl.pallas_call(
        _mod_kernel,
        grid=(depth, n // tn),
        in_specs=[pl.BlockSpec((bsz, d), lambda i, j: (0, 0)),
                  pl.BlockSpec((1, d, tn), lambda i, j: (i, 0, j)),
                  pl.BlockSpec((1, 1, tn), lambda i, j: (i, 0, j))],
        out_specs=pl.BlockSpec((1, bsz, tn), lambda i, j: (i, 0, j)),
        out_shape=jax.ShapeDtypeStruct((depth, bsz, n), F32),
        compiler_params=_params("arbitrary", "arbitrary"),
        name="adaln_mod",
    )(c, mod_w, mod_b.reshape(depth, 1, n))


def _rope_kernel(pos_col_ref, pos_row_ref, inv_row_ref, sgn_row_ref, inv_col_ref,
                 cos_ref, sin_ref, cos_t_ref, sin_t_ref):
    ang = pos_col_ref[0].astype(F32) * inv_row_ref[...]
    cos_ref[0] = jnp.cos(ang)
    sin_ref[0] = jnp.sin(ang) * sgn_row_ref[...]
    ang_t = inv_col_ref[...] * pos_row_ref[0].astype(F32)
    cos_t_ref[0] = jnp.cos(ang_t)
    sin_t_ref[0] = jnp.sin(ang_t)


def _rope_tables(positions):
    bsz, t = positions.shape
    inv = ROPE_THETA ** (-jnp.arange(0, ROPE_DIM, 2, dtype=F32) / ROPE_DIM)
    per_head_inv = jnp.concatenate([inv, inv, jnp.zeros((HEAD_DIM - ROPE_DIM,), F32)])
    per_head_sgn = jnp.concatenate([-jnp.ones((ROPE_HALF,), F32), jnp.ones((ROPE_HALF,), F32),
                                    jnp.zeros((HEAD_DIM - ROPE_DIM,), F32)])
    inv_row = jnp.tile(per_head_inv, LANES // HEAD_DIM)[None, :]
    sgn_row = jnp.tile(per_head_sgn, LANES // HEAD_DIM)[None, :]
    rows = min(t, 1024)
    tab = jax.ShapeDtypeStruct((bsz, t, LANES), F32)
    tab_t = jax.ShapeDtypeStruct((bsz, ROPE_HALF, t), F32)
    row_spec = pl.BlockSpec((1, LANES), lambda b, i: (0, 0))
    return pl.pallas_call(
        _rope_kernel,
        grid=(bsz, t // rows),
        in_specs=[pl.BlockSpec((1, rows, 1), lambda b, i: (b, i, 0)),
                  pl.BlockSpec((1, 1, rows), lambda b, i: (b, 0, i)),
                  row_spec, row_spec,
                  pl.BlockSpec((ROPE_HALF, 1), lambda b, i: (0, 0))],
        out_specs=[pl.BlockSpec((1, rows, LANES), lambda b, i: (b, i, 0))] * 2
        + [pl.BlockSpec((1, ROPE_HALF, rows), lambda b, i: (b, 0, i))] * 2,
        out_shape=[tab, tab, tab_t, tab_t],
        compiler_params=_params("arbitrary", "arbitrary"),
        name="rope_tables",
    )(positions.reshape(bsz, t, 1), positions.reshape(bsz, 1, t), inv_row, sgn_row, inv[:, None])


def _rope_chunk(y, cosv, sinv, first_half):
    ahead = pltpu.roll(y, LANES - ROPE_HALF, 1)
    behind = pltpu.roll(y, ROPE_HALF, 1)
    return y * cosv + jnp.where(first_half, ahead, behind) * sinv


def _rope_rows(y_t, cos_t, sin_t):
    heads = []
    for h in range(y_t.shape[0] // HEAD_DIM):
        blk = y_t[h * HEAD_DIM:(h + 1) * HEAD_DIM]
        x1, x2 = blk[:ROPE_HALF], blk[ROPE_HALF:ROPE_DIM]
        heads += [x1 * cos_t - x2 * sin_t, x2 * cos_t + x1 * sin_t, blk[ROPE_DIM:]]
    return jnp.concatenate(heads, axis=0)


class TokSeg(NamedTuple):
    start: int
    width: int
    rope: bool
    head_width: int
    dtype: object


class FeatSeg(NamedTuple):
    start: int
    rows: int
    rope: bool
    scale: float
    sigmoid: bool
    key_tiled: bool
    dtype: object


def _norm_modulate(x, g, shift, scale):
    ms = jnp.mean(x * x, axis=-1, keepdims=True)
    y = x * lax.rsqrt(ms + RMS_EPS) * g
    return y * (1.0 + scale) + shift


def _proj_kernel(x_ref, g_ref, sh_ref, sc_ref, w_ref, wt_ref, cos_ref, sin_ref, cos_t_ref, sin_t_ref,
                 *out_refs, tok_segs, feat_segs):
    hb = _norm_modulate(x_ref[0], g_ref[...], sh_ref[0], sc_ref[0]).astype(BF16)
    cosv, sinv = cos_ref[0], sin_ref[0]
    first_half = (_iota(cosv.shape, 1) & (HEAD_DIM - 1)) < ROPE_HALF
    tok_refs, feat_refs = out_refs[:len(tok_segs)], out_refs[len(tok_segs):]
    for seg, o_ref in zip(tok_segs, tok_refs):
        y = _dot(hb, w_ref[:, seg.start:seg.start + seg.width])
        for ch in range(seg.width // LANES):
            yc = y[:, ch * LANES:(ch + 1) * LANES]
            if seg.rope:
                yc = _rope_chunk(yc, cosv, sinv, first_half)
            yc = yc.astype(seg.dtype)
            if seg.head_width == 0:
                o_ref[0, :, ch * LANES:(ch + 1) * LANES] = yc
            else:
                o_ref[0, 2 * ch] = yc[:, :HEAD_DIM]
                o_ref[0, 2 * ch + 1] = yc[:, HEAD_DIM:]
    for seg, o_ref in zip(feat_segs, feat_refs):
        y_t = lax.dot_general(wt_ref[seg.start:seg.start + seg.rows, :], hb, _NT,
                              preferred_element_type=F32)
        if seg.rope:
            y_t = _rope_rows(y_t, cos_t_ref[0], sin_t_ref[0])
        if seg.scale != 1.0:
            y_t = y_t * seg.scale
        if seg.sigmoid:
            y_t = _sigmoid(y_t)
        y_t = y_t.astype(seg.dtype)
        if seg.key_tiled:
            for ch in range(y_t.shape[1] // KEY_TILE):
                o_ref[0, ch] = y_t[:, ch * KEY_TILE:(ch + 1) * KEY_TILE]
        else:
            o_ref[0] = y_t


def _project(x, g, shift, scale, w_tok, w_feat_t, tables, tok_segs, feat_segs, name):
    bsz, t, d = x.shape
    rows = min(t, PROJ_ROWS)
    cos_tab, sin_tab, cos_t, sin_t = tables
    out_shapes, out_specs = [], []
    for seg in tok_segs:
        if seg.head_width == 0:
            out_shapes.append(jax.ShapeDtypeStruct((bsz, t, seg.width), seg.dtype))
            out_specs.append(pl.BlockSpec((1, rows, seg.width), lambda b, i: (b, i, 0)))
        else:
            nh = seg.width // seg.head_width
            out_shapes.append(jax.ShapeDtypeStruct((bsz, nh, t, seg.head_width), seg.dtype))
            out_specs.append(pl.BlockSpec((1, nh, rows, seg.head_width), lambda b, i: (b, 0, i, 0)))
    for seg in feat_segs:
        if seg.key_tiled:
            out_shapes.append(jax.ShapeDtypeStruct((bsz, t // KEY_TILE, seg.rows, KEY_TILE), seg.dtype))
            out_specs.append(pl.BlockSpec((1, rows // KEY_TILE, seg.rows, KEY_TILE), lambda b, i: (b, i, 0, 0)))
        else:
            out_shapes.append(jax.ShapeDtypeStruct((bsz, seg.rows, t), seg.dtype))
            out_specs.append(pl.BlockSpec((1, seg.rows, rows), lambda b, i: (b, 0, i)))
    vec = pl.BlockSpec((1, 1, d), lambda b, i: (b, 0, 0))
    tab = pl.BlockSpec((1, rows, LANES), lambda b, i: (b, i, 0))
    tab_t = pl.BlockSpec((1, ROPE_HALF, rows), lambda b, i: (b, 0, i))
    return pl.pallas_call(
        functools.partial(_proj_kernel, tok_segs=tuple(tok_segs), feat_segs=tuple(feat_segs)),
        grid=(bsz, t // rows),
        in_specs=[pl.BlockSpec((1, rows, d), lambda b, i: (b, i, 0)),
                  pl.BlockSpec((1, d), lambda b, i: (0, 0)),
                  vec, vec,
                  pl.BlockSpec(w_tok.shape, lambda b, i: (0, 0)),
                  pl.BlockSpec(w_feat_t.shape, lambda b, i: (0, 0)),
                  tab, tab, tab_t, tab_t],
        out_specs=out_specs,
        out_shape=out_shapes,
        compiler_params=_params("arbitrary", "arbitrary"),
        name=name,
    )(x, g.reshape(1, d), shift, scale, w_tok, w_feat_t, cos_tab, sin_tab, cos_t, sin_t)


def _compress_kernel(r_ref, pa_ref, pb_ref, wa_ref, wb_ref, w2_ref, cos_ref, sin_ref, o_ref, *, is_key):
    r = r_ref[0]
    ncp = r.shape[0]
    a = _dot((r + pa_ref[...]).astype(BF16), wa_ref[...])
    b = _dot((r + pb_ref[...]).astype(BF16), wb_ref[...])
    hid = a + pltpu.roll(b, ncp - 1, 0)
    hid = (hid * _sigmoid(hid)).astype(BF16)
    if is_key:
        y = _dot(hid, w2_ref[...])
        first_half = (_iota(y.shape, 1) & (HEAD_DIM - 1)) < ROPE_HALF
        y = _rope_chunk(y, cos_ref[0], sin_ref[0], first_half).astype(o_ref.dtype)
        for g in range(NSA_KV_GROUPS):
            o_ref[0, g] = y[:, g * HEAD_DIM:(g + 1) * HEAD_DIM]
    else:
        o_ref[0] = lax.dot_general(w2_ref[...], hid, _NT, preferred_element_type=F32).astype(o_ref.dtype)


def _compress(kv, pos_emb, w1, w2, cos_c, sin_c, is_key, name):
    bsz, t, _ = kv.shape
    ncp = t // CMP_STRIDE
    kwid = CMP_STRIDE * NSA_KV_GROUPS * HEAD_DIM
    hid_w = NSA_KV_GROUPS * CMP_HIDDEN
    r = kv.reshape(bsz, ncp, kwid)
    per = CMP_BLOCK // CMP_STRIDE
    w1r = w1.reshape(per, CMP_STRIDE, HEAD_DIM, CMP_HIDDEN)
    zeros = jnp.zeros_like(w1r)
    grp0 = jnp.concatenate([w1r, zeros], axis=-1)
    grp1 = jnp.concatenate([zeros, w1r], axis=-1)
    wbig = jnp.stack([grp0, grp1], axis=2).reshape(per, kwid, hid_w).astype(BF16)
    posr = pos_emb.reshape(per, CMP_STRIDE, 1, HEAD_DIM)
    posbig = jnp.broadcast_to(posr, (per, CMP_STRIDE, NSA_KV_GROUPS, HEAD_DIM)).reshape(per, 1, kwid)
    z2 = jnp.zeros_like(w2)
    w2big = jnp.concatenate([jnp.concatenate([w2, z2], axis=1),
                             jnp.concatenate([z2, w2], axis=1)], axis=0).astype(BF16)
    const = lambda shape: pl.BlockSpec(shape, lambda b: (0,) * len(shape))
    tab = pl.BlockSpec((1, ncp, LANES), lambda b: (b, 0, 0))
    if is_key:
        w2_arg = w2big
        out_spec = pl.BlockSpec((1, NSA_KV_GROUPS, ncp, HEAD_DIM), lambda b: (b, 0, 0, 0))
        out_shape = jax.ShapeDtypeStruct((bsz, NSA_KV_GROUPS, ncp, HEAD_DIM), BF16)
    else:
        w2_arg = w2big.T
        out_spec = pl.BlockSpec((1, NSA_KV_GROUPS * HEAD_DIM, ncp), lambda b: (b, 0, 0))
        out_shape = jax.ShapeDtypeStruct((bsz, NSA_KV_GROUPS * HEAD_DIM, ncp), BF16)
    return pl.pallas_call(
        functools.partial(_compress_kernel, is_key=is_key),
        grid=(bsz,),
        in_specs=[pl.BlockSpec((1, ncp, kwid), lambda b: (b, 0, 0)),
                  const((1, kwid)), const((1, kwid)),
                  const((kwid, hid_w)), const((kwid, hid_w)),
                  const(w2_arg.shape), tab, tab],
        out_specs=out_spec,
        out_shape=out_shape,
        compiler_params=_params("arbitrary"),
        name=name,
    )(r, posbig[0], posbig[1], wbig[0], wbig[1], w2_arg, cos_c, sin_c)


def _softmax_cols(s, mask):
    sm = jnp.where(mask, s, NEG_INF)
    mx = jnp.max(sm, axis=0, keepdims=True)
    e = jnp.where(mask, jnp.exp(sm - mx), 0.0)
    l = jnp.sum(e, axis=0, keepdims=True)
    return e * (1.0 / jnp.where(l == 0.0, 1.0, l))


def _online_cols(s, v_t, m_old, l_old, acc_old, mask=None):
    m_new = jnp.maximum(m_old, jnp.max(s, axis=0, keepdims=True))
    alpha = jnp.exp(m_old - m_new)
    p = jnp.exp(s - m_new)
    if mask is not None:
        p = jnp.where(mask, p, 0.0)
    l_new = alpha * l_old + jnp.sum(p, axis=0, keepdims=True)
    acc_new = alpha * acc_old + _dot(v_t, p.astype(BF16))
    return m_new, l_new, acc_new


def _online_init(n, dv, qn):
    return tuple((jnp.full((1, qn), NEG_INF, F32), jnp.zeros((1, qn), F32), jnp.zeros((dv, qn), F32))
                 for _ in range(n))


def _split_bf16(v, terms):
    out, rest = [], v
    for i in range(terms):
        part = rest.astype(BF16)
        out.append(part)
        if i + 1 < terms:
            rest = rest - part.astype(F32)
    return out


def _nsa_kernel(q_ref, kc_ref, vc_ref, ks_ref, vs_ref, kw_ref, vw_ref, g_ref, ovt_ref, o_ref,
                score_scr, bias_scr, *, n_sel):
    qn = q_ref.shape[2]
    ncp = kc_ref.shape[2]
    ns = ovt_ref.shape[0]
    kt_n = KEY_TILE
    tiles_per_q = qn // kt_n
    qi = pl.program_id(2)
    q0 = qi * qn
    reps = range(NSA_REP)
    q_head = lambda r: q_ref[0, r * HEAD_DIM:(r + 1) * HEAD_DIM, :]

    kc = kc_ref[0, 0]
    vc_t = vc_ref[0]
    mask_c = (_iota((ncp, qn), 0) * CMP_STRIDE + (CMP_BLOCK - 1)) <= (q0 + _iota((ncp, qn), 1))
    o_cmp, p_sum = [], None
    for r in reps:
        p = _softmax_cols(_dot(kc, q_head(r)), mask_c)
        o_cmp.append(_dot(vc_t, p.astype(BF16)))
        p_sum = p if p_sum is None else p_sum + p
    ovt = ovt_ref[...]
    imp_t = None
    for term in _split_bf16(p_sum, 3):
        part = _dot(ovt, term)
        imp_t = part if imp_t is None else imp_t + part

    blk = _iota((ns, qn), 0)
    cur = (q0 + _iota((ns, qn), 1)) >> SLC_SHIFT
    forced = (blk == 0) | (blk == cur) | (blk == cur - 1)
    score = jnp.where(forced, FORCE, jnp.where(blk <= cur, imp_t, -FORCE))
    score_scr[...] = score

    def rank_body(m, cnt):
        other = score_scr[pl.ds(m, 1), :]
        tie = jnp.where(blk > m, 1.0, 0.0)
        return cnt + jnp.where(other > score, 1.0, jnp.where(other == score, tie, 0.0))

    n_rank = jnp.minimum(ns, (q0 + qn) >> SLC_SHIFT)
    cnt = lax.fori_loop(0, n_rank, rank_body, jnp.zeros((ns, qn), F32))
    bias_scr[...] = jnp.where(cnt < float(n_sel), 0.0, NEG_INF)

    blocks_per_tile = kt_n // SLC_BLOCK

    def sel_body(kt, carry):
        k0 = pl.multiple_of(kt * kt_n, kt_n)
        k = ks_ref[0, 0, pl.ds(k0, kt_n), :]
        v_t = vs_ref[0, kt]
        rows = [jnp.broadcast_to(bias_scr[pl.ds(kt * blocks_per_tile + i, 1), :], (SLC_BLOCK, qn))
                for i in range(blocks_per_tile)]
        causal = (k0 + _iota((kt_n, qn), 0)) <= (q0 + _iota((kt_n, qn), 1))
        bias = jnp.where(causal, jnp.concatenate(rows, axis=0), NEG_INF)
        return tuple(_online_cols(_dot(k, q_head(r)) + bias, v_t, *carry[r]) for r in reps)

    fin = lax.fori_loop(0, (qi + 1) * tiles_per_q, sel_body, _online_init(NSA_REP, HEAD_DIM, qn))
    o_sel = [acc * (1.0 / l) for (_, l, acc) in fin]

    first_tile = jnp.maximum(qi * tiles_per_q - WINDOW // kt_n, 0)
    carry = _online_init(NSA_REP, HEAD_DIM, qn)
    for i in range(WINDOW // kt_n + tiles_per_q):
        kt = first_tile + i
        k0 = pl.multiple_of(kt * kt_n, kt_n)
        k = kw_ref[0, 0, pl.ds(k0, kt_n), :]
        v_t = vw_ref[0, kt]
        kp = k0 + _iota((kt_n, qn), 0)
        tq = q0 + _iota((kt_n, qn), 1)
        mask = (kp <= tq) & (kp > tq - WINDOW)
        carry = tuple(_online_cols(jnp.where(mask, _dot(k, q_head(r)), NEG_INF), v_t, *carry[r], mask=mask)
                      for r in reps)
    o_win = [acc * (1.0 / l) for (_, l, acc) in carry]

    gates = g_ref[0]
    for r in reps:
        o = (gates[3 * r:3 * r + 1] * o_cmp[r] + gates[3 * r + 1:3 * r + 2] * o_sel[r]
             + gates[3 * r + 2:3 * r + 3] * o_win[r])
        o_ref[0, r * HEAD_DIM:(r + 1) * HEAD_DIM, :] = o.astype(o_ref.dtype)


def _nsa_attention(q_t, k_cmp, v_cmp_t, k_slc, v_slc_t, k_win, v_win_t, gates_t):
    bsz, _, t = q_t.shape
    ncp = k_cmp.shape[2]
    ns = t // SLC_BLOCK
    qn = min(Q_TILE, t)
    grp_w = NSA_REP * HEAD_DIM
    cmp_start = jnp.arange(ncp) * CMP_STRIDE
    slc_start = jnp.arange(ns) * SLC_BLOCK
    real = jnp.arange(ncp) < (t - CMP_BLOCK) // CMP_STRIDE + 1
    overlap_t = ((cmp_start[None, :] < slc_start[:, None] + SLC_BLOCK)
                 & (cmp_start[None, :] + CMP_BLOCK > slc_start[:, None]) & real[None, :]).astype(BF16)
    k_spec = lambda n: pl.BlockSpec((1, 1, n, HEAD_DIM), lambda b, g, i: (b, g, 0, 0))
    v_spec = pl.BlockSpec((1, t // KEY_TILE, HEAD_DIM, KEY_TILE), lambda b, g, i: (b, 0, g, 0))
    return pl.pallas_call(
        functools.partial(_nsa_kernel, n_sel=min(N_SELECT, ns)),
        grid=(bsz, NSA_KV_GROUPS, t // qn),
        in_specs=[pl.BlockSpec((1, grp_w, qn), lambda b, g, i: (b, g, i)),
                  k_spec(ncp),
                  pl.BlockSpec((1, HEAD_DIM, ncp), lambda b, g, i: (b, g, 0)),
                  k_spec(t), v_spec, k_spec(t), v_spec,
                  pl.BlockSpec((1, GATE_ROWS, qn), lambda b, g, i: (b, g, i)),
                  pl.BlockSpec((ns, ncp), lambda b, g, i: (0, 0))],
        out_specs=pl.BlockSpec((1, grp_w, qn), lambda b, g, i: (b, g, i)),
        out_shape=jax.ShapeDtypeStruct((bsz, NSA_HEADS * HEAD_DIM, t), BF16),
        scratch_shapes=[pltpu.VMEM((ns, qn), F32), pltpu.VMEM((ns, qn), F32)],
        compiler_params=_params("arbitrary", "arbitrary", "arbitrary"),
        name="nsa_attention",
    )(q_t, k_cmp, v_cmp_t, k_slc, v_slc_t, k_win, v_win_t, gates_t, overlap_t)


SB_HEADS_PER_STEP = 2


def _sb_kernel(q_ref, k_ref, v_ref, o_ref):
    qn = q_ref.shape[2]
    kt_n = KEY_TILE
    tiles_per_q = qn // kt_n
    qi = pl.program_id(2)
    q0 = qi * qn
    later_t = jnp.where(_iota((kt_n, kt_n), 1) > _iota((kt_n, kt_n), 0), 1.0, 0.0).astype(BF16)
    for h in range(SB_HEADS_PER_STEP):
        rows = slice(h * HEAD_DIM, (h + 1) * HEAD_DIM)

        def tile(kt, tail, acc, masked, h=h, rows=rows):
            k0 = pl.multiple_of(kt * kt_n, kt_n)
            z = _dot(k_ref[0, h, pl.ds(k0, kt_n), :], q_ref[0, rows, :])
            soft = jnp.log1p(jnp.exp(-jnp.abs(z)))
            log_keep = -(jnp.maximum(z, 0.0) + soft)
            log_beta = z + log_keep
            if masked:
                mask = (k0 + _iota((kt_n, qn), 0)) < (q0 + _iota((kt_n, qn), 1))
                log_keep = jnp.where(mask, log_keep, 0.0)
            within = None
            for term in _split_bf16(log_keep, 2):
                part = _dot(later_t, term)
                within = part if within is None else within + part
            a = jnp.exp(log_beta + within + tail)
            if masked:
                a = jnp.where(mask, a, 0.0)
            acc = acc + _dot(v_ref[0, kt, rows, :], a.astype(BF16))
            tail = tail + jnp.sum(log_keep, axis=0, keepdims=True)
            return tail, acc

        carry = (jnp.zeros((1, qn), F32), jnp.zeros((HEAD_DIM, qn), F32))
        for i in reversed(range(tiles_per_q)):
            carry = tile(qi * tiles_per_q + i, *carry, True)
        n_full = qi * tiles_per_q
        _, acc = lax.fori_loop(0, n_full, lambda i, c: tile(n_full - 1 - i, *c, False), carry)
        o_ref[0, rows, :] = acc.astype(o_ref.dtype)


def _sb_attention(q_t, k, v_t):
    bsz, nh, t, _ = k.shape
    qn = min(Q_TILE, t)
    hps = SB_HEADS_PER_STEP
    w = hps * HEAD_DIM
    return pl.pallas_call(
        _sb_kernel,
        grid=(bsz, nh // hps, t // qn),
        in_specs=[pl.BlockSpec((1, w, qn), lambda b, h, i: (b, h, i)),
                  pl.BlockSpec((1, hps, t, HEAD_DIM), lambda b, h, i: (b, h, 0, 0)),
                  pl.BlockSpec((1, t // KEY_TILE, w, KEY_TILE), lambda b, h, i: (b, 0, h, 0))],
        out_specs=pl.BlockSpec((1, w, qn), lambda b, h, i: (b, h, i)),
        out_shape=jax.ShapeDtypeStruct((bsz, nh * HEAD_DIM, t), BF16),
        compiler_params=_params("arbitrary", "arbitrary", "arbitrary"),
        name="stickbreak_attention",
    )(q_t, k, v_t)


def _diff_kernel(q_ref, k_ref, v_ref, lq1_ref, lk1_ref, lq2_ref, lk2_ref, sub_ref, o_ref, *, lambda_init):
    qn = q_ref.shape[2]
    vd = v_ref.shape[2]
    kt_n = KEY_TILE
    tiles_per_q = qn // kt_n
    qi = pl.program_id(2)
    q0 = qi * qn
    lam = (jnp.exp(jnp.sum(lq1_ref[...] * lk1_ref[...], axis=1, keepdims=True))
           - jnp.exp(jnp.sum(lq2_ref[...] * lk2_ref[...], axis=1, keepdims=True)) + lambda_init)

    def tile(kt, carry, masked):
        k0 = pl.multiple_of(kt * kt_n, kt_n)
        v_t = v_ref[0, kt]
        out = []
        for j in range(2):
            s = _dot(k_ref[0, j, pl.ds(k0, kt_n), :], q_ref[0, j * HEAD_DIM:(j + 1) * HEAD_DIM, :])
            if masked:
                s = jnp.where((k0 + _iota((kt_n, qn), 0)) <= (q0 + _iota((kt_n, qn), 1)), s, NEG_INF)
            out.append(_online_cols(s, v_t, *carry[j]))
        return tuple(out)

    n_full = qi * tiles_per_q
    carry = lax.fori_loop(0, n_full, lambda kt, c: tile(kt, c, False), _online_init(2, vd, qn))
    for i in range(tiles_per_q):
        carry = tile(n_full + i, carry, True)
    (_, l1, a1), (_, l2, a2) = carry
    o = a1 * (1.0 / l1) - lam * (a2 * (1.0 / l2))
    ms = jnp.mean(o * o, axis=0, keepdims=True)
    o = o * lax.rsqrt(ms + RMS_EPS) * sub_ref[...] * (1.0 - lambda_init)
    o_ref[0] = o.astype(o_ref.dtype)


def _diff_attention(q_t, k, v_t, lq1, lk1, lq2, lk2, subln, lambda_init):
    bsz, nh2, t, _ = k.shape
    nh = nh2 // 2
    vd = 2 * HEAD_DIM
    qn = min(Q_TILE, t)
    row = lambda a: a.reshape(1, -1)
    small = lambda n: pl.BlockSpec((1, n), lambda b, h, i: (0, 0))
    return pl.pallas_call(
        functools.partial(_diff_kernel, lambda_init=lambda_init),
        grid=(bsz, nh, t // qn),
        in_specs=[pl.BlockSpec((1, vd, qn), lambda b, h, i: (b, h, i)),
                  pl.BlockSpec((1, 2, t, HEAD_DIM), lambda b, h, i: (b, h, 0, 0)),
                  pl.BlockSpec((1, t // KEY_TILE, vd, KEY_TILE), lambda b, h, i: (b, 0, h, 0)),
                  small(HEAD_DIM), small(HEAD_DIM), small(HEAD_DIM), small(HEAD_DIM),
                  pl.BlockSpec((vd, 1), lambda b, h, i: (0, 0))],
        out_specs=pl.BlockSpec((1, vd, qn), lambda b, h, i: (b, h, i)),
        out_shape=jax.ShapeDtypeStruct((bsz, nh * vd, t), BF16),
        compiler_params=_params("arbitrary", "arbitrary", "arbitrary"),
        name="diff_attention",
    )(q_t, k, v_t, row(lq1), row(lk1), row(lq2), row(lk2), subln.reshape(vd, 1))


def _outproj_kernel(*refs, n_in):
    a_refs, w_refs = refs[:n_in], refs[n_in:2 * n_in]
    x_ref, gate_ref, o_ref = refs[2 * n_in:]
    y = None
    for a_ref, w_ref in zip(a_refs, w_refs):
        part = lax.dot_general(a_ref[0], w_ref[...], _TN, preferred_element_type=F32)
        y = part if y is None else y + part
    o_ref[0] = x_ref[0] + gate_ref[0] * y


def _out_project(acts_t, weights, x, gate, name):
    bsz, t, d = x.shape
    rows = min(t, PROJ_ROWS)
    tile = pl.BlockSpec((1, rows, d), lambda b, i: (b, i, 0))
    return pl.pallas_call(
        functools.partial(_outproj_kernel, n_in=len(acts_t)),
        grid=(bsz, t // rows),
        in_specs=([pl.BlockSpec((1, a.shape[1], rows), lambda b, i: (b, 0, i)) for a in acts_t]
                  + [pl.BlockSpec(w.shape, lambda b, i: (0, 0)) for w in weights]
                  + [tile, pl.BlockSpec((1, 1, d), lambda b, i: (b, 0, 0))]),
        out_specs=tile,
        out_shape=jax.ShapeDtypeStruct((bsz, t, d), F32),
        compiler_params=_params("arbitrary", "arbitrary"),
        name=name,
    )(*acts_t, *weights, x, gate)


def _ffn_kernel(x_ref, halo_ref, g_ref, sh_ref, sc_ref, gate_ref, wg_ref, wu_ref, cw_ref, cb_ref, wd_ref,
                nf_ref, o_ref, h_scr, halo_scr, acc_scr, *, final_norm):
    ti = pl.program_id(1)
    j = pl.program_id(2)

    @pl.when(j == 0)
    def _():
        h_scr[...] = _norm_modulate(x_ref[0], g_ref[...], sh_ref[0], sc_ref[0]).astype(BF16)
        halo_scr[...] = _norm_modulate(halo_ref[0], g_ref[...], sh_ref[0], sc_ref[0]).astype(BF16)
        acc_scr[...] = jnp.zeros_like(acc_scr)

    h = h_scr[...]
    wg = wg_ref[...]
    g = _dot(h, wg)
    g_prev = _dot(halo_scr[...], wg) * jnp.where(ti > 0, 1.0, 0.0)
    row = _iota(g.shape, 0)
    last = g_prev[HALO_ROWS - 1:HALO_ROWS]
    last2 = g_prev[HALO_ROWS - 2:HALO_ROWS - 1]
    g1 = jnp.where(row == 0, last, pltpu.roll(g, 1, 0))
    g2 = jnp.where(row == 0, last2, jnp.where(row == 1, last, pltpu.roll(g, 2, 0)))
    cw = cw_ref[...]
    conv = cw[0:1] * g2 + cw[1:2] * g1 + cw[2:3] * g + cb_ref[...]
    a = conv * _sigmoid(conv) * _dot(h, wu_ref[...])
    acc_scr[...] += _dot(a.astype(BF16), wd_ref[...])

    @pl.when(j == pl.num_programs(2) - 1)
    def _():
        y = x_ref[0] + gate_ref[0] * acc_scr[...]
        if final_norm:
            ms = jnp.mean(y * y, axis=-1, keepdims=True)
            y = y * lax.rsqrt(ms + RMS_EPS) * nf_ref[...]
        o_ref[0] = y


def _conv_ffn(x, g, shift, scale, gate, w_gate, w_up, conv_w, conv_b, w_down, norm_f, final_norm, name):
    bsz, t, d = x.shape
    f = w_gate.shape[1]
    rows = min(t, FFN_ROWS)
    cols = FFN_COLS
    halo_blocks = rows // HALO_ROWS
    vec = pl.BlockSpec((1, 1, d), lambda b, i, j: (b, 0, 0))
    drow = pl.BlockSpec((1, d), lambda b, i, j: (0, 0))
    return pl.pallas_call(
        functools.partial(_ffn_kernel, final_norm=final_norm),
        grid=(bsz, t // rows, f // cols),
        in_specs=[pl.BlockSpec((1, rows, d), lambda b, i, j: (b, i, 0)),
                  pl.BlockSpec((1, HALO_ROWS, d), lambda b, i, j: (b, jnp.maximum(i * halo_blocks - 1, 0), 0)),
                  drow, vec, vec, vec,
                  pl.BlockSpec((d, cols), lambda b, i, j: (0, j)),
                  pl.BlockSpec((d, cols), lambda b, i, j: (0, j)),
                  pl.BlockSpec((CONV_WIDTH, cols), lambda b, i, j: (0, j)),
                  pl.BlockSpec((1, cols), lambda b, i, j: (0, j)),
                  pl.BlockSpec((cols, d), lambda b, i, j: (j, 0)),
                  drow],
        out_specs=pl.BlockSpec((1, rows, d), lambda b, i, j: (b, i, 0)),
        out_shape=jax.ShapeDtypeStruct((bsz, t, d), F32),
        scratch_shapes=[pltpu.VMEM((rows, d), BF16), pltpu.VMEM((HALO_ROWS, d), BF16),
                        pltpu.VMEM((rows, d), F32)],
        compiler_params=_params("arbitrary", "arbitrary", "arbitrary"),
        name=name,
    )(x, x, g.reshape(1, d), shift, scale, gate, w_gate.astype(BF16), w_up.astype(BF16), conv_w,
      conv_b.reshape(1, f), w_down.astype(BF16), norm_f.reshape(1, d))


def _layout(widths):
    starts, start = [], 0
    for w in widths:
        starts.append(start)
        start += w
    return starts


def _hybrid_layer(x, mods, tables, norm_g, w_in, pos_k, pos_v, ck_w1, ck_w2, cv_w1, cv_w2, w_out):
    shift, scale, gate = mods
    bsz, t, d = x.shape
    qw, kvw, sbw = NSA_HEADS * HEAD_DIM, NSA_KV_GROUPS * HEAD_DIM, SB_HEADS * HEAD_DIM
    col = _layout((qw,) + (kvw,) * 6 + (3 * NSA_HEADS,) + (sbw,) * 3)
    cols = lambda i, w: w_in[:, col[i]:col[i] + w]
    q_n, kc, vc, ks, vs, kw, vw, gl, q_s, k_s, v_s = (cols(i, w) for i, w in enumerate(
        (qw,) + (kvw,) * 6 + (3 * NSA_HEADS,) + (sbw,) * 3))
    inv_sqrt = HEAD_DIM ** -0.5

    w_tok = jnp.concatenate([ks, kw, kc, vc, k_s], axis=1).astype(BF16)
    ts = _layout((kvw, kvw, kvw, kvw, sbw))
    tok_segs = [TokSeg(ts[0], kvw, True, HEAD_DIM, BF16),
                TokSeg(ts[1], kvw, True, HEAD_DIM, BF16),
                TokSeg(ts[2], kvw, False, 0, F32),
                TokSeg(ts[3], kvw, False, 0, F32),
                TokSeg(ts[4], sbw, False, HEAD_DIM, BF16)]
    per_group = 3 * NSA_REP
    gate_pad = jnp.zeros((d, GATE_ROWS - per_group), w_in.dtype)
    gl_pad = jnp.concatenate([gl[:, :per_group], gate_pad, gl[:, per_group:], gate_pad], axis=1)
    w_feat_t = jnp.concatenate([q_n, q_s, vs, vw, v_s, gl_pad], axis=1).T.astype(BF16)
    fs = _layout((qw, sbw, kvw, kvw, sbw, NSA_KV_GROUPS * GATE_ROWS))
    feat_segs = [FeatSeg(fs[0], qw, True, inv_sqrt, False, False, BF16),
                 FeatSeg(fs[1], sbw, False, inv_sqrt, False, False, BF16),
                 FeatSeg(fs[2], kvw, False, 1.0, False, True, BF16),
                 FeatSeg(fs[3], kvw, False, 1.0, False, True, BF16),
                 FeatSeg(fs[4], sbw, False, 1.0, False, True, BF16),
                 FeatSeg(fs[5], NSA_KV_GROUPS * GATE_ROWS, False, 1.0, True, False, F32)]
    (k_slc, k_win, kc_a, vc_a, k_sb, q_n_t, q_s_t, v_slc_t, v_win_t, v_sb_t, gates_t) = _project(
        x, norm_g, shift, scale, w_tok, w_feat_t, tables, tok_segs, feat_segs, "hybrid_in_proj")

    ncp = t // CMP_STRIDE
    end_rows = jnp.minimum(jnp.arange(ncp) * CMP_STRIDE + CMP_BLOCK - 1, t - 1)
    cos_c, sin_c = tables[0][:, end_rows], tables[1][:, end_rows]
    k_cmp = _compress(kc_a, pos_k, ck_w1, ck_w2, cos_c, sin_c, True, "compress_k")
    v_cmp_t = _compress(vc_a, pos_v, cv_w1, cv_w2, cos_c, sin_c, False, "compress_v")
    o_nsa_t = _nsa_attention(q_n_t, k_cmp, v_cmp_t, k_slc, v_slc_t, k_win, v_win_t, gates_t)
    o_sb_t = _sb_attention(q_s_t, k_sb, v_sb_t)
    w_out_b = w_out.astype(BF16)
    return _out_project([o_nsa_t, o_sb_t], [w_out_b[:qw], w_out_b[qw:]], x, gate, "hybrid_out_proj")


def _diff_layer(x, mods, tables, norm_g, w_qkv, lq1, lk1, lq2, lk2, subln, w_out, layer_idx):
    shift, scale, gate = mods
    dw = 2 * DIFF_HEADS * HEAD_DIM
    inv_sqrt = HEAD_DIM ** -0.5
    w_tok = w_qkv[:, dw:2 * dw].astype(BF16)
    w_feat_t = jnp.concatenate([w_qkv[:, :dw], w_qkv[:, 2 * dw:]], axis=1).T.astype(BF16)
    tok_segs = [TokSeg(0, dw, True, HEAD_DIM, BF16)]
    feat_segs = [FeatSeg(0, dw, True, inv_sqrt, False, False, BF16),
                 FeatSeg(dw, dw, False, 1.0, False, True, BF16)]
    k, q_t, v_t = _project(x, norm_g, shift, scale, w_tok, w_feat_t, tables, tok_segs, feat_segs, "diff_in_proj")
    lambda_init = 0.8 - 0.6 * math.exp(-0.3 * layer_idx)
    o_t = _diff_attention(q_t, k, v_t, lq1, lk1, lq2, lk2, subln, lambda_init)
    return _out_project([o_t], [w_out.astype(BF16)], x, gate, "diff_out_proj")


def kernel(x, c, positions, mod_w, mod_b, norm_mix, norm_ffn, ffn_w_gate, ffn_w_up, ffn_conv_w, ffn_conv_b, ffn_w_down, hyb_w_in, nsa_pos_k, nsa_pos_v, nsa_ck_w1, nsa_ck_w2, nsa_cv_w1, nsa_cv_w2, hyb_w_out, diff_w_qkv, diff_lq1, diff_lk1, diff_lq2, diff_lk2, diff_subln, diff_w_out, norm_f):
    bsz, t, d = x.shape
    depth = mod_w.shape[0]
    mod = _adaln_mod(c, mod_w, mod_b)
    tables = _rope_tables(positions)
    for i in range(depth):
        sh_m, sc_m, g_m, sh_f, sc_f, g_f = (mod[i, :, k * d:(k + 1) * d].reshape(bsz, 1, d) for k in range(6))
        j = i // 2
        if i % 2 == 0:
            x = _hybrid_layer(x, (sh_m, sc_m, g_m), tables, norm_mix[i], hyb_w_in[j],
                              nsa_pos_k[j], nsa_pos_v[j], nsa_ck_w1[j], nsa_ck_w2[j], nsa_cv_w1[j],
                              nsa_cv_w2[j], hyb_w_out[j])
        else:
            x = _diff_layer(x, (sh_m, sc_m, g_m), tables, norm_mix[i], diff_w_qkv[j], diff_lq1[j],
                            diff_lk1[j], diff_lq2[j], diff_lk2[j], diff_subln[j], diff_w_out[j], i)
        x = _conv_ffn(x, norm_ffn[i], sh_f, sc_f, g_f, ffn_w_gate[i], ffn_w_up[i], ffn_conv_w[i],
                      ffn_conv_b[i], ffn_w_down[i], norm_f, i == depth - 1, "conv_ffn_%d" % i)
    return x
```

```python
import functools
import math
from typing import NamedTuple

import jax
import jax.numpy as jnp
from jax import lax
from jax.experimental import pallas as pl
from jax.experimental.pallas import tpu as pltpu

F32 = jnp.float32
BF16 = jnp.bfloat16
I32 = jnp.int32

HEAD_DIM = 64
ROPE_DIM = HEAD_DIM // 4
ROPE_HALF = ROPE_DIM // 2
ROPE_THETA = 500000.0
NSA_HEADS = 8
NSA_KV_GROUPS = 2
NSA_REP = NSA_HEADS // NSA_KV_GROUPS
CMP_BLOCK = 32
CMP_STRIDE = 16
CMP_HIDDEN = 4 * HEAD_DIM
SLC_BLOCK = 64
SLC_SHIFT = 6
N_SELECT = 16
WINDOW = 512
SB_HEADS = 8
DIFF_HEADS = 8
CONV_WIDTH = 3
RMS_EPS = 1e-6
NEG_INF = -1e30
FORCE = 1e9
LOG2E = math.log2(math.e)

LANES = 128
Q_TILE = 512
KEY_TILE = 512
GATE_ROWS = 16
PROJ_ROWS = 512
FFN_ROWS = 512
FFN_COLS = 256
HALO_ROWS = 16
VMEM_LIMIT = 56 * 1024 * 1024

_NT = (((1,), (1,)), ((), ()))
_TN = (((0,), (0,)), ((), ()))


def _params(*sem):
    return pltpu.CompilerParams(dimension_semantics=sem, vmem_limit_bytes=VMEM_LIMIT)


def _sigmoid(v):
    return 1.0 / (1.0 + jnp.exp(-v))


def _iota(shape, axis):
    return lax.broadcasted_iota(I32, shape, axis)


def _dot(a, b):
    return jnp.dot(a, b, preferred_element_type=F32)


def _mod_kernel(c_ref, w_ref, b_ref, o_ref):
    c = c_ref[...]
    cond = c * _sigmoid(c)
    o_ref[0] = jnp.dot(cond, w_ref[0], preferred_element_type=F32,
                       precision=lax.Precision.HIGHEST) + b_ref[0]


def _adaln_mod(c, mod_w, mod_b):
    depth, d, n = mod_w.shape
    bsz = c.shape[0]
    tn = n // 4
    return pl.pallas_call(
        _mod_kernel,
        grid=(depth, n // tn),
        in_specs=[pl.BlockSpec((bsz, d), lambda i, j: (0, 0)),
                  pl.BlockSpec((1, d, tn), lambda i, j: (i, 0, j)),
                  pl.BlockSpec((1, 1, tn), lambda i, j: (i, 0, j))],
        out_specs=pl.BlockSpec((1, bsz, tn), lambda i, j: (i, 0, j)),
        out_shape=jax.ShapeDtypeStruct((depth, bsz, n), F32),
        compiler_params=_params("arbitrary", "arbitrary"),
        name="adaln_mod",
    )(c, mod_w, mod_b.reshape(depth, 1, n))


def _rope_kernel(pos_col_ref, pos_row_ref, inv_row_ref, sgn_row_ref, inv_col_ref,
                 cos_ref, sin_ref, cos_t_ref, sin_t_ref):
    ang = pos_col_ref[0].astype(F32) * inv_row_ref[...]
    cos_ref[0] = jnp.cos(ang)
    sin_ref[0] = jnp.sin(ang) * sgn_row_ref[...]
    ang_t = inv_col_ref[...] * pos_row_ref[0].astype(F32)
    cos_t_ref[0] = jnp.cos(ang_t)
    sin_t_ref[0] = jnp.sin(ang_t)


def _rope_tables(positions):
    bsz, t = positions.shape
    inv = ROPE_THETA ** (-jnp.arange(0, ROPE_DIM, 2, dtype=F32) / ROPE_DIM)
    per_head_inv = jnp.concatenate([inv, inv, jnp.zeros((HEAD_DIM - ROPE_DIM,), F32)])
    per_head_sgn = jnp.concatenate([-jnp.ones((ROPE_HALF,), F32), jnp.ones((ROPE_HALF,), F32),
                                    jnp.zeros((HEAD_DIM - ROPE_DIM,), F32)])
    inv_row = jnp.tile(per_head_inv, LANES // HEAD_DIM)[None, :]
    sgn_row = jnp.tile(per_head_sgn, LANES // HEAD_DIM)[None, :]
    rows = min(t, 1024)
    tab = jax.ShapeDtypeStruct((bsz, t, LANES), F32)
    tab_t = jax.ShapeDtypeStruct((bsz, ROPE_HALF, t), F32)
    row_spec = pl.BlockSpec((1, LANES), lambda b, i: (0, 0))
    return pl.pallas_call(
        _rope_kernel,
        grid=(bsz, t // rows),
        in_specs=[pl.BlockSpec((1, rows, 1), lambda b, i: (b, i, 0)),
                  pl.BlockSpec((1, 1, rows), lambda b, i: (b, 0, i)),
                  row_spec, row_spec,
                  pl.BlockSpec((ROPE_HALF, 1), lambda b, i: (0, 0))],
        out_specs=[pl.BlockSpec((1, rows, LANES), lambda b, i: (b, i, 0))] * 2
        + [pl.BlockSpec((1, ROPE_HALF, rows), lambda b, i: (b, 0, i))] * 2,
        out_shape=[tab, tab, tab_t, tab_t],
        compiler_params=_params("arbitrary", "arbitrary"),
        name="rope_tables",
    )(positions.reshape(bsz, t, 1), positions.reshape(bsz, 1, t), inv_row, sgn_row, inv[:, None])


def _rope_chunk(y, cosv, sinv, first_half):
    ahead = pltpu.roll(y, LANES - ROPE_HALF, 1)
    behind = pltpu.roll(y, ROPE_HALF, 1)
    return y * cosv + jnp.where(first_half, ahead, behind) * sinv


def _rope_rows(y_t, cos_t, sin_t):
    heads = []
    for h in range(y_t.shape[0] // HEAD_DIM):
        blk = y_t[h * HEAD_DIM:(h + 1) * HEAD_DIM]
        x1, x2 = blk[:ROPE_HALF], blk[ROPE_HALF:ROPE_DIM]
        heads += [x1 * cos_t - x2 * sin_t, x2 * cos_t + x1 * sin_t, blk[ROPE_DIM:]]
    return jnp.concatenate(heads, axis=0)


class TokSeg(NamedTuple):
    start: int
    width: int
    rope: bool
    head_width: int
    dtype: object


class FeatSeg(NamedTuple):
    start: int
    rows: int
    rope: bool
    scale: float
    sigmoid: bool
    key_tiled: bool
    dtype: object


def _norm_modulate(x, g, shift, scale):
    ms = jnp.mean(x * x, axis=-1, keepdims=True)
    y = x * lax.rsqrt(ms + RMS_EPS) * g
    return y * (1.0 + scale) + shift


def _proj_kernel(x_ref, g_ref, sh_ref, sc_ref, w_ref, wt_ref, cos_ref, sin_ref, cos_t_ref, sin_t_ref,
                 *out_refs, tok_segs, feat_segs):
    hb = _norm_modulate(x_ref[0], g_ref[...], sh_ref[0], sc_ref[0]).astype(BF16)
    cosv, sinv = cos_ref[0], sin_ref[0]
    first_half = (_iota(cosv.shape, 1) & (HEAD_DIM - 1)) < ROPE_HALF
    tok_refs, feat_refs = out_refs[:len(tok_segs)], out_refs[len(tok_segs):]
    for seg, o_ref in zip(tok_segs, tok_refs):
        y = _dot(hb, w_ref[:, seg.start:seg.start + seg.width])
        for ch in range(seg.width // LANES):
            yc = y[:, ch * LANES:(ch + 1) * LANES]
            if seg.rope:
                yc = _rope_chunk(yc, cosv, sinv, first_half)
            yc = yc.astype(seg.dtype)
            if seg.head_width == 0:
                o_ref[0, :, ch * LANES:(ch + 1) * LANES] = yc
            else:
                o_ref[0, 2 * ch] = yc[:, :HEAD_DIM]
                o_ref[0, 2 * ch + 1] = yc[:, HEAD_DIM:]
    for seg, o_ref in zip(feat_segs, feat_refs):
        y_t = lax.dot_general(wt_ref[seg.start:seg.start + seg.rows, :], hb, _NT,
                              preferred_element_type=F32)
        if seg.rope:
            y_t = _rope_rows(y_t, cos_t_ref[0], sin_t_ref[0])
        if seg.scale != 1.0:
            y_t = y_t * seg.scale
        if seg.sigmoid:
            y_t = _sigmoid(y_t)
        y_t = y_t.astype(seg.dtype)
        if seg.key_tiled:
            for ch in range(y_t.shape[1] // KEY_TILE):
                o_ref[0, ch] = y_t[:, ch * KEY_TILE:(ch + 1) * KEY_TILE]
        else:
            o_ref[0] = y_t


def _project(x, g, shift, scale, w_tok, w_feat_t, tables, tok_segs, feat_segs, name):
    bsz, t, d = x.shape
    rows = min(t, PROJ_ROWS)
    cos_tab, sin_tab, cos_t, sin_t = tables
    out_shapes, out_specs = [], []
    for seg in tok_segs:
        if seg.head_width == 0:
            out_shapes.append(jax.ShapeDtypeStruct((bsz, t, seg.width), seg.dtype))
            out_specs.append(pl.BlockSpec((1, rows, seg.width), lambda b, i: (b, i, 0)))
        else:
            nh = seg.width // seg.head_width
            out_shapes.append(jax.ShapeDtypeStruct((bsz, nh, t, seg.head_width), seg.dtype))
            out_specs.append(pl.BlockSpec((1, nh, rows, seg.head_width), lambda b, i: (b, 0, i, 0)))
    for seg in feat_segs:
        if seg.key_tiled:
            out_shapes.append(jax.ShapeDtypeStruct((bsz, t // KEY_TILE, seg.rows, KEY_TILE), seg.dtype))
            out_specs.append(pl.BlockSpec((1, rows // KEY_TILE, seg.rows, KEY_TILE), lambda b, i: (b, i, 0, 0)))
        else:
            out_shapes.append(jax.ShapeDtypeStruct((bsz, seg.rows, t), seg.dtype))
            out_specs.append(pl.BlockSpec((1, seg.rows, rows), lambda b, i: (b, 0, i)))
    vec = pl.BlockSpec((1, 1, d), lambda b, i: (b, 0, 0))
    tab = pl.BlockSpec((1, rows, LANES), lambda b, i: (b, i, 0))
    tab_t = pl.BlockSpec((1, ROPE_HALF, rows), lambda b, i: (b, 0, i))
    return pl.pallas_call(
        functools.partial(_proj_kernel, tok_segs=tuple(tok_segs), feat_segs=tuple(feat_segs)),
        grid=(bsz, t // rows),
        in_specs=[pl.BlockSpec((1, rows, d), lambda b, i: (b, i, 0)),
                  pl.BlockSpec((1, d), lambda b, i: (0, 0)),
                  vec, vec,
                  pl.BlockSpec(w_tok.shape, lambda b, i: (0, 0)),
                  pl.BlockSpec(w_feat_t.shape, lambda b, i: (0, 0)),
                  tab, tab, tab_t, tab_t],
        out_specs=out_specs,
        out_shape=out_shapes,
        compiler_params=_params("arbitrary", "arbitrary"),
        name=name,
    )(x, g.reshape(1, d), shift, scale, w_tok, w_feat_t, cos_tab, sin_tab, cos_t, sin_t)


def _compress_kernel(r_ref, pa_ref, pb_ref, wa_ref, wb_ref, w2_ref, cos_ref, sin_ref, o_ref, *, is_key):
    r = r_ref[0]
    ncp = r.shape[0]
    a = _dot((r + pa_ref[...]).astype(BF16), wa_ref[...])
    b = _dot((r + pb_ref[...]).astype(BF16), wb_ref[...])
    hid = a + pltpu.roll(b, ncp - 1, 0)
    hid = (hid * _sigmoid(hid)).astype(BF16)
    if is_key:
        y = _dot(hid, w2_ref[...])
        first_half = (_iota(y.shape, 1) & (HEAD_DIM - 1)) < ROPE_HALF
        y = _rope_chunk(y, cos_ref[0], sin_ref[0], first_half).astype(o_ref.dtype)
        for g in range(NSA_KV_GROUPS):
            o_ref[0, g] = y[:, g * HEAD_DIM:(g + 1) * HEAD_DIM]
    else:
        o_ref[0] = lax.dot_general(w2_ref[...], hid, _NT, preferred_element_type=F32).astype(o_ref.dtype)


def _compress(kv, pos_emb, w1, w2, cos_c, sin_c, is_key, name):
    bsz, t, _ = kv.shape
    ncp = t // CMP_STRIDE
    kwid = CMP_STRIDE * NSA_KV_GROUPS * HEAD_DIM
    hid_w = NSA_KV_GROUPS * CMP_HIDDEN
    r = kv.reshape(bsz, ncp, kwid)
    per = CMP_BLOCK // CMP_STRIDE
    w1r = w1.reshape(per, CMP_STRIDE, HEAD_DIM, CMP_HIDDEN)
    zeros = jnp.zeros_like(w1r)
    grp0 = jnp.concatenate([w1r, zeros], axis=-1)
    grp1 = jnp.concatenate([zeros, w1r], axis=-1)
    wbig = jnp.stack([grp0, grp1], axis=2).reshape(per, kwid, hid_w).astype(BF16)
    posr = pos_emb.reshape(per, CMP_STRIDE, 1, HEAD_DIM)
    posbig = jnp.broadcast_to(posr, (per, CMP_STRIDE, NSA_KV_GROUPS, HEAD_DIM)).reshape(per, 1, kwid)
    z2 = jnp.zeros_like(w2)
    w2big = jnp.concatenate([jnp.concatenate([w2, z2], axis=1),
                             jnp.concatenate([z2, w2], axis=1)], axis=0).astype(BF16)
    const = lambda shape: pl.BlockSpec(shape, lambda b: (0,) * len(shape))
    tab = pl.BlockSpec((1, ncp, LANES), lambda b: (b, 0, 0))
    if is_key:
        w2_arg = w2big
        out_spec = pl.BlockSpec((1, NSA_KV_GROUPS, ncp, HEAD_DIM), lambda b: (b, 0, 0, 0))
        out_shape = jax.ShapeDtypeStruct((bsz, NSA_KV_GROUPS, ncp, HEAD_DIM), BF16)
    else:
        w2_arg = w2big.T
        out_spec = pl.BlockSpec((1, NSA_KV_GROUPS * HEAD_DIM, ncp), lambda b: (b, 0, 0))
        out_shape = jax.ShapeDtypeStruct((bsz, NSA_KV_GROUPS * HEAD_DIM, ncp), BF16)
    return pl.pallas_call(
        functools.partial(_compress_kernel, is_key=is_key),
        grid=(bsz,),
        in_specs=[pl.BlockSpec((1, ncp, kwid), lambda b: (b, 0, 0)),
                  const((1, kwid)), const((1, kwid)),
                  const((kwid, hid_w)), const((kwid, hid_w)),
                  const(w2_arg.shape), tab, tab],
        out_specs=out_spec,
        out_shape=out_shape,
        compiler_params=_params("arbitrary"),
        name=name,
    )(r, posbig[0], posbig[1], wbig[0], wbig[1], w2_arg, cos_c, sin_c)


def _softmax_cols(s, mask):
    sm = jnp.where(mask, s, NEG_INF)
    mx = jnp.max(sm, axis=0, keepdims=True)
    e = jnp.where(mask, jnp.exp2(sm - mx), 0.0)
    l = jnp.sum(e, axis=0, keepdims=True)
    return e * (1.0 / jnp.where(l == 0.0, 1.0, l))


def _online_cols(s, v_t, m_old, l_old, acc_old, mask=None):
    m_new = jnp.maximum(m_old, jnp.max(s, axis=0, keepdims=True))
    alpha = jnp.exp2(m_old - m_new)
    p = jnp.exp2(s - m_new)
    if mask is not None:
        p = jnp.where(mask, p, 0.0)
    l_new = alpha * l_old + jnp.sum(p, axis=0, keepdims=True)
    acc_new = alpha * acc_old + _dot(v_t, p.astype(BF16))
    return m_new, l_new, acc_new


def _online_init(n, dv, qn):
    return tuple((jnp.full((1, qn), NEG_INF, F32), jnp.zeros((1, qn), F32), jnp.zeros((dv, qn), F32))
                 for _ in range(n))


def _split_bf16(v, terms):
    out, rest = [], v
    for i in range(terms):
        part = rest.astype(BF16)
        out.append(part)
        if i + 1 < terms:
            rest = rest - part.astype(F32)
    return out


def _nsa_kernel(q_ref, kc_ref, vc_ref, ks_ref, vs_ref, kw_ref, vw_ref, g_ref, ovt_ref, o_ref,
                score_scr, bias_scr, *, n_sel):
    qn = q_ref.shape[2]
    ncp = kc_ref.shape[2]
    ns = ovt_ref.shape[0]
    kt_n = KEY_TILE
    tiles_per_q = qn // kt_n
    qi = pl.program_id(2)
    q0 = qi * qn
    reps = range(NSA_REP)
    q_head = lambda r: q_ref[0, r * HEAD_DIM:(r + 1) * HEAD_DIM, :]

    kc = kc_ref[0, 0]
    vc_t = vc_ref[0]
    mask_c = (_iota((ncp, qn), 0) * CMP_STRIDE + (CMP_BLOCK - 1)) <= (q0 + _iota((ncp, qn), 1))
    o_cmp, p_sum = [], None
    for r in reps:
        p = _softmax_cols(_dot(kc, q_head(r)), mask_c)
        o_cmp.append(_dot(vc_t, p.astype(BF16)))
        p_sum = p if p_sum is None else p_sum + p
    ovt = ovt_ref[...]
    imp_t = None
    for term in _split_bf16(p_sum, 3):
        part = _dot(ovt, term)
        imp_t = part if imp_t is None else imp_t + part

    blk = _iota((ns, qn), 0)
    cur = (q0 + _iota((ns, qn), 1)) >> SLC_SHIFT
    forced = (blk == 0) | (blk == cur) | (blk == cur - 1)
    score = jnp.where(forced, FORCE, jnp.where(blk <= cur, imp_t, -FORCE))
    score_scr[...] = score

    def rank_body(m, cnt):
        other = score_scr[pl.ds(m, 1), :]
        tie = jnp.where(blk > m, 1.0, 0.0)
        return cnt + jnp.where(other > score, 1.0, jnp.where(other == score, tie, 0.0))

    n_rank = jnp.minimum(ns, (q0 + qn) >> SLC_SHIFT)
    cnt = lax.fori_loop(0, n_rank, rank_body, jnp.zeros((ns, qn), F32))
    bias_scr[...] = jnp.where(cnt < float(n_sel), 0.0, NEG_INF)

    blocks_per_tile = kt_n // SLC_BLOCK

    def sel_body(kt, carry):
        k0 = pl.multiple_of(kt * kt_n, kt_n)
        k = ks_ref[0, 0, pl.ds(k0, kt_n), :]
        v_t = vs_ref[0, kt]
        rows = [jnp.broadcast_to(bias_scr[pl.ds(kt * blocks_per_tile + i, 1), :], (SLC_BLOCK, qn))
                for i in range(blocks_per_tile)]
        causal = (k0 + _iota((kt_n, qn), 0)) <= (q0 + _iota((kt_n, qn), 1))
        bias = jnp.where(causal, jnp.concatenate(rows, axis=0), NEG_INF)
        scores = [_dot(k, q_head(r)) for r in reps]
        return tuple(_online_cols(scores[r] + bias, v_t, *carry[r]) for r in reps)

    fin = lax.fori_loop(0, (qi + 1) * tiles_per_q, sel_body, _online_init(NSA_REP, HEAD_DIM, qn))
    o_sel = [acc * (1.0 / l) for (_, l, acc) in fin]

    first_tile = jnp.maximum(qi * tiles_per_q - WINDOW // kt_n, 0)
    carry = _online_init(NSA_REP, HEAD_DIM, qn)
    for i in range(WINDOW // kt_n + tiles_per_q):
        kt = first_tile + i
        k0 = pl.multiple_of(kt * kt_n, kt_n)
        k = kw_ref[0, 0, pl.ds(k0, kt_n), :]
        v_t = vw_ref[0, kt]
        kp = k0 + _iota((kt_n, qn), 0)
        tq = q0 + _iota((kt_n, qn), 1)
        mask = (kp <= tq) & (kp > tq - WINDOW)
        scores = [_dot(k, q_head(r)) for r in reps]
        carry = tuple(_online_cols(jnp.where(mask, scores[r], NEG_INF), v_t, *carry[r], mask=mask)
                      for r in reps)
    o_win = [acc * (1.0 / l) for (_, l, acc) in carry]

    gates = g_ref[0]
    for r in reps:
        o = (gates[3 * r:3 * r + 1] * o_cmp[r] + gates[3 * r + 1:3 * r + 2] * o_sel[r]
             + gates[3 * r + 2:3 * r + 3] * o_win[r])
        o_ref[0, r * HEAD_DIM:(r + 1) * HEAD_DIM, :] = o.astype(o_ref.dtype)


def _nsa_attention(q_t, k_cmp, v_cmp_t, k_slc, v_slc_t, k_win, v_win_t, gates_t):
    bsz, _, t = q_t.shape
    ncp = k_cmp.shape[2]
    ns = t // SLC_BLOCK
    qn = min(Q_TILE, t)
    grp_w = NSA_REP * HEAD_DIM
    cmp_start = jnp.arange(ncp) * CMP_STRIDE
    slc_start = jnp.arange(ns) * SLC_BLOCK
    real = jnp.arange(ncp) < (t - CMP_BLOCK) // CMP_STRIDE + 1
    overlap_t = ((cmp_start[None, :] < slc_start[:, None] + SLC_BLOCK)
                 & (cmp_start[None, :] + CMP_BLOCK > slc_start[:, None]) & real[None, :]).astype(BF16)
    k_spec = lambda n: pl.BlockSpec((1, 1, n, HEAD_DIM), lambda b, g, i: (b, g, 0, 0))
    v_spec = pl.BlockSpec((1, t // KEY_TILE, HEAD_DIM, KEY_TILE), lambda b, g, i: (b, 0, g, 0))
    return pl.pallas_call(
        functools.partial(_nsa_kernel, n_sel=min(N_SELECT, ns)),
        grid=(bsz, NSA_KV_GROUPS, t // qn),
        in_specs=[pl.BlockSpec((1, grp_w, qn), lambda b, g, i: (b, g, i)),
                  k_spec(ncp),
                  pl.BlockSpec((1, HEAD_DIM, ncp), lambda b, g, i: (b, g, 0)),
                  k_spec(t), v_spec, k_spec(t), v_spec,
                  pl.BlockSpec((1, GATE_ROWS, qn), lambda b, g, i: (b, g, i)),
                  pl.BlockSpec((ns, ncp), lambda b, g, i: (0, 0))],
        out_specs=pl.BlockSpec((1, grp_w, qn), lambda b, g, i: (b, g, i)),
        out_shape=jax.ShapeDtypeStruct((bsz, NSA_HEADS * HEAD_DIM, t), BF16),
        scratch_shapes=[pltpu.VMEM((ns, qn), F32), pltpu.VMEM((ns, qn), F32)],
        compiler_params=_params("arbitrary", "arbitrary", "arbitrary"),
        name="nsa_attention",
    )(q_t, k_cmp, v_cmp_t, k_slc, v_slc_t, k_win, v_win_t, gates_t, overlap_t)


SB_HEADS_PER_STEP = 2
SB_SCAN_BLOCK = 128


def _sb_kernel(q_ref, k_ref, v_ref, o_ref):
    qn = q_ref.shape[2]
    kt_n = KEY_TILE
    tiles_per_q = qn // kt_n
    qi = pl.program_id(2)
    q0 = qi * qn
    heads = range(SB_HEADS_PER_STEP)
    rows = [slice(h * HEAD_DIM, (h + 1) * HEAD_DIM) for h in heads]
    sub = SB_SCAN_BLOCK
    n_sub = kt_n // sub
    tri = jnp.where(_iota((sub, sub), 1) > _iota((sub, sub), 0), 1.0, 0.0).astype(BF16)
    later2 = jnp.concatenate([tri, tri], axis=1)

    def tile(kt, carry, masked):
        k0 = pl.multiple_of(kt * kt_n, kt_n)
        neg_z = [_dot(k_ref[0, h, pl.ds(k0, kt_n), :], -q_ref[0, rows[h], :]) for h in heads]
        if masked:
            mask = (k0 + _iota((kt_n, qn), 0)) < (q0 + _iota((kt_n, qn), 1))
        out = []
        for h in heads:
            tail, acc = carry[h]
            nz = neg_z[h]
            neg_abs = lax.bitcast_convert_type(lax.bitcast_convert_type(nz, jnp.uint32) | jnp.uint32(0x80000000), F32)
            soft = jnp.log(1.0 + jnp.exp(neg_abs))
            log_keep = jnp.minimum(nz, 0.0) - soft
            log_beta = log_keep - nz
            if masked:
                log_keep = jnp.where(mask, log_keep, 0.0)
            offs = tail
            parts = [None] * n_sub
            for blk in reversed(range(n_sub)):
                lk = log_keep[blk * sub:(blk + 1) * sub]
                hi = lk.astype(BF16)
                lo = (lk - hi.astype(F32)).astype(BF16)
                within = _dot(later2, jnp.concatenate([hi, lo], axis=0))
                parts[blk] = log_beta[blk * sub:(blk + 1) * sub] + within + offs
                offs = offs + jnp.sum(lk, axis=0, keepdims=True)
            a = jnp.exp(jnp.concatenate(parts, axis=0))
            if masked:
                a = jnp.where(mask, a, 0.0)
            out.append((offs, acc + _dot(v_ref[0, kt, rows[h], :], a.astype(BF16))))
        return tuple(out)

    carry = tuple((jnp.zeros((1, qn), F32), jnp.zeros((HEAD_DIM, qn), F32)) for _ in heads)
    for i in reversed(range(tiles_per_q)):
        carry = tile(qi * tiles_per_q + i, carry, True)
    n_full = qi * tiles_per_q
    carry = lax.fori_loop(0, n_full, lambda i, c: tile(n_full - 1 - i, c, False), carry)
    for h in heads:
        o_ref[0, rows[h], :] = carry[h][1].astype(o_ref.dtype)


def _sb_attention(q_t, k, v_t):
    bsz, nh, t, _ = k.shape
    qn = min(Q_TILE, t)
    hps = SB_HEADS_PER_STEP
    w = hps * HEAD_DIM
    return pl.pallas_call(
        _sb_kernel,
        grid=(bsz, nh // hps, t // qn),
        in_specs=[pl.BlockSpec((1, w, qn), lambda b, h, i: (b, h, i)),
                  pl.BlockSpec((1, hps, t, HEAD_DIM), lambda b, h, i: (b, h, 0, 0)),
                  pl.BlockSpec((1, t // KEY_TILE, w, KEY_TILE), lambda b, h, i: (b, 0, h, 0))],
        out_specs=pl.BlockSpec((1, w, qn), lambda b, h, i: (b, h, i)),
        out_shape=jax.ShapeDtypeStruct((bsz, nh * HEAD_DIM, t), BF16),
        compiler_params=_params("arbitrary", "arbitrary", "arbitrary"),
        name="stickbreak_attention",
    )(q_t, k, v_t)


def _diff_kernel(q_ref, k_ref, v_ref, lq1_ref, lk1_ref, lq2_ref, lk2_ref, sub_ref, o_ref, *, lambda_init):
    qn = q_ref.shape[2]
    vd = v_ref.shape[2]
    kt_n = KEY_TILE
    tiles_per_q = qn // kt_n
    qi = pl.program_id(2)
    q0 = qi * qn
    lam = (jnp.exp(jnp.sum(lq1_ref[...] * lk1_ref[...], axis=1, keepdims=True))
           - jnp.exp(jnp.sum(lq2_ref[...] * lk2_ref[...], axis=1, keepdims=True)) + lambda_init)

    def tile(kt, carry, masked):
        k0 = pl.multiple_of(kt * kt_n, kt_n)
        v_t = v_ref[0, kt]
        scores = [_dot(k_ref[0, j, pl.ds(k0, kt_n), :], q_ref[0, j * HEAD_DIM:(j + 1) * HEAD_DIM, :])
                  for j in range(2)]
        if masked:
            causal = (k0 + _iota((kt_n, qn), 0)) <= (q0 + _iota((kt_n, qn), 1))
            scores = [jnp.where(causal, s, NEG_INF) for s in scores]
        return tuple(_online_cols(scores[j], v_t, *carry[j]) for j in range(2))

    n_full = qi * tiles_per_q
    carry = lax.fori_loop(0, n_full, lambda kt, c: tile(kt, c, False), _online_init(2, vd, qn))
    for i in range(tiles_per_q):
        carry = tile(n_full + i, carry, True)
    (_, l1, a1), (_, l2, a2) = carry
    o = a1 * (1.0 / l1) - lam * (a2 * (1.0 / l2))
    ms = jnp.mean(o * o, axis=0, keepdims=True)
    o = o * lax.rsqrt(ms + RMS_EPS) * sub_ref[...] * (1.0 - lambda_init)
    o_ref[0] = o.astype(o_ref.dtype)


def _diff_attention(q_t, k, v_t, lq1, lk1, lq2, lk2, subln, lambda_init):
    bsz, nh2, t, _ = k.shape
    nh = nh2 // 2
    vd = 2 * HEAD_DIM
    qn = min(Q_TILE, t)
    row = lambda a: a.reshape(1, -1)
    small = lambda n: pl.BlockSpec((1, n), lambda b, h, i: (0, 0))
    return pl.pallas_call(
        functools.partial(_diff_kernel, lambda_init=lambda_init),
        grid=(bsz, nh, t // qn),
        in_specs=[pl.BlockSpec((1, vd, qn), lambda b, h, i: (b, h, i)),
                  pl.BlockSpec((1, 2, t, HEAD_DIM), lambda b, h, i: (b, h, 0, 0)),
                  pl.BlockSpec((1, t // KEY_TILE, vd, KEY_TILE), lambda b, h, i: (b, 0, h, 0)),
                  small(HEAD_DIM), small(HEAD_DIM), small(HEAD_DIM), small(HEAD_DIM),
                  pl.BlockSpec((vd, 1), lambda b, h, i: (0, 0))],
        out_specs=pl.BlockSpec((1, vd, qn), lambda b, h, i: (b, h, i)),
        out_shape=jax.ShapeDtypeStruct((bsz, nh * vd, t), BF16),
        compiler_params=_params("arbitrary", "arbitrary", "arbitrary"),
        name="diff_attention",
    )(q_t, k, v_t, row(lq1), row(lk1), row(lq2), row(lk2), subln.reshape(vd, 1))


def _outproj_kernel(*refs, n_in):
    a_refs, w_refs = refs[:n_in], refs[n_in:2 * n_in]
    x_ref, gate_ref, o_ref = refs[2 * n_in:]
    y = None
    for a_ref, w_ref in zip(a_refs, w_refs):
        part = lax.dot_general(a_ref[0], w_ref[...], _TN, preferred_element_type=F32)
        y = part if y is None else y + part
    o_ref[0] = x_ref[0] + gate_ref[0] * y


def _out_project(acts_t, weights, x, gate, name):
    bsz, t, d = x.shape
    rows = min(t, PROJ_ROWS)
    tile = pl.BlockSpec((1, rows, d), lambda b, i: (b, i, 0))
    return pl.pallas_call(
        functools.partial(_outproj_kernel, n_in=len(acts_t)),
        grid=(bsz, t // rows),
        in_specs=([pl.BlockSpec((1, a.shape[1], rows), lambda b, i: (b, 0, i)) for a in acts_t]
                  + [pl.BlockSpec(w.shape, lambda b, i: (0, 0)) for w in weights]
                  + [tile, pl.BlockSpec((1, 1, d), lambda b, i: (b, 0, 0))]),
        out_specs=tile,
        out_shape=jax.ShapeDtypeStruct((bsz, t, d), F32),
        compiler_params=_params("arbitrary", "arbitrary"),
        name=name,
    )(*acts_t, *weights, x, gate)


def _ffn_kernel(x_ref, halo_ref, g_ref, sh_ref, sc_ref, gate_ref, wg_ref, wu_ref, cw_ref, cb_ref, wd_ref,
                nf_ref, o_ref, a_scr, *, final_norm):
    ti = pl.program_id(1)
    h = _norm_modulate(x_ref[0], g_ref[...], sh_ref[0], sc_ref[0]).astype(BF16)
    h_prev = _norm_modulate(halo_ref[0], g_ref[...], sh_ref[0], sc_ref[0]).astype(BF16)
    seen = jnp.where(ti > 0, 1.0, 0.0)
    cols = FFN_COLS
    row = _iota((h.shape[0], cols), 0)
    for c in range(a_scr.shape[1] // cols):
        sl = slice(c * cols, (c + 1) * cols)
        wg = wg_ref[:, sl]
        g = _dot(h, wg)
        g_prev = _dot(h_prev, wg) * seen
        last = g_prev[HALO_ROWS - 1:HALO_ROWS]
        last2 = g_prev[HALO_ROWS - 2:HALO_ROWS - 1]
        g1 = jnp.where(row == 0, last, pltpu.roll(g, 1, 0))
        g2 = jnp.where(row == 0, last2, jnp.where(row == 1, last, pltpu.roll(g, 2, 0)))
        cw = cw_ref[:, sl]
        conv = cw[0:1] * g2 + cw[1:2] * g1 + cw[2:3] * g + cb_ref[:, sl]
        a_scr[:, sl] = (conv * _sigmoid(conv) * _dot(h, wu_ref[:, sl])).astype(BF16)
    y = x_ref[0] + gate_ref[0] * _dot(a_scr[...], wd_ref[...])
    if final_norm:
        ms = jnp.mean(y * y, axis=-1, keepdims=True)
        y = y * lax.rsqrt(ms + RMS_EPS) * nf_ref[...]
    o_ref[0] = y


def _conv_ffn(x, g, shift, scale, gate, w_gate, w_up, conv_w, conv_b, w_down, norm_f, final_norm, name):
    bsz, t, d = x.shape
    f = w_gate.shape[1]
    rows = min(t, FFN_ROWS)
    halo_blocks = rows // HALO_ROWS
    vec = pl.BlockSpec((1, 1, d), lambda b, i: (b, 0, 0))
    drow = pl.BlockSpec((1, d), lambda b, i: (0, 0))
    resident = lambda shape: pl.BlockSpec(shape, lambda b, i: (0, 0), pipeline_mode=pl.Buffered(1))
    return pl.pallas_call(
        functools.partial(_ffn_kernel, final_norm=final_norm),
        grid=(bsz, t // rows),
        in_specs=[pl.BlockSpec((1, rows, d), lambda b, i: (b, i, 0)),
                  pl.BlockSpec((1, HALO_ROWS, d), lambda b, i: (b, jnp.maximum(i * halo_blocks - 1, 0), 0)),
                  drow, vec, vec, vec,
                  resident((d, f)), resident((d, f)), resident((CONV_WIDTH, f)), resident((1, f)),
                  resident((f, d)), drow],
        out_specs=pl.BlockSpec((1, rows, d), lambda b, i: (b, i, 0)),
        out_shape=jax.ShapeDtypeStruct((bsz, t, d), F32),
        scratch_shapes=[pltpu.VMEM((rows, f), BF16)],
        compiler_params=_params("arbitrary", "arbitrary"),
        name=name,
    )(x, x, g.reshape(1, d), shift, scale, gate, w_gate.astype(BF16), w_up.astype(BF16), conv_w,
      conv_b.reshape(1, f), w_down.astype(BF16), norm_f.reshape(1, d))


def _layout(widths):
    starts, start = [], 0
    for w in widths:
        starts.append(start)
        start += w
    return starts


def _hybrid_layer(x, mods, tables, norm_g, w_in, pos_k, pos_v, ck_w1, ck_w2, cv_w1, cv_w2, w_out):
    shift, scale, gate = mods
    bsz, t, d = x.shape
    qw, kvw, sbw = NSA_HEADS * HEAD_DIM, NSA_KV_GROUPS * HEAD_DIM, SB_HEADS * HEAD_DIM
    col = _layout((qw,) + (kvw,) * 6 + (3 * NSA_HEADS,) + (sbw,) * 3)
    cols = lambda i, w: w_in[:, col[i]:col[i] + w]
    q_n, kc, vc, ks, vs, kw, vw, gl, q_s, k_s, v_s = (cols(i, w) for i, w in enumerate(
        (qw,) + (kvw,) * 6 + (3 * NSA_HEADS,) + (sbw,) * 3))
    inv_sqrt = HEAD_DIM ** -0.5

    w_tok = jnp.concatenate([ks, kw, kc, vc, k_s], axis=1).astype(BF16)
    ts = _layout((kvw, kvw, kvw, kvw, sbw))
    tok_segs = [TokSeg(ts[0], kvw, True, HEAD_DIM, BF16),
                TokSeg(ts[1], kvw, True, HEAD_DIM, BF16),
                TokSeg(ts[2], kvw, False, 0, F32),
                TokSeg(ts[3], kvw, False, 0, F32),
                TokSeg(ts[4], sbw, False, HEAD_DIM, BF16)]
    per_group = 3 * NSA_REP
    gate_pad = jnp.zeros((d, GATE_ROWS - per_group), w_in.dtype)
    gl_pad = jnp.concatenate([gl[:, :per_group], gate_pad, gl[:, per_group:], gate_pad], axis=1)
    w_feat_t = jnp.concatenate([q_n, q_s, vs, vw, v_s, gl_pad], axis=1).T.astype(BF16)
    fs = _layout((qw, sbw, kvw, kvw, sbw, NSA_KV_GROUPS * GATE_ROWS))
    feat_segs = [FeatSeg(fs[0], qw, True, inv_sqrt * LOG2E, False, False, BF16),
                 FeatSeg(fs[1], sbw, False, inv_sqrt, False, False, BF16),
                 FeatSeg(fs[2], kvw, False, 1.0, False, True, BF16),
                 FeatSeg(fs[3], kvw, False, 1.0, False, True, BF16),
                 FeatSeg(fs[4], sbw, False, 1.0, False, True, BF16),
                 FeatSeg(fs[5], NSA_KV_GROUPS * GATE_ROWS, False, 1.0, True, False, F32)]
    (k_slc, k_win, kc_a, vc_a, k_sb, q_n_t, q_s_t, v_slc_t, v_win_t, v_sb_t, gates_t) = _project(
        x, norm_g, shift, scale, w_tok, w_feat_t, tables, tok_segs, feat_segs, "hybrid_in_proj")

    ncp = t // CMP_STRIDE
    end_rows = jnp.minimum(jnp.arange(ncp) * CMP_STRIDE + CMP_BLOCK - 1, t - 1)
    cos_c, sin_c = tables[0][:, end_rows], tables[1][:, end_rows]
    k_cmp = _compress(kc_a, pos_k, ck_w1, ck_w2, cos_c, sin_c, True, "compress_k")
    v_cmp_t = _compress(vc_a, pos_v, cv_w1, cv_w2, cos_c, sin_c, False, "compress_v")
    o_nsa_t = _nsa_attention(q_n_t, k_cmp, v_cmp_t, k_slc, v_slc_t, k_win, v_win_t, gates_t)
    o_sb_t = _sb_attention(q_s_t, k_sb, v_sb_t)
    w_out_b = w_out.astype(BF16)
    return _out_project([o_nsa_t, o_sb_t], [w_out_b[:qw], w_out_b[qw:]], x, gate, "hybrid_out_proj")


def _diff_layer(x, mods, tables, norm_g, w_qkv, lq1, lk1, lq2, lk2, subln, w_out, layer_idx):
    shift, scale, gate = mods
    dw = 2 * DIFF_HEADS * HEAD_DIM
    inv_sqrt = HEAD_DIM ** -0.5
    w_tok = w_qkv[:, dw:2 * dw].astype(BF16)
    w_feat_t = jnp.concatenate([w_qkv[:, :dw], w_qkv[:, 2 * dw:]], axis=1).T.astype(BF16)
    tok_segs = [TokSeg(0, dw, True, HEAD_DIM, BF16)]
    feat_segs = [FeatSeg(0, dw, True, inv_sqrt * LOG2E, False, False, BF16),
                 FeatSeg(dw, dw, False, 1.0, False, True, BF16)]
    k, q_t, v_t = _project(x, norm_g, shift, scale, w_tok, w_feat_t, tables, tok_segs, feat_segs, "diff_in_proj")
    lambda_init = 0.8 - 0.6 * math.exp(-0.3 * layer_idx)
    o_t = _diff_attention(q_t, k, v_t, lq1, lk1, lq2, lk2, subln, lambda_init)
    return _out_project([o_t], [w_out.astype(BF16)], x, gate, "diff_out_proj")


def kernel(x, c, positions, mod_w, mod_b, norm_mix, norm_ffn, ffn_w_gate, ffn_w_up, ffn_conv_w, ffn_conv_b, ffn_w_down, hyb_w_in, nsa_pos_k, nsa_pos_v, nsa_ck_w1, nsa_ck_w2, nsa_cv_w1, nsa_cv_w2, hyb_w_out, diff_w_qkv, diff_lq1, diff_lk1, diff_lq2, diff_lk2, diff_subln, diff_w_out, norm_f):
    bsz, t, d = x.shape
    depth = mod_w.shape[0]
    mod = _adaln_mod(c, mod_w, mod_b)
    tables = _rope_tables(positions)
    for i in range(depth):
        sh_m, sc_m, g_m, sh_f, sc_f, g_f = (mod[i, :, k * d:(k + 1) * d].reshape(bsz, 1, d) for k in range(6))
        j = i // 2
        if i % 2 == 0:
            x = _hybrid_layer(x, (sh_m, sc_m, g_m), tables, norm_mix[i], hyb_w_in[j],
                              nsa_pos_k[j], nsa_pos_v[j], nsa_ck_w1[j], nsa_ck_w2[j], nsa_cv_w1[j],
                              nsa_cv_w2[j], hyb_w_out[j])
        else:
            x = _diff_layer(x, (sh_m, sc_m, g_m), tables, norm_mix[i], diff_w_qkv[j], diff_lq1[j],
                            diff_lk1[j], diff_lq2[j], diff_lk2[j], diff_subln[j], diff_w_out[j], i)
        x = _conv_ffn(x, norm_ffn[i], sh_f, sc_f, g_f, ffn_w_gate[i], ffn_w_up[i], ffn_conv_w[i],
                      ffn_conv_b[i], ffn_w_down[i], norm_f, i == depth - 1, "conv_ffn_%d" % i)
    return x
```

```python
import functools
import math
from typing import NamedTuple

import jax
import jax.numpy as jnp
from jax import lax
from jax.experimental import pallas as pl
from jax.experimental.pallas import tpu as pltpu

F32 = jnp.float32
BF16 = jnp.bfloat16
I32 = jnp.int32

HEAD_DIM = 64
ROPE_DIM = HEAD_DIM // 4
ROPE_HALF = ROPE_DIM // 2
ROPE_THETA = 500000.0
NSA_HEADS = 8
NSA_KV_GROUPS = 2
NSA_REP = NSA_HEADS // NSA_KV_GROUPS
CMP_BLOCK = 32
CMP_STRIDE = 16
CMP_HIDDEN = 4 * HEAD_DIM
SLC_BLOCK = 64
SLC_SHIFT = 6
N_SELECT = 16
WINDOW = 512
SB_HEADS = 8
DIFF_HEADS = 8
CONV_WIDTH = 3
RMS_EPS = 1e-6
NEG_INF = -1e30
FORCE = 1e9
LOG2E = math.log2(math.e)

LANES = 128
Q_TILE = 512
KEY_TILE = 512
GATE_ROWS = 16
PROJ_ROWS = 512
FFN_ROWS = 512
FFN_COLS = 256
HALO_ROWS = 16
VMEM_LIMIT = 56 * 1024 * 1024

_NT = (((1,), (1,)), ((), ()))
_TN = (((0,), (0,)), ((), ()))


def _params(*sem):
    return pltpu.CompilerParams(dimension_semantics=sem, vmem_limit_bytes=VMEM_LIMIT)


def _sigmoid(v):
    return 1.0 / (1.0 + jnp.exp(-v))


def _iota(shape, axis):
    return lax.broadcasted_iota(I32, shape, axis)


def _dot(a, b):
    return jnp.dot(a, b, preferred_element_type=F32)


def _mod_kernel(c_ref, w_ref, b_ref, o_ref):
    c = c_ref[...]
    cond = c * _sigmoid(c)
    o_ref[0] = jnp.dot(cond, w_ref[0], preferred_element_type=F32,
                       precision=lax.Precision.HIGHEST) + b_ref[0]


def _adaln_mod(c, mod_w, mod_b):
    depth, d, n = mod_w.shape
    bsz = c.shape[0]
    tn = n // 4
    return pl.pallas_call(
        _mod_kernel,
        grid=(depth, n // tn),
        in_specs=[pl.BlockSpec((bsz, d), lambda i, j: (0, 0)),
                  pl.BlockSpec((1, d, tn), lambda i, j: (i, 0, j)),
                  pl.BlockSpec((1, 1, tn), lambda i, j: (i, 0, j))],
        out_specs=pl.BlockSpec((1, bsz, tn), lambda i, j: (i, 0, j)),
        out_shape=jax.ShapeDtypeStruct((depth, bsz, n), F32),
        compiler_params=_params("arbitrary", "arbitrary"),
        name="adaln_mod",
    )(c, mod_w, mod_b.reshape(depth, 1, n))


def _rope_kernel(pos_col_ref, pos_row_ref, inv_row_ref, sgn_row_ref, inv_col_ref,
                 cos_ref, sin_ref, cos_t_ref, sin_t_ref):
    ang = pos_col_ref[0].astype(F32) * inv_row_ref[...]
    cos_ref[0] = jnp.cos(ang)
    sin_ref[0] = jnp.sin(ang) * sgn_row_ref[...]
    ang_t = inv_col_ref[...] * pos_row_ref[0].astype(F32)
    cos_t_ref[0] = jnp.cos(ang_t)
    sin_t_ref[0] = jnp.sin(ang_t)


def _rope_tables(positions):
    bsz, t = positions.shape
    inv = ROPE_THETA ** (-jnp.arange(0, ROPE_DIM, 2, dtype=F32) / ROPE_DIM)
    per_head_inv = jnp.concatenate([inv, inv, jnp.zeros((HEAD_DIM - ROPE_DIM,), F32)])
    per_head_sgn = jnp.concatenate([-jnp.ones((ROPE_HALF,), F32), jnp.ones((ROPE_HALF,), F32),
                                    jnp.zeros((HEAD_DIM - ROPE_DIM,), F32)])
    inv_row = jnp.tile(per_head_inv, LANES // HEAD_DIM)[None, :]
    sgn_row = jnp.tile(per_head_sgn, LANES // HEAD_DIM)[None, :]
    rows = min(t, 1024)
    tab = jax.ShapeDtypeStruct((bsz, t, LANES), F32)
    tab_t = jax.ShapeDtypeStruct((bsz, ROPE_HALF, t), F32)
    row_spec = pl.BlockSpec((1, LANES), lambda b, i: (0, 0))
    return pl.pallas_call(
        _rope_kernel,
        grid=(bsz, t // rows),
        in_specs=[pl.BlockSpec((1, rows, 1), lambda b, i: (b, i, 0)),
                  pl.BlockSpec((1, 1, rows), lambda b, i: (b, 0, i)),
                  row_spec, row_spec,
                  pl.BlockSpec((ROPE_HALF, 1), lambda b, i: (0, 0))],
        out_specs=[pl.BlockSpec((1, rows, LANES), lambda b, i: (b, i, 0))] * 2
        + [pl.BlockSpec((1, ROPE_HALF, rows), lambda b, i: (b, 0, i))] * 2,
        out_shape=[tab, tab, tab_t, tab_t],
        compiler_params=_params("arbitrary", "arbitrary"),
        name="rope_tables",
    )(positions.reshape(bsz, t, 1), positions.reshape(bsz, 1, t), inv_row, sgn_row, inv[:, None])


def _rope_chunk(y, cosv, sinv, first_half):
    ahead = pltpu.roll(y, LANES - ROPE_HALF, 1)
    behind = pltpu.roll(y, ROPE_HALF, 1)
    return y * cosv + jnp.where(first_half, ahead, behind) * sinv


def _rope_rows(y_t, cos_t, sin_t):
    heads = []
    for h in range(y_t.shape[0] // HEAD_DIM):
        blk = y_t[h * HEAD_DIM:(h + 1) * HEAD_DIM]
        x1, x2 = blk[:ROPE_HALF], blk[ROPE_HALF:ROPE_DIM]
        heads += [x1 * cos_t - x2 * sin_t, x2 * cos_t + x1 * sin_t, blk[ROPE_DIM:]]
    return jnp.concatenate(heads, axis=0)


class TokSeg(NamedTuple):
    start: int
    width: int
    rope: bool
    head_width: int
    dtype: object


class FeatSeg(NamedTuple):
    start: int
    rows: int
    rope: bool
    scale: float
    sigmoid: bool
    key_tiled: bool
    dtype: object


def _norm_modulate(x, g, shift, scale):
    ms = jnp.mean(x * x, axis=-1, keepdims=True)
    y = x * lax.rsqrt(ms + RMS_EPS) * g
    return y * (1.0 + scale) + shift


def _proj_kernel(x_ref, g_ref, sh_ref, sc_ref, w_ref, wt_ref, cos_ref, sin_ref, cos_t_ref, sin_t_ref,
                 *out_refs, tok_segs, feat_segs):
    hb = _norm_modulate(x_ref[0], g_ref[...], sh_ref[0], sc_ref[0]).astype(BF16)
    cosv, sinv = cos_ref[0], sin_ref[0]
    first_half = (_iota(cosv.shape, 1) & (HEAD_DIM - 1)) < ROPE_HALF
    tok_refs, feat_refs = out_refs[:len(tok_segs)], out_refs[len(tok_segs):]
    for seg, o_ref in zip(tok_segs, tok_refs):
        y = _dot(hb, w_ref[:, seg.start:seg.start + seg.width])
        for ch in range(seg.width // LANES):
            yc = y[:, ch * LANES:(ch + 1) * LANES]
            if seg.rope:
                yc = _rope_chunk(yc, cosv, sinv, first_half)
            yc = yc.astype(seg.dtype)
            if seg.head_width == 0:
                o_ref[0, :, ch * LANES:(ch + 1) * LANES] = yc
            else:
                o_ref[0, 2 * ch] = yc[:, :HEAD_DIM]
                o_ref[0, 2 * ch + 1] = yc[:, HEAD_DIM:]
    for seg, o_ref in zip(feat_segs, feat_refs):
        y_t = lax.dot_general(wt_ref[seg.start:seg.start + seg.rows, :], hb, _NT,
                              preferred_element_type=F32)
        if seg.rope:
            y_t = _rope_rows(y_t, cos_t_ref[0], sin_t_ref[0])
        if seg.scale != 1.0:
            y_t = y_t * seg.scale
        if seg.sigmoid:
            y_t = _sigmoid(y_t)
        y_t = y_t.astype(seg.dtype)
        if seg.key_tiled:
            for ch in range(y_t.shape[1] // KEY_TILE):
                o_ref[0, ch] = y_t[:, ch * KEY_TILE:(ch + 1) * KEY_TILE]
        else:
            o_ref[0] = y_t


def _project(x, g, shift, scale, w_tok, w_feat_t, tables, tok_segs, feat_segs, name):
    bsz, t, d = x.shape
    rows = min(t, PROJ_ROWS)
    cos_tab, sin_tab, cos_t, sin_t = tables
    out_shapes, out_specs = [], []
    for seg in tok_segs:
        if seg.head_width == 0:
            out_shapes.append(jax.ShapeDtypeStruct((bsz, t, seg.width), seg.dtype))
            out_specs.append(pl.BlockSpec((1, rows, seg.width), lambda b, i: (b, i, 0)))
        else:
            nh = seg.width // seg.head_width
            out_shapes.append(jax.ShapeDtypeStruct((bsz, nh, t, seg.head_width), seg.dtype))
            out_specs.append(pl.BlockSpec((1, nh, rows, seg.head_width), lambda b, i: (b, 0, i, 0)))
    for seg in feat_segs:
        if seg.key_tiled:
            out_shapes.append(jax.ShapeDtypeStruct((bsz, t // KEY_TILE, seg.rows, KEY_TILE), seg.dtype))
            out_specs.append(pl.BlockSpec((1, rows // KEY_TILE, seg.rows, KEY_TILE), lambda b, i: (b, i, 0, 0)))
        else:
            out_shapes.append(jax.ShapeDtypeStruct((bsz, seg.rows, t), seg.dtype))
            out_specs.append(pl.BlockSpec((1, seg.rows, rows), lambda b, i: (b, 0, i)))
    vec = pl.BlockSpec((1, 1, d), lambda b, i: (b, 0, 0))
    tab = pl.BlockSpec((1, rows, LANES), lambda b, i: (b, i, 0))
    tab_t = pl.BlockSpec((1, ROPE_HALF, rows), lambda b, i: (b, 0, i))
    return pl.pallas_call(
        functools.partial(_proj_kernel, tok_segs=tuple(tok_segs), feat_segs=tuple(feat_segs)),
        grid=(bsz, t // rows),
        in_specs=[pl.BlockSpec((1, rows, d), lambda b, i: (b, i, 0)),
                  pl.BlockSpec((1, d), lambda b, i: (0, 0)),
                  vec, vec,
                  pl.BlockSpec(w_tok.shape, lambda b, i: (0, 0)),
                  pl.BlockSpec(w_feat_t.shape, lambda b, i: (0, 0)),
                  tab, tab, tab_t, tab_t],
        out_specs=out_specs,
        out_shape=out_shapes,
        compiler_params=_params("arbitrary", "arbitrary"),
        name=name,
    )(x, g.reshape(1, d), shift, scale, w_tok, w_feat_t, cos_tab, sin_tab, cos_t, sin_t)


def _compress_kernel(r_ref, pa_ref, pb_ref, wa_ref, wb_ref, w2_ref, cos_ref, sin_ref, o_ref, *, is_key):
    r = r_ref[0]
    ncp = r.shape[0]
    a = _dot((r + pa_ref[...]).astype(BF16), wa_ref[...])
    b = _dot((r + pb_ref[...]).astype(BF16), wb_ref[...])
    hid = a + pltpu.roll(b, ncp - 1, 0)
    hid = (hid * _sigmoid(hid)).astype(BF16)
    if is_key:
        y = _dot(hid, w2_ref[...])
        first_half = (_iota(y.shape, 1) & (HEAD_DIM - 1)) < ROPE_HALF
        y = _rope_chunk(y, cos_ref[0], sin_ref[0], first_half).astype(o_ref.dtype)
        for g in range(NSA_KV_GROUPS):
            o_ref[0, g] = y[:, g * HEAD_DIM:(g + 1) * HEAD_DIM]
    else:
        o_ref[0] = lax.dot_general(w2_ref[...], hid, _NT, preferred_element_type=F32).astype(o_ref.dtype)


def _compress(kv, pos_emb, w1, w2, cos_c, sin_c, is_key, name):
    bsz, t, _ = kv.shape
    ncp = t // CMP_STRIDE
    kwid = CMP_STRIDE * NSA_KV_GROUPS * HEAD_DIM
    hid_w = NSA_KV_GROUPS * CMP_HIDDEN
    r = kv.reshape(bsz, ncp, kwid)
    per = CMP_BLOCK // CMP_STRIDE
    w1r = w1.reshape(per, CMP_STRIDE, HEAD_DIM, CMP_HIDDEN)
    zeros = jnp.zeros_like(w1r)
    grp0 = jnp.concatenate([w1r, zeros], axis=-1)
    grp1 = jnp.concatenate([zeros, w1r], axis=-1)
    wbig = jnp.stack([grp0, grp1], axis=2).reshape(per, kwid, hid_w).astype(BF16)
    posr = pos_emb.reshape(per, CMP_STRIDE, 1, HEAD_DIM)
    posbig = jnp.broadcast_to(posr, (per, CMP_STRIDE, NSA_KV_GROUPS, HEAD_DIM)).reshape(per, 1, kwid)
    z2 = jnp.zeros_like(w2)
    w2big = jnp.concatenate([jnp.concatenate([w2, z2], axis=1),
                             jnp.concatenate([z2, w2], axis=1)], axis=0).astype(BF16)
    const = lambda shape: pl.BlockSpec(shape, lambda b: (0,) * len(shape))
    tab = pl.BlockSpec((1, ncp, LANES), lambda b: (b, 0, 0))
    if is_key:
        w2_arg = w2big
        out_spec = pl.BlockSpec((1, NSA_KV_GROUPS, ncp, HEAD_DIM), lambda b: (b, 0, 0, 0))
        out_shape = jax.ShapeDtypeStruct((bsz, NSA_KV_GROUPS, ncp, HEAD_DIM), BF16)
    else:
        w2_arg = w2big.T
        out_spec = pl.BlockSpec((1, NSA_KV_GROUPS * HEAD_DIM, ncp), lambda b: (b, 0, 0))
        out_shape = jax.ShapeDtypeStruct((bsz, NSA_KV_GROUPS * HEAD_DIM, ncp), BF16)
    return pl.pallas_call(
        functools.partial(_compress_kernel, is_key=is_key),
        grid=(bsz,),
        in_specs=[pl.BlockSpec((1, ncp, kwid), lambda b: (b, 0, 0)),
                  const((1, kwid)), const((1, kwid)),
                  const((kwid, hid_w)), const((kwid, hid_w)),
                  const(w2_arg.shape), tab, tab],
        out_specs=out_spec,
        out_shape=out_shape,
        compiler_params=_params("arbitrary"),
        name=name,
    )(r, posbig[0], posbig[1], wbig[0], wbig[1], w2_arg, cos_c, sin_c)


def _softmax_cols(s, mask):
    sm = jnp.where(mask, s, NEG_INF)
    mx = jnp.max(sm, axis=0, keepdims=True)
    e = jnp.where(mask, jnp.exp2(sm - mx), 0.0)
    l = jnp.sum(e, axis=0, keepdims=True)
    return e * (1.0 / jnp.where(l == 0.0, 1.0, l))


def _online_cols(s, v_t, m_old, l_old, acc_old, mask=None):
    m_new = jnp.maximum(m_old, jnp.max(s, axis=0, keepdims=True))
    alpha = jnp.exp2(m_old - m_new)
    p = jnp.exp2(s - m_new)
    if mask is not None:
        p = jnp.where(mask, p, 0.0)
    l_new = alpha * l_old + jnp.sum(p, axis=0, keepdims=True)
    acc_new = alpha * acc_old + _dot(v_t, p.astype(BF16))
    return m_new, l_new, acc_new


def _online_stored(s_ref, tile_max, v_t, m_old, l_old, acc_old):
    m_new = jnp.maximum(m_old, tile_max)
    alpha = jnp.exp2(m_old - m_new)
    p = jnp.exp2(s_ref[...] - m_new)
    l_new = alpha * l_old + jnp.sum(p, axis=0, keepdims=True)
    acc_new = alpha * acc_old + _dot(v_t, p.astype(BF16))
    return m_new, l_new, acc_new


def _sweep(n_earlier, produce, consume, finalize, state, alive=None):
    produce(0, 0, True)
    produce(1, 0, False)
    state = consume(0, 0, True, state)

    def pair(i, st):
        produce(0, 2 * i + 1, False)
        st = consume(1, 2 * i, False, st)
        produce(1, 2 * i + 2, False)
        return consume(0, 2 * i + 1, False, st)

    n_pairs = n_earlier >> 1
    odd = (n_earlier & 1) == 1
    if alive is None:
        state = lax.fori_loop(0, n_pairs, pair, state)
    else:
        _, state = lax.while_loop(lambda c: (c[0] < n_pairs) & alive(c[1]),
                                  lambda c: (c[0] + 1, pair(c[0], c[1])), (jnp.int32(0), state))
        odd = odd & alive(state)

    @pl.when(odd)
    def _():
        finalize(consume(1, n_earlier - 1, False, state))

    @pl.when(jnp.logical_not(odd))
    def _():
        finalize(state)


def _score_buffers(chains, qn):
    return [pltpu.VMEM((2, chains, KEY_TILE, qn), F32), pltpu.VMEM((2, chains, 1, qn), F32)]


def _online_init(n, dv, qn):
    return tuple((jnp.full((1, qn), NEG_INF, F32), jnp.zeros((1, qn), F32), jnp.zeros((dv, qn), F32))
                 for _ in range(n))


def _split_bf16(v, terms):
    out, rest = [], v
    for i in range(terms):
        part = rest.astype(BF16)
        out.append(part)
        if i + 1 < terms:
            rest = rest - part.astype(F32)
    return out


def _top_rows(key, n_top):
    n_rows, n_cols = key.shape

    def bisect(i, tau):
        cand = tau | lax.shift_left(jnp.int32(1), jnp.int32(30) - i)
        cnt = jnp.sum(jnp.where(key >= cand, 1.0, 0.0), axis=0, keepdims=True)
        return jnp.where(cnt >= float(n_top), cand, tau)

    tau = lax.fori_loop(0, 31, bisect, jnp.zeros((1, n_cols), I32))
    above = jnp.where(key > tau, 1.0, 0.0)
    equal = jnp.where(key == tau, 1.0, 0.0)
    need = float(n_top) - jnp.sum(above, axis=0, keepdims=True)
    lower = jnp.where(_iota((n_rows, n_rows), 1) < _iota((n_rows, n_rows), 0), 1.0, 0.0).astype(BF16)
    before = _dot(lower, equal.astype(BF16))
    return above + equal * jnp.where(before < need, 1.0, 0.0)

def _nsa_kernel(q_ref, kc_ref, vc_ref, ks_ref, vs_ref, kw_ref, vw_ref, g_ref, ovt_ref, o_ref,
                bias_scr, part_scr, s_scr, mx_scr, *, n_sel):
    qn = q_ref.shape[2]
    ncp = kc_ref.shape[2]
    ns = ovt_ref.shape[0]
    kt_n = KEY_TILE
    tiles_per_q = qn // kt_n
    qi = pl.program_id(2)
    q0 = qi * qn
    reps = range(NSA_REP)
    q_head = lambda r: q_ref[0, r * HEAD_DIM:(r + 1) * HEAD_DIM, :]

    kc = kc_ref[0, 0]
    vc_t = vc_ref[0]
    mask_c = (_iota((ncp, qn), 0) * CMP_STRIDE + (CMP_BLOCK - 1)) <= (q0 + _iota((ncp, qn), 1))
    o_cmp, p_sum = [], None
    for r in reps:
        p = _softmax_cols(_dot(kc, q_head(r)), mask_c)
        o_cmp.append(_dot(vc_t, p.astype(BF16)))
        p_sum = p if p_sum is None else p_sum + p
    ovt = ovt_ref[...]
    imp_t = None
    for term in _split_bf16(p_sum, 3):
        part = _dot(ovt, term)
        imp_t = part if imp_t is None else imp_t + part

    blk = _iota((ns, qn), 0)
    cur = (q0 + _iota((ns, qn), 1)) >> SLC_SHIFT
    forced = (blk == 0) | (blk == cur) | (blk == cur - 1)
    imp_bits = jnp.where(imp_t > 0.0, lax.bitcast_convert_type(imp_t, I32), 0)
    key = jnp.where(forced, jnp.int32(2 ** 31 - 1), jnp.where(blk <= cur, imp_bits, -1))

    picked = _top_rows(key, n_sel)
    bias_scr[...] = jnp.where(picked > 0.5, 0.0, NEG_INF)

    blocks_per_tile = kt_n // SLC_BLOCK
    causal = _iota((kt_n, qn), 0) <= _iota((kt_n, qn), 1)

    first_tile = jnp.maximum(qi - WINDOW // kt_n, 0)
    carry = _online_init(NSA_REP, HEAD_DIM, qn)
    for i in range(WINDOW // kt_n + 1):
        kt = first_tile + i
        k0 = pl.multiple_of(kt * kt_n, kt_n)
        k = kw_ref[0, 0, pl.ds(k0, kt_n), :]
        v_t = vw_ref[0, kt]
        kp = k0 + _iota((kt_n, qn), 0)
        tq = q0 + _iota((kt_n, qn), 1)
        mask = (kp <= tq) & (kp > tq - WINDOW)
        scores = [_dot(k, q_head(r)) for r in reps]
        carry = tuple(_online_cols(jnp.where(mask, scores[r], NEG_INF), v_t, *carry[r], mask=mask)
                      for r in reps)

    gates = g_ref[0]
    for r in reps:
        _, l, acc = carry[r]
        part_scr[r] = gates[3 * r:3 * r + 1] * o_cmp[r] + gates[3 * r + 2:3 * r + 3] * (acc * (1.0 / l))

    def produce(slot, f, diagonal):
        kt = qi if diagonal else jnp.minimum(f, jnp.maximum(qi - 1, 0))
        k0 = pl.multiple_of(kt * kt_n, kt_n)
        k = ks_ref[0, 0, pl.ds(k0, kt_n), :]
        rows = [jnp.broadcast_to(bias_scr[pl.ds(kt * blocks_per_tile + i, 1), :], (SLC_BLOCK, qn))
                for i in range(blocks_per_tile)]
        bias = jnp.concatenate(rows, axis=0)
        if diagonal:
            bias = jnp.where(causal, bias, NEG_INF)
        for r in reps:
            s = _dot(k, q_head(r)) + bias
            s_scr[slot, r] = s
            mx_scr[slot, r] = jnp.max(s, axis=0, keepdims=True)

    def consume(slot, f, diagonal, state):
        v_t = vs_ref[0, qi if diagonal else f]
        return tuple(_online_stored(s_scr.at[slot, r], mx_scr[slot, r], v_t, *state[r]) for r in reps)

    def finalize(state):
        for r in reps:
            _, l, acc = state[r]
            o = part_scr[r] + g_ref[0, 3 * r + 1:3 * r + 2, :] * (acc * (1.0 / l))
            o_ref[0, r * HEAD_DIM:(r + 1) * HEAD_DIM, :] = o.astype(o_ref.dtype)

    _sweep(qi, produce, consume, finalize, _online_init(NSA_REP, HEAD_DIM, qn))


def _nsa_attention(q_t, k_cmp, v_cmp_t, k_slc, v_slc_t, k_win, v_win_t, gates_t):
    bsz, _, t = q_t.shape
    ncp = k_cmp.shape[2]
    ns = t // SLC_BLOCK
    qn = min(Q_TILE, t)
    grp_w = NSA_REP * HEAD_DIM
    cmp_start = jnp.arange(ncp) * CMP_STRIDE
    slc_start = jnp.arange(ns) * SLC_BLOCK
    real = jnp.arange(ncp) < (t - CMP_BLOCK) // CMP_STRIDE + 1
    overlap_t = ((cmp_start[None, :] < slc_start[:, None] + SLC_BLOCK)
                 & (cmp_start[None, :] + CMP_BLOCK > slc_start[:, None]) & real[None, :]).astype(BF16)
    k_spec = lambda n: pl.BlockSpec((1, 1, n, HEAD_DIM), lambda b, g, i: (b, g, 0, 0))
    v_spec = pl.BlockSpec((1, t // KEY_TILE, HEAD_DIM, KEY_TILE), lambda b, g, i: (b, 0, g, 0))
    return pl.pallas_call(
        functools.partial(_nsa_kernel, n_sel=min(N_SELECT, ns)),
        grid=(bsz, NSA_KV_GROUPS, t // qn),
        in_specs=[pl.BlockSpec((1, grp_w, qn), lambda b, g, i: (b, g, i)),
                  k_spec(ncp),
                  pl.BlockSpec((1, HEAD_DIM, ncp), lambda b, g, i: (b, g, 0)),
                  k_spec(t), v_spec, k_spec(t), v_spec,
                  pl.BlockSpec((1, GATE_ROWS, qn), lambda b, g, i: (b, g, i)),
                  pl.BlockSpec((ns, ncp), lambda b, g, i: (0, 0))],
        out_specs=pl.BlockSpec((1, grp_w, qn), lambda b, g, i: (b, g, i)),
        out_shape=jax.ShapeDtypeStruct((bsz, NSA_HEADS * HEAD_DIM, t), BF16),
        scratch_shapes=[pltpu.VMEM((ns, qn), F32), pltpu.VMEM((NSA_REP, HEAD_DIM, qn), F32)] + _score_buffers(NSA_REP, qn),
        compiler_params=_params("arbitrary", "arbitrary", "arbitrary"),
        name="nsa_attention",
    )(q_t, k_cmp, v_cmp_t, k_slc, v_slc_t, k_win, v_win_t, gates_t, overlap_t)


SB_HEADS_PER_STEP = 2
SB_DEAD_TAIL = -110.0
SB_SCAN_BLOCK = 128


def _sb_kernel(q_ref, k_ref, v_ref, o_ref, nz_scr):
    qn = q_ref.shape[2]
    kt_n = KEY_TILE
    qi = pl.program_id(2)
    heads = range(SB_HEADS_PER_STEP)
    rows = [slice(h * HEAD_DIM, (h + 1) * HEAD_DIM) for h in heads]
    sub = SB_SCAN_BLOCK
    n_sub = kt_n // sub
    tri = jnp.where(_iota((sub, sub), 1) > _iota((sub, sub), 0), 1.0, 0.0).astype(BF16)
    later2 = jnp.concatenate([tri, tri], axis=1)
    strictly_before = _iota((kt_n, qn), 0) < _iota((kt_n, qn), 1)

    def tile_index(f, diagonal):
        return qi if diagonal else jnp.maximum(qi - 1 - f, 0)

    def produce(slot, f, diagonal):
        k0 = pl.multiple_of(tile_index(f, diagonal) * kt_n, kt_n)
        for h in heads:
            nz_scr[slot, h] = _dot(k_ref[0, h, pl.ds(k0, kt_n), :], -q_ref[0, rows[h], :])

    def consume(slot, f, diagonal, state):
        kt = tile_index(f, diagonal)
        masked = diagonal
        mask = strictly_before
        out = []
        for h in heads:
            tail, acc = state[h]
            nz = nz_scr[slot, h]
            neg_abs = lax.bitcast_convert_type(lax.bitcast_convert_type(nz, jnp.uint32) | jnp.uint32(0x80000000), F32)
            soft = jnp.log(1.0 + jnp.exp(neg_abs))
            log_keep = jnp.minimum(nz, 0.0) - soft
            log_beta = log_keep - nz
            if masked:
                log_keep = jnp.where(mask, log_keep, 0.0)
            offs = tail
            parts = [None] * n_sub
            for blk in reversed(range(n_sub)):
                lk = log_keep[blk * sub:(blk + 1) * sub]
                hi = lk.astype(BF16)
                lo = (lk - hi.astype(F32)).astype(BF16)
                within = _dot(later2, jnp.concatenate([hi, lo], axis=0))
                parts[blk] = log_beta[blk * sub:(blk + 1) * sub] + within + offs
                offs = offs + jnp.sum(lk, axis=0, keepdims=True)
            a = jnp.exp(jnp.concatenate(parts, axis=0))
            if masked:
                a = jnp.where(mask, a, 0.0)
            out.append((offs, acc + _dot(v_ref[0, kt, rows[h], :], a.astype(BF16))))
        return tuple(out)

    def finalize(state):
        for h in heads:
            o_ref[0, rows[h], :] = state[h][1].astype(o_ref.dtype)

    def alive(state):
        tails = [state[h][0] for h in heads]
        return jnp.max(functools.reduce(jnp.maximum, tails)) > SB_DEAD_TAIL

    init = tuple((jnp.zeros((1, qn), F32), jnp.zeros((HEAD_DIM, qn), F32)) for _ in heads)
    _sweep(qi, produce, consume, finalize, init, alive)


def _sb_attention(q_t, k, v_t):
    bsz, nh, t, _ = k.shape
    qn = min(Q_TILE, t)
    hps = SB_HEADS_PER_STEP
    w = hps * HEAD_DIM
    return pl.pallas_call(
        _sb_kernel,
        grid=(bsz, nh // hps, t // qn),
        in_specs=[pl.BlockSpec((1, w, qn), lambda b, h, i: (b, h, i)),
                  pl.BlockSpec((1, hps, t, HEAD_DIM), lambda b, h, i: (b, h, 0, 0)),
                  pl.BlockSpec((1, t // KEY_TILE, w, KEY_TILE), lambda b, h, i: (b, 0, h, 0))],
        out_specs=pl.BlockSpec((1, w, qn), lambda b, h, i: (b, h, i)),
        out_shape=jax.ShapeDtypeStruct((bsz, nh * HEAD_DIM, t), BF16),
        scratch_shapes=[pltpu.VMEM((2, hps, KEY_TILE, qn), F32)],
        compiler_params=_params("arbitrary", "arbitrary", "arbitrary"),
        name="stickbreak_attention",
    )(q_t, k, v_t)


def _diff_kernel(q_ref, k_ref, v_ref, lq1_ref, lk1_ref, lq2_ref, lk2_ref, sub_ref, o_ref, s_scr, mx_scr,
                 *, lambda_init):
    qn = q_ref.shape[2]
    vd = v_ref.shape[2]
    kt_n = KEY_TILE
    qi = pl.program_id(2)
    lam = (jnp.exp(jnp.sum(lq1_ref[...] * lk1_ref[...], axis=1, keepdims=True))
           - jnp.exp(jnp.sum(lq2_ref[...] * lk2_ref[...], axis=1, keepdims=True)) + lambda_init)

    causal = _iota((kt_n, qn), 0) <= _iota((kt_n, qn), 1)

    def produce(slot, f, diagonal):
        kt = qi if diagonal else jnp.minimum(f, jnp.maximum(qi - 1, 0))
        k0 = pl.multiple_of(kt * kt_n, kt_n)
        for j in range(2):
            s = _dot(k_ref[0, j, pl.ds(k0, kt_n), :], q_ref[0, j * HEAD_DIM:(j + 1) * HEAD_DIM, :])
            if diagonal:
                s = jnp.where(causal, s, NEG_INF)
            s_scr[slot, j] = s
            mx_scr[slot, j] = jnp.max(s, axis=0, keepdims=True)

    def consume(slot, f, diagonal, state):
        v_t = v_ref[0, qi if diagonal else f]
        return tuple(_online_stored(s_scr.at[slot, j], mx_scr[slot, j], v_t, *state[j]) for j in range(2))

    def finalize(state):
        (_, l1, a1), (_, l2, a2) = state
        o = a1 * (1.0 / l1) - lam * (a2 * (1.0 / l2))
        ms = jnp.mean(o * o, axis=0, keepdims=True)
        o = o * lax.rsqrt(ms + RMS_EPS) * sub_ref[...] * (1.0 - lambda_init)
        o_ref[0] = o.astype(o_ref.dtype)

    _sweep(qi, produce, consume, finalize, _online_init(2, vd, qn))


def _diff_attention(q_t, k, v_t, lq1, lk1, lq2, lk2, subln, lambda_init):
    bsz, nh2, t, _ = k.shape
    nh = nh2 // 2
    vd = 2 * HEAD_DIM
    qn = min(Q_TILE, t)
    row = lambda a: a.reshape(1, -1)
    small = lambda n: pl.BlockSpec((1, n), lambda b, h, i: (0, 0))
    return pl.pallas_call(
        functools.partial(_diff_kernel, lambda_init=lambda_init),
        grid=(bsz, nh, t // qn),
        in_specs=[pl.BlockSpec((1, vd, qn), lambda b, h, i: (b, h, i)),
                  pl.BlockSpec((1, 2, t, HEAD_DIM), lambda b, h, i: (b, h, 0, 0)),
                  pl.BlockSpec((1, t // KEY_TILE, vd, KEY_TILE), lambda b, h, i: (b, 0, h, 0)),
                  small(HEAD_DIM), small(HEAD_DIM), small(HEAD_DIM), small(HEAD_DIM),
                  pl.BlockSpec((vd, 1), lambda b, h, i: (0, 0))],
        out_specs=pl.BlockSpec((1, vd, qn), lambda b, h, i: (b, h, i)),
        out_shape=jax.ShapeDtypeStruct((bsz, nh * vd, t), BF16),
        scratch_shapes=_score_buffers(2, qn),
        compiler_params=_params("arbitrary", "arbitrary", "arbitrary"),
        name="diff_attention",
    )(q_t, k, v_t, row(lq1), row(lk1), row(lq2), row(lk2), subln.reshape(vd, 1))


def _outproj_kernel(*refs, n_in):
    a_refs, w_refs = refs[:n_in], refs[n_in:2 * n_in]
    x_ref, gate_ref, o_ref = refs[2 * n_in:]
    y = None
    for a_ref, w_ref in zip(a_refs, w_refs):
        part = lax.dot_general(a_ref[0], w_ref[...], _TN, preferred_element_type=F32)
        y = part if y is None else y + part
    o_ref[0] = x_ref[0] + gate_ref[0] * y


def _out_project(acts_t, weights, x, gate, name):
    bsz, t, d = x.shape
    rows = min(t, PROJ_ROWS)
    tile = pl.BlockSpec((1, rows, d), lambda b, i: (b, i, 0))
    return pl.pallas_call(
        functools.partial(_outproj_kernel, n_in=len(acts_t)),
        grid=(bsz, t // rows),
        in_specs=([pl.BlockSpec((1, a.shape[1], rows), lambda b, i: (b, 0, i)) for a in acts_t]
                  + [pl.BlockSpec(w.shape, lambda b, i: (0, 0)) for w in weights]
                  + [tile, pl.BlockSpec((1, 1, d), lambda b, i: (b, 0, 0))]),
        out_specs=tile,
        out_shape=jax.ShapeDtypeStruct((bsz, t, d), F32),
        compiler_params=_params("arbitrary", "arbitrary"),
        name=name,
    )(*acts_t, *weights, x, gate)


def _ffn_kernel(x_ref, halo_ref, g_ref, sh_ref, sc_ref, gate_ref, wg_ref, wu_ref, cw_ref, cb_ref, wd_ref,
                nf_ref, o_ref, a_scr, *, final_norm):
    ti = pl.program_id(1)
    h = _norm_modulate(x_ref[0], g_ref[...], sh_ref[0], sc_ref[0]).astype(BF16)
    h_prev = _norm_modulate(halo_ref[0], g_ref[...], sh_ref[0], sc_ref[0]).astype(BF16)
    seen = jnp.where(ti > 0, 1.0, 0.0)
    cols = FFN_COLS
    row = _iota((h.shape[0], cols), 0)
    for c in range(a_scr.shape[1] // cols):
        sl = slice(c * cols, (c + 1) * cols)
        wg = wg_ref[:, sl]
        g = _dot(h, wg)
        g_prev = _dot(h_prev, wg) * seen
        last = g_prev[HALO_ROWS - 1:HALO_ROWS]
        last2 = g_prev[HALO_ROWS - 2:HALO_ROWS - 1]
        g1 = jnp.where(row == 0, last, pltpu.roll(g, 1, 0))
        g2 = jnp.where(row == 0, last2, jnp.where(row == 1, last, pltpu.roll(g, 2, 0)))
        cw = cw_ref[:, sl]
        conv = cw[0:1] * g2 + cw[1:2] * g1 + cw[2:3] * g + cb_ref[:, sl]
        a_scr[:, sl] = (conv * _sigmoid(conv) * _dot(h, wu_ref[:, sl])).astype(BF16)
    y = x_ref[0] + gate_ref[0] * _dot(a_scr[...], wd_ref[...])
    if final_norm:
        ms = jnp.mean(y * y, axis=-1, keepdims=True)
        y = y * lax.rsqrt(ms + RMS_EPS) * nf_ref[...]
    o_ref[0] = y


def _conv_ffn(x, g, shift, scale, gate, w_gate, w_up, conv_w, conv_b, w_down, norm_f, final_norm, name):
    bsz, t, d = x.shape
    f = w_gate.shape[1]
    rows = min(t, FFN_ROWS)
    halo_blocks = rows // HALO_ROWS
    vec = pl.BlockSpec((1, 1, d), lambda b, i: (b, 0, 0))
    drow = pl.BlockSpec((1, d), lambda b, i: (0, 0))
    resident = lambda shape: pl.BlockSpec(shape, lambda b, i: (0, 0), pipeline_mode=pl.Buffered(1))
    return pl.pallas_call(
        functools.partial(_ffn_kernel, final_norm=final_norm),
        grid=(bsz, t // rows),
        in_specs=[pl.BlockSpec((1, rows, d), lambda b, i: (b, i, 0)),
                  pl.BlockSpec((1, HALO_ROWS, d), lambda b, i: (b, jnp.maximum(i * halo_blocks - 1, 0), 0)),
                  drow, vec, vec, vec,
                  resident((d, f)), resident((d, f)), resident((CONV_WIDTH, f)), resident((1, f)),
                  resident((f, d)), drow],
        out_specs=pl.BlockSpec((1, rows, d), lambda b, i: (b, i, 0)),
        out_shape=jax.ShapeDtypeStruct((bsz, t, d), F32),
        scratch_shapes=[pltpu.VMEM((rows, f), BF16)],
        compiler_params=_params("arbitrary", "arbitrary"),
        name=name,
    )(x, x, g.reshape(1, d), shift, scale, gate, w_gate.astype(BF16), w_up.astype(BF16), conv_w,
      conv_b.reshape(1, f), w_down.astype(BF16), norm_f.reshape(1, d))


def _layout(widths):
    starts, start = [], 0
    for w in widths:
        starts.append(start)
        start += w
    return starts


def _hybrid_layer(x, mods, tables, norm_g, w_in, pos_k, pos_v, ck_w1, ck_w2, cv_w1, cv_w2, w_out):
    shift, scale, gate = mods
    bsz, t, d = x.shape
    qw, kvw, sbw = NSA_HEADS * HEAD_DIM, NSA_KV_GROUPS * HEAD_DIM, SB_HEADS * HEAD_DIM
    col = _layout((qw,) + (kvw,) * 6 + (3 * NSA_HEADS,) + (sbw,) * 3)
    cols = lambda i, w: w_in[:, col[i]:col[i] + w]
    q_n, kc, vc, ks, vs, kw, vw, gl, q_s, k_s, v_s = (cols(i, w) for i, w in enumerate(
        (qw,) + (kvw,) * 6 + (3 * NSA_HEADS,) + (sbw,) * 3))
    inv_sqrt = HEAD_DIM ** -0.5

    w_tok = jnp.concatenate([ks, kw, kc, vc, k_s], axis=1).astype(BF16)
    ts = _layout((kvw, kvw, kvw, kvw, sbw))
    tok_segs = [TokSeg(ts[0], kvw, True, HEAD_DIM, BF16),
                TokSeg(ts[1], kvw, True, HEAD_DIM, BF16),
                TokSeg(ts[2], kvw, False, 0, F32),
                TokSeg(ts[3], kvw, False, 0, F32),
                TokSeg(ts[4], sbw, False, HEAD_DIM, BF16)]
    per_group = 3 * NSA_REP
    gate_pad = jnp.zeros((d, GATE_ROWS - per_group), w_in.dtype)
    gl_pad = jnp.concatenate([gl[:, :per_group], gate_pad, gl[:, per_group:], gate_pad], axis=1)
    w_feat_t = jnp.concatenate([q_n, q_s, vs, vw, v_s, gl_pad], axis=1).T.astype(BF16)
    fs = _layout((qw, sbw, kvw, kvw, sbw, NSA_KV_GROUPS * GATE_ROWS))
    feat_segs = [FeatSeg(fs[0], qw, True, inv_sqrt * LOG2E, False, False, BF16),
                 FeatSeg(fs[1], sbw, False, inv_sqrt, False, False, BF16),
                 FeatSeg(fs[2], kvw, False, 1.0, False, True, BF16),
                 FeatSeg(fs[3], kvw, False, 1.0, False, True, BF16),
                 FeatSeg(fs[4], sbw, False, 1.0, False, True, BF16),
                 FeatSeg(fs[5], NSA_KV_GROUPS * GATE_ROWS, False, 1.0, True, False, F32)]
    (k_slc, k_win, kc_a, vc_a, k_sb, q_n_t, q_s_t, v_slc_t, v_win_t, v_sb_t, gates_t) = _project(
        x, norm_g, shift, scale, w_tok, w_feat_t, tables, tok_segs, feat_segs, "hybrid_in_proj")

    ncp = t // CMP_STRIDE
    end_rows = jnp.minimum(jnp.arange(ncp) * CMP_STRIDE + CMP_BLOCK - 1, t - 1)
    cos_c, sin_c = tables[0][:, end_rows], tables[1][:, end_rows]
    k_cmp = _compress(kc_a, pos_k, ck_w1, ck_w2, cos_c, sin_c, True, "compress_k")
    v_cmp_t = _compress(vc_a, pos_v, cv_w1, cv_w2, cos_c, sin_c, False, "compress_v")
    o_nsa_t = _nsa_attention(q_n_t, k_cmp, v_cmp_t, k_slc, v_slc_t, k_win, v_win_t, gates_t)
    o_sb_t = _sb_attention(q_s_t, k_sb, v_sb_t)
    w_out_b = w_out.astype(BF16)
    return _out_project([o_nsa_t, o_sb_t], [w_out_b[:qw], w_out_b[qw:]], x, gate, "hybrid_out_proj")


def _diff_layer(x, mods, tables, norm_g, w_qkv, lq1, lk1, lq2, lk2, subln, w_out, layer_idx):
    shift, scale, gate = mods
    dw = 2 * DIFF_HEADS * HEAD_DIM
    inv_sqrt = HEAD_DIM ** -0.5
    w_tok = w_qkv[:, dw:2 * dw].astype(BF16)
    w_feat_t = jnp.concatenate([w_qkv[:, :dw], w_qkv[:, 2 * dw:]], axis=1).T.astype(BF16)
    tok_segs = [TokSeg(0, dw, True, HEAD_DIM, BF16)]
    feat_segs = [FeatSeg(0, dw, True, inv_sqrt * LOG2E, False, False, BF16),
                 FeatSeg(dw, dw, False, 1.0, False, True, BF16)]
    k, q_t, v_t = _project(x, norm_g, shift, scale, w_tok, w_feat_t, tables, tok_segs, feat_segs, "diff_in_proj")
    lambda_init = 0.8 - 0.6 * math.exp(-0.3 * layer_idx)
    o_t = _diff_attention(q_t, k, v_t, lq1, lk1, lq2, lk2, subln, lambda_init)
    return _out_project([o_t], [w_out.astype(BF16)], x, gate, "diff_out_proj")


def kernel(x, c, positions, mod_w, mod_b, norm_mix, norm_ffn, ffn_w_gate, ffn_w_up, ffn_conv_w, ffn_conv_b, ffn_w_down, hyb_w_in, nsa_pos_k, nsa_pos_v, nsa_ck_w1, nsa_ck_w2, nsa_cv_w1, nsa_cv_w2, hyb_w_out, diff_w_qkv, diff_lq1, diff_lk1, diff_lq2, diff_lk2, diff_subln, diff_w_out, norm_f):
    bsz, t, d = x.shape
    depth = mod_w.shape[0]
    mod = _adaln_mod(c, mod_w, mod_b)
    tables = _rope_tables(positions)
    for i in range(depth):
        sh_m, sc_m, g_m, sh_f, sc_f, g_f = (mod[i, :, k * d:(k + 1) * d].reshape(bsz, 1, d) for k in range(6))
        j = i // 2
        if i % 2 == 0:
            x = _hybrid_layer(x, (sh_m, sc_m, g_m), tables, norm_mix[i], hyb_w_in[j],
                              nsa_pos_k[j], nsa_pos_v[j], nsa_ck_w1[j], nsa_ck_w2[j], nsa_cv_w1[j],
                              nsa_cv_w2[j], hyb_w_out[j])
        else:
            x = _diff_layer(x, (sh_m, sc_m, g_m), tables, norm_mix[i], diff_w_qkv[j], diff_lq1[j],
                            diff_lk1[j], diff_lq2[j], diff_lk2[j], diff_subln[j], diff_w_out[j], i)
        x = _conv_ffn(x, norm_ffn[i], sh_f, sc_f, g_f, ffn_w_gate[i], ffn_w_up[i], ffn_conv_w[i],
                      ffn_conv_b[i], ffn_w_down[i], norm_f, i == depth - 1, "conv_ffn_%d" % i)
    return x
```

```python
import functools
import math
from typing import NamedTuple

import jax
import jax.numpy as jnp
from jax import lax
from jax.experimental import pallas as pl
from jax.experimental.pallas import tpu as pltpu

F32 = jnp.float32
BF16 = jnp.bfloat16
I32 = jnp.int32

HEAD_DIM = 64
ROPE_DIM = HEAD_DIM // 4
ROPE_HALF = ROPE_DIM // 2
ROPE_THETA = 500000.0
NSA_HEADS = 8
NSA_KV_GROUPS = 2
NSA_REP = NSA_HEADS // NSA_KV_GROUPS
CMP_BLOCK = 32
CMP_STRIDE = 16
CMP_HIDDEN = 4 * HEAD_DIM
SLC_BLOCK = 64
SLC_SHIFT = 6
N_SELECT = 16
WINDOW = 512
SB_HEADS = 8
DIFF_HEADS = 8
CONV_WIDTH = 3
RMS_EPS = 1e-6
NEG_INF = -1e30
FORCE = 1e9
LOG2E = math.log2(math.e)

LANES = 128
Q_TILE = 512
KEY_TILE = 512
GATE_ROWS = 16
PROJ_ROWS = 512
FFN_ROWS = 512
FFN_COLS = 256
HALO_ROWS = 16
VMEM_LIMIT = 56 * 1024 * 1024

_NT = (((1,), (1,)), ((), ()))
_TN = (((0,), (0,)), ((), ()))


def _params(*sem):
    return pltpu.CompilerParams(dimension_semantics=sem, vmem_limit_bytes=VMEM_LIMIT)


def _sigmoid(v):
    return 1.0 / (1.0 + jnp.exp(-v))


def _iota(shape, axis):
    return lax.broadcasted_iota(I32, shape, axis)


def _dot(a, b):
    return jnp.dot(a, b, preferred_element_type=F32)


def _mod_kernel(c_ref, w_ref, b_ref, o_ref):
    c = c_ref[...]
    cond = c * _sigmoid(c)
    o_ref[0] = jnp.dot(cond, w_ref[0], preferred_element_type=F32,
                       precision=lax.Precision.HIGHEST) + b_ref[0]


def _adaln_mod(c, mod_w, mod_b):
    depth, d, n = mod_w.shape
    bsz = c.shape[0]
    tn = n // 4
    return pl.pallas_call(
        _mod_kernel,
        grid=(depth, n // tn),
        in_specs=[pl.BlockSpec((bsz, d), lambda i, j: (0, 0)),
                  pl.BlockSpec((1, d, tn), lambda i, j: (i, 0, j)),
                  pl.BlockSpec((1, 1, tn), lambda i, j: (i, 0, j))],
        out_specs=pl.BlockSpec((1, bsz, tn), lambda i, j: (i, 0, j)),
        out_shape=jax.ShapeDtypeStruct((depth, bsz, n), F32),
        compiler_params=_params("arbitrary", "arbitrary"),
        name="adaln_mod",
    )(c, mod_w, mod_b.reshape(depth, 1, n))


def _rope_kernel(pos_col_ref, pos_row_ref, inv_row_ref, sgn_row_ref, inv_col_ref,
                 cos_ref, sin_ref, cos_t_ref, sin_t_ref):
    ang = pos_col_ref[0].astype(F32) * inv_row_ref[...]
    cos_ref[0] = jnp.cos(ang)
    sin_ref[0] = jnp.sin(ang) * sgn_row_ref[...]
    ang_t = inv_col_ref[...] * pos_row_ref[0].astype(F32)
    cos_t_ref[0] = jnp.cos(ang_t)
    sin_t_ref[0] = jnp.sin(ang_t)


def _rope_tables(positions):
    bsz, t = positions.shape
    inv = ROPE_THETA ** (-jnp.arange(0, ROPE_DIM, 2, dtype=F32) / ROPE_DIM)
    per_head_inv = jnp.concatenate([inv, inv, jnp.zeros((HEAD_DIM - ROPE_DIM,), F32)])
    per_head_sgn = jnp.concatenate([-jnp.ones((ROPE_HALF,), F32), jnp.ones((ROPE_HALF,), F32),
                                    jnp.zeros((HEAD_DIM - ROPE_DIM,), F32)])
    inv_row = jnp.tile(per_head_inv, LANES // HEAD_DIM)[None, :]
    sgn_row = jnp.tile(per_head_sgn, LANES // HEAD_DIM)[None, :]
    rows = min(t, 1024)
    tab = jax.ShapeDtypeStruct((bsz, t, LANES), F32)
    tab_t = jax.ShapeDtypeStruct((bsz, ROPE_HALF, t), F32)
    row_spec = pl.BlockSpec((1, LANES), lambda b, i: (0, 0))
    return pl.pallas_call(
        _rope_kernel,
        grid=(bsz, t // rows),
        in_specs=[pl.BlockSpec((1, rows, 1), lambda b, i: (b, i, 0)),
                  pl.BlockSpec((1, 1, rows), lambda b, i: (b, 0, i)),
                  row_spec, row_spec,
                  pl.BlockSpec((ROPE_HALF, 1), lambda b, i: (0, 0))],
        out_specs=[pl.BlockSpec((1, rows, LANES), lambda b, i: (b, i, 0))] * 2
        + [pl.BlockSpec((1, ROPE_HALF, rows), lambda b, i: (b, 0, i))] * 2,
        out_shape=[tab, tab, tab_t, tab_t],
        compiler_params=_params("arbitrary", "arbitrary"),
        name="rope_tables",
    )(positions.reshape(bsz, t, 1), positions.reshape(bsz, 1, t), inv_row, sgn_row, inv[:, None])


def _rope_chunk(y, cosv, sinv, first_half):
    ahead = pltpu.roll(y, LANES - ROPE_HALF, 1)
    behind = pltpu.roll(y, ROPE_HALF, 1)
    return y * cosv + jnp.where(first_half, ahead, behind) * sinv


def _rope_rows(y_t, cos_t, sin_t):
    heads = []
    for h in range(y_t.shape[0] // HEAD_DIM):
        blk = y_t[h * HEAD_DIM:(h + 1) * HEAD_DIM]
        x1, x2 = blk[:ROPE_HALF], blk[ROPE_HALF:ROPE_DIM]
        heads += [x1 * cos_t - x2 * sin_t, x2 * cos_t + x1 * sin_t, blk[ROPE_DIM:]]
    return jnp.concatenate(heads, axis=0)


class TokSeg(NamedTuple):
    start: int
    width: int
    rope: bool
    head_width: int
    dtype: object


class FeatSeg(NamedTuple):
    start: int
    rows: int
    rope: bool
    scale: float
    sigmoid: bool
    key_tiled: bool
    dtype: object


def _norm_modulate(x, g, shift, scale):
    ms = jnp.mean(x * x, axis=-1, keepdims=True)
    y = x * lax.rsqrt(ms + RMS_EPS) * g
    return y * (1.0 + scale) + shift


def _proj_kernel(x_ref, g_ref, sh_ref, sc_ref, w_ref, wt_ref, cos_ref, sin_ref, cos_t_ref, sin_t_ref,
                 *out_refs, tok_segs, feat_segs):
    hb = _norm_modulate(x_ref[0], g_ref[...], sh_ref[0], sc_ref[0]).astype(BF16)
    cosv, sinv = cos_ref[0], sin_ref[0]
    first_half = (_iota(cosv.shape, 1) & (HEAD_DIM - 1)) < ROPE_HALF
    tok_refs, feat_refs = out_refs[:len(tok_segs)], out_refs[len(tok_segs):]
    for seg, o_ref in zip(tok_segs, tok_refs):
        y = _dot(hb, w_ref[:, seg.start:seg.start + seg.width])
        for ch in range(seg.width // LANES):
            yc = y[:, ch * LANES:(ch + 1) * LANES]
            if seg.rope:
                yc = _rope_chunk(yc, cosv, sinv, first_half)
            yc = yc.astype(seg.dtype)
            if seg.head_width == 0:
                o_ref[0, :, ch * LANES:(ch + 1) * LANES] = yc
            else:
                o_ref[0, 2 * ch] = yc[:, :HEAD_DIM]
                o_ref[0, 2 * ch + 1] = yc[:, HEAD_DIM:]
    for seg, o_ref in zip(feat_segs, feat_refs):
        y_t = lax.dot_general(wt_ref[seg.start:seg.start + seg.rows, :], hb, _NT,
                              preferred_element_type=F32)
        if seg.rope:
            y_t = _rope_rows(y_t, cos_t_ref[0], sin_t_ref[0])
        if seg.scale != 1.0:
            y_t = y_t * seg.scale
        if seg.sigmoid:
            y_t = _sigmoid(y_t)
        y_t = y_t.astype(seg.dtype)
        if seg.key_tiled:
            for ch in range(y_t.shape[1] // KEY_TILE):
                o_ref[0, ch] = y_t[:, ch * KEY_TILE:(ch + 1) * KEY_TILE]
        else:
            o_ref[0] = y_t


def _project(x, g, shift, scale, w_tok, w_feat_t, tables, tok_segs, feat_segs, name):
    bsz, t, d = x.shape
    rows = min(t, PROJ_ROWS)
    cos_tab, sin_tab, cos_t, sin_t = tables
    out_shapes, out_specs = [], []
    for seg in tok_segs:
        if seg.head_width == 0:
            out_shapes.append(jax.ShapeDtypeStruct((bsz, t, seg.width), seg.dtype))
            out_specs.append(pl.BlockSpec((1, rows, seg.width), lambda b, i: (b, i, 0)))
        else:
            nh = seg.width // seg.head_width
            out_shapes.append(jax.ShapeDtypeStruct((bsz, nh, t, seg.head_width), seg.dtype))
            out_specs.append(pl.BlockSpec((1, nh, rows, seg.head_width), lambda b, i: (b, 0, i, 0)))
    for seg in feat_segs:
        if seg.key_tiled:
            out_shapes.append(jax.ShapeDtypeStruct((bsz, t // KEY_TILE, seg.rows, KEY_TILE), seg.dtype))
            out_specs.append(pl.BlockSpec((1, rows // KEY_TILE, seg.rows, KEY_TILE), lambda b, i: (b, i, 0, 0)))
        else:
            out_shapes.append(jax.ShapeDtypeStruct((bsz, seg.rows, t), seg.dtype))
            out_specs.append(pl.BlockSpec((1, seg.rows, rows), lambda b, i: (b, 0, i)))
    vec = pl.BlockSpec((1, 1, d), lambda b, i: (b, 0, 0))
    tab = pl.BlockSpec((1, rows, LANES), lambda b, i: (b, i, 0))
    tab_t = pl.BlockSpec((1, ROPE_HALF, rows), lambda b, i: (b, 0, i))
    return pl.pallas_call(
        functools.partial(_proj_kernel, tok_segs=tuple(tok_segs), feat_segs=tuple(feat_segs)),
        grid=(bsz, t // rows),
        in_specs=[pl.BlockSpec((1, rows, d), lambda b, i: (b, i, 0)),
                  pl.BlockSpec((1, d), lambda b, i: (0, 0)),
                  vec, vec,
                  pl.BlockSpec(w_tok.shape, lambda b, i: (0, 0)),
                  pl.BlockSpec(w_feat_t.shape, lambda b, i: (0, 0)),
                  tab, tab, tab_t, tab_t],
        out_specs=out_specs,
        out_shape=out_shapes,
        compiler_params=_params("arbitrary", "arbitrary"),
        name=name,
    )(x, g.reshape(1, d), shift, scale, w_tok, w_feat_t, cos_tab, sin_tab, cos_t, sin_t)


def _compress_kernel(r_ref, pa_ref, pb_ref, wa_ref, wb_ref, w2_ref, cos_ref, sin_ref, o_ref, *, is_key):
    r = r_ref[0]
    ncp = r.shape[0]
    a = _dot((r + pa_ref[...]).astype(BF16), wa_ref[...])
    b = _dot((r + pb_ref[...]).astype(BF16), wb_ref[...])
    hid = a + pltpu.roll(b, ncp - 1, 0)
    hid = (hid * _sigmoid(hid)).astype(BF16)
    if is_key:
        y = _dot(hid, w2_ref[...])
        first_half = (_iota(y.shape, 1) & (HEAD_DIM - 1)) < ROPE_HALF
        y = _rope_chunk(y, cos_ref[0], sin_ref[0], first_half).astype(o_ref.dtype)
        for g in range(NSA_KV_GROUPS):
            o_ref[0, g] = y[:, g * HEAD_DIM:(g + 1) * HEAD_DIM]
    else:
        o_ref[0] = lax.dot_general(w2_ref[...], hid, _NT, preferred_element_type=F32).astype(o_ref.dtype)


def _compress(kv, pos_emb, w1, w2, cos_c, sin_c, is_key, name):
    bsz, t, _ = kv.shape
    ncp = t // CMP_STRIDE
    kwid = CMP_STRIDE * NSA_KV_GROUPS * HEAD_DIM
    hid_w = NSA_KV_GROUPS * CMP_HIDDEN
    r = kv.reshape(bsz, ncp, kwid)
    per = CMP_BLOCK // CMP_STRIDE
    w1r = w1.reshape(per, CMP_STRIDE, HEAD_DIM, CMP_HIDDEN)
    zeros = jnp.zeros_like(w1r)
    grp0 = jnp.concatenate([w1r, zeros], axis=-1)
    grp1 = jnp.concatenate([zeros, w1r], axis=-1)
    wbig = jnp.stack([grp0, grp1], axis=2).reshape(per, kwid, hid_w).astype(BF16)
    posr = pos_emb.reshape(per, CMP_STRIDE, 1, HEAD_DIM)
    posbig = jnp.broadcast_to(posr, (per, CMP_STRIDE, NSA_KV_GROUPS, HEAD_DIM)).reshape(per, 1, kwid)
    z2 = jnp.zeros_like(w2)
    w2big = jnp.concatenate([jnp.concatenate([w2, z2], axis=1),
                             jnp.concatenate([z2, w2], axis=1)], axis=0).astype(BF16)
    const = lambda shape: pl.BlockSpec(shape, lambda b: (0,) * len(shape))
    tab = pl.BlockSpec((1, ncp, LANES), lambda b: (b, 0, 0))
    if is_key:
        w2_arg = w2big
        out_spec = pl.BlockSpec((1, NSA_KV_GROUPS, ncp, HEAD_DIM), lambda b: (b, 0, 0, 0))
        out_shape = jax.ShapeDtypeStruct((bsz, NSA_KV_GROUPS, ncp, HEAD_DIM), BF16)
    else:
        w2_arg = w2big.T
        out_spec = pl.BlockSpec((1, NSA_KV_GROUPS * HEAD_DIM, ncp), lambda b: (b, 0, 0))
        out_shape = jax.ShapeDtypeStruct((bsz, NSA_KV_GROUPS * HEAD_DIM, ncp), BF16)
    return pl.pallas_call(
        functools.partial(_compress_kernel, is_key=is_key),
        grid=(bsz,),
        in_specs=[pl.BlockSpec((1, ncp, kwid), lambda b: (b, 0, 0)),
                  const((1, kwid)), const((1, kwid)),
                  const((kwid, hid_w)), const((kwid, hid_w)),
                  const(w2_arg.shape), tab, tab],
        out_specs=out_spec,
        out_shape=out_shape,
        compiler_params=_params("arbitrary"),
        name=name,
    )(r, posbig[0], posbig[1], wbig[0], wbig[1], w2_arg, cos_c, sin_c)


def _softmax_cols(s, bias, any_visible):
    sb = s + bias
    e = jnp.exp2(sb - jnp.max(sb, axis=0, keepdims=True))
    l = jnp.sum(e, axis=0, keepdims=True)
    return e * jnp.where(any_visible, 1.0 / l, 0.0)


def _online_cols(s, v_t, m_old, l_old, acc_old):
    m_new = jnp.maximum(m_old, jnp.max(s, axis=0, keepdims=True))
    alpha = jnp.exp2(m_old - m_new)
    p = jnp.exp2(s - m_new)
    l_new = alpha * l_old + jnp.sum(p, axis=0, keepdims=True)
    acc_new = alpha * acc_old + _dot(v_t, p.astype(BF16))
    return m_new, l_new, acc_new


def _online_stored(s_ref, tile_max, v_t, m_old, l_old, acc_old):
    m_new = jnp.maximum(m_old, tile_max)
    alpha = jnp.exp2(m_old - m_new)
    p = jnp.exp2(s_ref[...] - m_new)
    l_new = alpha * l_old + jnp.sum(p, axis=0, keepdims=True)
    acc_new = alpha * acc_old + _dot(v_t, p.astype(BF16))
    return m_new, l_new, acc_new


def _sweep(n_earlier, produce, consume, finalize, state, alive=None):
    produce(0, 0, True)
    produce(1, 0, False)
    state = consume(0, 0, True, state)

    def pair(i, st):
        produce(0, 2 * i + 1, False)
        st = consume(1, 2 * i, False, st)
        produce(1, 2 * i + 2, False)
        if alive is None:
            return consume(0, 2 * i + 1, False, st)
        return lax.cond(alive(st), lambda s: consume(0, 2 * i + 1, False, s), lambda s: s, st)

    n_pairs = n_earlier >> 1
    odd = (n_earlier & 1) == 1
    if alive is None:
        state = lax.fori_loop(0, n_pairs, pair, state)
    else:
        _, state = lax.while_loop(lambda c: (c[0] < n_pairs) & alive(c[1]),
                                  lambda c: (c[0] + 1, pair(c[0], c[1])), (jnp.int32(0), state))
        odd = odd & alive(state)

    @pl.when(odd)
    def _():
        finalize(consume(1, n_earlier - 1, False, state))

    @pl.when(jnp.logical_not(odd))
    def _():
        finalize(state)


def _score_buffers(chains, qn):
    return [pltpu.VMEM((2, chains, KEY_TILE, qn), F32), pltpu.VMEM((2, chains, 1, qn), F32)]


def _online_init(n, dv, qn):
    return tuple((jnp.full((1, qn), NEG_INF, F32), jnp.zeros((1, qn), F32), jnp.zeros((dv, qn), F32))
                 for _ in range(n))


def _split_bf16(v, terms):
    out, rest = [], v
    for i in range(terms):
        part = rest.astype(BF16)
        out.append(part)
        if i + 1 < terms:
            rest = rest - part.astype(F32)
    return out


def _top_rows(key, n_top):
    n_rows, n_cols = key.shape

    def bisect(i, tau):
        cand = tau | lax.shift_left(jnp.int32(1), jnp.int32(30) - i)
        cnt = jnp.sum(jnp.where(key >= cand, 1.0, 0.0), axis=0, keepdims=True)
        return jnp.where(cnt >= float(n_top), cand, tau)

    tau = lax.fori_loop(0, 31, bisect, jnp.zeros((1, n_cols), I32))
    above = jnp.where(key > tau, 1.0, 0.0)
    equal = jnp.where(key == tau, 1.0, 0.0)
    need = float(n_top) - jnp.sum(above, axis=0, keepdims=True)
    lower = jnp.where(_iota((n_rows, n_rows), 1) < _iota((n_rows, n_rows), 0), 1.0, 0.0).astype(BF16)
    before = _dot(lower, equal.astype(BF16))
    return above + equal * jnp.where(before < need, 1.0, 0.0)

def _nsa_kernel(q_ref, kc_ref, vc_ref, ks_ref, vs_ref, kw_ref, vw_ref, g_ref, ovt_ref, o_ref,
                bias_scr, part_scr, s_scr, mx_scr, *, n_sel):
    qn = q_ref.shape[2]
    ncp = kc_ref.shape[2]
    ns = ovt_ref.shape[0]
    kt_n = KEY_TILE
    tiles_per_q = qn // kt_n
    qi = pl.program_id(2)
    q0 = qi * qn
    reps = range(NSA_REP)
    q_head = lambda r: q_ref[0, r * HEAD_DIM:(r + 1) * HEAD_DIM, :]

    kc = kc_ref[0, 0]
    vc_t = vc_ref[0]
    bias_c = jnp.where((_iota((ncp, qn), 0) * CMP_STRIDE + (CMP_BLOCK - 1)) <= (q0 + _iota((ncp, qn), 1)),
                       0.0, NEG_INF)
    sees_cmp = (q0 + _iota((1, qn), 1)) >= CMP_BLOCK - 1
    o_cmp, p_sum = [], None
    for r in reps:
        p = _softmax_cols(_dot(kc, q_head(r)), bias_c, sees_cmp)
        o_cmp.append(_dot(vc_t, p.astype(BF16)))
        p_sum = p if p_sum is None else p_sum + p
    ovt = ovt_ref[...]
    imp_t = None
    for term in _split_bf16(p_sum, 3):
        part = _dot(ovt, term)
        imp_t = part if imp_t is None else imp_t + part

    blk = _iota((ns, qn), 0)
    cur = (q0 + _iota((ns, qn), 1)) >> SLC_SHIFT
    forced = (blk == 0) | (blk == cur) | (blk == cur - 1)
    imp_bits = jnp.where(imp_t > 0.0, lax.bitcast_convert_type(imp_t, I32), 0)
    key = jnp.where(forced, jnp.int32(2 ** 31 - 1), jnp.where(blk <= cur, imp_bits, -1))

    picked = _top_rows(key, n_sel)
    bias_scr[...] = jnp.where(picked > 0.5, 0.0, NEG_INF)

    blocks_per_tile = kt_n // SLC_BLOCK
    causal = _iota((kt_n, qn), 0) <= _iota((kt_n, qn), 1)

    carry = _online_init(NSA_REP, HEAD_DIM, qn)
    for diagonal in (True, False):
        kt = qi if diagonal else jnp.maximum(qi - WINDOW // kt_n, 0)
        k0 = pl.multiple_of(kt * kt_n, kt_n)
        k = kw_ref[0, 0, pl.ds(k0, kt_n), :]
        v_t = vw_ref[0, kt]
        kp = k0 + _iota((kt_n, qn), 0)
        tq = q0 + _iota((kt_n, qn), 1)
        inside = (kp <= tq) if diagonal else ((kp > tq - WINDOW) & (kp < q0))
        bias_w = jnp.where(inside, 0.0, NEG_INF)
        scores = [_dot(k, q_head(r)) for r in reps]
        carry = tuple(_online_cols(scores[r] + bias_w, v_t, *carry[r]) for r in reps)

    gates = g_ref[0]
    for r in reps:
        _, l, acc = carry[r]
        part_scr[r] = gates[3 * r:3 * r + 1] * o_cmp[r] + gates[3 * r + 2:3 * r + 3] * (acc * (1.0 / l))

    def produce(slot, f, diagonal):
        kt = qi if diagonal else jnp.minimum(f, jnp.maximum(qi - 1, 0))
        k0 = pl.multiple_of(kt * kt_n, kt_n)
        k = ks_ref[0, 0, pl.ds(k0, kt_n), :]
        rows = [jnp.broadcast_to(bias_scr[pl.ds(kt * blocks_per_tile + i, 1), :], (SLC_BLOCK, qn))
                for i in range(blocks_per_tile)]
        bias = jnp.concatenate(rows, axis=0)
        if diagonal:
            bias = jnp.where(causal, bias, NEG_INF)
        for r in reps:
            s = _dot(k, q_head(r)) + bias
            s_scr[slot, r] = s
            mx_scr[slot, r] = jnp.max(s, axis=0, keepdims=True)

    def consume(slot, f, diagonal, state):
        v_t = vs_ref[0, qi if diagonal else f]
        return tuple(_online_stored(s_scr.at[slot, r], mx_scr[slot, r], v_t, *state[r]) for r in reps)

    def finalize(state):
        for r in reps:
            _, l, acc = state[r]
            o = part_scr[r] + g_ref[0, 3 * r + 1:3 * r + 2, :] * (acc * (1.0 / l))
            o_ref[0, r * HEAD_DIM:(r + 1) * HEAD_DIM, :] = o.astype(o_ref.dtype)

    _sweep(qi, produce, consume, finalize, _online_init(NSA_REP, HEAD_DIM, qn))


def _nsa_attention(q_t, k_cmp, v_cmp_t, k_slc, v_slc_t, k_win, v_win_t, gates_t):
    bsz, _, t = q_t.shape
    ncp = k_cmp.shape[2]
    ns = t // SLC_BLOCK
    qn = min(Q_TILE, t)
    grp_w = NSA_REP * HEAD_DIM
    cmp_start = jnp.arange(ncp) * CMP_STRIDE
    slc_start = jnp.arange(ns) * SLC_BLOCK
    real = jnp.arange(ncp) < (t - CMP_BLOCK) // CMP_STRIDE + 1
    overlap_t = ((cmp_start[None, :] < slc_start[:, None] + SLC_BLOCK)
                 & (cmp_start[None, :] + CMP_BLOCK > slc_start[:, None]) & real[None, :]).astype(BF16)
    k_spec = lambda n: pl.BlockSpec((1, 1, n, HEAD_DIM), lambda b, g, i: (b, g, 0, 0))
    v_spec = pl.BlockSpec((1, t // KEY_TILE, HEAD_DIM, KEY_TILE), lambda b, g, i: (b, 0, g, 0))
    return pl.pallas_call(
        functools.partial(_nsa_kernel, n_sel=min(N_SELECT, ns)),
        grid=(bsz, NSA_KV_GROUPS, t // qn),
        in_specs=[pl.BlockSpec((1, grp_w, qn), lambda b, g, i: (b, g, i)),
                  k_spec(ncp),
                  pl.BlockSpec((1, HEAD_DIM, ncp), lambda b, g, i: (b, g, 0)),
                  k_spec(t), v_spec, k_spec(t), v_spec,
                  pl.BlockSpec((1, GATE_ROWS, qn), lambda b, g, i: (b, g, i)),
                  pl.BlockSpec((ns, ncp), lambda b, g, i: (0, 0))],
        out_specs=pl.BlockSpec((1, grp_w, qn), lambda b, g, i: (b, g, i)),
        out_shape=jax.ShapeDtypeStruct((bsz, NSA_HEADS * HEAD_DIM, t), BF16),
        scratch_shapes=[pltpu.VMEM((ns, qn), F32), pltpu.VMEM((NSA_REP, HEAD_DIM, qn), F32)] + _score_buffers(NSA_REP, qn),
        compiler_params=_params("arbitrary", "arbitrary", "arbitrary"),
        name="nsa_attention",
    )(q_t, k_cmp, v_cmp_t, k_slc, v_slc_t, k_win, v_win_t, gates_t, overlap_t)


SB_HEADS_PER_STEP = 2
SB_DEAD_TAIL = -110.0
SB_SCAN_BLOCK = 128


def _sb_kernel(q_ref, k_ref, v_ref, o_ref, nz_scr):
    qn = q_ref.shape[2]
    kt_n = KEY_TILE
    qi = pl.program_id(2)
    heads = range(SB_HEADS_PER_STEP)
    rows = [slice(h * HEAD_DIM, (h + 1) * HEAD_DIM) for h in heads]
    sub = SB_SCAN_BLOCK
    n_sub = kt_n // sub
    tri = jnp.where(_iota((sub, sub), 1) > _iota((sub, sub), 0), 1.0, 0.0).astype(BF16)
    later2 = jnp.concatenate([tri, tri], axis=1)
    strictly_before = _iota((kt_n, qn), 0) < _iota((kt_n, qn), 1)

    def tile_index(f, diagonal):
        return qi if diagonal else jnp.maximum(qi - 1 - f, 0)

    def produce(slot, f, diagonal):
        k0 = pl.multiple_of(tile_index(f, diagonal) * kt_n, kt_n)
        for h in heads:
            nz_scr[slot, h] = _dot(k_ref[0, h, pl.ds(k0, kt_n), :], -q_ref[0, rows[h], :])

    def consume(slot, f, diagonal, state):
        kt = tile_index(f, diagonal)
        masked = diagonal
        mask = strictly_before
        out = []
        for h in heads:
            tail, acc = state[h]
            nz = nz_scr[slot, h]
            neg_abs = lax.bitcast_convert_type(lax.bitcast_convert_type(nz, jnp.uint32) | jnp.uint32(0x80000000), F32)
            soft = jnp.log(1.0 + jnp.exp(neg_abs))
            log_keep = jnp.minimum(nz, 0.0) - soft
            log_beta = log_keep - nz
            if masked:
                log_keep = jnp.where(mask, log_keep, 0.0)
            offs = tail
            parts = [None] * n_sub
            for blk in reversed(range(n_sub)):
                lk = log_keep[blk * sub:(blk + 1) * sub]
                hi = lk.astype(BF16)
                lo = (lk - hi.astype(F32)).astype(BF16)
                within = _dot(later2, jnp.concatenate([hi, lo], axis=0))
                parts[blk] = log_beta[blk * sub:(blk + 1) * sub] + within + offs
                offs = offs + jnp.sum(lk, axis=0, keepdims=True)
            a = jnp.exp(jnp.concatenate(parts, axis=0))
            if masked:
                a = jnp.where(mask, a, 0.0)
            out.append((offs, acc + _dot(v_ref[0, kt, rows[h], :], a.astype(BF16))))
        return tuple(out)

    def finalize(state):
        for h in heads:
            o_ref[0, rows[h], :] = state[h][1].astype(o_ref.dtype)

    def alive(state):
        tails = [state[h][0] for h in heads]
        return jnp.max(functools.reduce(jnp.maximum, tails)) > SB_DEAD_TAIL

    init = tuple((jnp.zeros((1, qn), F32), jnp.zeros((HEAD_DIM, qn), F32)) for _ in heads)
    _sweep(qi, produce, consume, finalize, init, alive)


def _sb_attention(q_t, k, v_t):
    bsz, nh, t, _ = k.shape
    qn = min(Q_TILE, t)
    hps = SB_HEADS_PER_STEP
    w = hps * HEAD_DIM
    return pl.pallas_call(
        _sb_kernel,
        grid=(bsz, nh // hps, t // qn),
        in_specs=[pl.BlockSpec((1, w, qn), lambda b, h, i: (b, h, i)),
                  pl.BlockSpec((1, hps, t, HEAD_DIM), lambda b, h, i: (b, h, 0, 0)),
                  pl.BlockSpec((1, t // KEY_TILE, w, KEY_TILE), lambda b, h, i: (b, 0, h, 0))],
        out_specs=pl.BlockSpec((1, w, qn), lambda b, h, i: (b, h, i)),
        out_shape=jax.ShapeDtypeStruct((bsz, nh * HEAD_DIM, t), BF16),
        scratch_shapes=[pltpu.VMEM((2, hps, KEY_TILE, qn), F32)],
        compiler_params=_params("arbitrary", "arbitrary", "arbitrary"),
        name="stickbreak_attention",
    )(q_t, k, v_t)


DIFF_HEADS_PER_STEP = 2


def _diff_kernel(q_ref, k_ref, v_ref, lq1_ref, lk1_ref, lq2_ref, lk2_ref, sub_ref, o_ref, s_scr, mx_scr,
                 *, lambda_init):
    qn = q_ref.shape[2]
    vd = 2 * HEAD_DIM
    kt_n = KEY_TILE
    qi = pl.program_id(2)
    chains = range(2 * DIFF_HEADS_PER_STEP)
    lam = (jnp.exp(jnp.sum(lq1_ref[...] * lk1_ref[...], axis=1, keepdims=True))
           - jnp.exp(jnp.sum(lq2_ref[...] * lk2_ref[...], axis=1, keepdims=True)) + lambda_init)

    causal = _iota((kt_n, qn), 0) <= _iota((kt_n, qn), 1)

    def produce(slot, f, diagonal):
        kt = qi if diagonal else jnp.minimum(f, jnp.maximum(qi - 1, 0))
        k0 = pl.multiple_of(kt * kt_n, kt_n)
        for c in chains:
            s = _dot(k_ref[0, c, pl.ds(k0, kt_n), :], q_ref[0, c * HEAD_DIM:(c + 1) * HEAD_DIM, :])
            if diagonal:
                s = jnp.where(causal, s, NEG_INF)
            s_scr[slot, c] = s
            mx_scr[slot, c] = jnp.max(s, axis=0, keepdims=True)

    def consume(slot, f, diagonal, state):
        kt = qi if diagonal else f
        return tuple(_online_stored(s_scr.at[slot, c], mx_scr[slot, c],
                                    v_ref[0, kt, (c // 2) * vd:(c // 2 + 1) * vd, :], *state[c]) for c in chains)

    def finalize(state):
        for h in range(DIFF_HEADS_PER_STEP):
            (_, l1, a1), (_, l2, a2) = state[2 * h], state[2 * h + 1]
            o = a1 * (1.0 / l1) - lam * (a2 * (1.0 / l2))
            ms = jnp.mean(o * o, axis=0, keepdims=True)
            o = o * lax.rsqrt(ms + RMS_EPS) * sub_ref[...] * (1.0 - lambda_init)
            o_ref[0, h * vd:(h + 1) * vd, :] = o.astype(o_ref.dtype)

    _sweep(qi, produce, consume, finalize, _online_init(len(chains), vd, qn))


def _diff_attention(q_t, k, v_t, lq1, lk1, lq2, lk2, subln, lambda_init):
    bsz, nh2, t, _ = k.shape
    nh = nh2 // 2
    vd = 2 * HEAD_DIM
    hps = DIFF_HEADS_PER_STEP
    qn = min(Q_TILE, t)
    row = lambda a: a.reshape(1, -1)
    small = lambda n: pl.BlockSpec((1, n), lambda b, h, i: (0, 0))
    return pl.pallas_call(
        functools.partial(_diff_kernel, lambda_init=lambda_init),
        grid=(bsz, nh // hps, t // qn),
        in_specs=[pl.BlockSpec((1, hps * vd, qn), lambda b, h, i: (b, h, i)),
                  pl.BlockSpec((1, 2 * hps, t, HEAD_DIM), lambda b, h, i: (b, h, 0, 0)),
                  pl.BlockSpec((1, t // KEY_TILE, hps * vd, KEY_TILE), lambda b, h, i: (b, 0, h, 0)),
                  small(HEAD_DIM), small(HEAD_DIM), small(HEAD_DIM), small(HEAD_DIM),
                  pl.BlockSpec((vd, 1), lambda b, h, i: (0, 0))],
        out_specs=pl.BlockSpec((1, hps * vd, qn), lambda b, h, i: (b, h, i)),
        out_shape=jax.ShapeDtypeStruct((bsz, nh * vd, t), BF16),
        scratch_shapes=_score_buffers(2 * hps, qn),
        compiler_params=_params("arbitrary", "arbitrary", "arbitrary"),
        name="diff_attention",
    )(q_t, k, v_t, row(lq1), row(lk1), row(lq2), row(lk2), subln.reshape(vd, 1))


def _outproj_kernel(*refs, n_in):
    a_refs, w_refs = refs[:n_in], refs[n_in:2 * n_in]
    x_ref, gate_ref, o_ref = refs[2 * n_in:]
    y = None
    for a_ref, w_ref in zip(a_refs, w_refs):
        part = lax.dot_general(a_ref[0], w_ref[...], _TN, preferred_element_type=F32)
        y = part if y is None else y + part
    o_ref[0] = x_ref[0] + gate_ref[0] * y


def _out_project(acts_t, weights, x, gate, name):
    bsz, t, d = x.shape
    rows = min(t, PROJ_ROWS)
    tile = pl.BlockSpec((1, rows, d), lambda b, i: (b, i, 0))
    return pl.pallas_call(
        functools.partial(_outproj_kernel, n_in=len(acts_t)),
        grid=(bsz, t // rows),
        in_specs=([pl.BlockSpec((1, a.shape[1], rows), lambda b, i: (b, 0, i)) for a in acts_t]
                  + [pl.BlockSpec(w.shape, lambda b, i: (0, 0)) for w in weights]
                  + [tile, pl.BlockSpec((1, 1, d), lambda b, i: (b, 0, 0))]),
        out_specs=tile,
        out_shape=jax.ShapeDtypeStruct((bsz, t, d), F32),
        compiler_params=_params("arbitrary", "arbitrary"),
        name=name,
    )(*acts_t, *weights, x, gate)


def _ffn_kernel(x_ref, halo_ref, g_ref, sh_ref, sc_ref, gate_ref, wg_ref, wu_ref, cw_ref, cb_ref, wd_ref,
                nf_ref, o_ref, a_scr, *, final_norm):
    ti = pl.program_id(1)
    h = _norm_modulate(x_ref[0], g_ref[...], sh_ref[0], sc_ref[0]).astype(BF16)
    h_prev = _norm_modulate(halo_ref[0], g_ref[...], sh_ref[0], sc_ref[0]).astype(BF16)
    seen = jnp.where(ti > 0, 1.0, 0.0)
    cols = FFN_COLS
    row = _iota((h.shape[0], cols), 0)
    for c in range(a_scr.shape[1] // cols):
        sl = slice(c * cols, (c + 1) * cols)
        wg = wg_ref[:, sl]
        g = _dot(h, wg)
        g_prev = _dot(h_prev, wg) * seen
        last = g_prev[HALO_ROWS - 1:HALO_ROWS]
        last2 = g_prev[HALO_ROWS - 2:HALO_ROWS - 1]
        g1 = jnp.where(row == 0, last, pltpu.roll(g, 1, 0))
        g2 = jnp.where(row == 0, last2, jnp.where(row == 1, last, pltpu.roll(g, 2, 0)))
        cw = cw_ref[:, sl]
        conv = cw[0:1] * g2 + cw[1:2] * g1 + cw[2:3] * g + cb_ref[:, sl]
        a_scr[:, sl] = (conv * _sigmoid(conv) * _dot(h, wu_ref[:, sl])).astype(BF16)
    y = x_ref[0] + gate_ref[0] * _dot(a_scr[...], wd_ref[...])
    if final_norm:
        ms = jnp.mean(y * y, axis=-1, keepdims=True)
        y = y * lax.rsqrt(ms + RMS_EPS) * nf_ref[...]
    o_ref[0] = y


def _conv_ffn(x, g, shift, scale, gate, w_gate, w_up, conv_w, conv_b, w_down, norm_f, final_norm, name):
    bsz, t, d = x.shape
    f = w_gate.shape[1]
    rows = min(t, FFN_ROWS)
    halo_blocks = rows // HALO_ROWS
    vec = pl.BlockSpec((1, 1, d), lambda b, i: (b, 0, 0))
    drow = pl.BlockSpec((1, d), lambda b, i: (0, 0))
    resident = lambda shape: pl.BlockSpec(shape, lambda b, i: (0, 0), pipeline_mode=pl.Buffered(1))
    return pl.pallas_call(
        functools.partial(_ffn_kernel, final_norm=final_norm),
        grid=(bsz, t // rows),
        in_specs=[pl.BlockSpec((1, rows, d), lambda b, i: (b, i, 0)),
                  pl.BlockSpec((1, HALO_ROWS, d), lambda b, i: (b, jnp.maximum(i * halo_blocks - 1, 0), 0)),
                  drow, vec, vec, vec,
                  resident((d, f)), resident((d, f)), resident((CONV_WIDTH, f)), resident((1, f)),
                  resident((f, d)), drow],
        out_specs=pl.BlockSpec((1, rows, d), lambda b, i: (b, i, 0)),
        out_shape=jax.ShapeDtypeStruct((bsz, t, d), F32),
        scratch_shapes=[pltpu.VMEM((rows, f), BF16)],
        compiler_params=_params("arbitrary", "arbitrary"),
        name=name,
    )(x, x, g.reshape(1, d), shift, scale, gate, w_gate.astype(BF16), w_up.astype(BF16), conv_w,
      conv_b.reshape(1, f), w_down.astype(BF16), norm_f.reshape(1, d))


def _layout(widths):
    starts, start = [], 0
    for w in widths:
        starts.append(start)
        start += w
    return starts


def _hybrid_layer(x, mods, tables, norm_g, w_in, pos_k, pos_v, ck_w1, ck_w2, cv_w1, cv_w2, w_out):
    shift, scale, gate = mods
    bsz, t, d = x.shape
    qw, kvw, sbw = NSA_HEADS * HEAD_DIM, NSA_KV_GROUPS * HEAD_DIM, SB_HEADS * HEAD_DIM
    col = _layout((qw,) + (kvw,) * 6 + (3 * NSA_HEADS,) + (sbw,) * 3)
    cols = lambda i, w: w_in[:, col[i]:col[i] + w]
    q_n, kc, vc, ks, vs, kw, vw, gl, q_s, k_s, v_s = (cols(i, w) for i, w in enumerate(
        (qw,) + (kvw,) * 6 + (3 * NSA_HEADS,) + (sbw,) * 3))
    inv_sqrt = HEAD_DIM ** -0.5

    w_tok = jnp.concatenate([ks, kw, kc, vc, k_s], axis=1).astype(BF16)
    ts = _layout((kvw, kvw, kvw, kvw, sbw))
    tok_segs = [TokSeg(ts[0], kvw, True, HEAD_DIM, BF16),
                TokSeg(ts[1], kvw, True, HEAD_DIM, BF16),
                TokSeg(ts[2], kvw, False, 0, F32),
                TokSeg(ts[3], kvw, False, 0, F32),
                TokSeg(ts[4], sbw, False, HEAD_DIM, BF16)]
    per_group = 3 * NSA_REP
    gate_pad = jnp.zeros((d, GATE_ROWS - per_group), w_in.dtype)
    gl_pad = jnp.concatenate([gl[:, :per_group], gate_pad, gl[:, per_group:], gate_pad], axis=1)
    w_feat_t = jnp.concatenate([q_n, q_s, vs, vw, v_s, gl_pad], axis=1).T.astype(BF16)
    fs = _layout((qw, sbw, kvw, kvw, sbw, NSA_KV_GROUPS * GATE_ROWS))
    feat_segs = [FeatSeg(fs[0], qw, True, inv_sqrt * LOG2E, False, False, BF16),
                 FeatSeg(fs[1], sbw, False, inv_sqrt, False, False, BF16),
                 FeatSeg(fs[2], kvw, False, 1.0, False, True, BF16),
                 FeatSeg(fs[3], kvw, False, 1.0, False, True, BF16),
                 FeatSeg(fs[4], sbw, False, 1.0, False, True, BF16),
                 FeatSeg(fs[5], NSA_KV_GROUPS * GATE_ROWS, False, 1.0, True, False, F32)]
    (k_slc, k_win, kc_a, vc_a, k_sb, q_n_t, q_s_t, v_slc_t, v_win_t, v_sb_t, gates_t) = _project(
        x, norm_g, shift, scale, w_tok, w_feat_t, tables, tok_segs, feat_segs, "hybrid_in_proj")

    ncp = t // CMP_STRIDE
    end_rows = jnp.minimum(jnp.arange(ncp) * CMP_STRIDE + CMP_BLOCK - 1, t - 1)
    cos_c, sin_c = tables[0][:, end_rows], tables[1][:, end_rows]
    k_cmp = _compress(kc_a, pos_k, ck_w1, ck_w2, cos_c, sin_c, True, "compress_k")
    v_cmp_t = _compress(vc_a, pos_v, cv_w1, cv_w2, cos_c, sin_c, False, "compress_v")
    o_nsa_t = _nsa_attention(q_n_t, k_cmp, v_cmp_t, k_slc, v_slc_t, k_win, v_win_t, gates_t)
    o_sb_t = _sb_attention(q_s_t, k_sb, v_sb_t)
    w_out_b = w_out.astype(BF16)
    return _out_project([o_nsa_t, o_sb_t], [w_out_b[:qw], w_out_b[qw:]], x, gate, "hybrid_out_proj")


def _diff_layer(x, mods, tables, norm_g, w_qkv, lq1, lk1, lq2, lk2, subln, w_out, layer_idx):
    shift, scale, gate = mods
    dw = 2 * DIFF_HEADS * HEAD_DIM
    inv_sqrt = HEAD_DIM ** -0.5
    w_tok = w_qkv[:, dw:2 * dw].astype(BF16)
    w_feat_t = jnp.concatenate([w_qkv[:, :dw], w_qkv[:, 2 * dw:]], axis=1).T.astype(BF16)
    tok_segs = [TokSeg(0, dw, True, HEAD_DIM, BF16)]
    feat_segs = [FeatSeg(0, dw, True, inv_sqrt * LOG2E, False, False, BF16),
                 FeatSeg(dw, dw, False, 1.0, False, True, BF16)]
    k, q_t, v_t = _project(x, norm_g, shift, scale, w_tok, w_feat_t, tables, tok_segs, feat_segs, "diff_in_proj")
    lambda_init = 0.8 - 0.6 * math.exp(-0.3 * layer_idx)
    o_t = _diff_attention(q_t, k, v_t, lq1, lk1, lq2, lk2, subln, lambda_init)
    return _out_project([o_t], [w_out.astype(BF16)], x, gate, "diff_out_proj")


def kernel(x, c, positions, mod_w, mod_b, norm_mix, norm_ffn, ffn_w_gate, ffn_w_up, ffn_conv_w, ffn_conv_b, ffn_w_down, hyb_w_in, nsa_pos_k, nsa_pos_v, nsa_ck_w1, nsa_ck_w2, nsa_cv_w1, nsa_cv_w2, hyb_w_out, diff_w_qkv, diff_lq1, diff_lk1, diff_lq2, diff_lk2, diff_subln, diff_w_out, norm_f):
    bsz, t, d = x.shape
    depth = mod_w.shape[0]
    mod = _adaln_mod(c, mod_w, mod_b)
    tables = _rope_tables(positions)
    for i in range(depth):
        sh_m, sc_m, g_m, sh_f, sc_f, g_f = (mod[i, :, k * d:(k + 1) * d].reshape(bsz, 1, d) for k in range(6))
        j = i // 2
        if i % 2 == 0:
            x = _hybrid_layer(x, (sh_m, sc_m, g_m), tables, norm_mix[i], hyb_w_in[j],
                              nsa_pos_k[j], nsa_pos_v[j], nsa_ck_w1[j], nsa_ck_w2[j], nsa_cv_w1[j],
                              nsa_cv_w2[j], hyb_w_out[j])
        else:
            x = _diff_layer(x, (sh_m, sc_m, g_m), tables, norm_mix[i], diff_w_qkv[j], diff_lq1[j],
                            diff_lk1[j], diff_lq2[j], diff_lk2[j], diff_subln[j], diff_w_out[j], i)
        x = _conv_ffn(x, norm_ffn[i], sh_f, sc_f, g_f, ffn_w_gate[i], ffn_w_up[i], ffn_conv_w[i],
                      ffn_conv_b[i], ffn_w_down[i], norm_f, i == depth - 1, "conv_ffn_%d" % i)
    return x
```

```python
import functools
import math
from typing import NamedTuple

import jax
import jax.numpy as jnp
from jax import lax
from jax.experimental import pallas as pl
from jax.experimental.pallas import tpu as pltpu

F32 = jnp.float32
BF16 = jnp.bfloat16
I32 = jnp.int32

HEAD_DIM = 64
ROPE_DIM = HEAD_DIM // 4
ROPE_HALF = ROPE_DIM // 2
ROPE_THETA = 500000.0
NSA_HEADS = 8
NSA_KV_GROUPS = 2
NSA_REP = NSA_HEADS // NSA_KV_GROUPS
CMP_BLOCK = 32
CMP_STRIDE = 16
CMP_HIDDEN = 4 * HEAD_DIM
SLC_BLOCK = 64
SLC_SHIFT = 6
N_SELECT = 16
WINDOW = 512
SB_HEADS = 8
DIFF_HEADS = 8
CONV_WIDTH = 3
RMS_EPS = 1e-6
NEG_INF = -1e30
FORCE = 1e9
LOG2E = math.log2(math.e)

LANES = 128
Q_TILE = 512
KEY_TILE = 512
GATE_ROWS = 16
PROJ_ROWS = 512
FFN_ROWS = 512
FFN_COLS = 256
HALO_ROWS = 16
VMEM_LIMIT = 56 * 1024 * 1024

_NT = (((1,), (1,)), ((), ()))
_TN = (((0,), (0,)), ((), ()))


def _params(*sem):
    return pltpu.CompilerParams(dimension_semantics=sem, vmem_limit_bytes=VMEM_LIMIT)


def _sigmoid(v):
    return 1.0 / (1.0 + jnp.exp(-v))


def _iota(shape, axis):
    return lax.broadcasted_iota(I32, shape, axis)


def _dot(a, b):
    return jnp.dot(a, b, preferred_element_type=F32)


def _mod_kernel(c_ref, w_ref, b_ref, o_ref):
    c = c_ref[...]
    cond = c * _sigmoid(c)
    o_ref[0] = jnp.dot(cond, w_ref[0], preferred_element_type=F32,
                       precision=lax.Precision.HIGHEST) + b_ref[0]


def _adaln_mod(c, mod_w, mod_b):
    depth, d, n = mod_w.shape
    bsz = c.shape[0]
    tn = n // 4
    return pl.pallas_call(
        _mod_kernel,
        grid=(depth, n // tn),
        in_specs=[pl.BlockSpec((bsz, d), lambda i, j: (0, 0)),
                  pl.BlockSpec((1, d, tn), lambda i, j: (i, 0, j)),
                  pl.BlockSpec((1, 1, tn), lambda i, j: (i, 0, j))],
        out_specs=pl.BlockSpec((1, bsz, tn), lambda i, j: (i, 0, j)),
        out_shape=jax.ShapeDtypeStruct((depth, bsz, n), F32),
        compiler_params=_params("arbitrary", "arbitrary"),
        name="adaln_mod",
    )(c, mod_w, mod_b.reshape(depth, 1, n))


def _rope_kernel(pos_col_ref, pos_row_ref, inv_row_ref, sgn_row_ref, inv_col_ref,
                 cos_ref, sin_ref, cos_t_ref, sin_t_ref):
    ang = pos_col_ref[0].astype(F32) * inv_row_ref[...]
    cos_ref[0] = jnp.cos(ang)
    sin_ref[0] = jnp.sin(ang) * sgn_row_ref[...]
    ang_t = inv_col_ref[...] * pos_row_ref[0].astype(F32)
    cos_t_ref[0] = jnp.cos(ang_t)
    sin_t_ref[0] = jnp.sin(ang_t)


def _rope_tables(positions):
    bsz, t = positions.shape
    inv = ROPE_THETA ** (-jnp.arange(0, ROPE_DIM, 2, dtype=F32) / ROPE_DIM)
    per_head_inv = jnp.concatenate([inv, inv, jnp.zeros((HEAD_DIM - ROPE_DIM,), F32)])
    per_head_sgn = jnp.concatenate([-jnp.ones((ROPE_HALF,), F32), jnp.ones((ROPE_HALF,), F32),
                                    jnp.zeros((HEAD_DIM - ROPE_DIM,), F32)])
    inv_row = jnp.tile(per_head_inv, LANES // HEAD_DIM)[None, :]
    sgn_row = jnp.tile(per_head_sgn, LANES // HEAD_DIM)[None, :]
    rows = min(t, 1024)
    tab = jax.ShapeDtypeStruct((bsz, t, LANES), F32)
    tab_t = jax.ShapeDtypeStruct((bsz, ROPE_HALF, t), F32)
    row_spec = pl.BlockSpec((1, LANES), lambda b, i: (0, 0))
    return pl.pallas_call(
        _rope_kernel,
        grid=(bsz, t // rows),
        in_specs=[pl.BlockSpec((1, rows, 1), lambda b, i: (b, i, 0)),
                  pl.BlockSpec((1, 1, rows), lambda b, i: (b, 0, i)),
                  row_spec, row_spec,
                  pl.BlockSpec((ROPE_HALF, 1), lambda b, i: (0, 0))],
        out_specs=[pl.BlockSpec((1, rows, LANES), lambda b, i: (b, i, 0))] * 2
        + [pl.BlockSpec((1, ROPE_HALF, rows), lambda b, i: (b, 0, i))] * 2,
        out_shape=[tab, tab, tab_t, tab_t],
        compiler_params=_params("arbitrary", "arbitrary"),
        name="rope_tables",
    )(positions.reshape(bsz, t, 1), positions.reshape(bsz, 1, t), inv_row, sgn_row, inv[:, None])


def _rope_chunk(y, cosv, sinv, first_half):
    ahead = pltpu.roll(y, LANES - ROPE_HALF, 1)
    behind = pltpu.roll(y, ROPE_HALF, 1)
    return y * cosv + jnp.where(first_half, ahead, behind) * sinv


def _rope_rows(y_t, cos_t, sin_t):
    heads = []
    for h in range(y_t.shape[0] // HEAD_DIM):
        blk = y_t[h * HEAD_DIM:(h + 1) * HEAD_DIM]
        x1, x2 = blk[:ROPE_HALF], blk[ROPE_HALF:ROPE_DIM]
        heads += [x1 * cos_t - x2 * sin_t, x2 * cos_t + x1 * sin_t, blk[ROPE_DIM:]]
    return jnp.concatenate(heads, axis=0)


class TokSeg(NamedTuple):
    start: int
    width: int
    rope: bool
    head_width: int
    dtype: object


class FeatSeg(NamedTuple):
    start: int
    rows: int
    rope: bool
    scale: float
    sigmoid: bool
    key_tiled: bool
    dtype: object


def _norm_modulate(x, g, shift, scale):
    ms = jnp.mean(x * x, axis=-1, keepdims=True)
    y = x * lax.rsqrt(ms + RMS_EPS) * g
    return y * (1.0 + scale) + shift


def _proj_kernel(x_ref, g_ref, sh_ref, sc_ref, w_ref, wt_ref, cos_ref, sin_ref, cos_t_ref, sin_t_ref,
                 *out_refs, tok_segs, feat_segs):
    hb = _norm_modulate(x_ref[0], g_ref[...], sh_ref[0], sc_ref[0]).astype(BF16)
    cosv, sinv = cos_ref[0], sin_ref[0]
    first_half = (_iota(cosv.shape, 1) & (HEAD_DIM - 1)) < ROPE_HALF
    tok_refs, feat_refs = out_refs[:len(tok_segs)], out_refs[len(tok_segs):]
    for seg, o_ref in zip(tok_segs, tok_refs):
        y = _dot(hb, w_ref[:, seg.start:seg.start + seg.width])
        for ch in range(seg.width // LANES):
            yc = y[:, ch * LANES:(ch + 1) * LANES]
            if seg.rope:
                yc = _rope_chunk(yc, cosv, sinv, first_half)
            yc = yc.astype(seg.dtype)
            if seg.head_width == 0:
                o_ref[0, :, ch * LANES:(ch + 1) * LANES] = yc
            else:
                o_ref[0, 2 * ch] = yc[:, :HEAD_DIM]
                o_ref[0, 2 * ch + 1] = yc[:, HEAD_DIM:]
    for seg, o_ref in zip(feat_segs, feat_refs):
        y_t = lax.dot_general(wt_ref[seg.start:seg.start + seg.rows, :], hb, _NT,
                              preferred_element_type=F32)
        if seg.rope:
            y_t = _rope_rows(y_t, cos_t_ref[0], sin_t_ref[0])
        if seg.scale != 1.0:
            y_t = y_t * seg.scale
        if seg.sigmoid:
            y_t = _sigmoid(y_t)
        y_t = y_t.astype(seg.dtype)
        if seg.key_tiled:
            for ch in range(y_t.shape[1] // KEY_TILE):
                o_ref[0, ch] = y_t[:, ch * KEY_TILE:(ch + 1) * KEY_TILE]
        else:
            o_ref[0] = y_t


def _project(x, g, shift, scale, w_tok, w_feat_t, tables, tok_segs, feat_segs, name):
    bsz, t, d = x.shape
    rows = min(t, PROJ_ROWS)
    cos_tab, sin_tab, cos_t, sin_t = tables
    out_shapes, out_specs = [], []
    for seg in tok_segs:
        if seg.head_width == 0:
            out_shapes.append(jax.ShapeDtypeStruct((bsz, t, seg.width), seg.dtype))
            out_specs.append(pl.BlockSpec((1, rows, seg.width), lambda b, i: (b, i, 0)))
        else:
            nh = seg.width // seg.head_width
            out_shapes.append(jax.ShapeDtypeStruct((bsz, nh, t, seg.head_width), seg.dtype))
            out_specs.append(pl.BlockSpec((1, nh, rows, seg.head_width), lambda b, i: (b, 0, i, 0)))
    for seg in feat_segs:
        if seg.key_tiled:
            out_shapes.append(jax.ShapeDtypeStruct((bsz, t // KEY_TILE, seg.rows, KEY_TILE), seg.dtype))
            out_specs.append(pl.BlockSpec((1, rows // KEY_TILE, seg.rows, KEY_TILE), lambda b, i: (b, i, 0, 0)))
        else:
            out_shapes.append(jax.ShapeDtypeStruct((bsz, seg.rows, t), seg.dtype))
            out_specs.append(pl.BlockSpec((1, seg.rows, rows), lambda b, i: (b, 0, i)))
    vec = pl.BlockSpec((1, 1, d), lambda b, i: (b, 0, 0))
    tab = pl.BlockSpec((1, rows, LANES), lambda b, i: (b, i, 0))
    tab_t = pl.BlockSpec((1, ROPE_HALF, rows), lambda b, i: (b, 0, i))
    return pl.pallas_call(
        functools.partial(_proj_kernel, tok_segs=tuple(tok_segs), feat_segs=tuple(feat_segs)),
        grid=(bsz, t // rows),
        in_specs=[pl.BlockSpec((1, rows, d), lambda b, i: (b, i, 0)),
                  pl.BlockSpec((1, d), lambda b, i: (0, 0)),
                  vec, vec,
                  pl.BlockSpec(w_tok.shape, lambda b, i: (0, 0)),
                  pl.BlockSpec(w_feat_t.shape, lambda b, i: (0, 0)),
                  tab, tab, tab_t, tab_t],
        out_specs=out_specs,
        out_shape=out_shapes,
        compiler_params=_params("arbitrary", "arbitrary"),
        name=name,
    )(x, g.reshape(1, d), shift, scale, w_tok, w_feat_t, cos_tab, sin_tab, cos_t, sin_t)


def _compress_kernel(r_ref, pa_ref, pb_ref, wa_ref, wb_ref, w2_ref, cos_ref, sin_ref, o_ref, *, is_key):
    r = r_ref[0]
    ncp = r.shape[0]
    a = _dot((r + pa_ref[...]).astype(BF16), wa_ref[...])
    b = _dot((r + pb_ref[...]).astype(BF16), wb_ref[...])
    hid = a + pltpu.roll(b, ncp - 1, 0)
    hid = (hid * _sigmoid(hid)).astype(BF16)
    if is_key:
        y = _dot(hid, w2_ref[...])
        first_half = (_iota(y.shape, 1) & (HEAD_DIM - 1)) < ROPE_HALF
        y = _rope_chunk(y, cos_ref[0], sin_ref[0], first_half).astype(o_ref.dtype)
        for g in range(NSA_KV_GROUPS):
            o_ref[0, g] = y[:, g * HEAD_DIM:(g + 1) * HEAD_DIM]
    else:
        o_ref[0] = lax.dot_general(w2_ref[...], hid, _NT, preferred_element_type=F32).astype(o_ref.dtype)


def _compress(kv, pos_emb, w1, w2, cos_c, sin_c, is_key, name):
    bsz, t, _ = kv.shape
    ncp = t // CMP_STRIDE
    kwid = CMP_STRIDE * NSA_KV_GROUPS * HEAD_DIM
    hid_w = NSA_KV_GROUPS * CMP_HIDDEN
    r = kv.reshape(bsz, ncp, kwid)
    per = CMP_BLOCK // CMP_STRIDE
    w1r = w1.reshape(per, CMP_STRIDE, HEAD_DIM, CMP_HIDDEN)
    zeros = jnp.zeros_like(w1r)
    grp0 = jnp.concatenate([w1r, zeros], axis=-1)
    grp1 = jnp.concatenate([zeros, w1r], axis=-1)
    wbig = jnp.stack([grp0, grp1], axis=2).reshape(per, kwid, hid_w).astype(BF16)
    posr = pos_emb.reshape(per, CMP_STRIDE, 1, HEAD_DIM)
    posbig = jnp.broadcast_to(posr, (per, CMP_STRIDE, NSA_KV_GROUPS, HEAD_DIM)).reshape(per, 1, kwid)
    z2 = jnp.zeros_like(w2)
    w2big = jnp.concatenate([jnp.concatenate([w2, z2], axis=1),
                             jnp.concatenate([z2, w2], axis=1)], axis=0).astype(BF16)
    const = lambda shape: pl.BlockSpec(shape, lambda b: (0,) * len(shape))
    tab = pl.BlockSpec((1, ncp, LANES), lambda b: (b, 0, 0))
    if is_key:
        w2_arg = w2big
        out_spec = pl.BlockSpec((1, NSA_KV_GROUPS, ncp, HEAD_DIM), lambda b: (b, 0, 0, 0))
        out_shape = jax.ShapeDtypeStruct((bsz, NSA_KV_GROUPS, ncp, HEAD_DIM), BF16)
    else:
        w2_arg = w2big.T
        out_spec = pl.BlockSpec((1, NSA_KV_GROUPS * HEAD_DIM, ncp), lambda b: (b, 0, 0))
        out_shape = jax.ShapeDtypeStruct((bsz, NSA_KV_GROUPS * HEAD_DIM, ncp), BF16)
    return pl.pallas_call(
        functools.partial(_compress_kernel, is_key=is_key),
        grid=(bsz,),
        in_specs=[pl.BlockSpec((1, ncp, kwid), lambda b: (b, 0, 0)),
                  const((1, kwid)), const((1, kwid)),
                  const((kwid, hid_w)), const((kwid, hid_w)),
                  const(w2_arg.shape), tab, tab],
        out_specs=out_spec,
        out_shape=out_shape,
        compiler_params=_params("arbitrary"),
        name=name,
    )(r, posbig[0], posbig[1], wbig[0], wbig[1], w2_arg, cos_c, sin_c)


def _softmax_cols(s, bias, any_visible):
    sb = s + bias
    e = jnp.exp2(sb - jnp.max(sb, axis=0, keepdims=True))
    l = jnp.sum(e, axis=0, keepdims=True)
    return e * jnp.where(any_visible, 1.0 / l, 0.0)


def _online_step(s, tile_max, v_t, m_old, l_old, acc_old):
    m_new = jnp.maximum(m_old, tile_max)
    alpha = jnp.exp2(m_old - m_new)
    p = jnp.exp2(s - m_new)
    l_new = alpha * l_old + jnp.sum(p, axis=0, keepdims=True)
    acc_new = alpha * acc_old + _dot(v_t, p.astype(BF16))
    return m_new, l_new, acc_new


def _online_cols(s, v_t, m_old, l_old, acc_old):
    return _online_step(s, jnp.max(s, axis=0, keepdims=True), v_t, m_old, l_old, acc_old)


def _online_stored(s_ref, tile_max, v_t, m_old, l_old, acc_old):
    return _online_step(s_ref[...], tile_max, v_t, m_old, l_old, acc_old)


def _normalized(state):
    _, l, acc = state
    return acc * (1.0 / l)


def _sweep(n_earlier, n_chains, prepare, produce, consume, finalize, state, alive=None):
    chains = range(n_chains)

    def produce_all(slot, f, diagonal):
        ctx = prepare(f, diagonal)
        for c in chains:
            produce(ctx, slot, c)

    def consume_all(slot, f, diagonal, st):
        return tuple(consume(slot, f, diagonal, c, st[c]) for c in chains)

    def overlapped(p_slot, p_f, c_slot, c_f, c_diagonal, st):
        ctx = prepare(p_f, False)
        out = []
        for c in chains:
            produce(ctx, p_slot, c)
            out.append(consume(c_slot, c_f, c_diagonal, c, st[c]))
        return tuple(out)

    produce_all(0, 0, True)
    state = overlapped(1, 0, 0, 0, True, state)

    def pair(i, st):
        st = overlapped(0, 2 * i + 1, 1, 2 * i, False, st)
        if alive is None:
            return overlapped(1, 2 * i + 2, 0, 2 * i + 1, False, st)
        produce_all(1, 2 * i + 2, False)
        return lax.cond(alive(st), lambda s: consume_all(0, 2 * i + 1, False, s), lambda s: s, st)

    n_pairs = n_earlier >> 1
    odd = (n_earlier & 1) == 1
    if alive is None:
        state = lax.fori_loop(0, n_pairs, pair, state)
    else:
        _, state = lax.while_loop(lambda c: (c[0] < n_pairs) & alive(c[1]),
                                  lambda c: (c[0] + 1, pair(c[0], c[1])), (jnp.int32(0), state))
        odd = odd & alive(state)

    @pl.when(odd)
    def _():
        finalize(consume_all(1, n_earlier - 1, False, state))

    @pl.when(jnp.logical_not(odd))
    def _():
        finalize(state)


def _score_buffers(chains, qn):
    return [pltpu.VMEM((2, chains, KEY_TILE, qn), F32), pltpu.VMEM((2, chains, 1, qn), F32)]


def _online_init(n, dv, qn):
    return tuple((jnp.full((1, qn), NEG_INF, F32), jnp.zeros((1, qn), F32), jnp.zeros((dv, qn), F32))
                 for _ in range(n))


def _split_bf16(v, terms):
    out, rest = [], v
    for i in range(terms):
        part = rest.astype(BF16)
        out.append(part)
        if i + 1 < terms:
            rest = rest - part.astype(F32)
    return out


def _top_rows(key, n_top):
    n_rows, n_cols = key.shape

    def bisect(i, tau):
        cand = tau | lax.shift_left(jnp.int32(1), jnp.int32(30) - i)
        cnt = jnp.sum(jnp.where(key >= cand, 1.0, 0.0), axis=0, keepdims=True)
        return jnp.where(cnt >= float(n_top), cand, tau)

    tau = lax.fori_loop(0, 31, bisect, jnp.zeros((1, n_cols), I32))
    above = jnp.where(key > tau, 1.0, 0.0)
    equal = jnp.where(key == tau, 1.0, 0.0)
    need = float(n_top) - jnp.sum(above, axis=0, keepdims=True)
    lower = jnp.where(_iota((n_rows, n_rows), 1) < _iota((n_rows, n_rows), 0), 1.0, 0.0).astype(BF16)
    before = _dot(lower, equal.astype(BF16))
    return above + equal * jnp.where(before < need, 1.0, 0.0)

def _nsa_kernel(q_ref, kc_ref, vc_ref, ks_ref, vs_ref, kw_ref, vw_ref, g_ref, ovt_ref, o_ref,
                bias_scr, part_scr, s_scr, mx_scr, *, n_sel):
    qn = q_ref.shape[2]
    ncp = kc_ref.shape[2]
    ns = ovt_ref.shape[0]
    kt_n = KEY_TILE
    tiles_per_q = qn // kt_n
    qi = pl.program_id(2)
    q0 = qi * qn
    reps = range(NSA_REP)
    q_head = lambda r: q_ref[0, r * HEAD_DIM:(r + 1) * HEAD_DIM, :]

    kc = kc_ref[0, 0]
    vc_t = vc_ref[0]
    bias_c = jnp.where((_iota((ncp, qn), 0) * CMP_STRIDE + (CMP_BLOCK - 1)) <= (q0 + _iota((ncp, qn), 1)),
                       0.0, NEG_INF)
    sees_cmp = (q0 + _iota((1, qn), 1)) >= CMP_BLOCK - 1
    o_cmp, p_sum = [], None
    s_next = _dot(kc, q_head(0))
    for r in reps:
        s_cur = s_next
        if r + 1 < NSA_REP:
            s_next = _dot(kc, q_head(r + 1))
        p = _softmax_cols(s_cur, bias_c, sees_cmp)
        o_cmp.append(_dot(vc_t, p.astype(BF16)))
        p_sum = p if p_sum is None else p_sum + p
    ovt = ovt_ref[...]
    imp_t = None
    for term in _split_bf16(p_sum, 3):
        part = _dot(ovt, term)
        imp_t = part if imp_t is None else imp_t + part

    blk = _iota((ns, qn), 0)
    cur = (q0 + _iota((ns, qn), 1)) >> SLC_SHIFT
    forced = (blk == 0) | (blk == cur) | (blk == cur - 1)
    imp_bits = jnp.where(imp_t > 0.0, lax.bitcast_convert_type(imp_t, I32), 0)
    key = jnp.where(forced, jnp.int32(2 ** 31 - 1), jnp.where(blk <= cur, imp_bits, -1))

    picked = _top_rows(key, n_sel)
    bias_scr[...] = jnp.where(picked > 0.5, 0.0, NEG_INF)

    blocks_per_tile = kt_n // SLC_BLOCK
    causal = _iota((kt_n, qn), 0) <= _iota((kt_n, qn), 1)

    carry = list(_online_init(NSA_REP, HEAD_DIM, qn))
    win = {}
    for diagonal in (True, False):
        kt = qi if diagonal else jnp.maximum(qi - WINDOW // kt_n, 0)
        k0 = pl.multiple_of(kt * kt_n, kt_n)
        kp = k0 + _iota((kt_n, qn), 0)
        tq = q0 + _iota((kt_n, qn), 1)
        inside = (kp <= tq) if diagonal else ((kp > tq - WINDOW) & (kp < q0))
        win[diagonal] = (k0, kt, jnp.where(inside, 0.0, NEG_INF))
    jobs = [(diagonal, r) for diagonal in (True, False) for r in reps]
    win_scores = lambda job: _dot(kw_ref[0, 0, pl.ds(win[job[0]][0], kt_n), :], q_head(job[1]))
    s_next = win_scores(jobs[0])
    for n, (diagonal, r) in enumerate(jobs):
        s_cur = s_next
        if n + 1 < len(jobs):
            s_next = win_scores(jobs[n + 1])
        _, kt, bias_w = win[diagonal]
        carry[r] = _online_cols(s_cur + bias_w, vw_ref[0, kt], *carry[r])

    gates = g_ref[0]
    for r in reps:
        part_scr[r] = (gates[3 * r:3 * r + 1] * o_cmp[r]
                       + gates[3 * r + 2:3 * r + 3] * _normalized(carry[r]))

    def prepare(f, diagonal):
        kt = qi if diagonal else jnp.minimum(f, jnp.maximum(qi - 1, 0))
        k0 = pl.multiple_of(kt * kt_n, kt_n)
        rows = [jnp.broadcast_to(bias_scr[pl.ds(kt * blocks_per_tile + i, 1), :], (SLC_BLOCK, qn))
                for i in range(blocks_per_tile)]
        bias = jnp.concatenate(rows, axis=0)
        if diagonal:
            bias = jnp.where(causal, bias, NEG_INF)
        return k0, bias

    def produce(ctx, slot, r):
        k0, bias = ctx
        s = _dot(ks_ref[0, 0, pl.ds(k0, kt_n), :], q_head(r)) + bias
        s_scr[slot, r] = s
        mx_scr[slot, r] = jnp.max(s, axis=0, keepdims=True)

    def consume(slot, f, diagonal, r, state_r):
        return _online_stored(s_scr.at[slot, r], mx_scr[slot, r], vs_ref[0, qi if diagonal else f], *state_r)

    def finalize(state):
        for r in reps:
            o = part_scr[r] + g_ref[0, 3 * r + 1:3 * r + 2, :] * _normalized(state[r])
            o_ref[0, r * HEAD_DIM:(r + 1) * HEAD_DIM, :] = o.astype(o_ref.dtype)

    _sweep(qi, NSA_REP, prepare, produce, consume, finalize, _online_init(NSA_REP, HEAD_DIM, qn))


def _nsa_attention(q_t, k_cmp, v_cmp_t, k_slc, v_slc_t, k_win, v_win_t, gates_t):
    bsz, _, t = q_t.shape
    ncp = k_cmp.shape[2]
    ns = t // SLC_BLOCK
    qn = min(Q_TILE, t)
    grp_w = NSA_REP * HEAD_DIM
    cmp_start = jnp.arange(ncp) * CMP_STRIDE
    slc_start = jnp.arange(ns) * SLC_BLOCK
    real = jnp.arange(ncp) < (t - CMP_BLOCK) // CMP_STRIDE + 1
    overlap_t = ((cmp_start[None, :] < slc_start[:, None] + SLC_BLOCK)
                 & (cmp_start[None, :] + CMP_BLOCK > slc_start[:, None]) & real[None, :]).astype(BF16)
    k_spec = lambda n: pl.BlockSpec((1, 1, n, HEAD_DIM), lambda b, g, i: (b, g, 0, 0))
    v_spec = pl.BlockSpec((1, t // KEY_TILE, HEAD_DIM, KEY_TILE), lambda b, g, i: (b, 0, g, 0))
    return pl.pallas_call(
        functools.partial(_nsa_kernel, n_sel=min(N_SELECT, ns)),
        grid=(bsz, NSA_KV_GROUPS, t // qn),
        in_specs=[pl.BlockSpec((1, grp_w, qn), lambda b, g, i: (b, g, i)),
                  k_spec(ncp),
                  pl.BlockSpec((1, HEAD_DIM, ncp), lambda b, g, i: (b, g, 0)),
                  k_spec(t), v_spec, k_spec(t), v_spec,
                  pl.BlockSpec((1, GATE_ROWS, qn), lambda b, g, i: (b, g, i)),
                  pl.BlockSpec((ns, ncp), lambda b, g, i: (0, 0))],
        out_specs=pl.BlockSpec((1, grp_w, qn), lambda b, g, i: (b, g, i)),
        out_shape=jax.ShapeDtypeStruct((bsz, NSA_HEADS * HEAD_DIM, t), BF16),
        scratch_shapes=[pltpu.VMEM((ns, qn), F32), pltpu.VMEM((NSA_REP, HEAD_DIM, qn), F32)] + _score_buffers(NSA_REP, qn),
        compiler_params=_params("arbitrary", "arbitrary", "arbitrary"),
        name="nsa_attention",
    )(q_t, k_cmp, v_cmp_t, k_slc, v_slc_t, k_win, v_win_t, gates_t, overlap_t)


SB_HEADS_PER_STEP = 2
SB_DEAD_TAIL = -110.0
SB_SCAN_BLOCK = 128


def _sb_kernel(q_ref, k_ref, v_ref, o_ref, nz_scr):
    qn = q_ref.shape[2]
    kt_n = KEY_TILE
    qi = pl.program_id(2)
    heads = range(SB_HEADS_PER_STEP)
    rows = [slice(h * HEAD_DIM, (h + 1) * HEAD_DIM) for h in heads]
    sub = SB_SCAN_BLOCK
    n_sub = kt_n // sub
    tri = jnp.where(_iota((sub, sub), 1) > _iota((sub, sub), 0), 1.0, 0.0).astype(BF16)
    later2 = jnp.concatenate([tri, tri], axis=1)
    strictly_before = _iota((kt_n, qn), 0) < _iota((kt_n, qn), 1)

    def tile_index(f, diagonal):
        return qi if diagonal else jnp.maximum(qi - 1 - f, 0)

    def prepare(f, diagonal):
        return pl.multiple_of(tile_index(f, diagonal) * kt_n, kt_n)

    def produce(k0, slot, h):
        nz_scr[slot, h] = _dot(k_ref[0, h, pl.ds(k0, kt_n), :], -q_ref[0, rows[h], :])

    def consume(slot, f, diagonal, h, state_h):
        tail, acc = state_h
        nz = nz_scr[slot, h]
        neg_abs = lax.bitcast_convert_type(lax.bitcast_convert_type(nz, jnp.uint32) | jnp.uint32(0x80000000), F32)
        soft = jnp.log(1.0 + jnp.exp(neg_abs))
        log_keep = jnp.minimum(nz, 0.0) - soft
        log_beta = log_keep - nz
        if diagonal:
            log_keep = jnp.where(strictly_before, log_keep, 0.0)
        offs = tail
        parts = [None] * n_sub
        for blk in reversed(range(n_sub)):
            lk = log_keep[blk * sub:(blk + 1) * sub]
            hi = lk.astype(BF16)
            lo = (lk - hi.astype(F32)).astype(BF16)
            within = _dot(later2, jnp.concatenate([hi, lo], axis=0))
            parts[blk] = log_beta[blk * sub:(blk + 1) * sub] + within + offs
            offs = offs + jnp.sum(lk, axis=0, keepdims=True)
        a = jnp.exp(jnp.concatenate(parts, axis=0))
        if diagonal:
            a = jnp.where(strictly_before, a, 0.0)
        return offs, acc + _dot(v_ref[0, tile_index(f, diagonal), rows[h], :], a.astype(BF16))

    def finalize(state):
        for h in heads:
            o_ref[0, rows[h], :] = state[h][1].astype(o_ref.dtype)

    def alive(state):
        tails = [state[h][0] for h in heads]
        return jnp.max(functools.reduce(jnp.maximum, tails)) > SB_DEAD_TAIL

    init = tuple((jnp.zeros((1, qn), F32), jnp.zeros((HEAD_DIM, qn), F32)) for _ in heads)
    _sweep(qi, SB_HEADS_PER_STEP, prepare, produce, consume, finalize, init, alive)


def _sb_attention(q_t, k, v_t):
    bsz, nh, t, _ = k.shape
    qn = min(Q_TILE, t)
    hps = SB_HEADS_PER_STEP
    w = hps * HEAD_DIM
    return pl.pallas_call(
        _sb_kernel,
        grid=(bsz, nh // hps, t // qn),
        in_specs=[pl.BlockSpec((1, w, qn), lambda b, h, i: (b, h, i)),
                  pl.BlockSpec((1, hps, t, HEAD_DIM), lambda b, h, i: (b, h, 0, 0)),
                  pl.BlockSpec((1, t // KEY_TILE, w, KEY_TILE), lambda b, h, i: (b, 0, h, 0))],
        out_specs=pl.BlockSpec((1, w, qn), lambda b, h, i: (b, h, i)),
        out_shape=jax.ShapeDtypeStruct((bsz, nh * HEAD_DIM, t), BF16),
        scratch_shapes=[pltpu.VMEM((2, hps, KEY_TILE, qn), F32)],
        compiler_params=_params("arbitrary", "arbitrary", "arbitrary"),
        name="stickbreak_attention",
    )(q_t, k, v_t)


DIFF_HEADS_PER_STEP = 2


def _diff_kernel(q_ref, k_ref, v_ref, lq1_ref, lk1_ref, lq2_ref, lk2_ref, sub_ref, o_ref, s_scr, mx_scr,
                 *, lambda_init):
    qn = q_ref.shape[2]
    vd = 2 * HEAD_DIM
    kt_n = KEY_TILE
    qi = pl.program_id(2)
    chains = range(2 * DIFF_HEADS_PER_STEP)
    lam = (jnp.exp(jnp.sum(lq1_ref[...] * lk1_ref[...], axis=1, keepdims=True))
           - jnp.exp(jnp.sum(lq2_ref[...] * lk2_ref[...], axis=1, keepdims=True)) + lambda_init)

    causal = _iota((kt_n, qn), 0) <= _iota((kt_n, qn), 1)

    def prepare(f, diagonal):
        kt = qi if diagonal else jnp.minimum(f, jnp.maximum(qi - 1, 0))
        return pl.multiple_of(kt * kt_n, kt_n), diagonal

    def produce(ctx, slot, c):
        k0, diagonal = ctx
        s = _dot(k_ref[0, c, pl.ds(k0, kt_n), :], q_ref[0, c * HEAD_DIM:(c + 1) * HEAD_DIM, :])
        if diagonal:
            s = jnp.where(causal, s, NEG_INF)
        s_scr[slot, c] = s
        mx_scr[slot, c] = jnp.max(s, axis=0, keepdims=True)

    def consume(slot, f, diagonal, c, state_c):
        kt = qi if diagonal else f
        return _online_stored(s_scr.at[slot, c], mx_scr[slot, c],
                              v_ref[0, kt, (c // 2) * vd:(c // 2 + 1) * vd, :], *state_c)

    def finalize(state):
        for h in range(DIFF_HEADS_PER_STEP):
            o = _normalized(state[2 * h]) - lam * _normalized(state[2 * h + 1])
            ms = jnp.mean(o * o, axis=0, keepdims=True)
            o = o * lax.rsqrt(ms + RMS_EPS) * sub_ref[...] * (1.0 - lambda_init)
            o_ref[0, h * vd:(h + 1) * vd, :] = o.astype(o_ref.dtype)

    _sweep(qi, len(chains), prepare, produce, consume, finalize, _online_init(len(chains), vd, qn))


def _diff_attention(q_t, k, v_t, lq1, lk1, lq2, lk2, subln, lambda_init):
    bsz, nh2, t, _ = k.shape
    nh = nh2 // 2
    vd = 2 * HEAD_DIM
    hps = DIFF_HEADS_PER_STEP
    qn = min(Q_TILE, t)
    row = lambda a: a.reshape(1, -1)
    small = lambda n: pl.BlockSpec((1, n), lambda b, h, i: (0, 0))
    return pl.pallas_call(
        functools.partial(_diff_kernel, lambda_init=lambda_init),
        grid=(bsz, nh // hps, t // qn),
        in_specs=[pl.BlockSpec((1, hps * vd, qn), lambda b, h, i: (b, h, i)),
                  pl.BlockSpec((1, 2 * hps, t, HEAD_DIM), lambda b, h, i: (b, h, 0, 0)),
                  pl.BlockSpec((1, t // KEY_TILE, hps * vd, KEY_TILE), lambda b, h, i: (b, 0, h, 0)),
                  small(HEAD_DIM), small(HEAD_DIM), small(HEAD_DIM), small(HEAD_DIM),
                  pl.BlockSpec((vd, 1), lambda b, h, i: (0, 0))],
        out_specs=pl.BlockSpec((1, hps * vd, qn), lambda b, h, i: (b, h, i)),
        out_shape=jax.ShapeDtypeStruct((bsz, nh * vd, t), BF16),
        scratch_shapes=_score_buffers(2 * hps, qn),
        compiler_params=_params("arbitrary", "arbitrary", "arbitrary"),
        name="diff_attention",
    )(q_t, k, v_t, row(lq1), row(lk1), row(lq2), row(lk2), subln.reshape(vd, 1))


def _outproj_kernel(*refs, n_in):
    a_refs, w_refs = refs[:n_in], refs[n_in:2 * n_in]
    x_ref, gate_ref, o_ref = refs[2 * n_in:]
    y = None
    for a_ref, w_ref in zip(a_refs, w_refs):
        part = lax.dot_general(a_ref[0], w_ref[...], _TN, preferred_element_type=F32)
        y = part if y is None else y + part
    o_ref[0] = x_ref[0] + gate_ref[0] * y


def _out_project(acts_t, weights, x, gate, name):
    bsz, t, d = x.shape
    rows = min(t, PROJ_ROWS)
    tile = pl.BlockSpec((1, rows, d), lambda b, i: (b, i, 0))
    return pl.pallas_call(
        functools.partial(_outproj_kernel, n_in=len(acts_t)),
        grid=(bsz, t // rows),
        in_specs=([pl.BlockSpec((1, a.shape[1], rows), lambda b, i: (b, 0, i)) for a in acts_t]
                  + [pl.BlockSpec(w.shape, lambda b, i: (0, 0)) for w in weights]
                  + [tile, pl.BlockSpec((1, 1, d), lambda b, i: (b, 0, 0))]),
        out_specs=tile,
        out_shape=jax.ShapeDtypeStruct((bsz, t, d), F32),
        compiler_params=_params("arbitrary", "arbitrary"),
        name=name,
    )(*acts_t, *weights, x, gate)


def _ffn_kernel(x_ref, halo_ref, g_ref, sh_ref, sc_ref, gate_ref, wg_ref, wu_ref, cw_ref, cb_ref, wd_ref,
                nf_ref, o_ref, a_scr, *, final_norm):
    ti = pl.program_id(1)
    h = _norm_modulate(x_ref[0], g_ref[...], sh_ref[0], sc_ref[0]).astype(BF16)
    h_prev = _norm_modulate(halo_ref[0], g_ref[...], sh_ref[0], sc_ref[0]).astype(BF16)
    seen = jnp.where(ti > 0, 1.0, 0.0)
    cols = FFN_COLS
    row = _iota((h.shape[0], cols), 0)
    for c in range(a_scr.shape[1] // cols):
        sl = slice(c * cols, (c + 1) * cols)
        wg = wg_ref[:, sl]
        g = _dot(h, wg)
        g_prev = _dot(h_prev, wg) * seen
        last = g_prev[HALO_ROWS - 1:HALO_ROWS]
        last2 = g_prev[HALO_ROWS - 2:HALO_ROWS - 1]
        g1 = jnp.where(row == 0, last, pltpu.roll(g, 1, 0))
        g2 = jnp.where(row == 0, last2, jnp.where(row == 1, last, pltpu.roll(g, 2, 0)))
        cw = cw_ref[:, sl]
        conv = cw[0:1] * g2 + cw[1:2] * g1 + cw[2:3] * g + cb_ref[:, sl]
        a_scr[:, sl] = (conv * _sigmoid(conv) * _dot(h, wu_ref[:, sl])).astype(BF16)
    y = x_ref[0] + gate_ref[0] * _dot(a_scr[...], wd_ref[...])
    if final_norm:
        ms = jnp.mean(y * y, axis=-1, keepdims=True)
        y = y * lax.rsqrt(ms + RMS_EPS) * nf_ref[...]
    o_ref[0] = y


def _conv_ffn(x, g, shift, scale, gate, w_gate, w_up, conv_w, conv_b, w_down, norm_f, final_norm, name):
    bsz, t, d = x.shape
    f = w_gate.shape[1]
    rows = min(t, FFN_ROWS)
    halo_blocks = rows // HALO_ROWS
    vec = pl.BlockSpec((1, 1, d), lambda b, i: (b, 0, 0))
    drow = pl.BlockSpec((1, d), lambda b, i: (0, 0))
    resident = lambda shape: pl.BlockSpec(shape, lambda b, i: (0, 0), pipeline_mode=pl.Buffered(1))
    return pl.pallas_call(
        functools.partial(_ffn_kernel, final_norm=final_norm),
        grid=(bsz, t // rows),
        in_specs=[pl.BlockSpec((1, rows, d), lambda b, i: (b, i, 0)),
                  pl.BlockSpec((1, HALO_ROWS, d), lambda b, i: (b, jnp.maximum(i * halo_blocks - 1, 0), 0)),
                  drow, vec, vec, vec,
                  resident((d, f)), resident((d, f)), resident((CONV_WIDTH, f)), resident((1, f)),
                  resident((f, d)), drow],
        out_specs=pl.BlockSpec((1, rows, d), lambda b, i: (b, i, 0)),
        out_shape=jax.ShapeDtypeStruct((bsz, t, d), F32),
        scratch_shapes=[pltpu.VMEM((rows, f), BF16)],
        compiler_params=_params("arbitrary", "arbitrary"),
        name=name,
    )(x, x, g.reshape(1, d), shift, scale, gate, w_gate.astype(BF16), w_up.astype(BF16), conv_w,
      conv_b.reshape(1, f), w_down.astype(BF16), norm_f.reshape(1, d))


def _layout(widths):
    starts, start = [], 0
    for w in widths:
        starts.append(start)
        start += w
    return starts


def _hybrid_layer(x, mods, tables, norm_g, w_in, pos_k, pos_v, ck_w1, ck_w2, cv_w1, cv_w2, w_out):
    shift, scale, gate = mods
    bsz, t, d = x.shape
    qw, kvw, sbw = NSA_HEADS * HEAD_DIM, NSA_KV_GROUPS * HEAD_DIM, SB_HEADS * HEAD_DIM
    col = _layout((qw,) + (kvw,) * 6 + (3 * NSA_HEADS,) + (sbw,) * 3)
    cols = lambda i, w: w_in[:, col[i]:col[i] + w]
    q_n, kc, vc, ks, vs, kw, vw, gl, q_s, k_s, v_s = (cols(i, w) for i, w in enumerate(
        (qw,) + (kvw,) * 6 + (3 * NSA_HEADS,) + (sbw,) * 3))
    inv_sqrt = HEAD_DIM ** -0.5

    w_tok = jnp.concatenate([ks, kw, kc, vc, k_s], axis=1).astype(BF16)
    ts = _layout((kvw, kvw, kvw, kvw, sbw))
    tok_segs = [TokSeg(ts[0], kvw, True, HEAD_DIM, BF16),
                TokSeg(ts[1], kvw, True, HEAD_DIM, BF16),
                TokSeg(ts[2], kvw, False, 0, F32),
                TokSeg(ts[3], kvw, False, 0, F32),
                TokSeg(ts[4], sbw, False, HEAD_DIM, BF16)]
    per_group = 3 * NSA_REP
    gate_pad = jnp.zeros((d, GATE_ROWS - per_group), w_in.dtype)
    gl_pad = jnp.concatenate([gl[:, :per_group], gate_pad, gl[:, per_group:], gate_pad], axis=1)
    w_feat_t = jnp.concatenate([q_n, q_s, vs, vw, v_s, gl_pad], axis=1).T.astype(BF16)
    fs = _layout((qw, sbw, kvw, kvw, sbw, NSA_KV_GROUPS * GATE_ROWS))
    feat_segs = [FeatSeg(fs[0], qw, True, inv_sqrt * LOG2E, False, False, BF16),
                 FeatSeg(fs[1], sbw, False, inv_sqrt, False, False, BF16),
                 FeatSeg(fs[2], kvw, False, 1.0, False, True, BF16),
                 FeatSeg(fs[3], kvw, False, 1.0, False, True, BF16),
                 FeatSeg(fs[4], sbw, False, 1.0, False, True, BF16),
                 FeatSeg(fs[5], NSA_KV_GROUPS * GATE_ROWS, False, 1.0, True, False, F32)]
    (k_slc, k_win, kc_a, vc_a, k_sb, q_n_t, q_s_t, v_slc_t, v_win_t, v_sb_t, gates_t) = _project(
        x, norm_g, shift, scale, w_tok, w_feat_t, tables, tok_segs, feat_segs, "hybrid_in_proj")

    ncp = t // CMP_STRIDE
    end_rows = jnp.minimum(jnp.arange(ncp) * CMP_STRIDE + CMP_BLOCK - 1, t - 1)
    cos_c, sin_c = tables[0][:, end_rows], tables[1][:, end_rows]
    k_cmp = _compress(kc_a, pos_k, ck_w1, ck_w2, cos_c, sin_c, True, "compress_k")
    v_cmp_t = _compress(vc_a, pos_v, cv_w1, cv_w2, cos_c, sin_c, False, "compress_v")
    o_nsa_t = _nsa_attention(q_n_t, k_cmp, v_cmp_t, k_slc, v_slc_t, k_win, v_win_t, gates_t)
    o_sb_t = _sb_attention(q_s_t, k_sb, v_sb_t)
    w_out_b = w_out.astype(BF16)
    return _out_project([o_nsa_t, o_sb_t], [w_out_b[:qw], w_out_b[qw:]], x, gate, "hybrid_out_proj")


def _diff_layer(x, mods, tables, norm_g, w_qkv, lq1, lk1, lq2, lk2, subln, w_out, layer_idx):
    shift, scale, gate = mods
    dw = 2 * DIFF_HEADS * HEAD_DIM
    inv_sqrt = HEAD_DIM ** -0.5
    w_tok = w_qkv[:, dw:2 * dw].astype(BF16)
    w_feat_t = jnp.concatenate([w_qkv[:, :dw], w_qkv[:, 2 * dw:]], axis=1).T.astype(BF16)
    tok_segs = [TokSeg(0, dw, True, HEAD_DIM, BF16)]
    feat_segs = [FeatSeg(0, dw, True, inv_sqrt * LOG2E, False, False, BF16),
                 FeatSeg(dw, dw, False, 1.0, False, True, BF16)]
    k, q_t, v_t = _project(x, norm_g, shift, scale, w_tok, w_feat_t, tables, tok_segs, feat_segs, "diff_in_proj")
    lambda_init = 0.8 - 0.6 * math.exp(-0.3 * layer_idx)
    o_t = _diff_attention(q_t, k, v_t, lq1, lk1, lq2, lk2, subln, lambda_init)
    return _out_project([o_t], [w_out.astype(BF16)], x, gate, "diff_out_proj")


def kernel(x, c, positions, mod_w, mod_b, norm_mix, norm_ffn, ffn_w_gate, ffn_w_up, ffn_conv_w, ffn_conv_b, ffn_w_down, hyb_w_in, nsa_pos_k, nsa_pos_v, nsa_ck_w1, nsa_ck_w2, nsa_cv_w1, nsa_cv_w2, hyb_w_out, diff_w_qkv, diff_lq1, diff_lk1, diff_lq2, diff_lk2, diff_subln, diff_w_out, norm_f):
    bsz, t, d = x.shape
    depth = mod_w.shape[0]
    mod = _adaln_mod(c, mod_w, mod_b)
    tables = _rope_tables(positions)
    for i in range(depth):
        sh_m, sc_m, g_m, sh_f, sc_f, g_f = (mod[i, :, k * d:(k + 1) * d].reshape(bsz, 1, d) for k in range(6))
        j = i // 2
        if i % 2 == 0:
            x = _hybrid_layer(x, (sh_m, sc_m, g_m), tables, norm_mix[i], hyb_w_in[j],
                              nsa_pos_k[j], nsa_pos_v[j], nsa_ck_w1[j], nsa_ck_w2[j], nsa_cv_w1[j],
                              nsa_cv_w2[j], hyb_w_out[j])
        else:
            x = _diff_layer(x, (sh_m, sc_m, g_m), tables, norm_mix[i], diff_w_qkv[j], diff_lq1[j],
                            diff_lk1[j], diff_lq2[j], diff_lk2[j], diff_subln[j], diff_w_out[j], i)
        x = _conv_ffn(x, norm_ffn[i], sh_f, sc_f, g_f, ffn_w_gate[i], ffn_w_up[i], ffn_conv_w[i],
                      ffn_conv_b[i], ffn_w_down[i], norm_f, i == depth - 1, "conv_ffn_%d" % i)
    return x
```

```python
import functools
import math
from typing import NamedTuple

import jax
import jax.numpy as jnp
from jax import lax
from jax.experimental import pallas as pl
from jax.experimental.pallas import tpu as pltpu

F32 = jnp.float32
BF16 = jnp.bfloat16
I32 = jnp.int32

HEAD_DIM = 64
ROPE_DIM = HEAD_DIM // 4
ROPE_HALF = ROPE_DIM // 2
ROPE_THETA = 500000.0
NSA_HEADS = 8
NSA_KV_GROUPS = 2
NSA_REP = NSA_HEADS // NSA_KV_GROUPS
CMP_BLOCK = 32
CMP_STRIDE = 16
CMP_HIDDEN = 4 * HEAD_DIM
SLC_BLOCK = 64
SLC_SHIFT = 6
N_SELECT = 16
WINDOW = 512
SB_HEADS = 8
DIFF_HEADS = 8
CONV_WIDTH = 3
RMS_EPS = 1e-6
NEG_INF = -1e30
FORCE = 1e9
LOG2E = math.log2(math.e)

LANES = 128
Q_TILE = 512
KEY_TILE = 512
GATE_ROWS = 16
PROJ_ROWS = 512
FFN_ROWS = 512
FFN_COLS = 256
HALO_ROWS = 16
VMEM_LIMIT = 56 * 1024 * 1024

_NT = (((1,), (1,)), ((), ()))
_TN = (((0,), (0,)), ((), ()))


def _params(*sem):
    return pltpu.CompilerParams(dimension_semantics=sem, vmem_limit_bytes=VMEM_LIMIT)


def _sigmoid(v):
    return 1.0 / (1.0 + jnp.exp(-v))


def _iota(shape, axis):
    return lax.broadcasted_iota(I32, shape, axis)


def _dot(a, b):
    return jnp.dot(a, b, preferred_element_type=F32)


def _mod_kernel(c_ref, w_ref, b_ref, o_ref):
    c = c_ref[...]
    cond = c * _sigmoid(c)
    o_ref[0] = jnp.dot(cond, w_ref[0], preferred_element_type=F32,
                       precision=lax.Precision.HIGHEST) + b_ref[0]


def _adaln_mod(c, mod_w, mod_b):
    depth, d, n = mod_w.shape
    bsz = c.shape[0]
    tn = n // 4
    return pl.pallas_call(
        _mod_kernel,
        grid=(depth, n // tn),
        in_specs=[pl.BlockSpec((bsz, d), lambda i, j: (0, 0)),
                  pl.BlockSpec((1, d, tn), lambda i, j: (i, 0, j)),
                  pl.BlockSpec((1, 1, tn), lambda i, j: (i, 0, j))],
        out_specs=pl.BlockSpec((1, bsz, tn), lambda i, j: (i, 0, j)),
        out_shape=jax.ShapeDtypeStruct((depth, bsz, n), F32),
        compiler_params=_params("arbitrary", "arbitrary"),
        name="adaln_mod",
    )(c, mod_w, mod_b.reshape(depth, 1, n))


def _rope_kernel(pos_col_ref, pos_row_ref, inv_row_ref, sgn_row_ref, inv_col_ref,
                 cos_ref, sin_ref, cos_t_ref, sin_t_ref):
    ang = pos_col_ref[0].astype(F32) * inv_row_ref[...]
    cos_ref[0] = jnp.cos(ang)
    sin_ref[0] = jnp.sin(ang) * sgn_row_ref[...]
    ang_t = inv_col_ref[...] * pos_row_ref[0].astype(F32)
    cos_t_ref[0] = jnp.cos(ang_t)
    sin_t_ref[0] = jnp.sin(ang_t)


def _rope_tables(positions):
    bsz, t = positions.shape
    inv = ROPE_THETA ** (-jnp.arange(0, ROPE_DIM, 2, dtype=F32) / ROPE_DIM)
    per_head_inv = jnp.concatenate([inv, inv, jnp.zeros((HEAD_DIM - ROPE_DIM,), F32)])
    per_head_sgn = jnp.concatenate([-jnp.ones((ROPE_HALF,), F32), jnp.ones((ROPE_HALF,), F32),
                                    jnp.zeros((HEAD_DIM - ROPE_DIM,), F32)])
    inv_row = jnp.tile(per_head_inv, LANES // HEAD_DIM)[None, :]
    sgn_row = jnp.tile(per_head_sgn, LANES // HEAD_DIM)[None, :]
    rows = min(t, 1024)
    tab = jax.ShapeDtypeStruct((bsz, t, LANES), F32)
    tab_t = jax.ShapeDtypeStruct((bsz, ROPE_HALF, t), F32)
    row_spec = pl.BlockSpec((1, LANES), lambda b, i: (0, 0))
    return pl.pallas_call(
        _rope_kernel,
        grid=(bsz, t // rows),
        in_specs=[pl.BlockSpec((1, rows, 1), lambda b, i: (b, i, 0)),
                  pl.BlockSpec((1, 1, rows), lambda b, i: (b, 0, i)),
                  row_spec, row_spec,
                  pl.BlockSpec((ROPE_HALF, 1), lambda b, i: (0, 0))],
        out_specs=[pl.BlockSpec((1, rows, LANES), lambda b, i: (b, i, 0))] * 2
        + [pl.BlockSpec((1, ROPE_HALF, rows), lambda b, i: (b, 0, i))] * 2,
        out_shape=[tab, tab, tab_t, tab_t],
        compiler_params=_params("arbitrary", "arbitrary"),
        name="rope_tables",
    )(positions.reshape(bsz, t, 1), positions.reshape(bsz, 1, t), inv_row, sgn_row, inv[:, None])


def _rope_chunk(y, cosv, sinv, first_half):
    ahead = pltpu.roll(y, LANES - ROPE_HALF, 1)
    behind = pltpu.roll(y, ROPE_HALF, 1)
    return y * cosv + jnp.where(first_half, ahead, behind) * sinv


def _rope_rows(y_t, cos_t, sin_t):
    heads = []
    for h in range(y_t.shape[0] // HEAD_DIM):
        blk = y_t[h * HEAD_DIM:(h + 1) * HEAD_DIM]
        x1, x2 = blk[:ROPE_HALF], blk[ROPE_HALF:ROPE_DIM]
        heads += [x1 * cos_t - x2 * sin_t, x2 * cos_t + x1 * sin_t, blk[ROPE_DIM:]]
    return jnp.concatenate(heads, axis=0)


class TokSeg(NamedTuple):
    start: int
    width: int
    rope: bool
    head_width: int
    dtype: object


class FeatSeg(NamedTuple):
    start: int
    rows: int
    rope: bool
    scale: float
    sigmoid: bool
    key_tiled: bool
    dtype: object


def _norm_modulate(x, g, shift, scale):
    ms = jnp.mean(x * x, axis=-1, keepdims=True)
    y = x * lax.rsqrt(ms + RMS_EPS) * g
    return y * (1.0 + scale) + shift


def _proj_kernel(x_ref, g_ref, sh_ref, sc_ref, w_ref, wt_ref, cos_ref, sin_ref, cos_t_ref, sin_t_ref,
                 *out_refs, tok_segs, feat_segs):
    hb = _norm_modulate(x_ref[0], g_ref[...], sh_ref[0], sc_ref[0]).astype(BF16)
    cosv, sinv = cos_ref[0], sin_ref[0]
    first_half = (_iota(cosv.shape, 1) & (HEAD_DIM - 1)) < ROPE_HALF
    tok_refs, feat_refs = out_refs[:len(tok_segs)], out_refs[len(tok_segs):]
    for seg, o_ref in zip(tok_segs, tok_refs):
        y = _dot(hb, w_ref[:, seg.start:seg.start + seg.width])
        for ch in range(seg.width // LANES):
            yc = y[:, ch * LANES:(ch + 1) * LANES]
            if seg.rope:
                yc = _rope_chunk(yc, cosv, sinv, first_half)
            yc = yc.astype(seg.dtype)
            if seg.head_width == 0:
                o_ref[0, :, ch * LANES:(ch + 1) * LANES] = yc
            else:
                o_ref[0, 2 * ch] = yc[:, :HEAD_DIM]
                o_ref[0, 2 * ch + 1] = yc[:, HEAD_DIM:]
    for seg, o_ref in zip(feat_segs, feat_refs):
        y_t = lax.dot_general(wt_ref[seg.start:seg.start + seg.rows, :], hb, _NT,
                              preferred_element_type=F32)
        if seg.rope:
            y_t = _rope_rows(y_t, cos_t_ref[0], sin_t_ref[0])
        if seg.scale != 1.0:
            y_t = y_t * seg.scale
        if seg.sigmoid:
            y_t = _sigmoid(y_t)
        y_t = y_t.astype(seg.dtype)
        if seg.key_tiled:
            for ch in range(y_t.shape[1] // KEY_TILE):
                o_ref[0, ch] = y_t[:, ch * KEY_TILE:(ch + 1) * KEY_TILE]
        else:
            o_ref[0] = y_t


def _project(x, g, shift, scale, w_tok, w_feat_t, tables, tok_segs, feat_segs, name):
    bsz, t, d = x.shape
    rows = min(t, PROJ_ROWS)
    cos_tab, sin_tab, cos_t, sin_t = tables
    out_shapes, out_specs = [], []
    for seg in tok_segs:
        if seg.head_width == 0:
            out_shapes.append(jax.ShapeDtypeStruct((bsz, t, seg.width), seg.dtype))
            out_specs.append(pl.BlockSpec((1, rows, seg.width), lambda b, i: (b, i, 0)))
        else:
            nh = seg.width // seg.head_width
            out_shapes.append(jax.ShapeDtypeStruct((bsz, nh, t, seg.head_width), seg.dtype))
            out_specs.append(pl.BlockSpec((1, nh, rows, seg.head_width), lambda b, i: (b, 0, i, 0)))
    for seg in feat_segs:
        if seg.key_tiled:
            out_shapes.append(jax.ShapeDtypeStruct((bsz, t // KEY_TILE, seg.rows, KEY_TILE), seg.dtype))
            out_specs.append(pl.BlockSpec((1, rows // KEY_TILE, seg.rows, KEY_TILE), lambda b, i: (b, i, 0, 0)))
        else:
            out_shapes.append(jax.ShapeDtypeStruct((bsz, seg.rows, t), seg.dtype))
            out_specs.append(pl.BlockSpec((1, seg.rows, rows), lambda b, i: (b, 0, i)))
    vec = pl.BlockSpec((1, 1, d), lambda b, i: (b, 0, 0))
    tab = pl.BlockSpec((1, rows, LANES), lambda b, i: (b, i, 0))
    tab_t = pl.BlockSpec((1, ROPE_HALF, rows), lambda b, i: (b, 0, i))
    return pl.pallas_call(
        functools.partial(_proj_kernel, tok_segs=tuple(tok_segs), feat_segs=tuple(feat_segs)),
        grid=(bsz, t // rows),
        in_specs=[pl.BlockSpec((1, rows, d), lambda b, i: (b, i, 0)),
                  pl.BlockSpec((1, d), lambda b, i: (0, 0)),
                  vec, vec,
                  pl.BlockSpec(w_tok.shape, lambda b, i: (0, 0)),
                  pl.BlockSpec(w_feat_t.shape, lambda b, i: (0, 0)),
                  tab, tab, tab_t, tab_t],
        out_specs=out_specs,
        out_shape=out_shapes,
        compiler_params=_params("arbitrary", "arbitrary"),
        name=name,
    )(x, g.reshape(1, d), shift, scale, w_tok, w_feat_t, cos_tab, sin_tab, cos_t, sin_t)


def _compress_kernel(r_ref, pa_ref, pb_ref, wa_ref, wb_ref, w2_ref, cos_ref, sin_ref, o_ref, *, is_key):
    r = r_ref[0]
    ncp = r.shape[0]
    a = _dot((r + pa_ref[...]).astype(BF16), wa_ref[...])
    b = _dot((r + pb_ref[...]).astype(BF16), wb_ref[...])
    hid = a + pltpu.roll(b, ncp - 1, 0)
    hid = (hid * _sigmoid(hid)).astype(BF16)
    if is_key:
        y = _dot(hid, w2_ref[...])
        first_half = (_iota(y.shape, 1) & (HEAD_DIM - 1)) < ROPE_HALF
        y = _rope_chunk(y, cos_ref[0], sin_ref[0], first_half).astype(o_ref.dtype)
        for g in range(NSA_KV_GROUPS):
            o_ref[0, g] = y[:, g * HEAD_DIM:(g + 1) * HEAD_DIM]
    else:
        o_ref[0] = lax.dot_general(w2_ref[...], hid, _NT, preferred_element_type=F32).astype(o_ref.dtype)


def _compress(kv, pos_emb, w1, w2, cos_c, sin_c, is_key, name):
    bsz, t, _ = kv.shape
    ncp = t // CMP_STRIDE
    kwid = CMP_STRIDE * NSA_KV_GROUPS * HEAD_DIM
    hid_w = NSA_KV_GROUPS * CMP_HIDDEN
    r = kv.reshape(bsz, ncp, kwid)
    per = CMP_BLOCK // CMP_STRIDE
    w1r = w1.reshape(per, CMP_STRIDE, HEAD_DIM, CMP_HIDDEN)
    zeros = jnp.zeros_like(w1r)
    grp0 = jnp.concatenate([w1r, zeros], axis=-1)
    grp1 = jnp.concatenate([zeros, w1r], axis=-1)
    wbig = jnp.stack([grp0, grp1], axis=2).reshape(per, kwid, hid_w).astype(BF16)
    posr = pos_emb.reshape(per, CMP_STRIDE, 1, HEAD_DIM)
    posbig = jnp.broadcast_to(posr, (per, CMP_STRIDE, NSA_KV_GROUPS, HEAD_DIM)).reshape(per, 1, kwid)
    z2 = jnp.zeros_like(w2)
    w2big = jnp.concatenate([jnp.concatenate([w2, z2], axis=1),
                             jnp.concatenate([z2, w2], axis=1)], axis=0).astype(BF16)
    const = lambda shape: pl.BlockSpec(shape, lambda b: (0,) * len(shape))
    tab = pl.BlockSpec((1, ncp, LANES), lambda b: (b, 0, 0))
    if is_key:
        w2_arg = w2big
        out_spec = pl.BlockSpec((1, NSA_KV_GROUPS, ncp, HEAD_DIM), lambda b: (b, 0, 0, 0))
        out_shape = jax.ShapeDtypeStruct((bsz, NSA_KV_GROUPS, ncp, HEAD_DIM), BF16)
    else:
        w2_arg = w2big.T
        out_spec = pl.BlockSpec((1, NSA_KV_GROUPS * HEAD_DIM, ncp), lambda b: (b, 0, 0))
        out_shape = jax.ShapeDtypeStruct((bsz, NSA_KV_GROUPS * HEAD_DIM, ncp), BF16)
    return pl.pallas_call(
        functools.partial(_compress_kernel, is_key=is_key),
        grid=(bsz,),
        in_specs=[pl.BlockSpec((1, ncp, kwid), lambda b: (b, 0, 0)),
                  const((1, kwid)), const((1, kwid)),
                  const((kwid, hid_w)), const((kwid, hid_w)),
                  const(w2_arg.shape), tab, tab],
        out_specs=out_spec,
        out_shape=out_shape,
        compiler_params=_params("arbitrary"),
        name=name,
    )(r, posbig[0], posbig[1], wbig[0], wbig[1], w2_arg, cos_c, sin_c)


def _softmax_cols(s, bias, any_visible):
    sb = s + bias
    e = jnp.exp2(sb - jnp.max(sb, axis=0, keepdims=True))
    l = jnp.sum(e, axis=0, keepdims=True)
    return e * jnp.where(any_visible, 1.0 / l, 0.0)


def _online_step(s, tile_max, v_t, m_old, l_old, acc_old):
    m_new = jnp.maximum(m_old, tile_max)
    alpha = jnp.exp2(m_old - m_new)
    p = jnp.exp2(s - m_new)
    l_new = alpha * l_old + jnp.sum(p, axis=0, keepdims=True)
    acc_new = alpha * acc_old + _dot(v_t, p.astype(BF16))
    return m_new, l_new, acc_new


def _online_cols(s, v_t, m_old, l_old, acc_old):
    return _online_step(s, jnp.max(s, axis=0, keepdims=True), v_t, m_old, l_old, acc_old)


def _online_stored(s_ref, tile_max, v_t, m_old, l_old, acc_old):
    return _online_step(s_ref[...], tile_max, v_t, m_old, l_old, acc_old)


def _normalized(state):
    _, l, acc = state
    return acc * (1.0 / l)


def _sweep(n_earlier, n_chains, prepare, produce, consume, finalize, state, alive=None):
    chains = range(n_chains)

    def produce_all(slot, f, diagonal):
        ctx = prepare(f, diagonal)
        for c in chains:
            produce(ctx, slot, c)

    def consume_all(slot, f, diagonal, st):
        return tuple(consume(slot, f, diagonal, c, st[c]) for c in chains)

    def overlapped(p_slot, p_f, c_slot, c_f, c_diagonal, st):
        ctx = prepare(p_f, False)
        out = []
        for c in chains:
            produce(ctx, p_slot, c)
            out.append(consume(c_slot, c_f, c_diagonal, c, st[c]))
        return tuple(out)

    produce_all(0, 0, True)
    state = overlapped(1, 0, 0, 0, True, state)

    def pair(i, st):
        st = overlapped(0, 2 * i + 1, 1, 2 * i, False, st)
        if alive is None:
            return overlapped(1, 2 * i + 2, 0, 2 * i + 1, False, st)
        produce_all(1, 2 * i + 2, False)
        return lax.cond(alive(st), lambda s: consume_all(0, 2 * i + 1, False, s), lambda s: s, st)

    n_pairs = n_earlier >> 1
    odd = (n_earlier & 1) == 1
    if alive is None:
        state = lax.fori_loop(0, n_pairs, pair, state)
    else:
        _, state = lax.while_loop(lambda c: (c[0] < n_pairs) & alive(c[1]),
                                  lambda c: (c[0] + 1, pair(c[0], c[1])), (jnp.int32(0), state))
        odd = odd & alive(state)

    @pl.when(odd)
    def _():
        finalize(consume_all(1, n_earlier - 1, False, state))

    @pl.when(jnp.logical_not(odd))
    def _():
        finalize(state)


def _score_buffers(chains, qn):
    return [pltpu.VMEM((2, chains, KEY_TILE, qn), F32), pltpu.VMEM((2, chains, 1, qn), F32)]


def _online_init(n, dv, qn):
    return tuple((jnp.full((1, qn), NEG_INF, F32), jnp.zeros((1, qn), F32), jnp.zeros((dv, qn), F32))
                 for _ in range(n))


def _split_bf16(v, terms):
    out, rest = [], v
    for i in range(terms):
        part = rest.astype(BF16)
        out.append(part)
        if i + 1 < terms:
            rest = rest - part.astype(F32)
    return out


def _top_rows(key, n_top):
    n_rows, n_cols = key.shape

    def bisect(i, tau):
        cand = tau | lax.shift_left(jnp.int32(1), jnp.int32(30) - i)
        cnt = jnp.sum(jnp.where(key >= cand, 1.0, 0.0), axis=0, keepdims=True)
        return jnp.where(cnt >= float(n_top), cand, tau)

    tau = lax.fori_loop(0, 31, bisect, jnp.zeros((1, n_cols), I32))
    above = jnp.where(key > tau, 1.0, 0.0)
    equal = jnp.where(key == tau, 1.0, 0.0)
    need = float(n_top) - jnp.sum(above, axis=0, keepdims=True)
    lower = jnp.where(_iota((n_rows, n_rows), 1) < _iota((n_rows, n_rows), 0), 1.0, 0.0).astype(BF16)
    before = _dot(lower, equal.astype(BF16))
    return above + equal * jnp.where(before < need, 1.0, 0.0)

def _nsa_kernel(q_ref, kc_ref, vc_ref, ks_ref, vs_ref, kw_ref, vw_ref, g_ref, ovt_ref, o_ref,
                bias_scr, part_scr, s_scr, mx_scr, *, n_sel):
    qn = q_ref.shape[2]
    ncp = kc_ref.shape[2]
    ns = ovt_ref.shape[0]
    kt_n = KEY_TILE
    tiles_per_q = qn // kt_n
    qi = pl.program_id(2)
    q0 = qi * qn
    reps = range(NSA_REP)
    q_head = lambda r: q_ref[0, r * HEAD_DIM:(r + 1) * HEAD_DIM, :]

    kc = kc_ref[0, 0]
    vc_t = vc_ref[0]
    bias_c = jnp.where((_iota((ncp, qn), 0) * CMP_STRIDE + (CMP_BLOCK - 1)) <= (q0 + _iota((ncp, qn), 1)),
                       0.0, NEG_INF)
    sees_cmp = (q0 + _iota((1, qn), 1)) >= CMP_BLOCK - 1
    o_cmp, p_sum = [], None
    s_next = _dot(kc, q_head(0))
    for r in reps:
        s_cur = s_next
        if r + 1 < NSA_REP:
            s_next = _dot(kc, q_head(r + 1))
        p = _softmax_cols(s_cur, bias_c, sees_cmp)
        o_cmp.append(_dot(vc_t, p.astype(BF16)))
        p_sum = p if p_sum is None else p_sum + p
    ovt = ovt_ref[...]
    imp_t = None
    for term in _split_bf16(p_sum, 3):
        part = _dot(ovt, term)
        imp_t = part if imp_t is None else imp_t + part

    blk = _iota((ns, qn), 0)
    cur = (q0 + _iota((ns, qn), 1)) >> SLC_SHIFT
    forced = (blk == 0) | (blk == cur) | (blk == cur - 1)
    imp_bits = jnp.where(imp_t > 0.0, lax.bitcast_convert_type(imp_t, I32), 0)
    key = jnp.where(forced, jnp.int32(2 ** 31 - 1), jnp.where(blk <= cur, imp_bits, -1))

    picked = _top_rows(key, n_sel)
    bias_scr[...] = jnp.where(picked > 0.5, 0.0, NEG_INF)

    blocks_per_tile = kt_n // SLC_BLOCK
    causal = _iota((kt_n, qn), 0) <= _iota((kt_n, qn), 1)

    carry = list(_online_init(NSA_REP, HEAD_DIM, qn))
    win = {}
    for diagonal in (True, False):
        kt = qi if diagonal else jnp.maximum(qi - WINDOW // kt_n, 0)
        k0 = pl.multiple_of(kt * kt_n, kt_n)
        kp = k0 + _iota((kt_n, qn), 0)
        tq = q0 + _iota((kt_n, qn), 1)
        inside = (kp <= tq) if diagonal else ((kp > tq - WINDOW) & (kp < q0))
        win[diagonal] = (k0, kt, jnp.where(inside, 0.0, NEG_INF))
    jobs = [(diagonal, r) for diagonal in (True, False) for r in reps]
    win_scores = lambda job: _dot(kw_ref[0, 0, pl.ds(win[job[0]][0], kt_n), :], q_head(job[1]))
    s_next = win_scores(jobs[0])
    for n, (diagonal, r) in enumerate(jobs):
        s_cur = s_next
        if n + 1 < len(jobs):
            s_next = win_scores(jobs[n + 1])
        _, kt, bias_w = win[diagonal]
        carry[r] = _online_cols(s_cur + bias_w, vw_ref[0, kt], *carry[r])

    gates = g_ref[0]
    for r in reps:
        part_scr[r] = (gates[3 * r:3 * r + 1] * o_cmp[r]
                       + gates[3 * r + 2:3 * r + 3] * _normalized(carry[r]))

    def prepare(f, diagonal):
        kt = qi if diagonal else jnp.minimum(f, jnp.maximum(qi - 1, 0))
        k0 = pl.multiple_of(kt * kt_n, kt_n)
        rows = [jnp.broadcast_to(bias_scr[pl.ds(kt * blocks_per_tile + i, 1), :], (SLC_BLOCK, qn))
                for i in range(blocks_per_tile)]
        bias = jnp.concatenate(rows, axis=0)
        if diagonal:
            bias = jnp.where(causal, bias, NEG_INF)
        return k0, bias

    def produce(ctx, slot, r):
        k0, bias = ctx
        s = _dot(ks_ref[0, 0, pl.ds(k0, kt_n), :], q_head(r)) + bias
        s_scr[slot, r] = s
        mx_scr[slot, r] = jnp.max(s, axis=0, keepdims=True)

    def consume(slot, f, diagonal, r, state_r):
        return _online_stored(s_scr.at[slot, r], mx_scr[slot, r], vs_ref[0, qi if diagonal else f], *state_r)

    def finalize(state):
        for r in reps:
            o = part_scr[r] + g_ref[0, 3 * r + 1:3 * r + 2, :] * _normalized(state[r])
            o_ref[0, r * HEAD_DIM:(r + 1) * HEAD_DIM, :] = o.astype(o_ref.dtype)

    _sweep(qi, NSA_REP, prepare, produce, consume, finalize, _online_init(NSA_REP, HEAD_DIM, qn))


def _nsa_attention(q_t, k_cmp, v_cmp_t, k_slc, v_slc_t, k_win, v_win_t, gates_t):
    bsz, _, t = q_t.shape
    ncp = k_cmp.shape[2]
    ns = t // SLC_BLOCK
    qn = min(Q_TILE, t)
    grp_w = NSA_REP * HEAD_DIM
    cmp_start = jnp.arange(ncp) * CMP_STRIDE
    slc_start = jnp.arange(ns) * SLC_BLOCK
    real = jnp.arange(ncp) < (t - CMP_BLOCK) // CMP_STRIDE + 1
    overlap_t = ((cmp_start[None, :] < slc_start[:, None] + SLC_BLOCK)
                 & (cmp_start[None, :] + CMP_BLOCK > slc_start[:, None]) & real[None, :]).astype(BF16)
    k_spec = lambda n: pl.BlockSpec((1, 1, n, HEAD_DIM), lambda b, g, i: (b, g, 0, 0))
    v_spec = pl.BlockSpec((1, t // KEY_TILE, HEAD_DIM, KEY_TILE), lambda b, g, i: (b, 0, g, 0))
    return pl.pallas_call(
        functools.partial(_nsa_kernel, n_sel=min(N_SELECT, ns)),
        grid=(bsz, NSA_KV_GROUPS, t // qn),
        in_specs=[pl.BlockSpec((1, grp_w, qn), lambda b, g, i: (b, g, i)),
                  k_spec(ncp),
                  pl.BlockSpec((1, HEAD_DIM, ncp), lambda b, g, i: (b, g, 0)),
                  k_spec(t), v_spec, k_spec(t), v_spec,
                  pl.BlockSpec((1, GATE_ROWS, qn), lambda b, g, i: (b, g, i)),
                  pl.BlockSpec((ns, ncp), lambda b, g, i: (0, 0))],
        out_specs=pl.BlockSpec((1, grp_w, qn), lambda b, g, i: (b, g, i)),
        out_shape=jax.ShapeDtypeStruct((bsz, NSA_HEADS * HEAD_DIM, t), BF16),
        scratch_shapes=[pltpu.VMEM((ns, qn), F32), pltpu.VMEM((NSA_REP, HEAD_DIM, qn), F32)] + _score_buffers(NSA_REP, qn),
        compiler_params=_params("arbitrary", "arbitrary", "arbitrary"),
        name="nsa_attention",
    )(q_t, k_cmp, v_cmp_t, k_slc, v_slc_t, k_win, v_win_t, gates_t, overlap_t)


SB_HEADS_PER_STEP = 4
SB_DEAD_TAIL = -110.0
SB_SCAN_BLOCK = 128


def _sb_kernel(q_ref, k_ref, v_ref, o_ref, nz_scr):
    qn = q_ref.shape[2]
    kt_n = KEY_TILE
    qi = pl.program_id(2)
    heads = range(SB_HEADS_PER_STEP)
    rows = [slice(h * HEAD_DIM, (h + 1) * HEAD_DIM) for h in heads]
    sub = SB_SCAN_BLOCK
    n_sub = kt_n // sub
    tri = jnp.where(_iota((sub, sub), 1) > _iota((sub, sub), 0), 1.0, 0.0).astype(BF16)
    later2 = jnp.concatenate([tri, tri], axis=1)
    strictly_before = _iota((kt_n, qn), 0) < _iota((kt_n, qn), 1)

    def tile_index(f, diagonal):
        return qi if diagonal else jnp.maximum(qi - 1 - f, 0)

    def prepare(f, diagonal):
        return pl.multiple_of(tile_index(f, diagonal) * kt_n, kt_n)

    def produce(k0, slot, h):
        nz_scr[slot, h] = _dot(k_ref[0, h, pl.ds(k0, kt_n), :], -q_ref[0, rows[h], :])

    def consume(slot, f, diagonal, h, state_h):
        tail, acc = state_h
        nz = nz_scr[slot, h]
        neg_abs = lax.bitcast_convert_type(lax.bitcast_convert_type(nz, jnp.uint32) | jnp.uint32(0x80000000), F32)
        soft = jnp.log(1.0 + jnp.exp(neg_abs))
        log_keep = jnp.minimum(nz, 0.0) - soft
        log_beta = log_keep - nz
        if diagonal:
            log_keep = jnp.where(strictly_before, log_keep, 0.0)
        offs = tail
        parts = [None] * n_sub
        for blk in reversed(range(n_sub)):
            lk = log_keep[blk * sub:(blk + 1) * sub]
            hi = lk.astype(BF16)
            lo = (lk - hi.astype(F32)).astype(BF16)
            within = _dot(later2, jnp.concatenate([hi, lo], axis=0))
            parts[blk] = log_beta[blk * sub:(blk + 1) * sub] + within + offs
            offs = offs + jnp.sum(lk, axis=0, keepdims=True)
        a = jnp.exp(jnp.concatenate(parts, axis=0))
        if diagonal:
            a = jnp.where(strictly_before, a, 0.0)
        return offs, acc + _dot(v_ref[0, tile_index(f, diagonal), rows[h], :], a.astype(BF16))

    def finalize(state):
        for h in heads:
            o_ref[0, rows[h], :] = state[h][1].astype(o_ref.dtype)

    def alive(state):
        tails = [state[h][0] for h in heads]
        return jnp.max(functools.reduce(jnp.maximum, tails)) > SB_DEAD_TAIL

    init = tuple((jnp.zeros((1, qn), F32), jnp.zeros((HEAD_DIM, qn), F32)) for _ in heads)
    _sweep(qi, SB_HEADS_PER_STEP, prepare, produce, consume, finalize, init, alive)


def _sb_attention(q_t, k, v_t):
    bsz, nh, t, _ = k.shape
    qn = min(Q_TILE, t)
    hps = SB_HEADS_PER_STEP
    w = hps * HEAD_DIM
    return pl.pallas_call(
        _sb_kernel,
        grid=(bsz, nh // hps, t // qn),
        in_specs=[pl.BlockSpec((1, w, qn), lambda b, h, i: (b, h, i)),
                  pl.BlockSpec((1, hps, t, HEAD_DIM), lambda b, h, i: (b, h, 0, 0)),
                  pl.BlockSpec((1, t // KEY_TILE, w, KEY_TILE), lambda b, h, i: (b, 0, h, 0))],
        out_specs=pl.BlockSpec((1, w, qn), lambda b, h, i: (b, h, i)),
        out_shape=jax.ShapeDtypeStruct((bsz, nh * HEAD_DIM, t), BF16),
        scratch_shapes=[pltpu.VMEM((2, hps, KEY_TILE, qn), F32)],
        compiler_params=_params("arbitrary", "arbitrary", "arbitrary"),
        name="stickbreak_attention",
    )(q_t, k, v_t)


DIFF_HEADS_PER_STEP = 2


def _diff_kernel(q_ref, k_ref, v_ref, lq1_ref, lk1_ref, lq2_ref, lk2_ref, sub_ref, o_ref, s_scr, mx_scr,
                 *, lambda_init):
    qn = q_ref.shape[2]
    vd = 2 * HEAD_DIM
    kt_n = KEY_TILE
    qi = pl.program_id(2)
    chains = range(2 * DIFF_HEADS_PER_STEP)
    lam = (jnp.exp(jnp.sum(lq1_ref[...] * lk1_ref[...], axis=1, keepdims=True))
           - jnp.exp(jnp.sum(lq2_ref[...] * lk2_ref[...], axis=1, keepdims=True)) + lambda_init)

    causal = _iota((kt_n, qn), 0) <= _iota((kt_n, qn), 1)

    def prepare(f, diagonal):
        kt = qi if diagonal else jnp.minimum(f, jnp.maximum(qi - 1, 0))
        return pl.multiple_of(kt * kt_n, kt_n), diagonal

    def produce(ctx, slot, c):
        k0, diagonal = ctx
        s = _dot(k_ref[0, c, pl.ds(k0, kt_n), :], q_ref[0, c * HEAD_DIM:(c + 1) * HEAD_DIM, :])
        if diagonal:
            s = jnp.where(causal, s, NEG_INF)
        s_scr[slot, c] = s
        mx_scr[slot, c] = jnp.max(s, axis=0, keepdims=True)

    def consume(slot, f, diagonal, c, state_c):
        kt = qi if diagonal else f
        return _online_stored(s_scr.at[slot, c], mx_scr[slot, c],
                              v_ref[0, kt, (c // 2) * vd:(c // 2 + 1) * vd, :], *state_c)

    def finalize(state):
        for h in range(DIFF_HEADS_PER_STEP):
            o = _normalized(state[2 * h]) - lam * _normalized(state[2 * h + 1])
            ms = jnp.mean(o * o, axis=0, keepdims=True)
            o = o * lax.rsqrt(ms + RMS_EPS) * sub_ref[...] * (1.0 - lambda_init)
            o_ref[0, h * vd:(h + 1) * vd, :] = o.astype(o_ref.dtype)

    _sweep(qi, len(chains), prepare, produce, consume, finalize, _online_init(len(chains), vd, qn))


def _diff_attention(q_t, k, v_t, lq1, lk1, lq2, lk2, subln, lambda_init):
    bsz, nh2, t, _ = k.shape
    nh = nh2 // 2
    vd = 2 * HEAD_DIM
    hps = DIFF_HEADS_PER_STEP
    qn = min(Q_TILE, t)
    row = lambda a: a.reshape(1, -1)
    small = lambda n: pl.BlockSpec((1, n), lambda b, h, i: (0, 0))
    return pl.pallas_call(
        functools.partial(_diff_kernel, lambda_init=lambda_init),
        grid=(bsz, nh // hps, t // qn),
        in_specs=[pl.BlockSpec((1, hps * vd, qn), lambda b, h, i: (b, h, i)),
                  pl.BlockSpec((1, 2 * hps, t, HEAD_DIM), lambda b, h, i: (b, h, 0, 0)),
                  pl.BlockSpec((1, t // KEY_TILE, hps * vd, KEY_TILE), lambda b, h, i: (b, 0, h, 0)),
                  small(HEAD_DIM), small(HEAD_DIM), small(HEAD_DIM), small(HEAD_DIM),
                  pl.BlockSpec((vd, 1), lambda b, h, i: (0, 0))],
        out_specs=pl.BlockSpec((1, hps * vd, qn), lambda b, h, i: (b, h, i)),
        out_shape=jax.ShapeDtypeStruct((bsz, nh * vd, t), BF16),
        scratch_shapes=_score_buffers(2 * hps, qn),
        compiler_params=_params("arbitrary", "arbitrary", "arbitrary"),
        name="diff_attention",
    )(q_t, k, v_t, row(lq1), row(lk1), row(lq2), row(lk2), subln.reshape(vd, 1))


def _outproj_kernel(*refs, n_in):
    a_refs, w_refs = refs[:n_in], refs[n_in:2 * n_in]
    x_ref, gate_ref, o_ref = refs[2 * n_in:]
    y = None
    for a_ref, w_ref in zip(a_refs, w_refs):
        part = lax.dot_general(a_ref[0], w_ref[...].astype(BF16), _TN, preferred_element_type=F32)
        y = part if y is None else y + part
    o_ref[0] = x_ref[0] + gate_ref[0] * y


def _out_project(acts_t, weight, x, gate, name):
    bsz, t, d = x.shape
    rows = min(t, PROJ_ROWS)
    k_rows = acts_t[0].shape[1]
    tile = pl.BlockSpec((1, rows, d), lambda b, i: (b, i, 0))
    return pl.pallas_call(
        functools.partial(_outproj_kernel, n_in=len(acts_t)),
        grid=(bsz, t // rows),
        in_specs=([pl.BlockSpec((1, a.shape[1], rows), lambda b, i: (b, 0, i)) for a in acts_t]
                  + [pl.BlockSpec((k_rows, d), lambda b, i, n=n: (n, 0)) for n in range(len(acts_t))]
                  + [tile, pl.BlockSpec((1, 1, d), lambda b, i: (b, 0, 0))]),
        out_specs=tile,
        out_shape=jax.ShapeDtypeStruct((bsz, t, d), F32),
        compiler_params=_params("arbitrary", "arbitrary"),
        name=name,
    )(*acts_t, *([weight] * len(acts_t)), x, gate)


def _ffn_kernel(x_ref, halo_ref, g_ref, sh_ref, sc_ref, gate_ref, wg_ref, wu_ref, cw_ref, cb_ref, wd_ref,
                nf_ref, o_ref, a_scr, *, final_norm):
    ti = pl.program_id(1)
    h = _norm_modulate(x_ref[0], g_ref[...], sh_ref[0], sc_ref[0]).astype(BF16)
    h_prev = _norm_modulate(halo_ref[0], g_ref[...], sh_ref[0], sc_ref[0]).astype(BF16)
    seen = jnp.where(ti > 0, 1.0, 0.0)
    cols = FFN_COLS
    row = _iota((h.shape[0], cols), 0)
    for c in range(a_scr.shape[1] // cols):
        sl = slice(c * cols, (c + 1) * cols)
        wg = wg_ref[:, sl].astype(BF16)
        g = _dot(h, wg)
        g_prev = _dot(h_prev, wg) * seen
        last = g_prev[HALO_ROWS - 1:HALO_ROWS]
        last2 = g_prev[HALO_ROWS - 2:HALO_ROWS - 1]
        g1 = jnp.where(row == 0, last, pltpu.roll(g, 1, 0))
        g2 = jnp.where(row == 0, last2, jnp.where(row == 1, last, pltpu.roll(g, 2, 0)))
        cw = cw_ref[:, sl]
        conv = cw[0:1] * g2 + cw[1:2] * g1 + cw[2:3] * g + cb_ref[:, sl]
        a_scr[:, sl] = (conv * _sigmoid(conv) * _dot(h, wu_ref[:, sl].astype(BF16))).astype(BF16)
    y = x_ref[0] + gate_ref[0] * _dot(a_scr[...], wd_ref[...])
    if final_norm:
        ms = jnp.mean(y * y, axis=-1, keepdims=True)
        y = y * lax.rsqrt(ms + RMS_EPS) * nf_ref[...]
    o_ref[0] = y


def _conv_ffn(x, g, shift, scale, gate, w_gate, w_up, conv_w, conv_b, w_down, norm_f, final_norm, name):
    bsz, t, d = x.shape
    f = w_gate.shape[1]
    rows = min(t, FFN_ROWS)
    halo_blocks = rows // HALO_ROWS
    vec = pl.BlockSpec((1, 1, d), lambda b, i: (b, 0, 0))
    drow = pl.BlockSpec((1, d), lambda b, i: (0, 0))
    resident = lambda shape: pl.BlockSpec(shape, lambda b, i: (0, 0), pipeline_mode=pl.Buffered(1))
    return pl.pallas_call(
        functools.partial(_ffn_kernel, final_norm=final_norm),
        grid=(bsz, t // rows),
        in_specs=[pl.BlockSpec((1, rows, d), lambda b, i: (b, i, 0)),
                  pl.BlockSpec((1, HALO_ROWS, d), lambda b, i: (b, jnp.maximum(i * halo_blocks - 1, 0), 0)),
                  drow, vec, vec, vec,
                  resident((d, f)), resident((d, f)), resident((CONV_WIDTH, f)), resident((1, f)),
                  resident((f, d)), drow],
        out_specs=pl.BlockSpec((1, rows, d), lambda b, i: (b, i, 0)),
        out_shape=jax.ShapeDtypeStruct((bsz, t, d), F32),
        scratch_shapes=[pltpu.VMEM((rows, f), BF16)],
        compiler_params=_params("arbitrary", "arbitrary"),
        name=name,
    )(x, x, g.reshape(1, d), shift, scale, gate, w_gate, w_up, conv_w,
      conv_b.reshape(1, f), w_down.astype(BF16), norm_f.reshape(1, d))


def _layout(widths):
    starts, start = [], 0
    for w in widths:
        starts.append(start)
        start += w
    return starts


def _hybrid_layer(x, mods, tables, norm_g, w_in, pos_k, pos_v, ck_w1, ck_w2, cv_w1, cv_w2, w_out):
    shift, scale, gate = mods
    bsz, t, d = x.shape
    qw, kvw, sbw = NSA_HEADS * HEAD_DIM, NSA_KV_GROUPS * HEAD_DIM, SB_HEADS * HEAD_DIM
    col = _layout((qw,) + (kvw,) * 6 + (3 * NSA_HEADS,) + (sbw,) * 3)
    cols = lambda i, w: w_in[:, col[i]:col[i] + w]
    q_n, kc, vc, ks, vs, kw, vw, gl, q_s, k_s, v_s = (cols(i, w) for i, w in enumerate(
        (qw,) + (kvw,) * 6 + (3 * NSA_HEADS,) + (sbw,) * 3))
    inv_sqrt = HEAD_DIM ** -0.5

    w_tok = jnp.concatenate([ks, kw, kc, vc, k_s], axis=1).astype(BF16)
    ts = _layout((kvw, kvw, kvw, kvw, sbw))
    tok_segs = [TokSeg(ts[0], kvw, True, HEAD_DIM, BF16),
                TokSeg(ts[1], kvw, True, HEAD_DIM, BF16),
                TokSeg(ts[2], kvw, False, 0, F32),
                TokSeg(ts[3], kvw, False, 0, F32),
                TokSeg(ts[4], sbw, False, HEAD_DIM, BF16)]
    per_group = 3 * NSA_REP
    gate_pad = jnp.zeros((d, GATE_ROWS - per_group), w_in.dtype)
    gl_pad = jnp.concatenate([gl[:, :per_group], gate_pad, gl[:, per_group:], gate_pad], axis=1)
    w_feat_t = jnp.concatenate([q_n, q_s, vs, vw, v_s, gl_pad], axis=1).T.astype(BF16)
    fs = _layout((qw, sbw, kvw, kvw, sbw, NSA_KV_GROUPS * GATE_ROWS))
    feat_segs = [FeatSeg(fs[0], qw, True, inv_sqrt * LOG2E, False, False, BF16),
                 FeatSeg(fs[1], sbw, False, inv_sqrt, False, False, BF16),
                 FeatSeg(fs[2], kvw, False, 1.0, False, True, BF16),
                 FeatSeg(fs[3], kvw, False, 1.0, False, True, BF16),
                 FeatSeg(fs[4], sbw, False, 1.0, False, True, BF16),
                 FeatSeg(fs[5], NSA_KV_GROUPS * GATE_ROWS, False, 1.0, True, False, F32)]
    (k_slc, k_win, kc_a, vc_a, k_sb, q_n_t, q_s_t, v_slc_t, v_win_t, v_sb_t, gates_t) = _project(
        x, norm_g, shift, scale, w_tok, w_feat_t, tables, tok_segs, feat_segs, "hybrid_in_proj")

    ncp = t // CMP_STRIDE
    end_rows = jnp.minimum(jnp.arange(ncp) * CMP_STRIDE + CMP_BLOCK - 1, t - 1)
    cos_c, sin_c = tables[0][:, end_rows], tables[1][:, end_rows]
    k_cmp = _compress(kc_a, pos_k, ck_w1, ck_w2, cos_c, sin_c, True, "compress_k")
    v_cmp_t = _compress(vc_a, pos_v, cv_w1, cv_w2, cos_c, sin_c, False, "compress_v")
    o_nsa_t = _nsa_attention(q_n_t, k_cmp, v_cmp_t, k_slc, v_slc_t, k_win, v_win_t, gates_t)
    o_sb_t = _sb_attention(q_s_t, k_sb, v_sb_t)
    return _out_project([o_nsa_t, o_sb_t], w_out, x, gate, "hybrid_out_proj")


def _diff_layer(x, mods, tables, norm_g, w_qkv, lq1, lk1, lq2, lk2, subln, w_out, layer_idx):
    shift, scale, gate = mods
    dw = 2 * DIFF_HEADS * HEAD_DIM
    inv_sqrt = HEAD_DIM ** -0.5
    w_tok = w_qkv[:, dw:2 * dw].astype(BF16)
    w_feat_t = jnp.concatenate([w_qkv[:, :dw], w_qkv[:, 2 * dw:]], axis=1).T.astype(BF16)
    tok_segs = [TokSeg(0, dw, True, HEAD_DIM, BF16)]
    feat_segs = [FeatSeg(0, dw, True, inv_sqrt * LOG2E, False, False, BF16),
                 FeatSeg(dw, dw, False, 1.0, False, True, BF16)]
    k, q_t, v_t = _project(x, norm_g, shift, scale, w_tok, w_feat_t, tables, tok_segs, feat_segs, "diff_in_proj")
    lambda_init = 0.8 - 0.6 * math.exp(-0.3 * layer_idx)
    o_t = _diff_attention(q_t, k, v_t, lq1, lk1, lq2, lk2, subln, lambda_init)
    return _out_project([o_t], w_out, x, gate, "diff_out_proj")


def kernel(x, c, positions, mod_w, mod_b, norm_mix, norm_ffn, ffn_w_gate, ffn_w_up, ffn_conv_w, ffn_conv_b, ffn_w_down, hyb_w_in, nsa_pos_k, nsa_pos_v, nsa_ck_w1, nsa_ck_w2, nsa_cv_w1, nsa_cv_w2, hyb_w_out, diff_w_qkv, diff_lq1, diff_lk1, diff_lq2, diff_lk2, diff_subln, diff_w_out, norm_f):
    bsz, t, d = x.shape
    depth = mod_w.shape[0]
    mod = _adaln_mod(c, mod_w, mod_b)
    tables = _rope_tables(positions)
    for i in range(depth):
        sh_m, sc_m, g_m, sh_f, sc_f, g_f = (mod[i, :, k * d:(k + 1) * d].reshape(bsz, 1, d) for k in range(6))
        j = i // 2
        if i % 2 == 0:
            x = _hybrid_layer(x, (sh_m, sc_m, g_m), tables, norm_mix[i], hyb_w_in[j],
                              nsa_pos_k[j], nsa_pos_v[j], nsa_ck_w1[j], nsa_ck_w2[j], nsa_cv_w1[j],
                              nsa_cv_w2[j], hyb_w_out[j])
        else:
            x = _diff_layer(x, (sh_m, sc_m, g_m), tables, norm_mix[i], diff_w_qkv[j], diff_lq1[j],
                            diff_lk1[j], diff_lq2[j], diff_lk2[j], diff_subln[j], diff_w_out[j], i)
        x = _conv_ffn(x, norm_ffn[i], sh_f, sc_f, g_f, ffn_w_gate[i], ffn_w_up[i], ffn_conv_w[i],
                      ffn_conv_b[i], ffn_w_down[i], norm_f, i == depth - 1, "conv_ffn_%d" % i)
    return x
```

```python
import functools
import math
from typing import NamedTuple

import jax
import jax.numpy as jnp
from jax import lax
from jax.experimental import pallas as pl
from jax.experimental.pallas import tpu as pltpu

F32 = jnp.float32
BF16 = jnp.bfloat16
I32 = jnp.int32

HEAD_DIM = 64
ROPE_DIM = HEAD_DIM // 4
ROPE_HALF = ROPE_DIM // 2
ROPE_THETA = 500000.0
NSA_HEADS = 8
NSA_KV_GROUPS = 2
NSA_REP = NSA_HEADS // NSA_KV_GROUPS
CMP_BLOCK = 32
CMP_STRIDE = 16
CMP_HIDDEN = 4 * HEAD_DIM
SLC_BLOCK = 64
SLC_SHIFT = 6
N_SELECT = 16
WINDOW = 512
SB_HEADS = 8
DIFF_HEADS = 8
CONV_WIDTH = 3
RMS_EPS = 1e-6
NEG_INF = -1e30
FORCE = 1e9
LOG2E = math.log2(math.e)

LANES = 128
Q_TILE = 512
KEY_TILE = 512
GATE_ROWS = 16
PROJ_ROWS = 512
FFN_ROWS = 512
FFN_COLS = 256
HALO_ROWS = 16
VMEM_LIMIT = 56 * 1024 * 1024

_NT = (((1,), (1,)), ((), ()))
_TN = (((0,), (0,)), ((), ()))


def _params(*sem):
    return pltpu.CompilerParams(dimension_semantics=sem, vmem_limit_bytes=VMEM_LIMIT)


def _sigmoid(v):
    return 1.0 / (1.0 + jnp.exp(-v))


def _iota(shape, axis):
    return lax.broadcasted_iota(I32, shape, axis)


def _dot(a, b):
    return jnp.dot(a, b, preferred_element_type=F32)


def _mod_kernel(c_ref, w_ref, b_ref, o_ref):
    c = c_ref[...]
    cond = c * _sigmoid(c)
    o_ref[0] = jnp.dot(cond, w_ref[0], preferred_element_type=F32,
                       precision=lax.Precision.HIGHEST) + b_ref[0]


def _adaln_mod(c, mod_w, mod_b):
    depth, d, n = mod_w.shape
    bsz = c.shape[0]
    tn = n // 4
    return pl.pallas_call(
        _mod_kernel,
        grid=(depth, n // tn),
        in_specs=[pl.BlockSpec((bsz, d), lambda i, j: (0, 0)),
                  pl.BlockSpec((1, d, tn), lambda i, j: (i, 0, j)),
                  pl.BlockSpec((1, 1, tn), lambda i, j: (i, 0, j))],
        out_specs=pl.BlockSpec((1, bsz, tn), lambda i, j: (i, 0, j)),
        out_shape=jax.ShapeDtypeStruct((depth, bsz, n), F32),
        compiler_params=_params("arbitrary", "arbitrary"),
        name="adaln_mod",
    )(c, mod_w, mod_b.reshape(depth, 1, n))


def _rope_kernel(pos_col_ref, pos_row_ref, inv_row_ref, sgn_row_ref, inv_col_ref,
                 cos_ref, sin_ref, cos_t_ref, sin_t_ref):
    ang = pos_col_ref[0].astype(F32) * inv_row_ref[...]
    cos_ref[0] = jnp.cos(ang)
    sin_ref[0] = jnp.sin(ang) * sgn_row_ref[...]
    ang_t = inv_col_ref[...] * pos_row_ref[0].astype(F32)
    cos_t_ref[0] = jnp.cos(ang_t)
    sin_t_ref[0] = jnp.sin(ang_t)


def _rope_tables(positions):
    bsz, t = positions.shape
    inv = ROPE_THETA ** (-jnp.arange(0, ROPE_DIM, 2, dtype=F32) / ROPE_DIM)
    per_head_inv = jnp.concatenate([inv, inv, jnp.zeros((HEAD_DIM - ROPE_DIM,), F32)])
    per_head_sgn = jnp.concatenate([-jnp.ones((ROPE_HALF,), F32), jnp.ones((ROPE_HALF,), F32),
                                    jnp.zeros((HEAD_DIM - ROPE_DIM,), F32)])
    inv_row = jnp.tile(per_head_inv, LANES // HEAD_DIM)[None, :]
    sgn_row = jnp.tile(per_head_sgn, LANES // HEAD_DIM)[None, :]
    rows = min(t, 1024)
    tab = jax.ShapeDtypeStruct((bsz, t, LANES), F32)
    tab_t = jax.ShapeDtypeStruct((bsz, ROPE_HALF, t), F32)
    row_spec = pl.BlockSpec((1, LANES), lambda b, i: (0, 0))
    return pl.pallas_call(
        _rope_kernel,
        grid=(bsz, t // rows),
        in_specs=[pl.BlockSpec((1, rows, 1), lambda b, i: (b, i, 0)),
                  pl.BlockSpec((1, 1, rows), lambda b, i: (b, 0, i)),
                  row_spec, row_spec,
                  pl.BlockSpec((ROPE_HALF, 1), lambda b, i: (0, 0))],
        out_specs=[pl.BlockSpec((1, rows, LANES), lambda b, i: (b, i, 0))] * 2
        + [pl.BlockSpec((1, ROPE_HALF, rows), lambda b, i: (b, 0, i))] * 2,
        out_shape=[tab, tab, tab_t, tab_t],
        compiler_params=_params("arbitrary", "arbitrary"),
        name="rope_tables",
    )(positions.reshape(bsz, t, 1), positions.reshape(bsz, 1, t), inv_row, sgn_row, inv[:, None])


def _rope_chunk(y, cosv, sinv, first_half):
    ahead = pltpu.roll(y, LANES - ROPE_HALF, 1)
    behind = pltpu.roll(y, ROPE_HALF, 1)
    return y * cosv + jnp.where(first_half, ahead, behind) * sinv


def _rope_rows(y_t, cos_t, sin_t):
    heads = []
    for h in range(y_t.shape[0] // HEAD_DIM):
        blk = y_t[h * HEAD_DIM:(h + 1) * HEAD_DIM]
        x1, x2 = blk[:ROPE_HALF], blk[ROPE_HALF:ROPE_DIM]
        heads += [x1 * cos_t - x2 * sin_t, x2 * cos_t + x1 * sin_t, blk[ROPE_DIM:]]
    return jnp.concatenate(heads, axis=0)


class TokSeg(NamedTuple):
    start: int
    width: int
    rope: bool
    head_width: int
    dtype: object


class FeatSeg(NamedTuple):
    start: int
    rows: int
    rope: bool
    scale: float
    sigmoid: bool
    key_tiled: bool
    dtype: object


def _norm_modulate(x, g, shift, scale):
    ms = jnp.mean(x * x, axis=-1, keepdims=True)
    y = x * lax.rsqrt(ms + RMS_EPS) * g
    return y * (1.0 + scale) + shift


def _proj_kernel(x_ref, g_ref, sh_ref, sc_ref, w_ref, wt_ref, cos_ref, sin_ref, cos_t_ref, sin_t_ref,
                 *out_refs, tok_segs, feat_segs):
    hb = _norm_modulate(x_ref[0], g_ref[...], sh_ref[0], sc_ref[0]).astype(BF16)
    cosv, sinv = cos_ref[0], sin_ref[0]
    first_half = (_iota(cosv.shape, 1) & (HEAD_DIM - 1)) < ROPE_HALF
    tok_refs, feat_refs = out_refs[:len(tok_segs)], out_refs[len(tok_segs):]
    for seg, o_ref in zip(tok_segs, tok_refs):
        y = _dot(hb, w_ref[:, seg.start:seg.start + seg.width])
        for ch in range(seg.width // LANES):
            yc = y[:, ch * LANES:(ch + 1) * LANES]
            if seg.rope:
                yc = _rope_chunk(yc, cosv, sinv, first_half)
            yc = yc.astype(seg.dtype)
            if seg.head_width == 0:
                o_ref[0, :, ch * LANES:(ch + 1) * LANES] = yc
            else:
                o_ref[0, 2 * ch] = yc[:, :HEAD_DIM]
                o_ref[0, 2 * ch + 1] = yc[:, HEAD_DIM:]
    for seg, o_ref in zip(feat_segs, feat_refs):
        y_t = lax.dot_general(wt_ref[seg.start:seg.start + seg.rows, :], hb, _NT,
                              preferred_element_type=F32)
        if seg.rope:
            y_t = _rope_rows(y_t, cos_t_ref[0], sin_t_ref[0])
        if seg.scale != 1.0:
            y_t = y_t * seg.scale
        if seg.sigmoid:
            y_t = _sigmoid(y_t)
        y_t = y_t.astype(seg.dtype)
        if seg.key_tiled:
            for ch in range(y_t.shape[1] // KEY_TILE):
                o_ref[0, ch] = y_t[:, ch * KEY_TILE:(ch + 1) * KEY_TILE]
        else:
            o_ref[0] = y_t


def _project(x, g, shift, scale, w_tok, w_feat_t, tables, tok_segs, feat_segs, name):
    bsz, t, d = x.shape
    rows = min(t, PROJ_ROWS)
    cos_tab, sin_tab, cos_t, sin_t = tables
    out_shapes, out_specs = [], []
    for seg in tok_segs:
        if seg.head_width == 0:
            out_shapes.append(jax.ShapeDtypeStruct((bsz, t, seg.width), seg.dtype))
            out_specs.append(pl.BlockSpec((1, rows, seg.width), lambda b, i: (b, i, 0)))
        else:
            nh = seg.width // seg.head_width
            out_shapes.append(jax.ShapeDtypeStruct((bsz, nh, t, seg.head_width), seg.dtype))
            out_specs.append(pl.BlockSpec((1, nh, rows, seg.head_width), lambda b, i: (b, 0, i, 0)))
    for seg in feat_segs:
        if seg.key_tiled:
            out_shapes.append(jax.ShapeDtypeStruct((bsz, t // KEY_TILE, seg.rows, KEY_TILE), seg.dtype))
            out_specs.append(pl.BlockSpec((1, rows // KEY_TILE, seg.rows, KEY_TILE), lambda b, i: (b, i, 0, 0)))
        else:
            out_shapes.append(jax.ShapeDtypeStruct((bsz, seg.rows, t), seg.dtype))
            out_specs.append(pl.BlockSpec((1, seg.rows, rows), lambda b, i: (b, 0, i)))
    vec = pl.BlockSpec((1, 1, d), lambda b, i: (b, 0, 0))
    tab = pl.BlockSpec((1, rows, LANES), lambda b, i: (b, i, 0))
    tab_t = pl.BlockSpec((1, ROPE_HALF, rows), lambda b, i: (b, 0, i))
    return pl.pallas_call(
        functools.partial(_proj_kernel, tok_segs=tuple(tok_segs), feat_segs=tuple(feat_segs)),
        grid=(bsz, t // rows),
        in_specs=[pl.BlockSpec((1, rows, d), lambda b, i: (b, i, 0)),
                  pl.BlockSpec((1, d), lambda b, i: (0, 0)),
                  vec, vec,
                  pl.BlockSpec(w_tok.shape, lambda b, i: (0, 0)),
                  pl.BlockSpec(w_feat_t.shape, lambda b, i: (0, 0)),
                  tab, tab, tab_t, tab_t],
        out_specs=out_specs,
        out_shape=out_shapes,
        compiler_params=_params("arbitrary", "arbitrary"),
        name=name,
    )(x, g.reshape(1, d), shift, scale, w_tok, w_feat_t, cos_tab, sin_tab, cos_t, sin_t)


def _compress_kernel(r_ref, pa_ref, pb_ref, wa_ref, wb_ref, w2_ref, cos_ref, sin_ref, o_ref, *, is_key):
    r = r_ref[0]
    ncp = r.shape[0]
    a = _dot((r + pa_ref[...]).astype(BF16), wa_ref[...])
    b = _dot((r + pb_ref[...]).astype(BF16), wb_ref[...])
    hid = a + pltpu.roll(b, ncp - 1, 0)
    hid = (hid * _sigmoid(hid)).astype(BF16)
    if is_key:
        y = _dot(hid, w2_ref[...])
        first_half = (_iota(y.shape, 1) & (HEAD_DIM - 1)) < ROPE_HALF
        y = _rope_chunk(y, cos_ref[0], sin_ref[0], first_half).astype(o_ref.dtype)
        for g in range(NSA_KV_GROUPS):
            o_ref[0, g] = y[:, g * HEAD_DIM:(g + 1) * HEAD_DIM]
    else:
        o_ref[0] = lax.dot_general(w2_ref[...], hid, _NT, preferred_element_type=F32).astype(o_ref.dtype)


def _compress(kv, pos_emb, w1, w2, cos_c, sin_c, is_key, name):
    bsz, t, _ = kv.shape
    ncp = t // CMP_STRIDE
    kwid = CMP_STRIDE * NSA_KV_GROUPS * HEAD_DIM
    hid_w = NSA_KV_GROUPS * CMP_HIDDEN
    r = kv.reshape(bsz, ncp, kwid)
    per = CMP_BLOCK // CMP_STRIDE
    w1r = w1.reshape(per, CMP_STRIDE, HEAD_DIM, CMP_HIDDEN)
    zeros = jnp.zeros_like(w1r)
    grp0 = jnp.concatenate([w1r, zeros], axis=-1)
    grp1 = jnp.concatenate([zeros, w1r], axis=-1)
    wbig = jnp.stack([grp0, grp1], axis=2).reshape(per, kwid, hid_w).astype(BF16)
    posr = pos_emb.reshape(per, CMP_STRIDE, 1, HEAD_DIM)
    posbig = jnp.broadcast_to(posr, (per, CMP_STRIDE, NSA_KV_GROUPS, HEAD_DIM)).reshape(per, 1, kwid)
    z2 = jnp.zeros_like(w2)
    w2big = jnp.concatenate([jnp.concatenate([w2, z2], axis=1),
                             jnp.concatenate([z2, w2], axis=1)], axis=0).astype(BF16)
    const = lambda shape: pl.BlockSpec(shape, lambda b: (0,) * len(shape))
    tab = pl.BlockSpec((1, ncp, LANES), lambda b: (b, 0, 0))
    if is_key:
        w2_arg = w2big
        out_spec = pl.BlockSpec((1, NSA_KV_GROUPS, ncp, HEAD_DIM), lambda b: (b, 0, 0, 0))
        out_shape = jax.ShapeDtypeStruct((bsz, NSA_KV_GROUPS, ncp, HEAD_DIM), BF16)
    else:
        w2_arg = w2big.T
        out_spec = pl.BlockSpec((1, NSA_KV_GROUPS * HEAD_DIM, ncp), lambda b: (b, 0, 0))
        out_shape = jax.ShapeDtypeStruct((bsz, NSA_KV_GROUPS * HEAD_DIM, ncp), BF16)
    return pl.pallas_call(
        functools.partial(_compress_kernel, is_key=is_key),
        grid=(bsz,),
        in_specs=[pl.BlockSpec((1, ncp, kwid), lambda b: (b, 0, 0)),
                  const((1, kwid)), const((1, kwid)),
                  const((kwid, hid_w)), const((kwid, hid_w)),
                  const(w2_arg.shape), tab, tab],
        out_specs=out_spec,
        out_shape=out_shape,
        compiler_params=_params("arbitrary"),
        name=name,
    )(r, posbig[0], posbig[1], wbig[0], wbig[1], w2_arg, cos_c, sin_c)


def _softmax_cols(s, bias, any_visible):
    sb = s + bias
    e = jnp.exp2(sb - jnp.max(sb, axis=0, keepdims=True))
    l = jnp.sum(e, axis=0, keepdims=True)
    return e * jnp.where(any_visible, 1.0 / l, 0.0)


def _online_step(s, tile_max, v_t, m_old, l_old, acc_old):
    m_new = jnp.maximum(m_old, tile_max)
    alpha = jnp.exp2(m_old - m_new)
    p = jnp.exp2(s - m_new)
    l_new = alpha * l_old + jnp.sum(p, axis=0, keepdims=True)
    acc_new = alpha * acc_old + _dot(v_t, p.astype(BF16))
    return m_new, l_new, acc_new


def _online_cols(s, v_t, m_old, l_old, acc_old):
    return _online_step(s, jnp.max(s, axis=0, keepdims=True), v_t, m_old, l_old, acc_old)


def _online_stored(s_ref, tile_max, v_t, m_old, l_old, acc_old):
    return _online_step(s_ref[...], tile_max, v_t, m_old, l_old, acc_old)


def _normalized(state):
    _, l, acc = state
    return acc * (1.0 / l)


def _sweep(n_earlier, n_chains, prepare, produce, consume, finalize, state, alive=None):
    chains = range(n_chains)

    def produce_all(slot, f, diagonal):
        ctx = prepare(f, diagonal)
        for c in chains:
            produce(ctx, slot, c)

    def consume_all(slot, f, diagonal, st):
        return tuple(consume(slot, f, diagonal, c, st[c]) for c in chains)

    def overlapped(p_slot, p_f, c_slot, c_f, c_diagonal, st):
        ctx = prepare(p_f, False)
        out = []
        for c in chains:
            produce(ctx, p_slot, c)
            out.append(consume(c_slot, c_f, c_diagonal, c, st[c]))
        return tuple(out)

    produce_all(0, 0, True)
    state = overlapped(1, 0, 0, 0, True, state)

    def pair(i, st):
        st = overlapped(0, 2 * i + 1, 1, 2 * i, False, st)
        if alive is None:
            return overlapped(1, 2 * i + 2, 0, 2 * i + 1, False, st)
        produce_all(1, 2 * i + 2, False)
        return lax.cond(alive(st), lambda s: consume_all(0, 2 * i + 1, False, s), lambda s: s, st)

    n_pairs = n_earlier >> 1
    odd = (n_earlier & 1) == 1
    if alive is None:
        state = lax.fori_loop(0, n_pairs, pair, state)
    else:
        _, state = lax.while_loop(lambda c: (c[0] < n_pairs) & alive(c[1]),
                                  lambda c: (c[0] + 1, pair(c[0], c[1])), (jnp.int32(0), state))
        odd = odd & alive(state)

    @pl.when(odd)
    def _():
        finalize(consume_all(1, n_earlier - 1, False, state))

    @pl.when(jnp.logical_not(odd))
    def _():
        finalize(state)


def _score_buffers(chains, qn):
    return [pltpu.VMEM((2, chains, KEY_TILE, qn), F32), pltpu.VMEM((2, chains, 1, qn), F32)]


def _online_init(n, dv, qn):
    return tuple((jnp.full((1, qn), NEG_INF, F32), jnp.zeros((1, qn), F32), jnp.zeros((dv, qn), F32))
                 for _ in range(n))


def _split_bf16(v, terms):
    out, rest = [], v
    for i in range(terms):
        part = rest.astype(BF16)
        out.append(part)
        if i + 1 < terms:
            rest = rest - part.astype(F32)
    return out


def _top_rows(key, n_top):
    n_rows, n_cols = key.shape

    def bisect(i, tau):
        cand = tau | lax.shift_left(jnp.int32(1), jnp.int32(30) - i)
        cnt = jnp.sum(jnp.where(key >= cand, 1.0, 0.0), axis=0, keepdims=True)
        return jnp.where(cnt >= float(n_top), cand, tau)

    tau = lax.fori_loop(0, 31, bisect, jnp.zeros((1, n_cols), I32))
    above = jnp.where(key > tau, 1.0, 0.0)
    equal = jnp.where(key == tau, 1.0, 0.0)
    need = float(n_top) - jnp.sum(above, axis=0, keepdims=True)
    lower = jnp.where(_iota((n_rows, n_rows), 1) < _iota((n_rows, n_rows), 0), 1.0, 0.0).astype(BF16)
    before = _dot(lower, equal.astype(BF16))
    return above + equal * jnp.where(before < need, 1.0, 0.0)

def _nsa_kernel(q_ref, kc_ref, vc_ref, ks_ref, vs_ref, kw_ref, vw_ref, g_ref, ovt_ref, o_ref,
                bias_scr, part_scr, s_scr, mx_scr, *, n_sel):
    qn = q_ref.shape[2]
    ncp = kc_ref.shape[2]
    ns = ovt_ref.shape[0]
    kt_n = KEY_TILE
    tiles_per_q = qn // kt_n
    qi = pl.program_id(2)
    q0 = qi * qn
    reps = range(NSA_REP)
    q_head = lambda r: q_ref[0, r * HEAD_DIM:(r + 1) * HEAD_DIM, :]

    kc = kc_ref[0, 0]
    vc_t = vc_ref[0]
    bias_c = jnp.where((_iota((ncp, qn), 0) * CMP_STRIDE + (CMP_BLOCK - 1)) <= (q0 + _iota((ncp, qn), 1)),
                       0.0, NEG_INF)
    sees_cmp = (q0 + _iota((1, qn), 1)) >= CMP_BLOCK - 1
    o_cmp, p_sum = [], None
    s_next = _dot(kc, q_head(0))
    for r in reps:
        s_cur = s_next
        if r + 1 < NSA_REP:
            s_next = _dot(kc, q_head(r + 1))
        p = _softmax_cols(s_cur, bias_c, sees_cmp)
        o_cmp.append(_dot(vc_t, p.astype(BF16)))
        p_sum = p if p_sum is None else p_sum + p
    ovt = ovt_ref[...]
    imp_t = None
    for term in _split_bf16(p_sum, 3):
        part = _dot(ovt, term)
        imp_t = part if imp_t is None else imp_t + part

    blk = _iota((ns, qn), 0)
    cur = (q0 + _iota((ns, qn), 1)) >> SLC_SHIFT
    forced = (blk == 0) | (blk == cur) | (blk == cur - 1)
    imp_bits = jnp.where(imp_t > 0.0, lax.bitcast_convert_type(imp_t, I32), 0)
    key = jnp.where(forced, jnp.int32(2 ** 31 - 1), jnp.where(blk <= cur, imp_bits, -1))

    picked = _top_rows(key, n_sel)
    bias_scr[...] = jnp.where(picked > 0.5, 0.0, NEG_INF)

    blocks_per_tile = kt_n // SLC_BLOCK
    causal = _iota((kt_n, qn), 0) <= _iota((kt_n, qn), 1)

    carry = list(_online_init(NSA_REP, HEAD_DIM, qn))
    win = {}
    for diagonal in (True, False):
        kt = qi if diagonal else jnp.maximum(qi - WINDOW // kt_n, 0)
        k0 = pl.multiple_of(kt * kt_n, kt_n)
        kp = k0 + _iota((kt_n, qn), 0)
        tq = q0 + _iota((kt_n, qn), 1)
        inside = (kp <= tq) if diagonal else ((kp > tq - WINDOW) & (kp < q0))
        win[diagonal] = (k0, kt, jnp.where(inside, 0.0, NEG_INF))
    jobs = [(diagonal, r) for diagonal in (True, False) for r in reps]
    win_scores = lambda job: _dot(kw_ref[0, 0, pl.ds(win[job[0]][0], kt_n), :], q_head(job[1]))
    s_next = win_scores(jobs[0])
    for n, (diagonal, r) in enumerate(jobs):
        s_cur = s_next
        if n + 1 < len(jobs):
            s_next = win_scores(jobs[n + 1])
        _, kt, bias_w = win[diagonal]
        carry[r] = _online_cols(s_cur + bias_w, vw_ref[0, kt], *carry[r])

    gates = g_ref[0]
    for r in reps:
        part_scr[r] = (gates[3 * r:3 * r + 1] * o_cmp[r]
                       + gates[3 * r + 2:3 * r + 3] * _normalized(carry[r]))

    def prepare(f, diagonal):
        kt = qi if diagonal else jnp.minimum(f, jnp.maximum(qi - 1, 0))
        k0 = pl.multiple_of(kt * kt_n, kt_n)
        rows = [jnp.broadcast_to(bias_scr[pl.ds(kt * blocks_per_tile + i, 1), :], (SLC_BLOCK, qn))
                for i in range(blocks_per_tile)]
        bias = jnp.concatenate(rows, axis=0)
        if diagonal:
            bias = jnp.where(causal, bias, NEG_INF)
        return k0, bias

    def produce(ctx, slot, r):
        k0, bias = ctx
        s = _dot(ks_ref[0, 0, pl.ds(k0, kt_n), :], q_head(r)) + bias
        s_scr[slot, r] = s
        mx_scr[slot, r] = jnp.max(s, axis=0, keepdims=True)

    def consume(slot, f, diagonal, r, state_r):
        return _online_stored(s_scr.at[slot, r], mx_scr[slot, r], vs_ref[0, qi if diagonal else f], *state_r)

    def finalize(state):
        for r in reps:
            o = part_scr[r] + g_ref[0, 3 * r + 1:3 * r + 2, :] * _normalized(state[r])
            o_ref[0, r * HEAD_DIM:(r + 1) * HEAD_DIM, :] = o.astype(o_ref.dtype)

    _sweep(qi, NSA_REP, prepare, produce, consume, finalize, _online_init(NSA_REP, HEAD_DIM, qn))


def _nsa_attention(q_t, k_cmp, v_cmp_t, k_slc, v_slc_t, k_win, v_win_t, gates_t):
    bsz, _, t = q_t.shape
    ncp = k_cmp.shape[2]
    ns = t // SLC_BLOCK
    qn = min(Q_TILE, t)
    grp_w = NSA_REP * HEAD_DIM
    cmp_start = jnp.arange(ncp) * CMP_STRIDE
    slc_start = jnp.arange(ns) * SLC_BLOCK
    real = jnp.arange(ncp) < (t - CMP_BLOCK) // CMP_STRIDE + 1
    overlap_t = ((cmp_start[None, :] < slc_start[:, None] + SLC_BLOCK)
                 & (cmp_start[None, :] + CMP_BLOCK > slc_start[:, None]) & real[None, :]).astype(BF16)
    k_spec = lambda n: pl.BlockSpec((1, 1, n, HEAD_DIM), lambda b, g, i: (b, g, 0, 0))
    v_spec = pl.BlockSpec((1, t // KEY_TILE, HEAD_DIM, KEY_TILE), lambda b, g, i: (b, 0, g, 0))
    return pl.pallas_call(
        functools.partial(_nsa_kernel, n_sel=min(N_SELECT, ns)),
        grid=(bsz, NSA_KV_GROUPS, t // qn),
        in_specs=[pl.BlockSpec((1, grp_w, qn), lambda b, g, i: (b, g, i)),
                  k_spec(ncp),
                  pl.BlockSpec((1, HEAD_DIM, ncp), lambda b, g, i: (b, g, 0)),
                  k_spec(t), v_spec, k_spec(t), v_spec,
                  pl.BlockSpec((1, GATE_ROWS, qn), lambda b, g, i: (b, g, i)),
                  pl.BlockSpec((ns, ncp), lambda b, g, i: (0, 0))],
        out_specs=pl.BlockSpec((1, grp_w, qn), lambda b, g, i: (b, g, i)),
        out_shape=jax.ShapeDtypeStruct((bsz, NSA_HEADS * HEAD_DIM, t), BF16),
        scratch_shapes=[pltpu.VMEM((ns, qn), F32), pltpu.VMEM((NSA_REP, HEAD_DIM, qn), F32)] + _score_buffers(NSA_REP, qn),
        compiler_params=_params("arbitrary", "arbitrary", "arbitrary"),
        name="nsa_attention",
    )(q_t, k_cmp, v_cmp_t, k_slc, v_slc_t, k_win, v_win_t, gates_t, overlap_t)


SB_HEADS_PER_STEP = 2
SB_DEAD_TAIL = -110.0
SB_SCAN_BLOCK = 128


def _sb_kernel(q_ref, k_ref, v_ref, o_ref, nz_scr):
    qn = q_ref.shape[2]
    kt_n = KEY_TILE
    qi = pl.program_id(2)
    heads = range(SB_HEADS_PER_STEP)
    rows = [slice(h * HEAD_DIM, (h + 1) * HEAD_DIM) for h in heads]
    sub = SB_SCAN_BLOCK
    n_sub = kt_n // sub
    tri = jnp.where(_iota((sub, sub), 1) > _iota((sub, sub), 0), 1.0, 0.0).astype(BF16)
    later2 = jnp.concatenate([tri, tri], axis=1)
    strictly_before = _iota((kt_n, qn), 0) < _iota((kt_n, qn), 1)

    def tile_index(f, diagonal):
        return qi if diagonal else jnp.maximum(qi - 1 - f, 0)

    def prepare(f, diagonal):
        return pl.multiple_of(tile_index(f, diagonal) * kt_n, kt_n)

    def produce(k0, slot, h):
        nz_scr[slot, h] = _dot(k_ref[0, h, pl.ds(k0, kt_n), :], -q_ref[0, rows[h], :])

    def consume(slot, f, diagonal, h, state_h):
        tail, acc = state_h
        nz = nz_scr[slot, h]
        neg_abs = lax.bitcast_convert_type(lax.bitcast_convert_type(nz, jnp.uint32) | jnp.uint32(0x80000000), F32)
        soft = jnp.log(1.0 + jnp.exp(neg_abs))
        log_keep = jnp.minimum(nz, 0.0) - soft
        log_beta = log_keep - nz
        if diagonal:
            log_keep = jnp.where(strictly_before, log_keep, 0.0)
        offs = tail
        parts = [None] * n_sub
        for blk in reversed(range(n_sub)):
            lk = log_keep[blk * sub:(blk + 1) * sub]
            hi = lk.astype(BF16)
            lo = (lk - hi.astype(F32)).astype(BF16)
            within = _dot(later2, jnp.concatenate([hi, lo], axis=0))
            parts[blk] = log_beta[blk * sub:(blk + 1) * sub] + within + offs
            offs = offs + jnp.sum(lk, axis=0, keepdims=True)
        a = jnp.exp(jnp.concatenate(parts, axis=0))
        if diagonal:
            a = jnp.where(strictly_before, a, 0.0)
        return offs, acc + _dot(v_ref[0, tile_index(f, diagonal), rows[h], :], a.astype(BF16))

    def finalize(state):
        for h in heads:
            o_ref[0, rows[h], :] = state[h][1].astype(o_ref.dtype)

    def alive(state):
        tails = [state[h][0] for h in heads]
        return jnp.max(functools.reduce(jnp.maximum, tails)) > SB_DEAD_TAIL

    init = tuple((jnp.zeros((1, qn), F32), jnp.zeros((HEAD_DIM, qn), F32)) for _ in heads)
    _sweep(qi, SB_HEADS_PER_STEP, prepare, produce, consume, finalize, init, alive)


def _sb_attention(q_t, k, v_t):
    bsz, nh, t, _ = k.shape
    qn = min(Q_TILE, t)
    hps = SB_HEADS_PER_STEP
    w = hps * HEAD_DIM
    return pl.pallas_call(
        _sb_kernel,
        grid=(bsz, nh // hps, t // qn),
        in_specs=[pl.BlockSpec((1, w, qn), lambda b, h, i: (b, h, i)),
                  pl.BlockSpec((1, hps, t, HEAD_DIM), lambda b, h, i: (b, h, 0, 0)),
                  pl.BlockSpec((1, t // KEY_TILE, w, KEY_TILE), lambda b, h, i: (b, 0, h, 0))],
        out_specs=pl.BlockSpec((1, w, qn), lambda b, h, i: (b, h, i)),
        out_shape=jax.ShapeDtypeStruct((bsz, nh * HEAD_DIM, t), BF16),
        scratch_shapes=[pltpu.VMEM((2, hps, KEY_TILE, qn), F32)],
        compiler_params=_params("arbitrary", "arbitrary", "arbitrary"),
        name="stickbreak_attention",
    )(q_t, k, v_t)


DIFF_HEADS_PER_STEP = 2


def _diff_kernel(q_ref, k_ref, v_ref, lq1_ref, lk1_ref, lq2_ref, lk2_ref, sub_ref, o_ref, s_scr, mx_scr,
                 *, lambda_init):
    qn = q_ref.shape[2]
    vd = 2 * HEAD_DIM
    kt_n = KEY_TILE
    qi = pl.program_id(2)
    chains = range(2 * DIFF_HEADS_PER_STEP)
    lam = (jnp.exp(jnp.sum(lq1_ref[...] * lk1_ref[...], axis=1, keepdims=True))
           - jnp.exp(jnp.sum(lq2_ref[...] * lk2_ref[...], axis=1, keepdims=True)) + lambda_init)

    causal = _iota((kt_n, qn), 0) <= _iota((kt_n, qn), 1)

    def prepare(f, diagonal):
        kt = qi if diagonal else jnp.minimum(f, jnp.maximum(qi - 1, 0))
        return pl.multiple_of(kt * kt_n, kt_n), diagonal

    def produce(ctx, slot, c):
        k0, diagonal = ctx
        s = _dot(k_ref[0, c, pl.ds(k0, kt_n), :], q_ref[0, c * HEAD_DIM:(c + 1) * HEAD_DIM, :])
        if diagonal:
            s = jnp.where(causal, s, NEG_INF)
        s_scr[slot, c] = s
        mx_scr[slot, c] = jnp.max(s, axis=0, keepdims=True)

    def consume(slot, f, diagonal, c, state_c):
        kt = qi if diagonal else f
        return _online_stored(s_scr.at[slot, c], mx_scr[slot, c],
                              v_ref[0, kt, (c // 2) * vd:(c // 2 + 1) * vd, :], *state_c)

    def finalize(state):
        for h in range(DIFF_HEADS_PER_STEP):
            o = _normalized(state[2 * h]) - lam * _normalized(state[2 * h + 1])
            ms = jnp.mean(o * o, axis=0, keepdims=True)
            o = o * lax.rsqrt(ms + RMS_EPS) * sub_ref[...] * (1.0 - lambda_init)
            o_ref[0, h * vd:(h + 1) * vd, :] = o.astype(o_ref.dtype)

    _sweep(qi, len(chains), prepare, produce, consume, finalize, _online_init(len(chains), vd, qn))


def _diff_attention(q_t, k, v_t, lq1, lk1, lq2, lk2, subln, lambda_init):
    bsz, nh2, t, _ = k.shape
    nh = nh2 // 2
    vd = 2 * HEAD_DIM
    hps = DIFF_HEADS_PER_STEP
    qn = min(Q_TILE, t)
    row = lambda a: a.reshape(1, -1)
    small = lambda n: pl.BlockSpec((1, n), lambda b, h, i: (0, 0))
    return pl.pallas_call(
        functools.partial(_diff_kernel, lambda_init=lambda_init),
        grid=(bsz, nh // hps, t // qn),
        in_specs=[pl.BlockSpec((1, hps * vd, qn), lambda b, h, i: (b, h, i)),
                  pl.BlockSpec((1, 2 * hps, t, HEAD_DIM), lambda b, h, i: (b, h, 0, 0)),
                  pl.BlockSpec((1, t // KEY_TILE, hps * vd, KEY_TILE), lambda b, h, i: (b, 0, h, 0)),
                  small(HEAD_DIM), small(HEAD_DIM), small(HEAD_DIM), small(HEAD_DIM),
                  pl.BlockSpec((vd, 1), lambda b, h, i: (0, 0))],
        out_specs=pl.BlockSpec((1, hps * vd, qn), lambda b, h, i: (b, h, i)),
        out_shape=jax.ShapeDtypeStruct((bsz, nh * vd, t), BF16),
        scratch_shapes=_score_buffers(2 * hps, qn),
        compiler_params=_params("arbitrary", "arbitrary", "arbitrary"),
        name="diff_attention",
    )(q_t, k, v_t, row(lq1), row(lk1), row(lq2), row(lk2), subln.reshape(vd, 1))


def _outproj_kernel(*refs, n_in):
    a_refs, w_refs = refs[:n_in], refs[n_in:2 * n_in]
    x_ref, gate_ref, o_ref = refs[2 * n_in:]
    y = None
    for a_ref, w_ref in zip(a_refs, w_refs):
        part = lax.dot_general(a_ref[0], w_ref[...].astype(BF16), _TN, preferred_element_type=F32)
        y = part if y is None else y + part
    o_ref[0] = x_ref[0] + gate_ref[0] * y


def _out_project(acts_t, weight, x, gate, name):
    bsz, t, d = x.shape
    rows = min(t, PROJ_ROWS)
    k_rows = acts_t[0].shape[1]
    tile = pl.BlockSpec((1, rows, d), lambda b, i: (b, i, 0))
    return pl.pallas_call(
        functools.partial(_outproj_kernel, n_in=len(acts_t)),
        grid=(bsz, t // rows),
        in_specs=([pl.BlockSpec((1, a.shape[1], rows), lambda b, i: (b, 0, i)) for a in acts_t]
                  + [pl.BlockSpec((k_rows, d), lambda b, i, n=n: (n, 0)) for n in range(len(acts_t))]
                  + [tile, pl.BlockSpec((1, 1, d), lambda b, i: (b, 0, 0))]),
        out_specs=tile,
        out_shape=jax.ShapeDtypeStruct((bsz, t, d), F32),
        compiler_params=_params("arbitrary", "arbitrary"),
        name=name,
    )(*acts_t, *([weight] * len(acts_t)), x, gate)


def _ffn_kernel(x_ref, halo_ref, g_ref, sh_ref, sc_ref, gate_ref, wg_ref, wu_ref, cw_ref, cb_ref, wd_ref,
                nf_ref, o_ref, a_scr, *, final_norm):
    ti = pl.program_id(1)
    h = _norm_modulate(x_ref[0], g_ref[...], sh_ref[0], sc_ref[0]).astype(BF16)
    h_prev = _norm_modulate(halo_ref[0], g_ref[...], sh_ref[0], sc_ref[0]).astype(BF16)
    seen = jnp.where(ti > 0, 1.0, 0.0)
    cols = FFN_COLS
    row = _iota((h.shape[0], cols), 0)
    for c in range(a_scr.shape[1] // cols):
        sl = slice(c * cols, (c + 1) * cols)
        wg = wg_ref[0, :, sl].astype(BF16)
        g = _dot(h, wg)
        g_prev = _dot(h_prev, wg) * seen
        last = g_prev[HALO_ROWS - 1:HALO_ROWS]
        last2 = g_prev[HALO_ROWS - 2:HALO_ROWS - 1]
        g1 = jnp.where(row == 0, last, pltpu.roll(g, 1, 0))
        g2 = jnp.where(row == 0, last2, jnp.where(row == 1, last, pltpu.roll(g, 2, 0)))
        cw = cw_ref[:, sl]
        conv = cw[0:1] * g2 + cw[1:2] * g1 + cw[2:3] * g + cb_ref[:, sl]
        a_scr[:, sl] = (conv * _sigmoid(conv) * _dot(h, wu_ref[0, :, sl].astype(BF16))).astype(BF16)
    y = x_ref[0] + gate_ref[0] * _dot(a_scr[...], wd_ref[...])
    if final_norm:
        ms = jnp.mean(y * y, axis=-1, keepdims=True)
        y = y * lax.rsqrt(ms + RMS_EPS) * nf_ref[...]
    o_ref[0] = y


def _conv_ffn(x, g, shift, scale, gate, layer, w_gate_all, w_up_all, conv_w, conv_b, w_down, norm_f, final_norm,
              name):
    bsz, t, d = x.shape
    f = w_gate_all.shape[2]
    rows = min(t, FFN_ROWS)
    halo_blocks = rows // HALO_ROWS
    vec = pl.BlockSpec((1, 1, d), lambda b, i: (b, 0, 0))
    drow = pl.BlockSpec((1, d), lambda b, i: (0, 0))
    resident = lambda shape: pl.BlockSpec(shape, lambda b, i: (0, 0), pipeline_mode=pl.Buffered(1))
    stacked = pl.BlockSpec((1, d, f), lambda b, i: (layer, 0, 0), pipeline_mode=pl.Buffered(1))
    return pl.pallas_call(
        functools.partial(_ffn_kernel, final_norm=final_norm),
        grid=(bsz, t // rows),
        in_specs=[pl.BlockSpec((1, rows, d), lambda b, i: (b, i, 0)),
                  pl.BlockSpec((1, HALO_ROWS, d), lambda b, i: (b, jnp.maximum(i * halo_blocks - 1, 0), 0)),
                  drow, vec, vec, vec,
                  stacked, stacked, resident((CONV_WIDTH, f)), resident((1, f)),
                  resident((f, d)), drow],
        out_specs=pl.BlockSpec((1, rows, d), lambda b, i: (b, i, 0)),
        out_shape=jax.ShapeDtypeStruct((bsz, t, d), F32),
        scratch_shapes=[pltpu.VMEM((rows, f), BF16)],
        compiler_params=_params("arbitrary", "arbitrary"),
        name=name,
    )(x, x, g.reshape(1, d), shift, scale, gate, w_gate_all, w_up_all, conv_w,
      conv_b.reshape(1, f), w_down.astype(BF16), norm_f.reshape(1, d))


def _layout(widths):
    starts, start = [], 0
    for w in widths:
        starts.append(start)
        start += w
    return starts


def _hybrid_layer(x, mods, tables, norm_g, w_in, pos_k, pos_v, ck_w1, ck_w2, cv_w1, cv_w2, w_out):
    shift, scale, gate = mods
    bsz, t, d = x.shape
    qw, kvw, sbw = NSA_HEADS * HEAD_DIM, NSA_KV_GROUPS * HEAD_DIM, SB_HEADS * HEAD_DIM
    col = _layout((qw,) + (kvw,) * 6 + (3 * NSA_HEADS,) + (sbw,) * 3)
    cols = lambda i, w: w_in[:, col[i]:col[i] + w]
    q_n, kc, vc, ks, vs, kw, vw, gl, q_s, k_s, v_s = (cols(i, w) for i, w in enumerate(
        (qw,) + (kvw,) * 6 + (3 * NSA_HEADS,) + (sbw,) * 3))
    inv_sqrt = HEAD_DIM ** -0.5

    w_tok = jnp.concatenate([ks, kw, kc, vc, k_s], axis=1).astype(BF16)
    ts = _layout((kvw, kvw, kvw, kvw, sbw))
    tok_segs = [TokSeg(ts[0], kvw, True, HEAD_DIM, BF16),
                TokSeg(ts[1], kvw, True, HEAD_DIM, BF16),
                TokSeg(ts[2], kvw, False, 0, F32),
                TokSeg(ts[3], kvw, False, 0, F32),
                TokSeg(ts[4], sbw, False, HEAD_DIM, BF16)]
    per_group = 3 * NSA_REP
    gate_pad = jnp.zeros((d, GATE_ROWS - per_group), w_in.dtype)
    gl_pad = jnp.concatenate([gl[:, :per_group], gate_pad, gl[:, per_group:], gate_pad], axis=1)
    w_feat_t = jnp.concatenate([q_n, q_s, vs, vw, v_s, gl_pad], axis=1).T.astype(BF16)
    fs = _layout((qw, sbw, kvw, kvw, sbw, NSA_KV_GROUPS * GATE_ROWS))
    feat_segs = [FeatSeg(fs[0], qw, True, inv_sqrt * LOG2E, False, False, BF16),
                 FeatSeg(fs[1], sbw, False, inv_sqrt, False, False, BF16),
                 FeatSeg(fs[2], kvw, False, 1.0, False, True, BF16),
                 FeatSeg(fs[3], kvw, False, 1.0, False, True, BF16),
                 FeatSeg(fs[4], sbw, False, 1.0, False, True, BF16),
                 FeatSeg(fs[5], NSA_KV_GROUPS * GATE_ROWS, False, 1.0, True, False, F32)]
    (k_slc, k_win, kc_a, vc_a, k_sb, q_n_t, q_s_t, v_slc_t, v_win_t, v_sb_t, gates_t) = _project(
        x, norm_g, shift, scale, w_tok, w_feat_t, tables, tok_segs, feat_segs, "hybrid_in_proj")

    ncp = t // CMP_STRIDE
    end_rows = jnp.minimum(jnp.arange(ncp) * CMP_STRIDE + CMP_BLOCK - 1, t - 1)
    cos_c, sin_c = tables[0][:, end_rows], tables[1][:, end_rows]
    k_cmp = _compress(kc_a, pos_k, ck_w1, ck_w2, cos_c, sin_c, True, "compress_k")
    v_cmp_t = _compress(vc_a, pos_v, cv_w1, cv_w2, cos_c, sin_c, False, "compress_v")
    o_nsa_t = _nsa_attention(q_n_t, k_cmp, v_cmp_t, k_slc, v_slc_t, k_win, v_win_t, gates_t)
    o_sb_t = _sb_attention(q_s_t, k_sb, v_sb_t)
    return _out_project([o_nsa_t, o_sb_t], w_out, x, gate, "hybrid_out_proj")


def _diff_layer(x, mods, tables, norm_g, w_qkv, lq1, lk1, lq2, lk2, subln, w_out, layer_idx):
    shift, scale, gate = mods
    dw = 2 * DIFF_HEADS * HEAD_DIM
    inv_sqrt = HEAD_DIM ** -0.5
    w_tok = w_qkv[:, dw:2 * dw].astype(BF16)
    w_feat_t = jnp.concatenate([w_qkv[:, :dw], w_qkv[:, 2 * dw:]], axis=1).T.astype(BF16)
    tok_segs = [TokSeg(0, dw, True, HEAD_DIM, BF16)]
    feat_segs = [FeatSeg(0, dw, True, inv_sqrt * LOG2E, False, False, BF16),
                 FeatSeg(dw, dw, False, 1.0, False, True, BF16)]
    k, q_t, v_t = _project(x, norm_g, shift, scale, w_tok, w_feat_t, tables, tok_segs, feat_segs, "diff_in_proj")
    lambda_init = 0.8 - 0.6 * math.exp(-0.3 * layer_idx)
    o_t = _diff_attention(q_t, k, v_t, lq1, lk1, lq2, lk2, subln, lambda_init)
    return _out_project([o_t], w_out, x, gate, "diff_out_proj")


def kernel(x, c, positions, mod_w, mod_b, norm_mix, norm_ffn, ffn_w_gate, ffn_w_up, ffn_conv_w, ffn_conv_b, ffn_w_down, hyb_w_in, nsa_pos_k, nsa_pos_v, nsa_ck_w1, nsa_ck_w2, nsa_cv_w1, nsa_cv_w2, hyb_w_out, diff_w_qkv, diff_lq1, diff_lk1, diff_lq2, diff_lk2, diff_subln, diff_w_out, norm_f):
    bsz, t, d = x.shape
    depth = mod_w.shape[0]
    mod = _adaln_mod(c, mod_w, mod_b)
    tables = _rope_tables(positions)
    for i in range(depth):
        sh_m, sc_m, g_m, sh_f, sc_f, g_f = (mod[i, :, k * d:(k + 1) * d].reshape(bsz, 1, d) for k in range(6))
        j = i // 2
        if i % 2 == 0:
            x = _hybrid_layer(x, (sh_m, sc_m, g_m), tables, norm_mix[i], hyb_w_in[j],
                              nsa_pos_k[j], nsa_pos_v[j], nsa_ck_w1[j], nsa_ck_w2[j], nsa_cv_w1[j],
                              nsa_cv_w2[j], hyb_w_out[j])
        else:
            x = _diff_layer(x, (sh_m, sc_m, g_m), tables, norm_mix[i], diff_w_qkv[j], diff_lq1[j],
                            diff_lk1[j], diff_lq2[j], diff_lk2[j], diff_subln[j], diff_w_out[j], i)
        x = _conv_ffn(x, norm_ffn[i], sh_f, sc_f, g_f, i, ffn_w_gate, ffn_w_up, ffn_conv_w[i],
                      ffn_conv_b[i], ffn_w_down[i], norm_f, i == depth - 1, "conv_ffn_%d" % i)
    return x
```

```python
import functools
import math
from typing import NamedTuple

import jax
import jax.numpy as jnp
from jax import lax
from jax.experimental import pallas as pl
from jax.experimental.pallas import tpu as pltpu

F32 = jnp.float32
BF16 = jnp.bfloat16
I32 = jnp.int32

HEAD_DIM = 64
ROPE_DIM = HEAD_DIM // 4
ROPE_HALF = ROPE_DIM // 2
ROPE_THETA = 500000.0
NSA_HEADS = 8
NSA_KV_GROUPS = 2
NSA_REP = NSA_HEADS // NSA_KV_GROUPS
CMP_BLOCK = 32
CMP_STRIDE = 16
CMP_HIDDEN = 4 * HEAD_DIM
SLC_BLOCK = 64
SLC_SHIFT = 6
N_SELECT = 16
WINDOW = 512
SB_HEADS = 8
DIFF_HEADS = 8
CONV_WIDTH = 3
RMS_EPS = 1e-6
NEG_INF = -1e30
LOG2E = math.log2(math.e)

LANES = 128
Q_TILE = 512
KEY_TILE = 512
GATE_ROWS = 16
PROJ_ROWS = 512
FFN_ROWS = 512
FFN_COLS = 256
HALO_ROWS = 16
VMEM_LIMIT = 56 * 1024 * 1024

_NT = (((1,), (1,)), ((), ()))
_TN = (((0,), (0,)), ((), ()))


def _params(*sem):
    return pltpu.CompilerParams(dimension_semantics=sem, vmem_limit_bytes=VMEM_LIMIT)


def _sigmoid(v):
    return 1.0 / (1.0 + jnp.exp(-v))


def _iota(shape, axis):
    return lax.broadcasted_iota(I32, shape, axis)


def _dot(a, b):
    return jnp.dot(a, b, preferred_element_type=F32)


def _mod_kernel(c_ref, w_ref, b_ref, o_ref):
    c = c_ref[...]
    cond = c * _sigmoid(c)
    o_ref[0] = jnp.dot(cond, w_ref[0], preferred_element_type=F32,
                       precision=lax.Precision.HIGHEST) + b_ref[0]


def _adaln_mod(c, mod_w, mod_b):
    depth, d, n = mod_w.shape
    bsz = c.shape[0]
    tn = n // 4
    return pl.pallas_call(
        _mod_kernel,
        grid=(depth, n // tn),
        in_specs=[pl.BlockSpec((bsz, d), lambda i, j: (0, 0)),
                  pl.BlockSpec((1, d, tn), lambda i, j: (i, 0, j)),
                  pl.BlockSpec((1, 1, tn), lambda i, j: (i, 0, j))],
        out_specs=pl.BlockSpec((1, bsz, tn), lambda i, j: (i, 0, j)),
        out_shape=jax.ShapeDtypeStruct((depth, bsz, n), F32),
        compiler_params=_params("arbitrary", "arbitrary"),
        name="adaln_mod",
    )(c, mod_w, mod_b.reshape(depth, 1, n))


def _rope_kernel(pos_col_ref, pos_row_ref, inv_row_ref, sgn_row_ref, inv_col_ref,
                 cos_ref, sin_ref, cos_t_ref, sin_t_ref):
    ang = pos_col_ref[0].astype(F32) * inv_row_ref[...]
    cos_ref[0] = jnp.cos(ang)
    sin_ref[0] = jnp.sin(ang) * sgn_row_ref[...]
    ang_t = inv_col_ref[...] * pos_row_ref[0].astype(F32)
    cos_t_ref[0] = jnp.cos(ang_t)
    sin_t_ref[0] = jnp.sin(ang_t)


def _rope_tables(positions):
    bsz, t = positions.shape
    inv = ROPE_THETA ** (-jnp.arange(0, ROPE_DIM, 2, dtype=F32) / ROPE_DIM)
    per_head_inv = jnp.concatenate([inv, inv, jnp.zeros((HEAD_DIM - ROPE_DIM,), F32)])
    per_head_sgn = jnp.concatenate([-jnp.ones((ROPE_HALF,), F32), jnp.ones((ROPE_HALF,), F32),
                                    jnp.zeros((HEAD_DIM - ROPE_DIM,), F32)])
    inv_row = jnp.tile(per_head_inv, LANES // HEAD_DIM)[None, :]
    sgn_row = jnp.tile(per_head_sgn, LANES // HEAD_DIM)[None, :]
    rows = min(t, 1024)
    tab = jax.ShapeDtypeStruct((bsz, t, LANES), F32)
    tab_t = jax.ShapeDtypeStruct((bsz, ROPE_HALF, t), F32)
    row_spec = pl.BlockSpec((1, LANES), lambda b, i: (0, 0))
    return pl.pallas_call(
        _rope_kernel,
        grid=(bsz, t // rows),
        in_specs=[pl.BlockSpec((1, rows, 1), lambda b, i: (b, i, 0)),
                  pl.BlockSpec((1, 1, rows), lambda b, i: (b, 0, i)),
                  row_spec, row_spec,
                  pl.BlockSpec((ROPE_HALF, 1), lambda b, i: (0, 0))],
        out_specs=[pl.BlockSpec((1, rows, LANES), lambda b, i: (b, i, 0))] * 2
        + [pl.BlockSpec((1, ROPE_HALF, rows), lambda b, i: (b, 0, i))] * 2,
        out_shape=[tab, tab, tab_t, tab_t],
        compiler_params=_params("arbitrary", "arbitrary"),
        name="rope_tables",
    )(positions.reshape(bsz, t, 1), positions.reshape(bsz, 1, t), inv_row, sgn_row, inv[:, None])


def _rope_chunk(y, cosv, sinv, first_half):
    ahead = pltpu.roll(y, LANES - ROPE_HALF, 1)
    behind = pltpu.roll(y, ROPE_HALF, 1)
    return y * cosv + jnp.where(first_half, ahead, behind) * sinv


def _rope_rows(y_t, cos_t, sin_t):
    heads = []
    for h in range(y_t.shape[0] // HEAD_DIM):
        blk = y_t[h * HEAD_DIM:(h + 1) * HEAD_DIM]
        x1, x2 = blk[:ROPE_HALF], blk[ROPE_HALF:ROPE_DIM]
        heads += [x1 * cos_t - x2 * sin_t, x2 * cos_t + x1 * sin_t, blk[ROPE_DIM:]]
    return jnp.concatenate(heads, axis=0)


class TokSeg(NamedTuple):
    start: int
    width: int
    rope: bool
    head_width: int
    dtype: object


class FeatSeg(NamedTuple):
    start: int
    rows: int
    rope: bool
    scale: float
    sigmoid: bool
    key_tiled: bool
    dtype: object


def _norm_modulate(x, g, shift, scale):
    ms = jnp.mean(x * x, axis=-1, keepdims=True)
    y = x * lax.rsqrt(ms + RMS_EPS) * g
    return y * (1.0 + scale) + shift


def _proj_kernel(x_ref, g_ref, sh_ref, sc_ref, w_ref, wt_ref, cos_ref, sin_ref, cos_t_ref, sin_t_ref,
                 *out_refs, tok_segs, feat_segs):
    hb = _norm_modulate(x_ref[0], g_ref[...], sh_ref[0], sc_ref[0]).astype(BF16)
    cosv, sinv = cos_ref[0], sin_ref[0]
    first_half = (_iota(cosv.shape, 1) & (HEAD_DIM - 1)) < ROPE_HALF
    tok_refs, feat_refs = out_refs[:len(tok_segs)], out_refs[len(tok_segs):]
    for seg, o_ref in zip(tok_segs, tok_refs):
        y = _dot(hb, w_ref[:, seg.start:seg.start + seg.width])
        for ch in range(seg.width // LANES):
            yc = y[:, ch * LANES:(ch + 1) * LANES]
            if seg.rope:
                yc = _rope_chunk(yc, cosv, sinv, first_half)
            yc = yc.astype(seg.dtype)
            if seg.head_width == 0:
                o_ref[0, :, ch * LANES:(ch + 1) * LANES] = yc
            else:
                o_ref[0, 2 * ch] = yc[:, :HEAD_DIM]
                o_ref[0, 2 * ch + 1] = yc[:, HEAD_DIM:]
    for seg, o_ref in zip(feat_segs, feat_refs):
        y_t = lax.dot_general(wt_ref[seg.start:seg.start + seg.rows, :], hb, _NT,
                              preferred_element_type=F32)
        if seg.rope:
            y_t = _rope_rows(y_t, cos_t_ref[0], sin_t_ref[0])
        if seg.scale != 1.0:
            y_t = y_t * seg.scale
        if seg.sigmoid:
            y_t = _sigmoid(y_t)
        y_t = y_t.astype(seg.dtype)
        if seg.key_tiled:
            for ch in range(y_t.shape[1] // KEY_TILE):
                o_ref[0, ch] = y_t[:, ch * KEY_TILE:(ch + 1) * KEY_TILE]
        else:
            o_ref[0] = y_t


def _project(x, g, shift, scale, w_tok, w_feat_t, tables, tok_segs, feat_segs, name):
    bsz, t, d = x.shape
    rows = min(t, PROJ_ROWS)
    cos_tab, sin_tab, cos_t, sin_t = tables
    out_shapes, out_specs = [], []
    for seg in tok_segs:
        if seg.head_width == 0:
            out_shapes.append(jax.ShapeDtypeStruct((bsz, t, seg.width), seg.dtype))
            out_specs.append(pl.BlockSpec((1, rows, seg.width), lambda b, i: (b, i, 0)))
        else:
            nh = seg.width // seg.head_width
            out_shapes.append(jax.ShapeDtypeStruct((bsz, nh, t, seg.head_width), seg.dtype))
            out_specs.append(pl.BlockSpec((1, nh, rows, seg.head_width), lambda b, i: (b, 0, i, 0)))
    for seg in feat_segs:
        if seg.key_tiled:
            out_shapes.append(jax.ShapeDtypeStruct((bsz, t // KEY_TILE, seg.rows, KEY_TILE), seg.dtype))
            out_specs.append(pl.BlockSpec((1, rows // KEY_TILE, seg.rows, KEY_TILE), lambda b, i: (b, i, 0, 0)))
        else:
            out_shapes.append(jax.ShapeDtypeStruct((bsz, seg.rows, t), seg.dtype))
            out_specs.append(pl.BlockSpec((1, seg.rows, rows), lambda b, i: (b, 0, i)))
    vec = pl.BlockSpec((1, 1, d), lambda b, i: (b, 0, 0))
    tab = pl.BlockSpec((1, rows, LANES), lambda b, i: (b, i, 0))
    tab_t = pl.BlockSpec((1, ROPE_HALF, rows), lambda b, i: (b, 0, i))
    return pl.pallas_call(
        functools.partial(_proj_kernel, tok_segs=tuple(tok_segs), feat_segs=tuple(feat_segs)),
        grid=(bsz, t // rows),
        in_specs=[pl.BlockSpec((1, rows, d), lambda b, i: (b, i, 0)),
                  pl.BlockSpec((1, d), lambda b, i: (0, 0)),
                  vec, vec,
                  pl.BlockSpec(w_tok.shape, lambda b, i: (0, 0)),
                  pl.BlockSpec(w_feat_t.shape, lambda b, i: (0, 0)),
                  tab, tab, tab_t, tab_t],
        out_specs=out_specs,
        out_shape=out_shapes,
        compiler_params=_params("arbitrary", "arbitrary"),
        name=name,
    )(x, g.reshape(1, d), shift, scale, w_tok, w_feat_t, cos_tab, sin_tab, cos_t, sin_t)


def _compress_kernel(r_ref, pa_ref, pb_ref, wa_ref, wb_ref, w2_ref, cos_ref, sin_ref, o_ref, *, is_key):
    r = r_ref[0]
    ncp = r.shape[0]
    a = _dot((r + pa_ref[...]).astype(BF16), wa_ref[...])
    b = _dot((r + pb_ref[...]).astype(BF16), wb_ref[...])
    hid = a + pltpu.roll(b, ncp - 1, 0)
    hid = (hid * _sigmoid(hid)).astype(BF16)
    if is_key:
        y = _dot(hid, w2_ref[...])
        first_half = (_iota(y.shape, 1) & (HEAD_DIM - 1)) < ROPE_HALF
        y = _rope_chunk(y, cos_ref[0], sin_ref[0], first_half).astype(o_ref.dtype)
        for g in range(NSA_KV_GROUPS):
            o_ref[0, g] = y[:, g * HEAD_DIM:(g + 1) * HEAD_DIM]
    else:
        o_ref[0] = lax.dot_general(w2_ref[...], hid, _NT, preferred_element_type=F32).astype(o_ref.dtype)


def _compress(kv, pos_emb, w1, w2, cos_c, sin_c, is_key, name):
    bsz, t, _ = kv.shape
    ncp = t // CMP_STRIDE
    kwid = CMP_STRIDE * NSA_KV_GROUPS * HEAD_DIM
    hid_w = NSA_KV_GROUPS * CMP_HIDDEN
    r = kv.reshape(bsz, ncp, kwid)
    per = CMP_BLOCK // CMP_STRIDE
    w1r = w1.reshape(per, CMP_STRIDE, HEAD_DIM, CMP_HIDDEN)
    zeros = jnp.zeros_like(w1r)
    grp0 = jnp.concatenate([w1r, zeros], axis=-1)
    grp1 = jnp.concatenate([zeros, w1r], axis=-1)
    wbig = jnp.stack([grp0, grp1], axis=2).reshape(per, kwid, hid_w).astype(BF16)
    posr = pos_emb.reshape(per, CMP_STRIDE, 1, HEAD_DIM)
    posbig = jnp.broadcast_to(posr, (per, CMP_STRIDE, NSA_KV_GROUPS, HEAD_DIM)).reshape(per, 1, kwid)
    z2 = jnp.zeros_like(w2)
    w2big = jnp.concatenate([jnp.concatenate([w2, z2], axis=1),
                             jnp.concatenate([z2, w2], axis=1)], axis=0).astype(BF16)
    const = lambda shape: pl.BlockSpec(shape, lambda b: (0,) * len(shape))
    tab = pl.BlockSpec((1, ncp, LANES), lambda b: (b, 0, 0))
    if is_key:
        w2_arg = w2big
        out_spec = pl.BlockSpec((1, NSA_KV_GROUPS, ncp, HEAD_DIM), lambda b: (b, 0, 0, 0))
        out_shape = jax.ShapeDtypeStruct((bsz, NSA_KV_GROUPS, ncp, HEAD_DIM), BF16)
    else:
        w2_arg = w2big.T
        out_spec = pl.BlockSpec((1, NSA_KV_GROUPS * HEAD_DIM, ncp), lambda b: (b, 0, 0))
        out_shape = jax.ShapeDtypeStruct((bsz, NSA_KV_GROUPS * HEAD_DIM, ncp), BF16)
    return pl.pallas_call(
        functools.partial(_compress_kernel, is_key=is_key),
        grid=(bsz,),
        in_specs=[pl.BlockSpec((1, ncp, kwid), lambda b: (b, 0, 0)),
                  const((1, kwid)), const((1, kwid)),
                  const((kwid, hid_w)), const((kwid, hid_w)),
                  const(w2_arg.shape), tab, tab],
        out_specs=out_spec,
        out_shape=out_shape,
        compiler_params=_params("arbitrary"),
        name=name,
    )(r, posbig[0], posbig[1], wbig[0], wbig[1], w2_arg, cos_c, sin_c)


def _softmax_cols(s, bias, any_visible):
    sb = s + bias
    e = jnp.exp2(sb - jnp.max(sb, axis=0, keepdims=True))
    l = jnp.sum(e, axis=0, keepdims=True)
    return e * jnp.where(any_visible, 1.0 / l, 0.0)


def _online_step(s, tile_max, v_t, m_old, l_old, acc_old):
    m_new = jnp.maximum(m_old, tile_max)
    alpha = jnp.exp2(m_old - m_new)
    p = jnp.exp2(s - m_new)
    l_new = alpha * l_old + jnp.sum(p, axis=0, keepdims=True)
    acc_new = alpha * acc_old + _dot(v_t, p.astype(BF16))
    return m_new, l_new, acc_new


def _online_cols(s, v_t, m_old, l_old, acc_old):
    return _online_step(s, jnp.max(s, axis=0, keepdims=True), v_t, m_old, l_old, acc_old)


def _online_stored(s_ref, tile_max, v_t, m_old, l_old, acc_old):
    return _online_step(s_ref[...], tile_max, v_t, m_old, l_old, acc_old)


def _normalized(state):
    _, l, acc = state
    return acc * (1.0 / l)


def _sweep(n_earlier, n_chains, prepare, produce, consume, finalize, state, alive=None):
    chains = range(n_chains)

    def produce_all(slot, f, diagonal):
        ctx = prepare(f, diagonal)
        for c in chains:
            produce(ctx, slot, c)

    def consume_all(slot, f, diagonal, st):
        return tuple(consume(slot, f, diagonal, c, st[c]) for c in chains)

    def overlapped(p_slot, p_f, c_slot, c_f, c_diagonal, st):
        ctx = prepare(p_f, False)
        out = []
        for c in chains:
            produce(ctx, p_slot, c)
            out.append(consume(c_slot, c_f, c_diagonal, c, st[c]))
        return tuple(out)

    produce_all(0, 0, True)
    state = overlapped(1, 0, 0, 0, True, state)

    def pair(i, st):
        st = overlapped(0, 2 * i + 1, 1, 2 * i, False, st)
        if alive is None:
            return overlapped(1, 2 * i + 2, 0, 2 * i + 1, False, st)
        produce_all(1, 2 * i + 2, False)
        return lax.cond(alive(st), lambda s: consume_all(0, 2 * i + 1, False, s), lambda s: s, st)

    n_pairs = n_earlier >> 1
    odd = (n_earlier & 1) == 1
    if alive is None:
        state = lax.fori_loop(0, n_pairs, pair, state)
    else:
        _, state = lax.while_loop(lambda c: (c[0] < n_pairs) & alive(c[1]),
                                  lambda c: (c[0] + 1, pair(c[0], c[1])), (jnp.int32(0), state))
        odd = odd & alive(state)

    @pl.when(odd)
    def _():
        finalize(consume_all(1, n_earlier - 1, False, state))

    @pl.when(jnp.logical_not(odd))
    def _():
        finalize(state)


def _score_buffers(chains, qn):
    return [pltpu.VMEM((2, chains, KEY_TILE, qn), F32), pltpu.VMEM((2, chains, 1, qn), F32)]


def _online_init(n, dv, qn):
    return tuple((jnp.full((1, qn), NEG_INF, F32), jnp.zeros((1, qn), F32), jnp.zeros((dv, qn), F32))
                 for _ in range(n))


def _split_bf16(v, terms):
    out, rest = [], v
    for i in range(terms):
        part = rest.astype(BF16)
        out.append(part)
        if i + 1 < terms:
            rest = rest - part.astype(F32)
    return out


def _top_rows(key, n_top):
    n_rows, n_cols = key.shape

    def bisect(i, tau):
        cand = tau | lax.shift_left(jnp.int32(1), jnp.int32(30) - i)
        cnt = jnp.sum(jnp.where(key >= cand, 1.0, 0.0), axis=0, keepdims=True)
        return jnp.where(cnt >= float(n_top), cand, tau)

    tau = lax.fori_loop(0, 31, bisect, jnp.zeros((1, n_cols), I32))
    above = jnp.where(key > tau, 1.0, 0.0)
    equal = jnp.where(key == tau, 1.0, 0.0)
    need = float(n_top) - jnp.sum(above, axis=0, keepdims=True)
    lower = jnp.where(_iota((n_rows, n_rows), 1) < _iota((n_rows, n_rows), 0), 1.0, 0.0).astype(BF16)
    before = _dot(lower, equal.astype(BF16))
    return above + equal * jnp.where(before < need, 1.0, 0.0)

def _nsa_kernel(q_ref, kc_ref, vc_ref, ks_ref, vs_ref, kw_ref, vw_ref, g_ref, ovt_ref, o_ref,
                bias_scr, part_scr, s_scr, mx_scr, *, n_sel):
    qn = q_ref.shape[2]
    ncp = kc_ref.shape[2]
    ns = ovt_ref.shape[0]
    kt_n = KEY_TILE
    qi = pl.program_id(2)
    q0 = qi * qn
    reps = range(NSA_REP)
    q_head = lambda r: q_ref[0, r * HEAD_DIM:(r + 1) * HEAD_DIM, :]

    kc = kc_ref[0, 0]
    vc_t = vc_ref[0]
    bias_c = jnp.where((_iota((ncp, qn), 0) * CMP_STRIDE + (CMP_BLOCK - 1)) <= (q0 + _iota((ncp, qn), 1)),
                       0.0, NEG_INF)
    sees_cmp = (q0 + _iota((1, qn), 1)) >= CMP_BLOCK - 1
    o_cmp, p_sum = [], None
    for r in reps:
        p = _softmax_cols(_dot(kc, q_head(r)), bias_c, sees_cmp)
        o_cmp.append(_dot(vc_t, p.astype(BF16)))
        p_sum = p if p_sum is None else p_sum + p
    ovt = ovt_ref[...]
    imp_t = None
    for term in _split_bf16(p_sum, 3):
        part = _dot(ovt, term)
        imp_t = part if imp_t is None else imp_t + part

    blk = _iota((ns, qn), 0)
    cur = (q0 + _iota((ns, qn), 1)) >> SLC_SHIFT
    forced = (blk == 0) | (blk == cur) | (blk == cur - 1)
    imp_bits = jnp.where(imp_t > 0.0, lax.bitcast_convert_type(imp_t, I32), 0)
    key = jnp.where(forced, jnp.int32(2 ** 31 - 1), jnp.where(blk <= cur, imp_bits, -1))

    picked = _top_rows(key, n_sel)
    bias_scr[...] = jnp.where(picked > 0.5, 0.0, NEG_INF)

    carry = list(_online_init(NSA_REP, HEAD_DIM, qn))
    for diagonal in (True, False):
        kt = qi if diagonal else jnp.maximum(qi - WINDOW // kt_n, 0)
        k0 = pl.multiple_of(kt * kt_n, kt_n)
        kp = k0 + _iota((kt_n, qn), 0)
        tq = q0 + _iota((kt_n, qn), 1)
        inside = (kp <= tq) if diagonal else ((kp > tq - WINDOW) & (kp < q0))
        bias_w = jnp.where(inside, 0.0, NEG_INF)
        k = kw_ref[0, 0, pl.ds(k0, kt_n), :]
        scores = [_dot(k, q_head(r)) for r in reps]
        for r in reps:
            carry[r] = _online_cols(scores[r] + bias_w, vw_ref[0, kt], *carry[r])

    gates = g_ref[0]
    for r in reps:
        part_scr[r] = (gates[3 * r:3 * r + 1] * o_cmp[r]
                       + gates[3 * r + 2:3 * r + 3] * _normalized(carry[r]))

    blocks_per_tile = kt_n // SLC_BLOCK
    causal = _iota((kt_n, qn), 0) <= _iota((kt_n, qn), 1)
    def prepare(f, diagonal):
        kt = qi if diagonal else jnp.minimum(f, jnp.maximum(qi - 1, 0))
        k0 = pl.multiple_of(kt * kt_n, kt_n)
        rows = [jnp.broadcast_to(bias_scr[pl.ds(kt * blocks_per_tile + i, 1), :], (SLC_BLOCK, qn))
                for i in range(blocks_per_tile)]
        bias = jnp.concatenate(rows, axis=0)
        if diagonal:
            bias = jnp.where(causal, bias, NEG_INF)
        return k0, bias

    def produce(ctx, slot, r):
        k0, bias = ctx
        s = _dot(ks_ref[0, 0, pl.ds(k0, kt_n), :], q_head(r)) + bias
        s_scr[slot, r] = s
        mx_scr[slot, r] = jnp.max(s, axis=0, keepdims=True)

    def consume(slot, f, diagonal, r, state_r):
        return _online_stored(s_scr.at[slot, r], mx_scr[slot, r], vs_ref[0, qi if diagonal else f], *state_r)

    def finalize(state):
        for r in reps:
            o = part_scr[r] + g_ref[0, 3 * r + 1:3 * r + 2, :] * _normalized(state[r])
            o_ref[0, r * HEAD_DIM:(r + 1) * HEAD_DIM, :] = o.astype(o_ref.dtype)

    _sweep(qi, NSA_REP, prepare, produce, consume, finalize, _online_init(NSA_REP, HEAD_DIM, qn))


def _nsa_attention(q_t, k_cmp, v_cmp_t, k_slc, v_slc_t, k_win, v_win_t, gates_t):
    bsz, _, t = q_t.shape
    ncp = k_cmp.shape[2]
    ns = t // SLC_BLOCK
    qn = min(Q_TILE, t)
    grp_w = NSA_REP * HEAD_DIM
    cmp_start = jnp.arange(ncp) * CMP_STRIDE
    slc_start = jnp.arange(ns) * SLC_BLOCK
    real = jnp.arange(ncp) < (t - CMP_BLOCK) // CMP_STRIDE + 1
    overlap_t = ((cmp_start[None, :] < slc_start[:, None] + SLC_BLOCK)
                 & (cmp_start[None, :] + CMP_BLOCK > slc_start[:, None]) & real[None, :]).astype(BF16)
    k_spec = lambda n: pl.BlockSpec((1, 1, n, HEAD_DIM), lambda b, g, i: (b, g, 0, 0))
    v_spec = pl.BlockSpec((1, t // KEY_TILE, HEAD_DIM, KEY_TILE), lambda b, g, i: (b, 0, g, 0))
    return pl.pallas_call(
        functools.partial(_nsa_kernel, n_sel=min(N_SELECT, ns)),
        grid=(bsz, NSA_KV_GROUPS, t // qn),
        in_specs=[pl.BlockSpec((1, grp_w, qn), lambda b, g, i: (b, g, i)),
                  k_spec(ncp),
                  pl.BlockSpec((1, HEAD_DIM, ncp), lambda b, g, i: (b, g, 0)),
                  k_spec(t), v_spec, k_spec(t), v_spec,
                  pl.BlockSpec((1, GATE_ROWS, qn), lambda b, g, i: (b, g, i)),
                  pl.BlockSpec((ns, ncp), lambda b, g, i: (0, 0))],
        out_specs=pl.BlockSpec((1, grp_w, qn), lambda b, g, i: (b, g, i)),
        out_shape=jax.ShapeDtypeStruct((bsz, NSA_HEADS * HEAD_DIM, t), BF16),
        scratch_shapes=[pltpu.VMEM((ns, qn), F32), pltpu.VMEM((NSA_REP, HEAD_DIM, qn), F32)] + _score_buffers(NSA_REP, qn),
        compiler_params=_params("arbitrary", "arbitrary", "arbitrary"),
        name="nsa_attention",
    )(q_t, k_cmp, v_cmp_t, k_slc, v_slc_t, k_win, v_win_t, gates_t, overlap_t)


SB_HEADS_PER_STEP = 2
SB_DEAD_TAIL = -110.0
SB_SCAN_BLOCK = 128


def _sb_kernel(q_ref, k_ref, v_ref, o_ref, nz_scr):
    qn = q_ref.shape[2]
    kt_n = KEY_TILE
    qi = pl.program_id(2)
    heads = range(SB_HEADS_PER_STEP)
    rows = [slice(h * HEAD_DIM, (h + 1) * HEAD_DIM) for h in heads]
    sub = SB_SCAN_BLOCK
    n_sub = kt_n // sub
    tri = jnp.where(_iota((sub, sub), 1) > _iota((sub, sub), 0), 1.0, 0.0).astype(BF16)
    later2 = jnp.concatenate([tri, tri], axis=1)
    strictly_before = _iota((kt_n, qn), 0) < _iota((kt_n, qn), 1)

    def tile_index(f, diagonal):
        return qi if diagonal else jnp.maximum(qi - 1 - f, 0)

    def prepare(f, diagonal):
        return pl.multiple_of(tile_index(f, diagonal) * kt_n, kt_n)

    def produce(k0, slot, h):
        nz_scr[slot, h] = _dot(k_ref[0, h, pl.ds(k0, kt_n), :], -q_ref[0, rows[h], :])

    def consume(slot, f, diagonal, h, state_h):
        tail, acc = state_h
        nz = nz_scr[slot, h]
        neg_abs = lax.bitcast_convert_type(lax.bitcast_convert_type(nz, jnp.uint32) | jnp.uint32(0x80000000), F32)
        soft = jnp.log(1.0 + jnp.exp(neg_abs))
        log_keep = jnp.minimum(nz, 0.0) - soft
        log_beta = log_keep - nz
        if diagonal:
            log_keep = jnp.where(strictly_before, log_keep, 0.0)
        offs = tail
        parts = [None] * n_sub
        for blk in reversed(range(n_sub)):
            lk = log_keep[blk * sub:(blk + 1) * sub]
            hi = lk.astype(BF16)
            lo = (lk - hi.astype(F32)).astype(BF16)
            within = _dot(later2, jnp.concatenate([hi, lo], axis=0))
            parts[blk] = log_beta[blk * sub:(blk + 1) * sub] + within + offs
            offs = offs + jnp.sum(lk, axis=0, keepdims=True)
        a = jnp.exp(jnp.concatenate(parts, axis=0))
        if diagonal:
            a = jnp.where(strictly_before, a, 0.0)
        return offs, acc + _dot(v_ref[0, tile_index(f, diagonal), rows[h], :], a.astype(BF16))

    def finalize(state):
        for h in heads:
            o_ref[0, rows[h], :] = state[h][1].astype(o_ref.dtype)

    def alive(state):
        tails = [state[h][0] for h in heads]
        return jnp.max(functools.reduce(jnp.maximum, tails)) > SB_DEAD_TAIL

    init = tuple((jnp.zeros((1, qn), F32), jnp.zeros((HEAD_DIM, qn), F32)) for _ in heads)
    _sweep(qi, SB_HEADS_PER_STEP, prepare, produce, consume, finalize, init, alive)


def _sb_attention(q_t, k, v_t):
    bsz, nh, t, _ = k.shape
    qn = min(Q_TILE, t)
    hps = SB_HEADS_PER_STEP
    w = hps * HEAD_DIM
    return pl.pallas_call(
        _sb_kernel,
        grid=(bsz, nh // hps, t // qn),
        in_specs=[pl.BlockSpec((1, w, qn), lambda b, h, i: (b, h, i)),
                  pl.BlockSpec((1, hps, t, HEAD_DIM), lambda b, h, i: (b, h, 0, 0)),
                  pl.BlockSpec((1, t // KEY_TILE, w, KEY_TILE), lambda b, h, i: (b, 0, h, 0))],
        out_specs=pl.BlockSpec((1, w, qn), lambda b, h, i: (b, h, i)),
        out_shape=jax.ShapeDtypeStruct((bsz, nh * HEAD_DIM, t), BF16),
        scratch_shapes=[pltpu.VMEM((2, hps, KEY_TILE, qn), F32)],
        compiler_params=_params("arbitrary", "arbitrary", "arbitrary"),
        name="stickbreak_attention",
    )(q_t, k, v_t)


DIFF_HEADS_PER_STEP = 2


def _diff_kernel(q_ref, k_ref, v_ref, lq1_ref, lk1_ref, lq2_ref, lk2_ref, sub_ref, o_ref, s_scr, mx_scr,
                 *, lambda_init):
    qn = q_ref.shape[2]
    vd = 2 * HEAD_DIM
    kt_n = KEY_TILE
    qi = pl.program_id(2)
    chains = range(2 * DIFF_HEADS_PER_STEP)
    lam = (jnp.exp(jnp.sum(lq1_ref[...] * lk1_ref[...], axis=1, keepdims=True))
           - jnp.exp(jnp.sum(lq2_ref[...] * lk2_ref[...], axis=1, keepdims=True)) + lambda_init)

    causal = _iota((kt_n, qn), 0) <= _iota((kt_n, qn), 1)

    def prepare(f, diagonal):
        kt = qi if diagonal else jnp.minimum(f, jnp.maximum(qi - 1, 0))
        return pl.multiple_of(kt * kt_n, kt_n), diagonal

    def produce(ctx, slot, c):
        k0, diagonal = ctx
        s = _dot(k_ref[0, c, pl.ds(k0, kt_n), :], q_ref[0, c * HEAD_DIM:(c + 1) * HEAD_DIM, :])
        if diagonal:
            s = jnp.where(causal, s, NEG_INF)
        s_scr[slot, c] = s
        mx_scr[slot, c] = jnp.max(s, axis=0, keepdims=True)

    def consume(slot, f, diagonal, c, state_c):
        kt = qi if diagonal else f
        return _online_stored(s_scr.at[slot, c], mx_scr[slot, c],
                              v_ref[0, kt, (c // 2) * vd:(c // 2 + 1) * vd, :], *state_c)

    def finalize(state):
        for h in range(DIFF_HEADS_PER_STEP):
            o = _normalized(state[2 * h]) - lam * _normalized(state[2 * h + 1])
            ms = jnp.mean(o * o, axis=0, keepdims=True)
            o = o * lax.rsqrt(ms + RMS_EPS) * sub_ref[...] * (1.0 - lambda_init)
            o_ref[0, h * vd:(h + 1) * vd, :] = o.astype(o_ref.dtype)

    _sweep(qi, len(chains), prepare, produce, consume, finalize, _online_init(len(chains), vd, qn))


def _diff_attention(q_t, k, v_t, lq1, lk1, lq2, lk2, subln, lambda_init):
    bsz, nh2, t, _ = k.shape
    nh = nh2 // 2
    vd = 2 * HEAD_DIM
    hps = DIFF_HEADS_PER_STEP
    qn = min(Q_TILE, t)
    row = lambda a: a.reshape(1, -1)
    small = lambda n: pl.BlockSpec((1, n), lambda b, h, i: (0, 0))
    return pl.pallas_call(
        functools.partial(_diff_kernel, lambda_init=lambda_init),
        grid=(bsz, nh // hps, t // qn),
        in_specs=[pl.BlockSpec((1, hps * vd, qn), lambda b, h, i: (b, h, i)),
                  pl.BlockSpec((1, 2 * hps, t, HEAD_DIM), lambda b, h, i: (b, h, 0, 0)),
                  pl.BlockSpec((1, t // KEY_TILE, hps * vd, KEY_TILE), lambda b, h, i: (b, 0, h, 0)),
                  small(HEAD_DIM), small(HEAD_DIM), small(HEAD_DIM), small(HEAD_DIM),
                  pl.BlockSpec((vd, 1), lambda b, h, i: (0, 0))],
        out_specs=pl.BlockSpec((1, hps * vd, qn), lambda b, h, i: (b, h, i)),
        out_shape=jax.ShapeDtypeStruct((bsz, nh * vd, t), BF16),
        scratch_shapes=_score_buffers(2 * hps, qn),
        compiler_params=_params("arbitrary", "arbitrary", "arbitrary"),
        name="diff_attention",
    )(q_t, k, v_t, row(lq1), row(lk1), row(lq2), row(lk2), subln.reshape(vd, 1))


def _outproj_kernel(*refs, n_in):
    a_refs, w_refs = refs[:n_in], refs[n_in:2 * n_in]
    x_ref, gate_ref, o_ref = refs[2 * n_in:]
    y = None
    for a_ref, w_ref in zip(a_refs, w_refs):
        part = lax.dot_general(a_ref[0], w_ref[...].astype(BF16), _TN, preferred_element_type=F32)
        y = part if y is None else y + part
    o_ref[0] = x_ref[0] + gate_ref[0] * y


def _out_project(acts_t, weight, x, gate, name):
    bsz, t, d = x.shape
    rows = min(t, PROJ_ROWS)
    k_rows = acts_t[0].shape[1]
    tile = pl.BlockSpec((1, rows, d), lambda b, i: (b, i, 0))
    return pl.pallas_call(
        functools.partial(_outproj_kernel, n_in=len(acts_t)),
        grid=(bsz, t // rows),
        in_specs=([pl.BlockSpec((1, a.shape[1], rows), lambda b, i: (b, 0, i)) for a in acts_t]
                  + [pl.BlockSpec((k_rows, d), lambda b, i, n=n: (n, 0)) for n in range(len(acts_t))]
                  + [tile, pl.BlockSpec((1, 1, d), lambda b, i: (b, 0, 0))]),
        out_specs=tile,
        out_shape=jax.ShapeDtypeStruct((bsz, t, d), F32),
        compiler_params=_params("arbitrary", "arbitrary"),
        name=name,
    )(*acts_t, *([weight] * len(acts_t)), x, gate)


def _ffn_kernel(x_ref, halo_ref, g_ref, sh_ref, sc_ref, gate_ref, wg_ref, wu_ref, cw_ref, cb_ref, wd_ref,
                nf_ref, o_ref, a_scr, *, final_norm):
    ti = pl.program_id(1)
    h = _norm_modulate(x_ref[0], g_ref[...], sh_ref[0], sc_ref[0]).astype(BF16)
    h_prev = _norm_modulate(halo_ref[0], g_ref[...], sh_ref[0], sc_ref[0]).astype(BF16)
    seen = jnp.where(ti > 0, 1.0, 0.0)
    cols = FFN_COLS
    row = _iota((h.shape[0], cols), 0)
    for c in range(a_scr.shape[1] // cols):
        sl = slice(c * cols, (c + 1) * cols)
        wg = wg_ref[0, :, sl].astype(BF16)
        g = _dot(h, wg)
        g_prev = _dot(h_prev, wg) * seen
        last = g_prev[HALO_ROWS - 1:HALO_ROWS]
        last2 = g_prev[HALO_ROWS - 2:HALO_ROWS - 1]
        g1 = jnp.where(row == 0, last, pltpu.roll(g, 1, 0))
        g2 = jnp.where(row == 0, last2, jnp.where(row == 1, last, pltpu.roll(g, 2, 0)))
        cw = cw_ref[:, sl]
        conv = cw[0:1] * g2 + cw[1:2] * g1 + cw[2:3] * g + cb_ref[:, sl]
        a_scr[:, sl] = (conv * _sigmoid(conv) * _dot(h, wu_ref[0, :, sl].astype(BF16))).astype(BF16)
    y = x_ref[0] + gate_ref[0] * _dot(a_scr[...], wd_ref[...])
    if final_norm:
        ms = jnp.mean(y * y, axis=-1, keepdims=True)
        y = y * lax.rsqrt(ms + RMS_EPS) * nf_ref[...]
    o_ref[0] = y


def _conv_ffn(x, g, shift, scale, gate, layer, w_gate_all, w_up_all, conv_w, conv_b, w_down, norm_f, final_norm,
              name):
    bsz, t, d = x.shape
    f = w_gate_all.shape[2]
    rows = min(t, FFN_ROWS)
    halo_blocks = rows // HALO_ROWS
    vec = pl.BlockSpec((1, 1, d), lambda b, i: (b, 0, 0))
    drow = pl.BlockSpec((1, d), lambda b, i: (0, 0))
    resident = lambda shape: pl.BlockSpec(shape, lambda b, i: (0, 0), pipeline_mode=pl.Buffered(1))
    stacked = pl.BlockSpec((1, d, f), lambda b, i: (layer, 0, 0), pipeline_mode=pl.Buffered(1))
    return pl.pallas_call(
        functools.partial(_ffn_kernel, final_norm=final_norm),
        grid=(bsz, t // rows),
        in_specs=[pl.BlockSpec((1, rows, d), lambda b, i: (b, i, 0)),
                  pl.BlockSpec((1, HALO_ROWS, d), lambda b, i: (b, jnp.maximum(i * halo_blocks - 1, 0), 0)),
                  drow, vec, vec, vec,
                  stacked, stacked, resident((CONV_WIDTH, f)), resident((1, f)),
                  resident((f, d)), drow],
        out_specs=pl.BlockSpec((1, rows, d), lambda b, i: (b, i, 0)),
        out_shape=jax.ShapeDtypeStruct((bsz, t, d), F32),
        scratch_shapes=[pltpu.VMEM((rows, f), BF16)],
        compiler_params=_params("arbitrary", "arbitrary"),
        name=name,
    )(x, x, g.reshape(1, d), shift, scale, gate, w_gate_all, w_up_all, conv_w,
      conv_b.reshape(1, f), w_down.astype(BF16), norm_f.reshape(1, d))


def _layout(widths):
    starts, start = [], 0
    for w in widths:
        starts.append(start)
        start += w
    return starts


def _hybrid_layer(x, mods, tables, norm_g, w_in, pos_k, pos_v, ck_w1, ck_w2, cv_w1, cv_w2, w_out):
    shift, scale, gate = mods
    bsz, t, d = x.shape
    qw, kvw, sbw = NSA_HEADS * HEAD_DIM, NSA_KV_GROUPS * HEAD_DIM, SB_HEADS * HEAD_DIM
    col = _layout((qw,) + (kvw,) * 6 + (3 * NSA_HEADS,) + (sbw,) * 3)
    cols = lambda i, w: w_in[:, col[i]:col[i] + w]
    q_n, kc, vc, ks, vs, kw, vw, gl, q_s, k_s, v_s = (cols(i, w) for i, w in enumerate(
        (qw,) + (kvw,) * 6 + (3 * NSA_HEADS,) + (sbw,) * 3))
    inv_sqrt = HEAD_DIM ** -0.5

    w_tok = jnp.concatenate([ks, kw, kc, vc, k_s], axis=1).astype(BF16)
    ts = _layout((kvw, kvw, kvw, kvw, sbw))
    tok_segs = [TokSeg(ts[0], kvw, True, HEAD_DIM, BF16),
                TokSeg(ts[1], kvw, True, HEAD_DIM, BF16),
                TokSeg(ts[2], kvw, False, 0, F32),
                TokSeg(ts[3], kvw, False, 0, F32),
                TokSeg(ts[4], sbw, False, HEAD_DIM, BF16)]
    per_group = 3 * NSA_REP
    gate_pad = jnp.zeros((d, GATE_ROWS - per_group), w_in.dtype)
    gl_pad = jnp.concatenate([gl[:, :per_group], gate_pad, gl[:, per_group:], gate_pad], axis=1)
    w_feat_t = jnp.concatenate([q_n, q_s, vs, vw, v_s, gl_pad], axis=1).T.astype(BF16)
    fs = _layout((qw, sbw, kvw, kvw, sbw, NSA_KV_GROUPS * GATE_ROWS))
    feat_segs = [FeatSeg(fs[0], qw, True, inv_sqrt * LOG2E, False, False, BF16),
                 FeatSeg(fs[1], sbw, False, inv_sqrt, False, False, BF16),
                 FeatSeg(fs[2], kvw, False, 1.0, False, True, BF16),
                 FeatSeg(fs[3], kvw, False, 1.0, False, True, BF16),
                 FeatSeg(fs[4], sbw, False, 1.0, False, True, BF16),
                 FeatSeg(fs[5], NSA_KV_GROUPS * GATE_ROWS, False, 1.0, True, False, F32)]
    (k_slc, k_win, kc_a, vc_a, k_sb, q_n_t, q_s_t, v_slc_t, v_win_t, v_sb_t, gates_t) = _project(
        x, norm_g, shift, scale, w_tok, w_feat_t, tables, tok_segs, feat_segs, "hybrid_in_proj")

    ncp = t // CMP_STRIDE
    end_rows = jnp.minimum(jnp.arange(ncp) * CMP_STRIDE + CMP_BLOCK - 1, t - 1)
    cos_c, sin_c = tables[0][:, end_rows], tables[1][:, end_rows]
    k_cmp = _compress(kc_a, pos_k, ck_w1, ck_w2, cos_c, sin_c, True, "compress_k")
    v_cmp_t = _compress(vc_a, pos_v, cv_w1, cv_w2, cos_c, sin_c, False, "compress_v")
    o_nsa_t = _nsa_attention(q_n_t, k_cmp, v_cmp_t, k_slc, v_slc_t, k_win, v_win_t, gates_t)
    o_sb_t = _sb_attention(q_s_t, k_sb, v_sb_t)
    return _out_project([o_nsa_t, o_sb_t], w_out, x, gate, "hybrid_out_proj")


def _diff_layer(x, mods, tables, norm_g, w_qkv, lq1, lk1, lq2, lk2, subln, w_out, layer_idx):
    shift, scale, gate = mods
    dw = 2 * DIFF_HEADS * HEAD_DIM
    inv_sqrt = HEAD_DIM ** -0.5
    w_tok = w_qkv[:, dw:2 * dw].astype(BF16)
    w_feat_t = jnp.concatenate([w_qkv[:, :dw], w_qkv[:, 2 * dw:]], axis=1).T.astype(BF16)
    tok_segs = [TokSeg(0, dw, True, HEAD_DIM, BF16)]
    feat_segs = [FeatSeg(0, dw, True, inv_sqrt * LOG2E, False, False, BF16),
                 FeatSeg(dw, dw, False, 1.0, False, True, BF16)]
    k, q_t, v_t = _project(x, norm_g, shift, scale, w_tok, w_feat_t, tables, tok_segs, feat_segs, "diff_in_proj")
    lambda_init = 0.8 - 0.6 * math.exp(-0.3 * layer_idx)
    o_t = _diff_attention(q_t, k, v_t, lq1, lk1, lq2, lk2, subln, lambda_init)
    return _out_project([o_t], w_out, x, gate, "diff_out_proj")


def kernel(x, c, positions, mod_w, mod_b, norm_mix, norm_ffn, ffn_w_gate, ffn_w_up, ffn_conv_w, ffn_conv_b, ffn_w_down, hyb_w_in, nsa_pos_k, nsa_pos_v, nsa_ck_w1, nsa_ck_w2, nsa_cv_w1, nsa_cv_w2, hyb_w_out, diff_w_qkv, diff_lq1, diff_lk1, diff_lq2, diff_lk2, diff_subln, diff_w_out, norm_f):
    bsz, t, d = x.shape
    depth = mod_w.shape[0]
    mod = _adaln_mod(c, mod_w, mod_b)
    tables = _rope_tables(positions)
    for i in range(depth):
        sh_m, sc_m, g_m, sh_f, sc_f, g_f = (mod[i, :, k * d:(k + 1) * d].reshape(bsz, 1, d) for k in range(6))
        j = i // 2
        if i % 2 == 0:
            x = _hybrid_layer(x, (sh_m, sc_m, g_m), tables, norm_mix[i], hyb_w_in[j],
                              nsa_pos_k[j], nsa_pos_v[j], nsa_ck_w1[j], nsa_ck_w2[j], nsa_cv_w1[j],
                              nsa_cv_w2[j], hyb_w_out[j])
        else:
            x = _diff_layer(x, (sh_m, sc_m, g_m), tables, norm_mix[i], diff_w_qkv[j], diff_lq1[j],
                            diff_lk1[j], diff_lq2[j], diff_lk2[j], diff_subln[j], diff_w_out[j], i)
        x = _conv_ffn(x, norm_ffn[i], sh_f, sc_f, g_f, i, ffn_w_gate, ffn_w_up, ffn_conv_w[i],
                      ffn_conv_b[i], ffn_w_down[i], norm_f, i == depth - 1, "conv_ffn_%d" % i)
    return x
```

```python
import functools
import math
from typing import NamedTuple

import jax
import jax.numpy as jnp
from jax import lax
from jax.experimental import pallas as pl
from jax.experimental.pallas import tpu as pltpu

F32 = jnp.float32
BF16 = jnp.bfloat16
I32 = jnp.int32

HEAD_DIM = 64
ROPE_DIM = HEAD_DIM // 4
ROPE_HALF = ROPE_DIM // 2
ROPE_THETA = 500000.0
NSA_HEADS = 8
NSA_KV_GROUPS = 2
NSA_REP = NSA_HEADS // NSA_KV_GROUPS
CMP_BLOCK = 32
CMP_STRIDE = 16
CMP_HIDDEN = 4 * HEAD_DIM
SLC_BLOCK = 64
SLC_SHIFT = 6
N_SELECT = 16
WINDOW = 512
SB_HEADS = 8
DIFF_HEADS = 8
CONV_WIDTH = 3
RMS_EPS = 1e-6
NEG_INF = -1e30
LOG2E = math.log2(math.e)

LANES = 128
Q_TILE = 512
KEY_TILE = 512
GATE_ROWS = 16
PROJ_ROWS = 512
FFN_ROWS = 512
FFN_COLS = 256
HALO_ROWS = 16
VMEM_LIMIT = 56 * 1024 * 1024

_NT = (((1,), (1,)), ((), ()))
_TN = (((0,), (0,)), ((), ()))


def _params(*sem):
    return pltpu.CompilerParams(dimension_semantics=sem, vmem_limit_bytes=VMEM_LIMIT)


def _sigmoid(v):
    return 1.0 / (1.0 + jnp.exp(-v))


def _iota(shape, axis):
    return lax.broadcasted_iota(I32, shape, axis)


def _dot(a, b):
    return jnp.dot(a, b, preferred_element_type=F32)


def _mod_kernel(c_ref, w_ref, b_ref, o_ref):
    c = c_ref[...]
    cond = c * _sigmoid(c)
    o_ref[0] = jnp.dot(cond, w_ref[0], preferred_element_type=F32,
                       precision=lax.Precision.HIGHEST) + b_ref[0]


def _adaln_mod(c, mod_w, mod_b):
    depth, d, n = mod_w.shape
    bsz = c.shape[0]
    tn = n // 4
    return pl.pallas_call(
        _mod_kernel,
        grid=(depth, n // tn),
        in_specs=[pl.BlockSpec((bsz, d), lambda i, j: (0, 0)),
                  pl.BlockSpec((1, d, tn), lambda i, j: (i, 0, j)),
                  pl.BlockSpec((1, 1, tn), lambda i, j: (i, 0, j))],
        out_specs=pl.BlockSpec((1, bsz, tn), lambda i, j: (i, 0, j)),
        out_shape=jax.ShapeDtypeStruct((depth, bsz, n), F32),
        compiler_params=_params("arbitrary", "arbitrary"),
        name="adaln_mod",
    )(c, mod_w, mod_b.reshape(depth, 1, n))


def _rope_kernel(pos_col_ref, pos_row_ref, inv_row_ref, sgn_row_ref, inv_col_ref,
                 cos_ref, sin_ref, cos_t_ref, sin_t_ref):
    ang = pos_col_ref[0].astype(F32) * inv_row_ref[...]
    cos_ref[0] = jnp.cos(ang)
    sin_ref[0] = jnp.sin(ang) * sgn_row_ref[...]
    ang_t = inv_col_ref[...] * pos_row_ref[0].astype(F32)
    cos_t_ref[0] = jnp.cos(ang_t)
    sin_t_ref[0] = jnp.sin(ang_t)


def _rope_tables(positions):
    bsz, t = positions.shape
    inv = ROPE_THETA ** (-jnp.arange(0, ROPE_DIM, 2, dtype=F32) / ROPE_DIM)
    per_head_inv = jnp.concatenate([inv, inv, jnp.zeros((HEAD_DIM - ROPE_DIM,), F32)])
    per_head_sgn = jnp.concatenate([-jnp.ones((ROPE_HALF,), F32), jnp.ones((ROPE_HALF,), F32),
                                    jnp.zeros((HEAD_DIM - ROPE_DIM,), F32)])
    inv_row = jnp.tile(per_head_inv, LANES // HEAD_DIM)[None, :]
    sgn_row = jnp.tile(per_head_sgn, LANES // HEAD_DIM)[None, :]
    rows = min(t, 1024)
    tab = jax.ShapeDtypeStruct((bsz, t, LANES), F32)
    tab_t = jax.ShapeDtypeStruct((bsz, ROPE_HALF, t), F32)
    row_spec = pl.BlockSpec((1, LANES), lambda b, i: (0, 0))
    return pl.pallas_call(
        _rope_kernel,
        grid=(bsz, t // rows),
        in_specs=[pl.BlockSpec((1, rows, 1), lambda b, i: (b, i, 0)),
                  pl.BlockSpec((1, 1, rows), lambda b, i: (b, 0, i)),
                  row_spec, row_spec,
                  pl.BlockSpec((ROPE_HALF, 1), lambda b, i: (0, 0))],
        out_specs=[pl.BlockSpec((1, rows, LANES), lambda b, i: (b, i, 0))] * 2
        + [pl.BlockSpec((1, ROPE_HALF, rows), lambda b, i: (b, 0, i))] * 2,
        out_shape=[tab, tab, tab_t, tab_t],
        compiler_params=_params("arbitrary", "arbitrary"),
        name="rope_tables",
    )(positions.reshape(bsz, t, 1), positions.reshape(bsz, 1, t), inv_row, sgn_row, inv[:, None])


def _rope_chunk(y, cosv, sinv, first_half):
    ahead = pltpu.roll(y, LANES - ROPE_HALF, 1)
    behind = pltpu.roll(y, ROPE_HALF, 1)
    return y * cosv + jnp.where(first_half, ahead, behind) * sinv


def _rope_rows(y_t, cos_t, sin_t):
    heads = []
    for h in range(y_t.shape[0] // HEAD_DIM):
        blk = y_t[h * HEAD_DIM:(h + 1) * HEAD_DIM]
        x1, x2 = blk[:ROPE_HALF], blk[ROPE_HALF:ROPE_DIM]
        heads += [x1 * cos_t - x2 * sin_t, x2 * cos_t + x1 * sin_t, blk[ROPE_DIM:]]
    return jnp.concatenate(heads, axis=0)


class TokSeg(NamedTuple):
    start: int
    width: int
    rope: bool
    head_width: int
    dtype: object


class FeatSeg(NamedTuple):
    start: int
    rows: int
    rope: bool
    scale: float
    sigmoid: bool
    key_tiled: bool
    dtype: object


def _norm_modulate(x, g, shift, scale):
    ms = jnp.mean(x * x, axis=-1, keepdims=True)
    y = x * lax.rsqrt(ms + RMS_EPS) * g
    return y * (1.0 + scale) + shift


def _proj_kernel(x_ref, g_ref, sh_ref, sc_ref, w_ref, wt_ref, cos_ref, sin_ref, cos_t_ref, sin_t_ref,
                 *out_refs, tok_segs, feat_segs):
    hb = _norm_modulate(x_ref[0], g_ref[...], sh_ref[0], sc_ref[0]).astype(BF16)
    cosv, sinv = cos_ref[0], sin_ref[0]
    first_half = (_iota(cosv.shape, 1) & (HEAD_DIM - 1)) < ROPE_HALF
    tok_refs, feat_refs = out_refs[:len(tok_segs)], out_refs[len(tok_segs):]
    for seg, o_ref in zip(tok_segs, tok_refs):
        y = _dot(hb, w_ref[:, seg.start:seg.start + seg.width])
        for ch in range(seg.width // LANES):
            yc = y[:, ch * LANES:(ch + 1) * LANES]
            if seg.rope:
                yc = _rope_chunk(yc, cosv, sinv, first_half)
            yc = yc.astype(seg.dtype)
            if seg.head_width == 0:
                o_ref[0, :, ch * LANES:(ch + 1) * LANES] = yc
            else:
                o_ref[0, 2 * ch] = yc[:, :HEAD_DIM]
                o_ref[0, 2 * ch + 1] = yc[:, HEAD_DIM:]
    y_t_all = lax.dot_general(wt_ref[...], hb, _NT, preferred_element_type=F32)
    for seg, o_ref in zip(feat_segs, feat_refs):
        y_t = y_t_all[seg.start:seg.start + seg.rows]
        if seg.rope:
            y_t = _rope_rows(y_t, cos_t_ref[0], sin_t_ref[0])
        if seg.scale != 1.0:
            y_t = y_t * seg.scale
        if seg.sigmoid:
            y_t = _sigmoid(y_t)
        y_t = y_t.astype(seg.dtype)
        if seg.key_tiled:
            for ch in range(y_t.shape[1] // KEY_TILE):
                o_ref[0, ch] = y_t[:, ch * KEY_TILE:(ch + 1) * KEY_TILE]
        else:
            o_ref[0] = y_t


def _project(x, g, shift, scale, w_tok, w_feat_t, tables, tok_segs, feat_segs, name):
    bsz, t, d = x.shape
    rows = min(t, PROJ_ROWS)
    cos_tab, sin_tab, cos_t, sin_t = tables
    out_shapes, out_specs = [], []
    for seg in tok_segs:
        if seg.head_width == 0:
            out_shapes.append(jax.ShapeDtypeStruct((bsz, t, seg.width), seg.dtype))
            out_specs.append(pl.BlockSpec((1, rows, seg.width), lambda b, i: (b, i, 0)))
        else:
            nh = seg.width // seg.head_width
            out_shapes.append(jax.ShapeDtypeStruct((bsz, nh, t, seg.head_width), seg.dtype))
            out_specs.append(pl.BlockSpec((1, nh, rows, seg.head_width), lambda b, i: (b, 0, i, 0)))
    for seg in feat_segs:
        if seg.key_tiled:
            out_shapes.append(jax.ShapeDtypeStruct((bsz, t // KEY_TILE, seg.rows, KEY_TILE), seg.dtype))
            out_specs.append(pl.BlockSpec((1, rows // KEY_TILE, seg.rows, KEY_TILE), lambda b, i: (b, i, 0, 0)))
        else:
            out_shapes.append(jax.ShapeDtypeStruct((bsz, seg.rows, t), seg.dtype))
            out_specs.append(pl.BlockSpec((1, seg.rows, rows), lambda b, i: (b, 0, i)))
    vec = pl.BlockSpec((1, 1, d), lambda b, i: (b, 0, 0))
    tab = pl.BlockSpec((1, rows, LANES), lambda b, i: (b, i, 0))
    tab_t = pl.BlockSpec((1, ROPE_HALF, rows), lambda b, i: (b, 0, i))
    return pl.pallas_call(
        functools.partial(_proj_kernel, tok_segs=tuple(tok_segs), feat_segs=tuple(feat_segs)),
        grid=(bsz, t // rows),
        in_specs=[pl.BlockSpec((1, rows, d), lambda b, i: (b, i, 0)),
                  pl.BlockSpec((1, d), lambda b, i: (0, 0)),
                  vec, vec,
                  pl.BlockSpec(w_tok.shape, lambda b, i: (0, 0)),
                  pl.BlockSpec(w_feat_t.shape, lambda b, i: (0, 0)),
                  tab, tab, tab_t, tab_t],
        out_specs=out_specs,
        out_shape=out_shapes,
        compiler_params=_params("arbitrary", "arbitrary"),
        name=name,
    )(x, g.reshape(1, d), shift, scale, w_tok, w_feat_t, cos_tab, sin_tab, cos_t, sin_t)


def _compress_kernel(r_ref, pa_ref, pb_ref, wa_ref, wb_ref, w2_ref, cos_ref, sin_ref, o_ref, *, is_key):
    r = r_ref[0]
    ncp = r.shape[0]
    a = _dot((r + pa_ref[...]).astype(BF16), wa_ref[...])
    b = _dot((r + pb_ref[...]).astype(BF16), wb_ref[...])
    hid = a + pltpu.roll(b, ncp - 1, 0)
    hid = (hid * _sigmoid(hid)).astype(BF16)
    if is_key:
        y = _dot(hid, w2_ref[...])
        first_half = (_iota(y.shape, 1) & (HEAD_DIM - 1)) < ROPE_HALF
        y = _rope_chunk(y, cos_ref[0], sin_ref[0], first_half).astype(o_ref.dtype)
        for g in range(NSA_KV_GROUPS):
            o_ref[0, g] = y[:, g * HEAD_DIM:(g + 1) * HEAD_DIM]
    else:
        o_ref[0] = lax.dot_general(w2_ref[...], hid, _NT, preferred_element_type=F32).astype(o_ref.dtype)


def _compress(kv, pos_emb, w1, w2, cos_c, sin_c, is_key, name):
    bsz, t, _ = kv.shape
    ncp = t // CMP_STRIDE
    kwid = CMP_STRIDE * NSA_KV_GROUPS * HEAD_DIM
    hid_w = NSA_KV_GROUPS * CMP_HIDDEN
    r = kv.reshape(bsz, ncp, kwid)
    per = CMP_BLOCK // CMP_STRIDE
    w1r = w1.reshape(per, CMP_STRIDE, HEAD_DIM, CMP_HIDDEN)
    zeros = jnp.zeros_like(w1r)
    grp0 = jnp.concatenate([w1r, zeros], axis=-1)
    grp1 = jnp.concatenate([zeros, w1r], axis=-1)
    wbig = jnp.stack([grp0, grp1], axis=2).reshape(per, kwid, hid_w).astype(BF16)
    posr = pos_emb.reshape(per, CMP_STRIDE, 1, HEAD_DIM)
    posbig = jnp.broadcast_to(posr, (per, CMP_STRIDE, NSA_KV_GROUPS, HEAD_DIM)).reshape(per, 1, kwid)
    z2 = jnp.zeros_like(w2)
    w2big = jnp.concatenate([jnp.concatenate([w2, z2], axis=1),
                             jnp.concatenate([z2, w2], axis=1)], axis=0).astype(BF16)
    const = lambda shape: pl.BlockSpec(shape, lambda b: (0,) * len(shape))
    tab = pl.BlockSpec((1, ncp, LANES), lambda b: (b, 0, 0))
    if is_key:
        w2_arg = w2big
        out_spec = pl.BlockSpec((1, NSA_KV_GROUPS, ncp, HEAD_DIM), lambda b: (b, 0, 0, 0))
        out_shape = jax.ShapeDtypeStruct((bsz, NSA_KV_GROUPS, ncp, HEAD_DIM), BF16)
    else:
        w2_arg = w2big.T
        out_spec = pl.BlockSpec((1, NSA_KV_GROUPS * HEAD_DIM, ncp), lambda b: (b, 0, 0))
        out_shape = jax.ShapeDtypeStruct((bsz, NSA_KV_GROUPS * HEAD_DIM, ncp), BF16)
    return pl.pallas_call(
        functools.partial(_compress_kernel, is_key=is_key),
        grid=(bsz,),
        in_specs=[pl.BlockSpec((1, ncp, kwid), lambda b: (b, 0, 0)),
                  const((1, kwid)), const((1, kwid)),
                  const((kwid, hid_w)), const((kwid, hid_w)),
                  const(w2_arg.shape), tab, tab],
        out_specs=out_spec,
        out_shape=out_shape,
        compiler_params=_params("arbitrary"),
        name=name,
    )(r, posbig[0], posbig[1], wbig[0], wbig[1], w2_arg, cos_c, sin_c)


def _softmax_cols(s, bias, any_visible):
    sb = s + bias
    e = jnp.exp2(sb - jnp.max(sb, axis=0, keepdims=True))
    l = jnp.sum(e, axis=0, keepdims=True)
    return e * jnp.where(any_visible, 1.0 / l, 0.0)


def _online_step(s, tile_max, v_t, m_old, l_old, acc_old):
    m_new = jnp.maximum(m_old, tile_max)
    alpha = jnp.exp2(m_old - m_new)
    p = jnp.exp2(s - m_new)
    l_new = alpha * l_old + jnp.sum(p, axis=0, keepdims=True)
    acc_new = alpha * acc_old + _dot(v_t, p.astype(BF16))
    return m_new, l_new, acc_new


def _online_cols(s, v_t, m_old, l_old, acc_old):
    return _online_step(s, jnp.max(s, axis=0, keepdims=True), v_t, m_old, l_old, acc_old)


def _online_stored(s_ref, tile_max, v_t, m_old, l_old, acc_old):
    return _online_step(s_ref[...], tile_max, v_t, m_old, l_old, acc_old)


def _normalized(state):
    _, l, acc = state
    return acc * (1.0 / l)


def _sweep(n_earlier, n_chains, prepare, produce, consume, finalize, state, alive=None):
    chains = range(n_chains)

    def produce_all(slot, f, diagonal):
        ctx = prepare(f, diagonal)
        for c in chains:
            produce(ctx, slot, c)

    def consume_all(slot, f, diagonal, st):
        return tuple(consume(slot, f, diagonal, c, st[c]) for c in chains)

    def overlapped(p_slot, p_f, c_slot, c_f, c_diagonal, st):
        ctx = prepare(p_f, False)
        out = []
        for c in chains:
            produce(ctx, p_slot, c)
            out.append(consume(c_slot, c_f, c_diagonal, c, st[c]))
        return tuple(out)

    produce_all(0, 0, True)
    state = overlapped(1, 0, 0, 0, True, state)

    def pair(i, st):
        st = overlapped(0, 2 * i + 1, 1, 2 * i, False, st)
        if alive is None:
            return overlapped(1, 2 * i + 2, 0, 2 * i + 1, False, st)
        produce_all(1, 2 * i + 2, False)
        return lax.cond(alive(st), lambda s: consume_all(0, 2 * i + 1, False, s), lambda s: s, st)

    n_pairs = n_earlier >> 1
    odd = (n_earlier & 1) == 1
    if alive is None:
        state = lax.fori_loop(0, n_pairs, pair, state)
    else:
        _, state = lax.while_loop(lambda c: (c[0] < n_pairs) & alive(c[1]),
                                  lambda c: (c[0] + 1, pair(c[0], c[1])), (jnp.int32(0), state))
        odd = odd & alive(state)

    @pl.when(odd)
    def _():
        finalize(consume_all(1, n_earlier - 1, False, state))

    @pl.when(jnp.logical_not(odd))
    def _():
        finalize(state)


def _score_buffers(chains, qn):
    return [pltpu.VMEM((2, chains, KEY_TILE, qn), F32), pltpu.VMEM((2, chains, 1, qn), F32)]


def _online_init(n, dv, qn):
    return tuple((jnp.full((1, qn), NEG_INF, F32), jnp.zeros((1, qn), F32), jnp.zeros((dv, qn), F32))
                 for _ in range(n))


def _split_bf16(v, terms):
    out, rest = [], v
    for i in range(terms):
        part = rest.astype(BF16)
        out.append(part)
        if i + 1 < terms:
            rest = rest - part.astype(F32)
    return out


def _top_rows(key, n_top):
    n_rows, n_cols = key.shape

    def bisect(i, tau):
        cand = tau | lax.shift_left(jnp.int32(1), jnp.int32(30) - i)
        cnt = jnp.sum(jnp.where(key >= cand, 1.0, 0.0), axis=0, keepdims=True)
        return jnp.where(cnt >= float(n_top), cand, tau)

    tau = lax.fori_loop(0, 31, bisect, jnp.zeros((1, n_cols), I32))
    above = jnp.where(key > tau, 1.0, 0.0)
    equal = jnp.where(key == tau, 1.0, 0.0)
    need = float(n_top) - jnp.sum(above, axis=0, keepdims=True)
    lower = jnp.where(_iota((n_rows, n_rows), 1) < _iota((n_rows, n_rows), 0), 1.0, 0.0).astype(BF16)
    before = _dot(lower, equal.astype(BF16))
    return above + equal * jnp.where(before < need, 1.0, 0.0)

def _nsa_kernel(q_ref, kc_ref, vc_ref, ks_ref, vs_ref, kw_ref, vw_ref, g_ref, ovt_ref, o_ref,
                bias_scr, part_scr, s_scr, mx_scr, *, n_sel):
    qn = q_ref.shape[2]
    ncp = kc_ref.shape[2]
    ns = ovt_ref.shape[0]
    kt_n = KEY_TILE
    qi = pl.program_id(2)
    q0 = qi * qn
    reps = range(NSA_REP)
    q_head = lambda r: q_ref[0, r * HEAD_DIM:(r + 1) * HEAD_DIM, :]

    kc = kc_ref[0, 0]
    vc_t = vc_ref[0]
    bias_c = jnp.where((_iota((ncp, qn), 0) * CMP_STRIDE + (CMP_BLOCK - 1)) <= (q0 + _iota((ncp, qn), 1)),
                       0.0, NEG_INF)
    sees_cmp = (q0 + _iota((1, qn), 1)) >= CMP_BLOCK - 1
    o_cmp, p_sum = [], None
    for r in reps:
        p = _softmax_cols(_dot(kc, q_head(r)), bias_c, sees_cmp)
        o_cmp.append(_dot(vc_t, p.astype(BF16)))
        p_sum = p if p_sum is None else p_sum + p
    ovt = ovt_ref[...]
    imp_t = None
    for term in _split_bf16(p_sum, 3):
        part = _dot(ovt, term)
        imp_t = part if imp_t is None else imp_t + part

    blk = _iota((ns, qn), 0)
    cur = (q0 + _iota((ns, qn), 1)) >> SLC_SHIFT
    forced = (blk == 0) | (blk == cur) | (blk == cur - 1)
    imp_bits = jnp.where(imp_t > 0.0, lax.bitcast_convert_type(imp_t, I32), 0)
    key = jnp.where(forced, jnp.int32(2 ** 31 - 1), jnp.where(blk <= cur, imp_bits, -1))

    picked = _top_rows(key, n_sel)
    bias_scr[...] = jnp.where(picked > 0.5, 0.0, NEG_INF)

    carry = list(_online_init(NSA_REP, HEAD_DIM, qn))
    for diagonal in (True, False):
        kt = qi if diagonal else jnp.maximum(qi - WINDOW // kt_n, 0)
        k0 = pl.multiple_of(kt * kt_n, kt_n)
        kp = k0 + _iota((kt_n, qn), 0)
        tq = q0 + _iota((kt_n, qn), 1)
        inside = (kp <= tq) if diagonal else ((kp > tq - WINDOW) & (kp < q0))
        bias_w = jnp.where(inside, 0.0, NEG_INF)
        k = kw_ref[0, 0, pl.ds(k0, kt_n), :]
        scores = [_dot(k, q_head(r)) for r in reps]
        for r in reps:
            carry[r] = _online_cols(scores[r] + bias_w, vw_ref[0, kt], *carry[r])

    gates = g_ref[0]
    for r in reps:
        part_scr[r] = (gates[3 * r:3 * r + 1] * o_cmp[r]
                       + gates[3 * r + 2:3 * r + 3] * _normalized(carry[r]))

    blocks_per_tile = kt_n // SLC_BLOCK
    causal = _iota((kt_n, qn), 0) <= _iota((kt_n, qn), 1)
    def prepare(f, diagonal):
        kt = qi if diagonal else jnp.minimum(f, jnp.maximum(qi - 1, 0))
        k0 = pl.multiple_of(kt * kt_n, kt_n)
        rows = [jnp.broadcast_to(bias_scr[pl.ds(kt * blocks_per_tile + i, 1), :], (SLC_BLOCK, qn))
                for i in range(blocks_per_tile)]
        bias = jnp.concatenate(rows, axis=0)
        if diagonal:
            bias = jnp.where(causal, bias, NEG_INF)
        return k0, bias

    def produce(ctx, slot, r):
        k0, bias = ctx
        s = _dot(ks_ref[0, 0, pl.ds(k0, kt_n), :], q_head(r)) + bias
        s_scr[slot, r] = s
        mx_scr[slot, r] = jnp.max(s, axis=0, keepdims=True)

    def consume(slot, f, diagonal, r, state_r):
        return _online_stored(s_scr.at[slot, r], mx_scr[slot, r], vs_ref[0, qi if diagonal else f], *state_r)

    def finalize(state):
        for r in reps:
            o = part_scr[r] + g_ref[0, 3 * r + 1:3 * r + 2, :] * _normalized(state[r])
            o_ref[0, r * HEAD_DIM:(r + 1) * HEAD_DIM, :] = o.astype(o_ref.dtype)

    _sweep(qi, NSA_REP, prepare, produce, consume, finalize, _online_init(NSA_REP, HEAD_DIM, qn))


def _nsa_attention(q_t, k_cmp, v_cmp_t, k_slc, v_slc_t, k_win, v_win_t, gates_t):
    bsz, _, t = q_t.shape
    ncp = k_cmp.shape[2]
    ns = t // SLC_BLOCK
    qn = min(Q_TILE, t)
    grp_w = NSA_REP * HEAD_DIM
    cmp_start = jnp.arange(ncp) * CMP_STRIDE
    slc_start = jnp.arange(ns) * SLC_BLOCK
    real = jnp.arange(ncp) < (t - CMP_BLOCK) // CMP_STRIDE + 1
    overlap_t = ((cmp_start[None, :] < slc_start[:, None] + SLC_BLOCK)
                 & (cmp_start[None, :] + CMP_BLOCK > slc_start[:, None]) & real[None, :]).astype(BF16)
    k_spec = lambda n: pl.BlockSpec((1, 1, n, HEAD_DIM), lambda b, g, i: (b, g, 0, 0))
    v_spec = pl.BlockSpec((1, t // KEY_TILE, HEAD_DIM, KEY_TILE), lambda b, g, i: (b, 0, g, 0))
    return pl.pallas_call(
        functools.partial(_nsa_kernel, n_sel=min(N_SELECT, ns)),
        grid=(bsz, NSA_KV_GROUPS, t // qn),
        in_specs=[pl.BlockSpec((1, grp_w, qn), lambda b, g, i: (b, g, i)),
                  k_spec(ncp),
                  pl.BlockSpec((1, HEAD_DIM, ncp), lambda b, g, i: (b, g, 0)),
                  k_spec(t), v_spec, k_spec(t), v_spec,
                  pl.BlockSpec((1, GATE_ROWS, qn), lambda b, g, i: (b, g, i)),
                  pl.BlockSpec((ns, ncp), lambda b, g, i: (0, 0))],
        out_specs=pl.BlockSpec((1, grp_w, qn), lambda b, g, i: (b, g, i)),
        out_shape=jax.ShapeDtypeStruct((bsz, NSA_HEADS * HEAD_DIM, t), BF16),
        scratch_shapes=[pltpu.VMEM((ns, qn), F32), pltpu.VMEM((NSA_REP, HEAD_DIM, qn), F32)] + _score_buffers(NSA_REP, qn),
        compiler_params=_params("arbitrary", "arbitrary", "arbitrary"),
        name="nsa_attention",
    )(q_t, k_cmp, v_cmp_t, k_slc, v_slc_t, k_win, v_win_t, gates_t, overlap_t)


SB_HEADS_PER_STEP = 2
SB_DEAD_TAIL = -110.0
SB_SCAN_BLOCK = 128


def _sb_kernel(q_ref, k_ref, v_ref, o_ref, nz_scr):
    qn = q_ref.shape[2]
    kt_n = KEY_TILE
    qi = pl.program_id(2)
    heads = range(SB_HEADS_PER_STEP)
    rows = [slice(h * HEAD_DIM, (h + 1) * HEAD_DIM) for h in heads]
    sub = SB_SCAN_BLOCK
    n_sub = kt_n // sub
    tri = jnp.where(_iota((sub, sub), 1) > _iota((sub, sub), 0), 1.0, 0.0).astype(BF16)
    later2 = jnp.concatenate([tri, tri], axis=1)
    strictly_before = _iota((kt_n, qn), 0) < _iota((kt_n, qn), 1)

    def tile_index(f, diagonal):
        return qi if diagonal else jnp.maximum(qi - 1 - f, 0)

    def prepare(f, diagonal):
        return pl.multiple_of(tile_index(f, diagonal) * kt_n, kt_n)

    def produce(k0, slot, h):
        nz_scr[slot, h] = _dot(k_ref[0, h, pl.ds(k0, kt_n), :], -q_ref[0, rows[h], :])

    def consume(slot, f, diagonal, h, state_h):
        tail, acc = state_h
        nz = nz_scr[slot, h]
        neg_abs = lax.bitcast_convert_type(lax.bitcast_convert_type(nz, jnp.uint32) | jnp.uint32(0x80000000), F32)
        soft = jnp.log(1.0 + jnp.exp(neg_abs))
        log_keep = jnp.minimum(nz, 0.0) - soft
        log_beta = log_keep - nz
        if diagonal:
            log_keep = jnp.where(strictly_before, log_keep, 0.0)
        offs = tail
        parts = [None] * n_sub
        for blk in reversed(range(n_sub)):
            lk = log_keep[blk * sub:(blk + 1) * sub]
            hi = lk.astype(BF16)
            lo = (lk - hi.astype(F32)).astype(BF16)
            within = _dot(later2, jnp.concatenate([hi, lo], axis=0))
            parts[blk] = log_beta[blk * sub:(blk + 1) * sub] + within + offs
            offs = offs + jnp.sum(lk, axis=0, keepdims=True)
        a = jnp.exp(jnp.concatenate(parts, axis=0))
        if diagonal:
            a = jnp.where(strictly_before, a, 0.0)
        return offs, acc + _dot(v_ref[0, tile_index(f, diagonal), rows[h], :], a.astype(BF16))

    def finalize(state):
        for h in heads:
            o_ref[0, rows[h], :] = state[h][1].astype(o_ref.dtype)

    def alive(state):
        tails = [state[h][0] for h in heads]
        return jnp.max(functools.reduce(jnp.maximum, tails)) > SB_DEAD_TAIL

    init = tuple((jnp.zeros((1, qn), F32), jnp.zeros((HEAD_DIM, qn), F32)) for _ in heads)
    _sweep(qi, SB_HEADS_PER_STEP, prepare, produce, consume, finalize, init, alive)


def _sb_attention(q_t, k, v_t):
    bsz, nh, t, _ = k.shape
    qn = min(Q_TILE, t)
    hps = SB_HEADS_PER_STEP
    w = hps * HEAD_DIM
    return pl.pallas_call(
        _sb_kernel,
        grid=(bsz, nh // hps, t // qn),
        in_specs=[pl.BlockSpec((1, w, qn), lambda b, h, i: (b, h, i)),
                  pl.BlockSpec((1, hps, t, HEAD_DIM), lambda b, h, i: (b, h, 0, 0)),
                  pl.BlockSpec((1, t // KEY_TILE, w, KEY_TILE), lambda b, h, i: (b, 0, h, 0))],
        out_specs=pl.BlockSpec((1, w, qn), lambda b, h, i: (b, h, i)),
        out_shape=jax.ShapeDtypeStruct((bsz, nh * HEAD_DIM, t), BF16),
        scratch_shapes=[pltpu.VMEM((2, hps, KEY_TILE, qn), F32)],
        compiler_params=_params("arbitrary", "arbitrary", "arbitrary"),
        name="stickbreak_attention",
    )(q_t, k, v_t)


DIFF_HEADS_PER_STEP = 2


def _diff_kernel(q_ref, k_ref, v_ref, lq1_ref, lk1_ref, lq2_ref, lk2_ref, sub_ref, o_ref, s_scr, mx_scr,
                 *, lambda_init):
    qn = q_ref.shape[2]
    vd = 2 * HEAD_DIM
    kt_n = KEY_TILE
    qi = pl.program_id(2)
    chains = range(2 * DIFF_HEADS_PER_STEP)
    lam = (jnp.exp(jnp.sum(lq1_ref[...] * lk1_ref[...], axis=1, keepdims=True))
           - jnp.exp(jnp.sum(lq2_ref[...] * lk2_ref[...], axis=1, keepdims=True)) + lambda_init)

    causal = _iota((kt_n, qn), 0) <= _iota((kt_n, qn), 1)

    def prepare(f, diagonal):
        kt = qi if diagonal else jnp.minimum(f, jnp.maximum(qi - 1, 0))
        return pl.multiple_of(kt * kt_n, kt_n), diagonal

    def produce(ctx, slot, c):
        k0, diagonal = ctx
        s = _dot(k_ref[0, c, pl.ds(k0, kt_n), :], q_ref[0, c * HEAD_DIM:(c + 1) * HEAD_DIM, :])
        if diagonal:
            s = jnp.where(causal, s, NEG_INF)
        s_scr[slot, c] = s
        mx_scr[slot, c] = jnp.max(s, axis=0, keepdims=True)

    def consume(slot, f, diagonal, c, state_c):
        kt = qi if diagonal else f
        return _online_stored(s_scr.at[slot, c], mx_scr[slot, c],
                              v_ref[0, kt, (c // 2) * vd:(c // 2 + 1) * vd, :], *state_c)

    def finalize(state):
        for h in range(DIFF_HEADS_PER_STEP):
            o = _normalized(state[2 * h]) - lam * _normalized(state[2 * h + 1])
            ms = jnp.mean(o * o, axis=0, keepdims=True)
            o = o * lax.rsqrt(ms + RMS_EPS) * sub_ref[...] * (1.0 - lambda_init)
            o_ref[0, h * vd:(h + 1) * vd, :] = o.astype(o_ref.dtype)

    _sweep(qi, len(chains), prepare, produce, consume, finalize, _online_init(len(chains), vd, qn))


def _diff_attention(q_t, k, v_t, lq1, lk1, lq2, lk2, subln, lambda_init):
    bsz, nh2, t, _ = k.shape
    nh = nh2 // 2
    vd = 2 * HEAD_DIM
    hps = DIFF_HEADS_PER_STEP
    qn = min(Q_TILE, t)
    row = lambda a: a.reshape(1, -1)
    small = lambda n: pl.BlockSpec((1, n), lambda b, h, i: (0, 0))
    return pl.pallas_call(
        functools.partial(_diff_kernel, lambda_init=lambda_init),
        grid=(bsz, nh // hps, t // qn),
        in_specs=[pl.BlockSpec((1, hps * vd, qn), lambda b, h, i: (b, h, i)),
                  pl.BlockSpec((1, 2 * hps, t, HEAD_DIM), lambda b, h, i: (b, h, 0, 0)),
                  pl.BlockSpec((1, t // KEY_TILE, hps * vd, KEY_TILE), lambda b, h, i: (b, 0, h, 0)),
                  small(HEAD_DIM), small(HEAD_DIM), small(HEAD_DIM), small(HEAD_DIM),
                  pl.BlockSpec((vd, 1), lambda b, h, i: (0, 0))],
        out_specs=pl.BlockSpec((1, hps * vd, qn), lambda b, h, i: (b, h, i)),
        out_shape=jax.ShapeDtypeStruct((bsz, nh * vd, t), BF16),
        scratch_shapes=_score_buffers(2 * hps, qn),
        compiler_params=_params("arbitrary", "arbitrary", "arbitrary"),
        name="diff_attention",
    )(q_t, k, v_t, row(lq1), row(lk1), row(lq2), row(lk2), subln.reshape(vd, 1))


def _outproj_kernel(*refs, n_in):
    a_refs, w_refs = refs[:n_in], refs[n_in:2 * n_in]
    x_ref, gate_ref, o_ref = refs[2 * n_in:]
    y = None
    for a_ref, w_ref in zip(a_refs, w_refs):
        part = lax.dot_general(a_ref[0], w_ref[...].astype(BF16), _TN, preferred_element_type=F32)
        y = part if y is None else y + part
    o_ref[0] = x_ref[0] + gate_ref[0] * y


def _out_project(acts_t, weight, x, gate, name):
    bsz, t, d = x.shape
    rows = min(t, PROJ_ROWS)
    k_rows = acts_t[0].shape[1]
    tile = pl.BlockSpec((1, rows, d), lambda b, i: (b, i, 0))
    return pl.pallas_call(
        functools.partial(_outproj_kernel, n_in=len(acts_t)),
        grid=(bsz, t // rows),
        in_specs=([pl.BlockSpec((1, a.shape[1], rows), lambda b, i: (b, 0, i)) for a in acts_t]
                  + [pl.BlockSpec((k_rows, d), lambda b, i, n=n: (n, 0)) for n in range(len(acts_t))]
                  + [tile, pl.BlockSpec((1, 1, d), lambda b, i: (b, 0, 0))]),
        out_specs=tile,
        out_shape=jax.ShapeDtypeStruct((bsz, t, d), F32),
        compiler_params=_params("arbitrary", "arbitrary"),
        name=name,
    )(*acts_t, *([weight] * len(acts_t)), x, gate)


def _ffn_kernel(x_ref, halo_ref, g_ref, sh_ref, sc_ref, gate_ref, wg_ref, wu_ref, cw_ref, cb_ref, wd_ref,
                nf_ref, o_ref, a_scr, *, final_norm):
    ti = pl.program_id(1)
    h = _norm_modulate(x_ref[0], g_ref[...], sh_ref[0], sc_ref[0]).astype(BF16)
    h_prev = _norm_modulate(halo_ref[0], g_ref[...], sh_ref[0], sc_ref[0]).astype(BF16)
    h_both = jnp.concatenate([h_prev, h], axis=0)
    seen = jnp.where(ti > 0, 1.0, 0.0)
    cols = FFN_COLS
    row = _iota((h.shape[0], cols), 0)
    for c in range(a_scr.shape[1] // cols):
        sl = slice(c * cols, (c + 1) * cols)
        g_both = _dot(h_both, wg_ref[0, :, sl].astype(BF16))
        g = g_both[HALO_ROWS:]
        g_prev = g_both[:HALO_ROWS] * seen
        last = g_prev[HALO_ROWS - 1:HALO_ROWS]
        last2 = g_prev[HALO_ROWS - 2:HALO_ROWS - 1]
        g1 = jnp.where(row == 0, last, pltpu.roll(g, 1, 0))
        g2 = jnp.where(row == 0, last2, jnp.where(row == 1, last, pltpu.roll(g, 2, 0)))
        cw = cw_ref[:, sl]
        conv = cw[0:1] * g2 + cw[1:2] * g1 + cw[2:3] * g + cb_ref[:, sl]
        a_scr[:, sl] = (conv * _sigmoid(conv) * _dot(h, wu_ref[0, :, sl].astype(BF16))).astype(BF16)
    y = x_ref[0] + gate_ref[0] * _dot(a_scr[...], wd_ref[...])
    if final_norm:
        ms = jnp.mean(y * y, axis=-1, keepdims=True)
        y = y * lax.rsqrt(ms + RMS_EPS) * nf_ref[...]
    o_ref[0] = y


def _conv_ffn(x, g, shift, scale, gate, layer, w_gate_all, w_up_all, conv_w, conv_b, w_down, norm_f, final_norm,
              name):
    bsz, t, d = x.shape
    f = w_gate_all.shape[2]
    rows = min(t, FFN_ROWS)
    halo_blocks = rows // HALO_ROWS
    vec = pl.BlockSpec((1, 1, d), lambda b, i: (b, 0, 0))
    drow = pl.BlockSpec((1, d), lambda b, i: (0, 0))
    resident = lambda shape: pl.BlockSpec(shape, lambda b, i: (0, 0), pipeline_mode=pl.Buffered(1))
    stacked = pl.BlockSpec((1, d, f), lambda b, i: (layer, 0, 0), pipeline_mode=pl.Buffered(1))
    return pl.pallas_call(
        functools.partial(_ffn_kernel, final_norm=final_norm),
        grid=(bsz, t // rows),
        in_specs=[pl.BlockSpec((1, rows, d), lambda b, i: (b, i, 0)),
                  pl.BlockSpec((1, HALO_ROWS, d), lambda b, i: (b, jnp.maximum(i * halo_blocks - 1, 0), 0)),
                  drow, vec, vec, vec,
                  stacked, stacked, resident((CONV_WIDTH, f)), resident((1, f)),
                  resident((f, d)), drow],
        out_specs=pl.BlockSpec((1, rows, d), lambda b, i: (b, i, 0)),
        out_shape=jax.ShapeDtypeStruct((bsz, t, d), F32),
        scratch_shapes=[pltpu.VMEM((rows, f), BF16)],
        compiler_params=_params("arbitrary", "arbitrary"),
        name=name,
    )(x, x, g.reshape(1, d), shift, scale, gate, w_gate_all, w_up_all, conv_w,
      conv_b.reshape(1, f), w_down.astype(BF16), norm_f.reshape(1, d))


def _layout(widths):
    starts, start = [], 0
    for w in widths:
        starts.append(start)
        start += w
    return starts


def _hybrid_layer(x, mods, tables, norm_g, w_in, pos_k, pos_v, ck_w1, ck_w2, cv_w1, cv_w2, w_out):
    shift, scale, gate = mods
    bsz, t, d = x.shape
    qw, kvw, sbw = NSA_HEADS * HEAD_DIM, NSA_KV_GROUPS * HEAD_DIM, SB_HEADS * HEAD_DIM
    col = _layout((qw,) + (kvw,) * 6 + (3 * NSA_HEADS,) + (sbw,) * 3)
    cols = lambda i, w: w_in[:, col[i]:col[i] + w]
    q_n, kc, vc, ks, vs, kw, vw, gl, q_s, k_s, v_s = (cols(i, w) for i, w in enumerate(
        (qw,) + (kvw,) * 6 + (3 * NSA_HEADS,) + (sbw,) * 3))
    inv_sqrt = HEAD_DIM ** -0.5

    w_tok = jnp.concatenate([ks, kw, kc, vc, k_s], axis=1).astype(BF16)
    ts = _layout((kvw, kvw, kvw, kvw, sbw))
    tok_segs = [TokSeg(ts[0], kvw, True, HEAD_DIM, BF16),
                TokSeg(ts[1], kvw, True, HEAD_DIM, BF16),
                TokSeg(ts[2], kvw, False, 0, F32),
                TokSeg(ts[3], kvw, False, 0, F32),
                TokSeg(ts[4], sbw, False, HEAD_DIM, BF16)]
    per_group = 3 * NSA_REP
    gate_pad = jnp.zeros((d, GATE_ROWS - per_group), w_in.dtype)
    gl_pad = jnp.concatenate([gl[:, :per_group], gate_pad, gl[:, per_group:], gate_pad], axis=1)
    w_feat_t = jnp.concatenate([q_n, q_s, vs, vw, v_s, gl_pad], axis=1).T.astype(BF16)
    fs = _layout((qw, sbw, kvw, kvw, sbw, NSA_KV_GROUPS * GATE_ROWS))
    feat_segs = [FeatSeg(fs[0], qw, True, inv_sqrt * LOG2E, False, False, BF16),
                 FeatSeg(fs[1], sbw, False, inv_sqrt, False, False, BF16),
                 FeatSeg(fs[2], kvw, False, 1.0, False, True, BF16),
                 FeatSeg(fs[3], kvw, False, 1.0, False, True, BF16),
                 FeatSeg(fs[4], sbw, False, 1.0, False, True, BF16),
                 FeatSeg(fs[5], NSA_KV_GROUPS * GATE_ROWS, False, 1.0, True, False, F32)]
    (k_slc, k_win, kc_a, vc_a, k_sb, q_n_t, q_s_t, v_slc_t, v_win_t, v_sb_t, gates_t) = _project(
        x, norm_g, shift, scale, w_tok, w_feat_t, tables, tok_segs, feat_segs, "hybrid_in_proj")

    ncp = t // CMP_STRIDE
    end_rows = jnp.minimum(jnp.arange(ncp) * CMP_STRIDE + CMP_BLOCK - 1, t - 1)
    cos_c, sin_c = tables[0][:, end_rows], tables[1][:, end_rows]
    k_cmp = _compress(kc_a, pos_k, ck_w1, ck_w2, cos_c, sin_c, True, "compress_k")
    v_cmp_t = _compress(vc_a, pos_v, cv_w1, cv_w2, cos_c, sin_c, False, "compress_v")
    o_nsa_t = _nsa_attention(q_n_t, k_cmp, v_cmp_t, k_slc, v_slc_t, k_win, v_win_t, gates_t)
    o_sb_t = _sb_attention(q_s_t, k_sb, v_sb_t)
    return _out_project([o_nsa_t, o_sb_t], w_out, x, gate, "hybrid_out_proj")


def _diff_layer(x, mods, tables, norm_g, w_qkv, lq1, lk1, lq2, lk2, subln, w_out, layer_idx):
    shift, scale, gate = mods
    dw = 2 * DIFF_HEADS * HEAD_DIM
    inv_sqrt = HEAD_DIM ** -0.5
    w_tok = w_qkv[:, dw:2 * dw].astype(BF16)
    w_feat_t = jnp.concatenate([w_qkv[:, :dw], w_qkv[:, 2 * dw:]], axis=1).T.astype(BF16)
    tok_segs = [TokSeg(0, dw, True, HEAD_DIM, BF16)]
    feat_segs = [FeatSeg(0, dw, True, inv_sqrt * LOG2E, False, False, BF16),
                 FeatSeg(dw, dw, False, 1.0, False, True, BF16)]
    k, q_t, v_t = _project(x, norm_g, shift, scale, w_tok, w_feat_t, tables, tok_segs, feat_segs, "diff_in_proj")
    lambda_init = 0.8 - 0.6 * math.exp(-0.3 * layer_idx)
    o_t = _diff_attention(q_t, k, v_t, lq1, lk1, lq2, lk2, subln, lambda_init)
    return _out_project([o_t], w_out, x, gate, "diff_out_proj")


def kernel(x, c, positions, mod_w, mod_b, norm_mix, norm_ffn, ffn_w_gate, ffn_w_up, ffn_conv_w, ffn_conv_b, ffn_w_down, hyb_w_in, nsa_pos_k, nsa_pos_v, nsa_ck_w1, nsa_ck_w2, nsa_cv_w1, nsa_cv_w2, hyb_w_out, diff_w_qkv, diff_lq1, diff_lk1, diff_lq2, diff_lk2, diff_subln, diff_w_out, norm_f):
    bsz, t, d = x.shape
    depth = mod_w.shape[0]
    mod = _adaln_mod(c, mod_w, mod_b)
    tables = _rope_tables(positions)
    for i in range(depth):
        sh_m, sc_m, g_m, sh_f, sc_f, g_f = (mod[i, :, k * d:(k + 1) * d].reshape(bsz, 1, d) for k in range(6))
        j = i // 2
        if i % 2 == 0:
            x = _hybrid_layer(x, (sh_m, sc_m, g_m), tables, norm_mix[i], hyb_w_in[j],
                              nsa_pos_k[j], nsa_pos_v[j], nsa_ck_w1[j], nsa_ck_w2[j], nsa_cv_w1[j],
                              nsa_cv_w2[j], hyb_w_out[j])
        else:
            x = _diff_layer(x, (sh_m, sc_m, g_m), tables, norm_mix[i], diff_w_qkv[j], diff_lq1[j],
                            diff_lk1[j], diff_lq2[j], diff_lk2[j], diff_subln[j], diff_w_out[j], i)
        x = _conv_ffn(x, norm_ffn[i], sh_f, sc_f, g_f, i, ffn_w_gate, ffn_w_up, ffn_conv_w[i],
                      ffn_conv_b[i], ffn_w_down[i], norm_f, i == depth - 1, "conv_ffn_%d" % i)
    return x
```

```python
import functools
import math
from typing import NamedTuple

import jax
import jax.numpy as jnp
from jax import lax
from jax.experimental import pallas as pl
from jax.experimental.pallas import tpu as pltpu

F32 = jnp.float32
BF16 = jnp.bfloat16
I32 = jnp.int32

HEAD_DIM = 64
ROPE_DIM = HEAD_DIM // 4
ROPE_HALF = ROPE_DIM // 2
ROPE_THETA = 500000.0
NSA_HEADS = 8
NSA_KV_GROUPS = 2
NSA_REP = NSA_HEADS // NSA_KV_GROUPS
CMP_BLOCK = 32
CMP_STRIDE = 16
CMP_HIDDEN = 4 * HEAD_DIM
SLC_BLOCK = 64
SLC_SHIFT = 6
N_SELECT = 16
WINDOW = 512
SB_HEADS = 8
DIFF_HEADS = 8
CONV_WIDTH = 3
RMS_EPS = 1e-6
NEG_INF = -1e30
LOG2E = math.log2(math.e)

LANES = 128
Q_TILE = 512
KEY_TILE = 512
GATE_ROWS = 16
PROJ_ROWS = 1024
FFN_ROWS = 512
FFN_COLS = 256
HALO_ROWS = 16
VMEM_LIMIT = 56 * 1024 * 1024

_NT = (((1,), (1,)), ((), ()))
_TN = (((0,), (0,)), ((), ()))


def _params(*sem):
    return pltpu.CompilerParams(dimension_semantics=sem, vmem_limit_bytes=VMEM_LIMIT)


def _sigmoid(v):
    return 1.0 / (1.0 + jnp.exp(-v))


def _iota(shape, axis):
    return lax.broadcasted_iota(I32, shape, axis)


def _dot(a, b):
    return jnp.dot(a, b, preferred_element_type=F32)


def _mod_kernel(c_ref, w_ref, b_ref, o_ref):
    c = c_ref[...]
    cond = c * _sigmoid(c)
    o_ref[0] = jnp.dot(cond, w_ref[0], preferred_element_type=F32,
                       precision=lax.Precision.HIGHEST) + b_ref[0]


def _adaln_mod(c, mod_w, mod_b):
    depth, d, n = mod_w.shape
    bsz = c.shape[0]
    tn = n // 4
    return pl.pallas_call(
        _mod_kernel,
        grid=(depth, n // tn),
        in_specs=[pl.BlockSpec((bsz, d), lambda i, j: (0, 0)),
                  pl.BlockSpec((1, d, tn), lambda i, j: (i, 0, j)),
                  pl.BlockSpec((1, 1, tn), lambda i, j: (i, 0, j))],
        out_specs=pl.BlockSpec((1, bsz, tn), lambda i, j: (i, 0, j)),
        out_shape=jax.ShapeDtypeStruct((depth, bsz, n), F32),
        compiler_params=_params("arbitrary", "arbitrary"),
        name="adaln_mod",
    )(c, mod_w, mod_b.reshape(depth, 1, n))


def _rope_kernel(pos_col_ref, pos_row_ref, inv_row_ref, sgn_row_ref, inv_col_ref,
                 cos_ref, sin_ref, cos_t_ref, sin_t_ref):
    ang = pos_col_ref[0].astype(F32) * inv_row_ref[...]
    cos_ref[0] = jnp.cos(ang)
    sin_ref[0] = jnp.sin(ang) * sgn_row_ref[...]
    ang_t = inv_col_ref[...] * pos_row_ref[0].astype(F32)
    cos_t_ref[0] = jnp.cos(ang_t)
    sin_t_ref[0] = jnp.sin(ang_t)


def _rope_tables(positions):
    bsz, t = positions.shape
    inv = ROPE_THETA ** (-jnp.arange(0, ROPE_DIM, 2, dtype=F32) / ROPE_DIM)
    per_head_inv = jnp.concatenate([inv, inv, jnp.zeros((HEAD_DIM - ROPE_DIM,), F32)])
    per_head_sgn = jnp.concatenate([-jnp.ones((ROPE_HALF,), F32), jnp.ones((ROPE_HALF,), F32),
                                    jnp.zeros((HEAD_DIM - ROPE_DIM,), F32)])
    inv_row = jnp.tile(per_head_inv, LANES // HEAD_DIM)[None, :]
    sgn_row = jnp.tile(per_head_sgn, LANES // HEAD_DIM)[None, :]
    rows = min(t, 1024)
    tab = jax.ShapeDtypeStruct((bsz, t, LANES), F32)
    tab_t = jax.ShapeDtypeStruct((bsz, ROPE_HALF, t), F32)
    row_spec = pl.BlockSpec((1, LANES), lambda b, i: (0, 0))
    return pl.pallas_call(
        _rope_kernel,
        grid=(bsz, t // rows),
        in_specs=[pl.BlockSpec((1, rows, 1), lambda b, i: (b, i, 0)),
                  pl.BlockSpec((1, 1, rows), lambda b, i: (b, 0, i)),
                  row_spec, row_spec,
                  pl.BlockSpec((ROPE_HALF, 1), lambda b, i: (0, 0))],
        out_specs=[pl.BlockSpec((1, rows, LANES), lambda b, i: (b, i, 0))] * 2
        + [pl.BlockSpec((1, ROPE_HALF, rows), lambda b, i: (b, 0, i))] * 2,
        out_shape=[tab, tab, tab_t, tab_t],
        compiler_params=_params("arbitrary", "arbitrary"),
        name="rope_tables",
    )(positions.reshape(bsz, t, 1), positions.reshape(bsz, 1, t), inv_row, sgn_row, inv[:, None])


def _rope_chunk(y, cosv, sinv, first_half):
    ahead = pltpu.roll(y, LANES - ROPE_HALF, 1)
    behind = pltpu.roll(y, ROPE_HALF, 1)
    return y * cosv + jnp.where(first_half, ahead, behind) * sinv


def _rope_rows(y_t, cos_t, sin_t):
    heads = []
    for h in range(y_t.shape[0] // HEAD_DIM):
        blk = y_t[h * HEAD_DIM:(h + 1) * HEAD_DIM]
        x1, x2 = blk[:ROPE_HALF], blk[ROPE_HALF:ROPE_DIM]
        heads += [x1 * cos_t - x2 * sin_t, x2 * cos_t + x1 * sin_t, blk[ROPE_DIM:]]
    return jnp.concatenate(heads, axis=0)


class TokSeg(NamedTuple):
    start: int
    width: int
    rope: bool
    head_width: int
    dtype: object


class FeatSeg(NamedTuple):
    start: int
    rows: int
    rope: bool
    scale: float
    sigmoid: bool
    key_tiled: bool
    dtype: object


def _norm_modulate(x, g, shift, scale):
    ms = jnp.mean(x * x, axis=-1, keepdims=True)
    y = x * lax.rsqrt(ms + RMS_EPS) * g
    return y * (1.0 + scale) + shift


def _proj_kernel(x_ref, g_ref, sh_ref, sc_ref, w_ref, wt_ref, cos_ref, sin_ref, cos_t_ref, sin_t_ref,
                 *out_refs, tok_segs, feat_segs):
    hb = _norm_modulate(x_ref[0], g_ref[...], sh_ref[0], sc_ref[0]).astype(BF16)
    cosv, sinv = cos_ref[0], sin_ref[0]
    first_half = (_iota(cosv.shape, 1) & (HEAD_DIM - 1)) < ROPE_HALF
    tok_refs, feat_refs = out_refs[:len(tok_segs)], out_refs[len(tok_segs):]
    for seg, o_ref in zip(tok_segs, tok_refs):
        y = _dot(hb, w_ref[:, seg.start:seg.start + seg.width])
        for ch in range(seg.width // LANES):
            yc = y[:, ch * LANES:(ch + 1) * LANES]
            if seg.rope:
                yc = _rope_chunk(yc, cosv, sinv, first_half)
            yc = yc.astype(seg.dtype)
            if seg.head_width == 0:
                o_ref[0, :, ch * LANES:(ch + 1) * LANES] = yc
            else:
                o_ref[0, 2 * ch] = yc[:, :HEAD_DIM]
                o_ref[0, 2 * ch + 1] = yc[:, HEAD_DIM:]
    y_t_all = lax.dot_general(wt_ref[...], hb, _NT, preferred_element_type=F32)
    for seg, o_ref in zip(feat_segs, feat_refs):
        y_t = y_t_all[seg.start:seg.start + seg.rows]
        if seg.rope:
            y_t = _rope_rows(y_t, cos_t_ref[0], sin_t_ref[0])
        if seg.scale != 1.0:
            y_t = y_t * seg.scale
        if seg.sigmoid:
            y_t = _sigmoid(y_t)
        y_t = y_t.astype(seg.dtype)
        if seg.key_tiled:
            for ch in range(y_t.shape[1] // KEY_TILE):
                o_ref[0, ch] = y_t[:, ch * KEY_TILE:(ch + 1) * KEY_TILE]
        else:
            o_ref[0] = y_t


def _project(x, g, shift, scale, w_tok, w_feat_t, tables, tok_segs, feat_segs, name):
    bsz, t, d = x.shape
    rows = min(t, PROJ_ROWS)
    cos_tab, sin_tab, cos_t, sin_t = tables
    out_shapes, out_specs = [], []
    for seg in tok_segs:
        if seg.head_width == 0:
            out_shapes.append(jax.ShapeDtypeStruct((bsz, t, seg.width), seg.dtype))
            out_specs.append(pl.BlockSpec((1, rows, seg.width), lambda b, i: (b, i, 0)))
        else:
            nh = seg.width // seg.head_width
            out_shapes.append(jax.ShapeDtypeStruct((bsz, nh, t, seg.head_width), seg.dtype))
            out_specs.append(pl.BlockSpec((1, nh, rows, seg.head_width), lambda b, i: (b, 0, i, 0)))
    for seg in feat_segs:
        if seg.key_tiled:
            out_shapes.append(jax.ShapeDtypeStruct((bsz, t // KEY_TILE, seg.rows, KEY_TILE), seg.dtype))
            out_specs.append(pl.BlockSpec((1, rows // KEY_TILE, seg.rows, KEY_TILE), lambda b, i: (b, i, 0, 0)))
        else:
            out_shapes.append(jax.ShapeDtypeStruct((bsz, seg.rows, t), seg.dtype))
            out_specs.append(pl.BlockSpec((1, seg.rows, rows), lambda b, i: (b, 0, i)))
    vec = pl.BlockSpec((1, 1, d), lambda b, i: (b, 0, 0))
    tab = pl.BlockSpec((1, rows, LANES), lambda b, i: (b, i, 0))
    tab_t = pl.BlockSpec((1, ROPE_HALF, rows), lambda b, i: (b, 0, i))
    return pl.pallas_call(
        functools.partial(_proj_kernel, tok_segs=tuple(tok_segs), feat_segs=tuple(feat_segs)),
        grid=(bsz, t // rows),
        in_specs=[pl.BlockSpec((1, rows, d), lambda b, i: (b, i, 0)),
                  pl.BlockSpec((1, d), lambda b, i: (0, 0)),
                  vec, vec,
                  pl.BlockSpec(w_tok.shape, lambda b, i: (0, 0)),
                  pl.BlockSpec(w_feat_t.shape, lambda b, i: (0, 0)),
                  tab, tab, tab_t, tab_t],
        out_specs=out_specs,
        out_shape=out_shapes,
        compiler_params=_params("arbitrary", "arbitrary"),
        name=name,
    )(x, g.reshape(1, d), shift, scale, w_tok, w_feat_t, cos_tab, sin_tab, cos_t, sin_t)


def _compress_kernel(r_ref, pa_ref, pb_ref, wa_ref, wb_ref, w2_ref, cos_ref, sin_ref, o_ref, *, is_key):
    r = r_ref[0]
    ncp = r.shape[0]
    a = _dot((r + pa_ref[...]).astype(BF16), wa_ref[...])
    b = _dot((r + pb_ref[...]).astype(BF16), wb_ref[...])
    hid = a + pltpu.roll(b, ncp - 1, 0)
    hid = (hid * _sigmoid(hid)).astype(BF16)
    if is_key:
        y = _dot(hid, w2_ref[...])
        first_half = (_iota(y.shape, 1) & (HEAD_DIM - 1)) < ROPE_HALF
        y = _rope_chunk(y, cos_ref[0], sin_ref[0], first_half).astype(o_ref.dtype)
        for g in range(NSA_KV_GROUPS):
            o_ref[0, g] = y[:, g * HEAD_DIM:(g + 1) * HEAD_DIM]
    else:
        o_ref[0] = lax.dot_general(w2_ref[...], hid, _NT, preferred_element_type=F32).astype(o_ref.dtype)


def _compress(kv, pos_emb, w1, w2, cos_c, sin_c, is_key, name):
    bsz, t, _ = kv.shape
    ncp = t // CMP_STRIDE
    kwid = CMP_STRIDE * NSA_KV_GROUPS * HEAD_DIM
    hid_w = NSA_KV_GROUPS * CMP_HIDDEN
    r = kv.reshape(bsz, ncp, kwid)
    per = CMP_BLOCK // CMP_STRIDE
    w1r = w1.reshape(per, CMP_STRIDE, HEAD_DIM, CMP_HIDDEN)
    zeros = jnp.zeros_like(w1r)
    grp0 = jnp.concatenate([w1r, zeros], axis=-1)
    grp1 = jnp.concatenate([zeros, w1r], axis=-1)
    wbig = jnp.stack([grp0, grp1], axis=2).reshape(per, kwid, hid_w).astype(BF16)
    posr = pos_emb.reshape(per, CMP_STRIDE, 1, HEAD_DIM)
    posbig = jnp.broadcast_to(posr, (per, CMP_STRIDE, NSA_KV_GROUPS, HEAD_DIM)).reshape(per, 1, kwid)
    z2 = jnp.zeros_like(w2)
    w2big = jnp.concatenate([jnp.concatenate([w2, z2], axis=1),
                             jnp.concatenate([z2, w2], axis=1)], axis=0).astype(BF16)
    const = lambda shape: pl.BlockSpec(shape, lambda b: (0,) * len(shape))
    tab = pl.BlockSpec((1, ncp, LANES), lambda b: (b, 0, 0))
    if is_key:
        w2_arg = w2big
        out_spec = pl.BlockSpec((1, NSA_KV_GROUPS, ncp, HEAD_DIM), lambda b: (b, 0, 0, 0))
        out_shape = jax.ShapeDtypeStruct((bsz, NSA_KV_GROUPS, ncp, HEAD_DIM), BF16)
    else:
        w2_arg = w2big.T
        out_spec = pl.BlockSpec((1, NSA_KV_GROUPS * HEAD_DIM, ncp), lambda b: (b, 0, 0))
        out_shape = jax.ShapeDtypeStruct((bsz, NSA_KV_GROUPS * HEAD_DIM, ncp), BF16)
    return pl.pallas_call(
        functools.partial(_compress_kernel, is_key=is_key),
        grid=(bsz,),
        in_specs=[pl.BlockSpec((1, ncp, kwid), lambda b: (b, 0, 0)),
                  const((1, kwid)), const((1, kwid)),
                  const((kwid, hid_w)), const((kwid, hid_w)),
                  const(w2_arg.shape), tab, tab],
        out_specs=out_spec,
        out_shape=out_shape,
        compiler_params=_params("arbitrary"),
        name=name,
    )(r, posbig[0], posbig[1], wbig[0], wbig[1], w2_arg, cos_c, sin_c)


def _softmax_cols(s, bias, any_visible):
    sb = s + bias
    e = jnp.exp2(sb - jnp.max(sb, axis=0, keepdims=True))
    l = jnp.sum(e, axis=0, keepdims=True)
    return e * jnp.where(any_visible, 1.0 / l, 0.0)


def _online_step(s, tile_max, v_t, m_old, l_old, acc_old):
    m_new = jnp.maximum(m_old, tile_max)
    alpha = jnp.exp2(m_old - m_new)
    p = jnp.exp2(s - m_new)
    l_new = alpha * l_old + jnp.sum(p, axis=0, keepdims=True)
    acc_new = alpha * acc_old + _dot(v_t, p.astype(BF16))
    return m_new, l_new, acc_new


def _online_cols(s, v_t, m_old, l_old, acc_old):
    return _online_step(s, jnp.max(s, axis=0, keepdims=True), v_t, m_old, l_old, acc_old)


def _online_stored(s_ref, tile_max, v_t, m_old, l_old, acc_old):
    return _online_step(s_ref[...], tile_max, v_t, m_old, l_old, acc_old)


def _normalized(state):
    _, l, acc = state
    return acc * (1.0 / l)


def _sweep(n_earlier, n_chains, prepare, produce, consume, finalize, state, alive=None):
    chains = range(n_chains)

    def produce_all(slot, f, diagonal):
        ctx = prepare(f, diagonal)
        for c in chains:
            produce(ctx, slot, c)

    def consume_all(slot, f, diagonal, st):
        return tuple(consume(slot, f, diagonal, c, st[c]) for c in chains)

    def overlapped(p_slot, p_f, c_slot, c_f, c_diagonal, st):
        ctx = prepare(p_f, False)
        out = []
        for c in chains:
            produce(ctx, p_slot, c)
            out.append(consume(c_slot, c_f, c_diagonal, c, st[c]))
        return tuple(out)

    produce_all(0, 0, True)
    state = overlapped(1, 0, 0, 0, True, state)

    def pair(i, st):
        st = overlapped(0, 2 * i + 1, 1, 2 * i, False, st)
        if alive is None:
            return overlapped(1, 2 * i + 2, 0, 2 * i + 1, False, st)
        produce_all(1, 2 * i + 2, False)
        return lax.cond(alive(st), lambda s: consume_all(0, 2 * i + 1, False, s), lambda s: s, st)

    n_pairs = n_earlier >> 1
    odd = (n_earlier & 1) == 1
    if alive is None:
        state = lax.fori_loop(0, n_pairs, pair, state)
    else:
        _, state = lax.while_loop(lambda c: (c[0] < n_pairs) & alive(c[1]),
                                  lambda c: (c[0] + 1, pair(c[0], c[1])), (jnp.int32(0), state))
        odd = odd & alive(state)

    @pl.when(odd)
    def _():
        finalize(consume_all(1, n_earlier - 1, False, state))

    @pl.when(jnp.logical_not(odd))
    def _():
        finalize(state)


def _score_buffers(chains, qn):
    return [pltpu.VMEM((2, chains, KEY_TILE, qn), F32), pltpu.VMEM((2, chains, 1, qn), F32)]


def _online_init(n, dv, qn):
    return tuple((jnp.full((1, qn), NEG_INF, F32), jnp.zeros((1, qn), F32), jnp.zeros((dv, qn), F32))
                 for _ in range(n))


def _split_bf16(v, terms):
    out, rest = [], v
    for i in range(terms):
        part = rest.astype(BF16)
        out.append(part)
        if i + 1 < terms:
            rest = rest - part.astype(F32)
    return out


def _top_rows(key, n_top):
    n_rows, n_cols = key.shape

    def bisect(i, tau):
        cand = tau | lax.shift_left(jnp.int32(1), jnp.int32(30) - i)
        cnt = jnp.sum(jnp.where(key >= cand, 1.0, 0.0), axis=0, keepdims=True)
        return jnp.where(cnt >= float(n_top), cand, tau)

    tau = lax.fori_loop(0, 31, bisect, jnp.zeros((1, n_cols), I32))
    above = jnp.where(key > tau, 1.0, 0.0)
    equal = jnp.where(key == tau, 1.0, 0.0)
    need = float(n_top) - jnp.sum(above, axis=0, keepdims=True)
    lower = jnp.where(_iota((n_rows, n_rows), 1) < _iota((n_rows, n_rows), 0), 1.0, 0.0).astype(BF16)
    before = _dot(lower, equal.astype(BF16))
    return above + equal * jnp.where(before < need, 1.0, 0.0)

def _nsa_kernel(q_ref, kc_ref, vc_ref, ks_ref, vs_ref, kw_ref, vw_ref, g_ref, ovt_ref, o_ref,
                bias_scr, part_scr, s_scr, mx_scr, *, n_sel):
    qn = q_ref.shape[2]
    ncp = kc_ref.shape[2]
    ns = ovt_ref.shape[0]
    kt_n = KEY_TILE
    qi = pl.program_id(2)
    q0 = qi * qn
    reps = range(NSA_REP)
    q_head = lambda r: q_ref[0, r * HEAD_DIM:(r + 1) * HEAD_DIM, :]

    kc = kc_ref[0, 0]
    vc_t = vc_ref[0]
    bias_c = jnp.where((_iota((ncp, qn), 0) * CMP_STRIDE + (CMP_BLOCK - 1)) <= (q0 + _iota((ncp, qn), 1)),
                       0.0, NEG_INF)
    sees_cmp = (q0 + _iota((1, qn), 1)) >= CMP_BLOCK - 1
    o_cmp, p_sum = [], None
    for r in reps:
        p = _softmax_cols(_dot(kc, q_head(r)), bias_c, sees_cmp)
        o_cmp.append(_dot(vc_t, p.astype(BF16)))
        p_sum = p if p_sum is None else p_sum + p
    ovt = ovt_ref[...]
    imp_t = None
    for term in _split_bf16(p_sum, 3):
        part = _dot(ovt, term)
        imp_t = part if imp_t is None else imp_t + part

    blk = _iota((ns, qn), 0)
    cur = (q0 + _iota((ns, qn), 1)) >> SLC_SHIFT
    forced = (blk == 0) | (blk == cur) | (blk == cur - 1)
    imp_bits = jnp.where(imp_t > 0.0, lax.bitcast_convert_type(imp_t, I32), 0)
    key = jnp.where(forced, jnp.int32(2 ** 31 - 1), jnp.where(blk <= cur, imp_bits, -1))

    picked = _top_rows(key, n_sel)
    bias_scr[...] = jnp.where(picked > 0.5, 0.0, NEG_INF)

    carry = list(_online_init(NSA_REP, HEAD_DIM, qn))
    for diagonal in (True, False):
        kt = qi if diagonal else jnp.maximum(qi - WINDOW // kt_n, 0)
        k0 = pl.multiple_of(kt * kt_n, kt_n)
        kp = k0 + _iota((kt_n, qn), 0)
        tq = q0 + _iota((kt_n, qn), 1)
        inside = (kp <= tq) if diagonal else ((kp > tq - WINDOW) & (kp < q0))
        bias_w = jnp.where(inside, 0.0, NEG_INF)
        k = kw_ref[0, 0, pl.ds(k0, kt_n), :]
        scores = [_dot(k, q_head(r)) for r in reps]
        for r in reps:
            carry[r] = _online_cols(scores[r] + bias_w, vw_ref[0, kt], *carry[r])

    gates = g_ref[0]
    for r in reps:
        part_scr[r] = (gates[3 * r:3 * r + 1] * o_cmp[r]
                       + gates[3 * r + 2:3 * r + 3] * _normalized(carry[r]))

    blocks_per_tile = kt_n // SLC_BLOCK
    causal = _iota((kt_n, qn), 0) <= _iota((kt_n, qn), 1)
    def prepare(f, diagonal):
        kt = qi if diagonal else jnp.minimum(f, jnp.maximum(qi - 1, 0))
        k0 = pl.multiple_of(kt * kt_n, kt_n)
        rows = [jnp.broadcast_to(bias_scr[pl.ds(kt * blocks_per_tile + i, 1), :], (SLC_BLOCK, qn))
                for i in range(blocks_per_tile)]
        bias = jnp.concatenate(rows, axis=0)
        if diagonal:
            bias = jnp.where(causal, bias, NEG_INF)
        return k0, bias

    def produce(ctx, slot, r):
        k0, bias = ctx
        s = _dot(ks_ref[0, 0, pl.ds(k0, kt_n), :], q_head(r)) + bias
        s_scr[slot, r] = s
        mx_scr[slot, r] = jnp.max(s, axis=0, keepdims=True)

    def consume(slot, f, diagonal, r, state_r):
        return _online_stored(s_scr.at[slot, r], mx_scr[slot, r], vs_ref[0, qi if diagonal else f], *state_r)

    def finalize(state):
        for r in reps:
            o = part_scr[r] + g_ref[0, 3 * r + 1:3 * r + 2, :] * _normalized(state[r])
            o_ref[0, r * HEAD_DIM:(r + 1) * HEAD_DIM, :] = o.astype(o_ref.dtype)

    _sweep(qi, NSA_REP, prepare, produce, consume, finalize, _online_init(NSA_REP, HEAD_DIM, qn))


def _nsa_attention(q_t, k_cmp, v_cmp_t, k_slc, v_slc_t, k_win, v_win_t, gates_t):
    bsz, _, t = q_t.shape
    ncp = k_cmp.shape[2]
    ns = t // SLC_BLOCK
    qn = min(Q_TILE, t)
    grp_w = NSA_REP * HEAD_DIM
    cmp_start = jnp.arange(ncp) * CMP_STRIDE
    slc_start = jnp.arange(ns) * SLC_BLOCK
    real = jnp.arange(ncp) < (t - CMP_BLOCK) // CMP_STRIDE + 1
    overlap_t = ((cmp_start[None, :] < slc_start[:, None] + SLC_BLOCK)
                 & (cmp_start[None, :] + CMP_BLOCK > slc_start[:, None]) & real[None, :]).astype(BF16)
    k_spec = lambda n: pl.BlockSpec((1, 1, n, HEAD_DIM), lambda b, g, i: (b, g, 0, 0))
    v_spec = pl.BlockSpec((1, t // KEY_TILE, HEAD_DIM, KEY_TILE), lambda b, g, i: (b, 0, g, 0))
    return pl.pallas_call(
        functools.partial(_nsa_kernel, n_sel=min(N_SELECT, ns)),
        grid=(bsz, NSA_KV_GROUPS, t // qn),
        in_specs=[pl.BlockSpec((1, grp_w, qn), lambda b, g, i: (b, g, i)),
                  k_spec(ncp),
                  pl.BlockSpec((1, HEAD_DIM, ncp), lambda b, g, i: (b, g, 0)),
                  k_spec(t), v_spec, k_spec(t), v_spec,
                  pl.BlockSpec((1, GATE_ROWS, qn), lambda b, g, i: (b, g, i)),
                  pl.BlockSpec((ns, ncp), lambda b, g, i: (0, 0))],
        out_specs=pl.BlockSpec((1, grp_w, qn), lambda b, g, i: (b, g, i)),
        out_shape=jax.ShapeDtypeStruct((bsz, NSA_HEADS * HEAD_DIM, t), BF16),
        scratch_shapes=[pltpu.VMEM((ns, qn), F32), pltpu.VMEM((NSA_REP, HEAD_DIM, qn), F32)] + _score_buffers(NSA_REP, qn),
        compiler_params=_params("arbitrary", "arbitrary", "arbitrary"),
        name="nsa_attention",
    )(q_t, k_cmp, v_cmp_t, k_slc, v_slc_t, k_win, v_win_t, gates_t, overlap_t)


SB_HEADS_PER_STEP = 2
SB_DEAD_TAIL = -110.0
SB_SCAN_BLOCK = 128


def _sb_kernel(q_ref, k_ref, v_ref, o_ref, nz_scr):
    qn = q_ref.shape[2]
    kt_n = KEY_TILE
    qi = pl.program_id(2)
    heads = range(SB_HEADS_PER_STEP)
    rows = [slice(h * HEAD_DIM, (h + 1) * HEAD_DIM) for h in heads]
    sub = SB_SCAN_BLOCK
    n_sub = kt_n // sub
    tri = jnp.where(_iota((sub, sub), 1) > _iota((sub, sub), 0), 1.0, 0.0).astype(BF16)
    later2 = jnp.concatenate([tri, tri], axis=1)
    strictly_before = _iota((kt_n, qn), 0) < _iota((kt_n, qn), 1)

    def tile_index(f, diagonal):
        return qi if diagonal else jnp.maximum(qi - 1 - f, 0)

    def prepare(f, diagonal):
        return pl.multiple_of(tile_index(f, diagonal) * kt_n, kt_n)

    def produce(k0, slot, h):
        nz_scr[slot, h] = _dot(k_ref[0, h, pl.ds(k0, kt_n), :], -q_ref[0, rows[h], :])

    def consume(slot, f, diagonal, h, state_h):
        tail, acc = state_h
        nz = nz_scr[slot, h]
        neg_abs = lax.bitcast_convert_type(lax.bitcast_convert_type(nz, jnp.uint32) | jnp.uint32(0x80000000), F32)
        soft = jnp.log(1.0 + jnp.exp(neg_abs))
        log_keep = jnp.minimum(nz, 0.0) - soft
        log_beta = log_keep - nz
        if diagonal:
            log_keep = jnp.where(strictly_before, log_keep, 0.0)
        offs = tail
        parts = [None] * n_sub
        for blk in reversed(range(n_sub)):
            lk = log_keep[blk * sub:(blk + 1) * sub]
            hi = lk.astype(BF16)
            lo = (lk - hi.astype(F32)).astype(BF16)
            within = _dot(later2, jnp.concatenate([hi, lo], axis=0))
            parts[blk] = log_beta[blk * sub:(blk + 1) * sub] + within + offs
            offs = offs + jnp.sum(lk, axis=0, keepdims=True)
        a = jnp.exp(jnp.concatenate(parts, axis=0))
        if diagonal:
            a = jnp.where(strictly_before, a, 0.0)
        return offs, acc + _dot(v_ref[0, tile_index(f, diagonal), rows[h], :], a.astype(BF16))

    def finalize(state):
        for h in heads:
            o_ref[0, rows[h], :] = state[h][1].astype(o_ref.dtype)

    def alive(state):
        tails = [state[h][0] for h in heads]
        return jnp.max(functools.reduce(jnp.maximum, tails)) > SB_DEAD_TAIL

    init = tuple((jnp.zeros((1, qn), F32), jnp.zeros((HEAD_DIM, qn), F32)) for _ in heads)
    _sweep(qi, SB_HEADS_PER_STEP, prepare, produce, consume, finalize, init, alive)


def _sb_attention(q_t, k, v_t):
    bsz, nh, t, _ = k.shape
    qn = min(Q_TILE, t)
    hps = SB_HEADS_PER_STEP
    w = hps * HEAD_DIM
    return pl.pallas_call(
        _sb_kernel,
        grid=(bsz, nh // hps, t // qn),
        in_specs=[pl.BlockSpec((1, w, qn), lambda b, h, i: (b, h, i)),
                  pl.BlockSpec((1, hps, t, HEAD_DIM), lambda b, h, i: (b, h, 0, 0)),
                  pl.BlockSpec((1, t // KEY_TILE, w, KEY_TILE), lambda b, h, i: (b, 0, h, 0))],
        out_specs=pl.BlockSpec((1, w, qn), lambda b, h, i: (b, h, i)),
        out_shape=jax.ShapeDtypeStruct((bsz, nh * HEAD_DIM, t), BF16),
        scratch_shapes=[pltpu.VMEM((2, hps, KEY_TILE, qn), F32)],
        compiler_params=_params("arbitrary", "arbitrary", "arbitrary"),
        name="stickbreak_attention",
    )(q_t, k, v_t)


DIFF_HEADS_PER_STEP = 2


def _diff_kernel(q_ref, k_ref, v_ref, lq1_ref, lk1_ref, lq2_ref, lk2_ref, sub_ref, o_ref, s_scr, mx_scr,
                 *, lambda_init):
    qn = q_ref.shape[2]
    vd = 2 * HEAD_DIM
    kt_n = KEY_TILE
    qi = pl.program_id(2)
    chains = range(2 * DIFF_HEADS_PER_STEP)
    lam = (jnp.exp(jnp.sum(lq1_ref[...] * lk1_ref[...], axis=1, keepdims=True))
           - jnp.exp(jnp.sum(lq2_ref[...] * lk2_ref[...], axis=1, keepdims=True)) + lambda_init)

    causal = _iota((kt_n, qn), 0) <= _iota((kt_n, qn), 1)

    def prepare(f, diagonal):
        kt = qi if diagonal else jnp.minimum(f, jnp.maximum(qi - 1, 0))
        return pl.multiple_of(kt * kt_n, kt_n), diagonal

    def produce(ctx, slot, c):
        k0, diagonal = ctx
        s = _dot(k_ref[0, c, pl.ds(k0, kt_n), :], q_ref[0, c * HEAD_DIM:(c + 1) * HEAD_DIM, :])
        if diagonal:
            s = jnp.where(causal, s, NEG_INF)
        s_scr[slot, c] = s
        mx_scr[slot, c] = jnp.max(s, axis=0, keepdims=True)

    def consume(slot, f, diagonal, c, state_c):
        kt = qi if diagonal else f
        return _online_stored(s_scr.at[slot, c], mx_scr[slot, c],
                              v_ref[0, kt, (c // 2) * vd:(c // 2 + 1) * vd, :], *state_c)

    def finalize(state):
        for h in range(DIFF_HEADS_PER_STEP):
            o = _normalized(state[2 * h]) - lam * _normalized(state[2 * h + 1])
            ms = jnp.mean(o * o, axis=0, keepdims=True)
            o = o * lax.rsqrt(ms + RMS_EPS) * sub_ref[...] * (1.0 - lambda_init)
            o_ref[0, h * vd:(h + 1) * vd, :] = o.astype(o_ref.dtype)

    _sweep(qi, len(chains), prepare, produce, consume, finalize, _online_init(len(chains), vd, qn))


def _diff_attention(q_t, k, v_t, lq1, lk1, lq2, lk2, subln, lambda_init):
    bsz, nh2, t, _ = k.shape
    nh = nh2 // 2
    vd = 2 * HEAD_DIM
    hps = DIFF_HEADS_PER_STEP
    qn = min(Q_TILE, t)
    row = lambda a: a.reshape(1, -1)
    small = lambda n: pl.BlockSpec((1, n), lambda b, h, i: (0, 0))
    return pl.pallas_call(
        functools.partial(_diff_kernel, lambda_init=lambda_init),
        grid=(bsz, nh // hps, t // qn),
        in_specs=[pl.BlockSpec((1, hps * vd, qn), lambda b, h, i: (b, h, i)),
                  pl.BlockSpec((1, 2 * hps, t, HEAD_DIM), lambda b, h, i: (b, h, 0, 0)),
                  pl.BlockSpec((1, t // KEY_TILE, hps * vd, KEY_TILE), lambda b, h, i: (b, 0, h, 0)),
                  small(HEAD_DIM), small(HEAD_DIM), small(HEAD_DIM), small(HEAD_DIM),
                  pl.BlockSpec((vd, 1), lambda b, h, i: (0, 0))],
        out_specs=pl.BlockSpec((1, hps * vd, qn), lambda b, h, i: (b, h, i)),
        out_shape=jax.ShapeDtypeStruct((bsz, nh * vd, t), BF16),
        scratch_shapes=_score_buffers(2 * hps, qn),
        compiler_params=_params("arbitrary", "arbitrary", "arbitrary"),
        name="diff_attention",
    )(q_t, k, v_t, row(lq1), row(lk1), row(lq2), row(lk2), subln.reshape(vd, 1))


def _outproj_kernel(*refs, n_in):
    a_refs, w_refs = refs[:n_in], refs[n_in:2 * n_in]
    x_ref, gate_ref, o_ref = refs[2 * n_in:]
    y = None
    for a_ref, w_ref in zip(a_refs, w_refs):
        part = lax.dot_general(a_ref[0], w_ref[...].astype(BF16), _TN, preferred_element_type=F32)
        y = part if y is None else y + part
    o_ref[0] = x_ref[0] + gate_ref[0] * y


def _out_project(acts_t, weight, x, gate, name):
    bsz, t, d = x.shape
    rows = min(t, PROJ_ROWS)
    k_rows = acts_t[0].shape[1]
    tile = pl.BlockSpec((1, rows, d), lambda b, i: (b, i, 0))
    return pl.pallas_call(
        functools.partial(_outproj_kernel, n_in=len(acts_t)),
        grid=(bsz, t // rows),
        in_specs=([pl.BlockSpec((1, a.shape[1], rows), lambda b, i: (b, 0, i)) for a in acts_t]
                  + [pl.BlockSpec((k_rows, d), lambda b, i, n=n: (n, 0)) for n in range(len(acts_t))]
                  + [tile, pl.BlockSpec((1, 1, d), lambda b, i: (b, 0, 0))]),
        out_specs=tile,
        out_shape=jax.ShapeDtypeStruct((bsz, t, d), F32),
        compiler_params=_params("arbitrary", "arbitrary"),
        name=name,
    )(*acts_t, *([weight] * len(acts_t)), x, gate)


def _ffn_kernel(x_ref, halo_ref, g_ref, sh_ref, sc_ref, gate_ref, wg_ref, wu_ref, cw_ref, cb_ref, wd_ref,
                nf_ref, o_ref, a_scr, *, final_norm):
    ti = pl.program_id(1)
    h = _norm_modulate(x_ref[0], g_ref[...], sh_ref[0], sc_ref[0]).astype(BF16)
    h_prev = _norm_modulate(halo_ref[0], g_ref[...], sh_ref[0], sc_ref[0]).astype(BF16)
    h_both = jnp.concatenate([h_prev, h], axis=0)
    seen = jnp.where(ti > 0, 1.0, 0.0)
    cols = FFN_COLS
    row = _iota((h.shape[0], cols), 0)
    for c in range(a_scr.shape[1] // cols):
        sl = slice(c * cols, (c + 1) * cols)
        g_both = _dot(h_both, wg_ref[0, :, sl].astype(BF16))
        g = g_both[HALO_ROWS:]
        g_prev = g_both[:HALO_ROWS] * seen
        last = g_prev[HALO_ROWS - 1:HALO_ROWS]
        last2 = g_prev[HALO_ROWS - 2:HALO_ROWS - 1]
        g1 = jnp.where(row == 0, last, pltpu.roll(g, 1, 0))
        g2 = jnp.where(row == 0, last2, jnp.where(row == 1, last, pltpu.roll(g, 2, 0)))
        cw = cw_ref[:, sl]
        conv = cw[0:1] * g2 + cw[1:2] * g1 + cw[2:3] * g + cb_ref[:, sl]
        a_scr[:, sl] = (conv * _sigmoid(conv) * _dot(h, wu_ref[0, :, sl].astype(BF16))).astype(BF16)
    y = x_ref[0] + gate_ref[0] * _dot(a_scr[...], wd_ref[...])
    if final_norm:
        ms = jnp.mean(y * y, axis=-1, keepdims=True)
        y = y * lax.rsqrt(ms + RMS_EPS) * nf_ref[...]
    o_ref[0] = y


def _conv_ffn(x, g, shift, scale, gate, layer, w_gate_all, w_up_all, conv_w, conv_b, w_down, norm_f, final_norm,
              name):
    bsz, t, d = x.shape
    f = w_gate_all.shape[2]
    rows = min(t, FFN_ROWS)
    halo_blocks = rows // HALO_ROWS
    vec = pl.BlockSpec((1, 1, d), lambda b, i: (b, 0, 0))
    drow = pl.BlockSpec((1, d), lambda b, i: (0, 0))
    resident = lambda shape: pl.BlockSpec(shape, lambda b, i: (0, 0), pipeline_mode=pl.Buffered(1))
    stacked = pl.BlockSpec((1, d, f), lambda b, i: (layer, 0, 0), pipeline_mode=pl.Buffered(1))
    return pl.pallas_call(
        functools.partial(_ffn_kernel, final_norm=final_norm),
        grid=(bsz, t // rows),
        in_specs=[pl.BlockSpec((1, rows, d), lambda b, i: (b, i, 0)),
                  pl.BlockSpec((1, HALO_ROWS, d), lambda b, i: (b, jnp.maximum(i * halo_blocks - 1, 0), 0)),
                  drow, vec, vec, vec,
                  stacked, stacked, resident((CONV_WIDTH, f)), resident((1, f)),
                  resident((f, d)), drow],
        out_specs=pl.BlockSpec((1, rows, d), lambda b, i: (b, i, 0)),
        out_shape=jax.ShapeDtypeStruct((bsz, t, d), F32),
        scratch_shapes=[pltpu.VMEM((rows, f), BF16)],
        compiler_params=_params("arbitrary", "arbitrary"),
        name=name,
    )(x, x, g.reshape(1, d), shift, scale, gate, w_gate_all, w_up_all, conv_w,
      conv_b.reshape(1, f), w_down.astype(BF16), norm_f.reshape(1, d))


def _layout(widths):
    starts, start = [], 0
    for w in widths:
        starts.append(start)
        start += w
    return starts


def _hybrid_layer(x, mods, tables, norm_g, w_in, pos_k, pos_v, ck_w1, ck_w2, cv_w1, cv_w2, w_out):
    shift, scale, gate = mods
    bsz, t, d = x.shape
    qw, kvw, sbw = NSA_HEADS * HEAD_DIM, NSA_KV_GROUPS * HEAD_DIM, SB_HEADS * HEAD_DIM
    col = _layout((qw,) + (kvw,) * 6 + (3 * NSA_HEADS,) + (sbw,) * 3)
    cols = lambda i, w: w_in[:, col[i]:col[i] + w]
    q_n, kc, vc, ks, vs, kw, vw, gl, q_s, k_s, v_s = (cols(i, w) for i, w in enumerate(
        (qw,) + (kvw,) * 6 + (3 * NSA_HEADS,) + (sbw,) * 3))
    inv_sqrt = HEAD_DIM ** -0.5

    w_tok = jnp.concatenate([ks, kw, kc, vc, k_s], axis=1).astype(BF16)
    ts = _layout((kvw, kvw, kvw, kvw, sbw))
    tok_segs = [TokSeg(ts[0], kvw, True, HEAD_DIM, BF16),
                TokSeg(ts[1], kvw, True, HEAD_DIM, BF16),
                TokSeg(ts[2], kvw, False, 0, F32),
                TokSeg(ts[3], kvw, False, 0, F32),
                TokSeg(ts[4], sbw, False, HEAD_DIM, BF16)]
    per_group = 3 * NSA_REP
    gate_pad = jnp.zeros((d, GATE_ROWS - per_group), w_in.dtype)
    gl_pad = jnp.concatenate([gl[:, :per_group], gate_pad, gl[:, per_group:], gate_pad], axis=1)
    w_feat_t = jnp.concatenate([q_n, q_s, vs, vw, v_s, gl_pad], axis=1).T.astype(BF16)
    fs = _layout((qw, sbw, kvw, kvw, sbw, NSA_KV_GROUPS * GATE_ROWS))
    feat_segs = [FeatSeg(fs[0], qw, True, inv_sqrt * LOG2E, False, False, BF16),
                 FeatSeg(fs[1], sbw, False, inv_sqrt, False, False, BF16),
                 FeatSeg(fs[2], kvw, False, 1.0, False, True, BF16),
                 FeatSeg(fs[3], kvw, False, 1.0, False, True, BF16),
                 FeatSeg(fs[4], sbw, False, 1.0, False, True, BF16),
                 FeatSeg(fs[5], NSA_KV_GROUPS * GATE_ROWS, False, 1.0, True, False, F32)]
    (k_slc, k_win, kc_a, vc_a, k_sb, q_n_t, q_s_t, v_slc_t, v_win_t, v_sb_t, gates_t) = _project(
        x, norm_g, shift, scale, w_tok, w_feat_t, tables, tok_segs, feat_segs, "hybrid_in_proj")

    ncp = t // CMP_STRIDE
    end_rows = jnp.minimum(jnp.arange(ncp) * CMP_STRIDE + CMP_BLOCK - 1, t - 1)
    cos_c, sin_c = tables[0][:, end_rows], tables[1][:, end_rows]
    k_cmp = _compress(kc_a, pos_k, ck_w1, ck_w2, cos_c, sin_c, True, "compress_k")
    v_cmp_t = _compress(vc_a, pos_v, cv_w1, cv_w2, cos_c, sin_c, False, "compress_v")
    o_nsa_t = _nsa_attention(q_n_t, k_cmp, v_cmp_t, k_slc, v_slc_t, k_win, v_win_t, gates_t)
    o_sb_t = _sb_attention(q_s_t, k_sb, v_sb_t)
    return _out_project([o_nsa_t, o_sb_t], w_out, x, gate, "hybrid_out_proj")


def _diff_layer(x, mods, tables, norm_g, w_qkv, lq1, lk1, lq2, lk2, subln, w_out, layer_idx):
    shift, scale, gate = mods
    dw = 2 * DIFF_HEADS * HEAD_DIM
    inv_sqrt = HEAD_DIM ** -0.5
    w_tok = w_qkv[:, dw:2 * dw].astype(BF16)
    w_feat_t = jnp.concatenate([w_qkv[:, :dw], w_qkv[:, 2 * dw:]], axis=1).T.astype(BF16)
    tok_segs = [TokSeg(0, dw, True, HEAD_DIM, BF16)]
    feat_segs = [FeatSeg(0, dw, True, inv_sqrt * LOG2E, False, False, BF16),
                 FeatSeg(dw, dw, False, 1.0, False, True, BF16)]
    k, q_t, v_t = _project(x, norm_g, shift, scale, w_tok, w_feat_t, tables, tok_segs, feat_segs, "diff_in_proj")
    lambda_init = 0.8 - 0.6 * math.exp(-0.3 * layer_idx)
    o_t = _diff_attention(q_t, k, v_t, lq1, lk1, lq2, lk2, subln, lambda_init)
    return _out_project([o_t], w_out, x, gate, "diff_out_proj")


def kernel(x, c, positions, mod_w, mod_b, norm_mix, norm_ffn, ffn_w_gate, ffn_w_up, ffn_conv_w, ffn_conv_b, ffn_w_down, hyb_w_in, nsa_pos_k, nsa_pos_v, nsa_ck_w1, nsa_ck_w2, nsa_cv_w1, nsa_cv_w2, hyb_w_out, diff_w_qkv, diff_lq1, diff_lk1, diff_lq2, diff_lk2, diff_subln, diff_w_out, norm_f):
    bsz, t, d = x.shape
    depth = mod_w.shape[0]
    mod = _adaln_mod(c, mod_w, mod_b)
    tables = _rope_tables(positions)
    for i in range(depth):
        sh_m, sc_m, g_m, sh_f, sc_f, g_f = (mod[i, :, k * d:(k + 1) * d].reshape(bsz, 1, d) for k in range(6))
        j = i // 2
        if i % 2 == 0:
            x = _hybrid_layer(x, (sh_m, sc_m, g_m), tables, norm_mix[i], hyb_w_in[j],
                              nsa_pos_k[j], nsa_pos_v[j], nsa_ck_w1[j], nsa_ck_w2[j], nsa_cv_w1[j],
                              nsa_cv_w2[j], hyb_w_out[j])
        else:
            x = _diff_layer(x, (sh_m, sc_m, g_m), tables, norm_mix[i], diff_w_qkv[j], diff_lq1[j],
                            diff_lk1[j], diff_lq2[j], diff_lk2[j], diff_subln[j], diff_w_out[j], i)
        x = _conv_ffn(x, norm_ffn[i], sh_f, sc_f, g_f, i, ffn_w_gate, ffn_w_up, ffn_conv_w[i],
                      ffn_conv_b[i], ffn_w_down[i], norm_f, i == depth - 1, "conv_ffn_%d" % i)
    return x
```

```python
import functools
import math
from typing import NamedTuple

import jax
import jax.numpy as jnp
from jax import lax
from jax.experimental import pallas as pl
from jax.experimental.pallas import tpu as pltpu

F32 = jnp.float32
BF16 = jnp.bfloat16
I32 = jnp.int32

HEAD_DIM = 64
ROPE_DIM = HEAD_DIM // 4
ROPE_HALF = ROPE_DIM // 2
ROPE_THETA = 500000.0
NSA_HEADS = 8
NSA_KV_GROUPS = 2
NSA_REP = NSA_HEADS // NSA_KV_GROUPS
CMP_BLOCK = 32
CMP_STRIDE = 16
CMP_HIDDEN = 4 * HEAD_DIM
SLC_BLOCK = 64
SLC_SHIFT = 6
N_SELECT = 16
WINDOW = 512
SB_HEADS = 8
DIFF_HEADS = 8
CONV_WIDTH = 3
RMS_EPS = 1e-6
NEG_INF = -1e30
LOG2E = math.log2(math.e)

LANES = 128
Q_TILE = 512
KEY_TILE = 512
GATE_ROWS = 16
PROJ_ROWS = 1024
FFN_ROWS = 512
FFN_COLS = 256
HALO_ROWS = 16
VMEM_LIMIT = 56 * 1024 * 1024

_NT = (((1,), (1,)), ((), ()))
_TN = (((0,), (0,)), ((), ()))


def _params(*sem):
    return pltpu.CompilerParams(dimension_semantics=sem, vmem_limit_bytes=VMEM_LIMIT)


def _sigmoid(v):
    return 1.0 / (1.0 + jnp.exp(-v))


def _iota(shape, axis):
    return lax.broadcasted_iota(I32, shape, axis)


def _dot(a, b):
    return jnp.dot(a, b, preferred_element_type=F32)


def _mod_kernel(c_ref, w_ref, b_ref, o_ref):
    c = c_ref[...]
    cond = c * _sigmoid(c)
    o_ref[0] = jnp.dot(cond, w_ref[0], preferred_element_type=F32,
                       precision=lax.Precision.HIGHEST) + b_ref[0]


def _adaln_mod(c, mod_w, mod_b):
    depth, d, n = mod_w.shape
    bsz = c.shape[0]
    tn = n // 4
    return pl.pallas_call(
        _mod_kernel,
        grid=(depth, n // tn),
        in_specs=[pl.BlockSpec((bsz, d), lambda i, j: (0, 0)),
                  pl.BlockSpec((1, d, tn), lambda i, j: (i, 0, j)),
                  pl.BlockSpec((1, 1, tn), lambda i, j: (i, 0, j))],
        out_specs=pl.BlockSpec((1, bsz, tn), lambda i, j: (i, 0, j)),
        out_shape=jax.ShapeDtypeStruct((depth, bsz, n), F32),
        compiler_params=_params("arbitrary", "arbitrary"),
        name="adaln_mod",
    )(c, mod_w, mod_b.reshape(depth, 1, n))


def _rope_kernel(pos_col_ref, pos_row_ref, inv_row_ref, sgn_row_ref, inv_col_ref,
                 cos_ref, sin_ref, cos_t_ref, sin_t_ref):
    ang = pos_col_ref[0].astype(F32) * inv_row_ref[...]
    cos_ref[0] = jnp.cos(ang)
    sin_ref[0] = jnp.sin(ang) * sgn_row_ref[...]
    ang_t = inv_col_ref[...] * pos_row_ref[0].astype(F32)
    cos_t_ref[0] = jnp.cos(ang_t)
    sin_t_ref[0] = jnp.sin(ang_t)


def _rope_tables(positions):
    bsz, t = positions.shape
    inv = ROPE_THETA ** (-jnp.arange(0, ROPE_DIM, 2, dtype=F32) / ROPE_DIM)
    per_head_inv = jnp.concatenate([inv, inv, jnp.zeros((HEAD_DIM - ROPE_DIM,), F32)])
    per_head_sgn = jnp.concatenate([-jnp.ones((ROPE_HALF,), F32), jnp.ones((ROPE_HALF,), F32),
                                    jnp.zeros((HEAD_DIM - ROPE_DIM,), F32)])
    inv_row = jnp.tile(per_head_inv, LANES // HEAD_DIM)[None, :]
    sgn_row = jnp.tile(per_head_sgn, LANES // HEAD_DIM)[None, :]
    rows = min(t, 1024)
    tab = jax.ShapeDtypeStruct((bsz, t, LANES), F32)
    tab_t = jax.ShapeDtypeStruct((bsz, ROPE_HALF, t), F32)
    row_spec = pl.BlockSpec((1, LANES), lambda b, i: (0, 0))
    return pl.pallas_call(
        _rope_kernel,
        grid=(bsz, t // rows),
        in_specs=[pl.BlockSpec((1, rows, 1), lambda b, i: (b, i, 0)),
                  pl.BlockSpec((1, 1, rows), lambda b, i: (b, 0, i)),
                  row_spec, row_spec,
                  pl.BlockSpec((ROPE_HALF, 1), lambda b, i: (0, 0))],
        out_specs=[pl.BlockSpec((1, rows, LANES), lambda b, i: (b, i, 0))] * 2
        + [pl.BlockSpec((1, ROPE_HALF, rows), lambda b, i: (b, 0, i))] * 2,
        out_shape=[tab, tab, tab_t, tab_t],
        compiler_params=_params("arbitrary", "arbitrary"),
        name="rope_tables",
    )(positions.reshape(bsz, t, 1), positions.reshape(bsz, 1, t), inv_row, sgn_row, inv[:, None])


def _rope_chunk(y, cosv, sinv, first_half):
    ahead = pltpu.roll(y, LANES - ROPE_HALF, 1)
    behind = pltpu.roll(y, ROPE_HALF, 1)
    return y * cosv + jnp.where(first_half, ahead, behind) * sinv


def _rope_rows(y_t, cos_t, sin_t):
    heads = []
    for h in range(y_t.shape[0] // HEAD_DIM):
        blk = y_t[h * HEAD_DIM:(h + 1) * HEAD_DIM]
        x1, x2 = blk[:ROPE_HALF], blk[ROPE_HALF:ROPE_DIM]
        heads += [x1 * cos_t - x2 * sin_t, x2 * cos_t + x1 * sin_t, blk[ROPE_DIM:]]
    return jnp.concatenate(heads, axis=0)


class TokSeg(NamedTuple):
    start: int
    width: int
    rope: bool
    head_width: int
    dtype: object


class FeatSeg(NamedTuple):
    start: int
    rows: int
    rope: bool
    scale: float
    sigmoid: bool
    key_tiled: bool
    dtype: object


def _norm_modulate(x, g, shift, scale):
    ms = jnp.mean(x * x, axis=-1, keepdims=True)
    y = x * lax.rsqrt(ms + RMS_EPS) * g
    return y * (1.0 + scale) + shift


def _proj_kernel(x_ref, g_ref, sh_ref, sc_ref, w_ref, wt_ref, cos_ref, sin_ref, cos_t_ref, sin_t_ref,
                 *out_refs, tok_segs, feat_segs):
    hb = _norm_modulate(x_ref[0], g_ref[...], sh_ref[0], sc_ref[0]).astype(BF16)
    cosv, sinv = cos_ref[0], sin_ref[0]
    first_half = (_iota(cosv.shape, 1) & (HEAD_DIM - 1)) < ROPE_HALF
    tok_refs, feat_refs = out_refs[:len(tok_segs)], out_refs[len(tok_segs):]
    for seg, o_ref in zip(tok_segs, tok_refs):
        y = _dot(hb, w_ref[:, seg.start:seg.start + seg.width])
        for ch in range(seg.width // LANES):
            yc = y[:, ch * LANES:(ch + 1) * LANES]
            if seg.rope:
                yc = _rope_chunk(yc, cosv, sinv, first_half)
            yc = yc.astype(seg.dtype)
            if seg.head_width == 0:
                o_ref[0, :, ch * LANES:(ch + 1) * LANES] = yc
            else:
                o_ref[0, 2 * ch] = yc[:, :HEAD_DIM]
                o_ref[0, 2 * ch + 1] = yc[:, HEAD_DIM:]
    y_t_all = lax.dot_general(wt_ref[...], hb, _NT, preferred_element_type=F32)
    for seg, o_ref in zip(feat_segs, feat_refs):
        y_t = y_t_all[seg.start:seg.start + seg.rows]
        if seg.rope:
            y_t = _rope_rows(y_t, cos_t_ref[0], sin_t_ref[0])
        if seg.scale != 1.0:
            y_t = y_t * seg.scale
        if seg.sigmoid:
            y_t = _sigmoid(y_t)
        y_t = y_t.astype(seg.dtype)
        if seg.key_tiled:
            for ch in range(y_t.shape[1] // KEY_TILE):
                o_ref[0, ch] = y_t[:, ch * KEY_TILE:(ch + 1) * KEY_TILE]
        else:
            o_ref[0] = y_t


def _project(x, g, shift, scale, w_tok, w_feat_t, tables, tok_segs, feat_segs, name):
    bsz, t, d = x.shape
    rows = min(t, PROJ_ROWS)
    cos_tab, sin_tab, cos_t, sin_t = tables
    out_shapes, out_specs = [], []
    for seg in tok_segs:
        if seg.head_width == 0:
            out_shapes.append(jax.ShapeDtypeStruct((bsz, t, seg.width), seg.dtype))
            out_specs.append(pl.BlockSpec((1, rows, seg.width), lambda b, i: (b, i, 0)))
        else:
            nh = seg.width // seg.head_width
            out_shapes.append(jax.ShapeDtypeStruct((bsz, nh, t, seg.head_width), seg.dtype))
            out_specs.append(pl.BlockSpec((1, nh, rows, seg.head_width), lambda b, i: (b, 0, i, 0)))
    for seg in feat_segs:
        if seg.key_tiled:
            out_shapes.append(jax.ShapeDtypeStruct((bsz, t // KEY_TILE, seg.rows, KEY_TILE), seg.dtype))
            out_specs.append(pl.BlockSpec((1, rows // KEY_TILE, seg.rows, KEY_TILE), lambda b, i: (b, i, 0, 0)))
        else:
            out_shapes.append(jax.ShapeDtypeStruct((bsz, seg.rows, t), seg.dtype))
            out_specs.append(pl.BlockSpec((1, seg.rows, rows), lambda b, i: (b, 0, i)))
    vec = pl.BlockSpec((1, 1, d), lambda b, i: (b, 0, 0))
    tab = pl.BlockSpec((1, rows, LANES), lambda b, i: (b, i, 0))
    tab_t = pl.BlockSpec((1, ROPE_HALF, rows), lambda b, i: (b, 0, i))
    return pl.pallas_call(
        functools.partial(_proj_kernel, tok_segs=tuple(tok_segs), feat_segs=tuple(feat_segs)),
        grid=(bsz, t // rows),
        in_specs=[pl.BlockSpec((1, rows, d), lambda b, i: (b, i, 0)),
                  pl.BlockSpec((1, d), lambda b, i: (0, 0)),
                  vec, vec,
                  pl.BlockSpec(w_tok.shape, lambda b, i: (0, 0)),
                  pl.BlockSpec(w_feat_t.shape, lambda b, i: (0, 0)),
                  tab, tab, tab_t, tab_t],
        out_specs=out_specs,
        out_shape=out_shapes,
        compiler_params=_params("arbitrary", "arbitrary"),
        name=name,
    )(x, g.reshape(1, d), shift, scale, w_tok, w_feat_t, cos_tab, sin_tab, cos_t, sin_t)


def _compress_kernel(r_ref, pa_ref, pb_ref, wa_ref, wb_ref, w2_ref, cos_ref, sin_ref, o_ref, *, is_key):
    r = r_ref[0]
    ncp = r.shape[0]
    a = _dot((r + pa_ref[...]).astype(BF16), wa_ref[...])
    b = _dot((r + pb_ref[...]).astype(BF16), wb_ref[...])
    hid = a + pltpu.roll(b, ncp - 1, 0)
    hid = (hid * _sigmoid(hid)).astype(BF16)
    if is_key:
        y = _dot(hid, w2_ref[...])
        first_half = (_iota(y.shape, 1) & (HEAD_DIM - 1)) < ROPE_HALF
        y = _rope_chunk(y, cos_ref[0], sin_ref[0], first_half).astype(o_ref.dtype)
        for g in range(NSA_KV_GROUPS):
            o_ref[0, g] = y[:, g * HEAD_DIM:(g + 1) * HEAD_DIM]
    else:
        o_ref[0] = lax.dot_general(w2_ref[...], hid, _NT, preferred_element_type=F32).astype(o_ref.dtype)


def _compress(kv, pos_emb, w1, w2, cos_c, sin_c, is_key, name):
    bsz, t, _ = kv.shape
    ncp = t // CMP_STRIDE
    kwid = CMP_STRIDE * NSA_KV_GROUPS * HEAD_DIM
    hid_w = NSA_KV_GROUPS * CMP_HIDDEN
    r = kv.reshape(bsz, ncp, kwid)
    per = CMP_BLOCK // CMP_STRIDE
    w1r = w1.reshape(per, CMP_STRIDE, HEAD_DIM, CMP_HIDDEN)
    zeros = jnp.zeros_like(w1r)
    grp0 = jnp.concatenate([w1r, zeros], axis=-1)
    grp1 = jnp.concatenate([zeros, w1r], axis=-1)
    wbig = jnp.stack([grp0, grp1], axis=2).reshape(per, kwid, hid_w).astype(BF16)
    posr = pos_emb.reshape(per, CMP_STRIDE, 1, HEAD_DIM)
    posbig = jnp.broadcast_to(posr, (per, CMP_STRIDE, NSA_KV_GROUPS, HEAD_DIM)).reshape(per, 1, kwid)
    z2 = jnp.zeros_like(w2)
    w2big = jnp.concatenate([jnp.concatenate([w2, z2], axis=1),
                             jnp.concatenate([z2, w2], axis=1)], axis=0).astype(BF16)
    const = lambda shape: pl.BlockSpec(shape, lambda b: (0,) * len(shape))
    tab = pl.BlockSpec((1, ncp, LANES), lambda b: (b, 0, 0))
    if is_key:
        w2_arg = w2big
        out_spec = pl.BlockSpec((1, NSA_KV_GROUPS, ncp, HEAD_DIM), lambda b: (b, 0, 0, 0))
        out_shape = jax.ShapeDtypeStruct((bsz, NSA_KV_GROUPS, ncp, HEAD_DIM), BF16)
    else:
        w2_arg = w2big.T
        out_spec = pl.BlockSpec((1, NSA_KV_GROUPS * HEAD_DIM, ncp), lambda b: (b, 0, 0))
        out_shape = jax.ShapeDtypeStruct((bsz, NSA_KV_GROUPS * HEAD_DIM, ncp), BF16)
    return pl.pallas_call(
        functools.partial(_compress_kernel, is_key=is_key),
        grid=(bsz,),
        in_specs=[pl.BlockSpec((1, ncp, kwid), lambda b: (b, 0, 0)),
                  const((1, kwid)), const((1, kwid)),
                  const((kwid, hid_w)), const((kwid, hid_w)),
                  const(w2_arg.shape), tab, tab],
        out_specs=out_spec,
        out_shape=out_shape,
        compiler_params=_params("arbitrary"),
        name=name,
    )(r, posbig[0], posbig[1], wbig[0], wbig[1], w2_arg, cos_c, sin_c)


def _softmax_cols(s, bias, any_visible):
    sb = s + bias
    e = jnp.exp2(sb - jnp.max(sb, axis=0, keepdims=True))
    l = jnp.sum(e, axis=0, keepdims=True)
    return e * jnp.where(any_visible, 1.0 / l, 0.0)


def _online_step(s, tile_max, v_t, m_old, l_old, acc_old):
    m_new = jnp.maximum(m_old, tile_max)
    alpha = jnp.exp2(m_old - m_new)
    p = jnp.exp2(s - m_new)
    l_new = alpha * l_old + jnp.sum(p, axis=0, keepdims=True)
    acc_new = alpha * acc_old + _dot(v_t, p.astype(BF16))
    return m_new, l_new, acc_new


def _online_cols(s, v_t, m_old, l_old, acc_old):
    return _online_step(s, jnp.max(s, axis=0, keepdims=True), v_t, m_old, l_old, acc_old)


def _online_stored(s_ref, tile_max, v_t, m_old, l_old, acc_old):
    return _online_step(s_ref[...], tile_max, v_t, m_old, l_old, acc_old)


def _normalized(state):
    _, l, acc = state
    return acc * (1.0 / l)


def _sweep(n_earlier, n_chains, prepare, produce, consume, finalize, state, alive=None):
    chains = range(n_chains)

    def produce_all(slot, f, diagonal):
        ctx = prepare(f, diagonal)
        for c in chains:
            produce(ctx, slot, c)

    def consume_all(slot, f, diagonal, st):
        return tuple(consume(slot, f, diagonal, c, st[c]) for c in chains)

    def overlapped(p_slot, p_f, c_slot, c_f, c_diagonal, st):
        ctx = prepare(p_f, False)
        out = []
        for c in chains:
            produce(ctx, p_slot, c)
            out.append(consume(c_slot, c_f, c_diagonal, c, st[c]))
        return tuple(out)

    produce_all(0, 0, True)
    state = overlapped(1, 0, 0, 0, True, state)

    def pair(i, st):
        st = overlapped(0, 2 * i + 1, 1, 2 * i, False, st)
        if alive is None:
            return overlapped(1, 2 * i + 2, 0, 2 * i + 1, False, st)
        produce_all(1, 2 * i + 2, False)
        return lax.cond(alive(st), lambda s: consume_all(0, 2 * i + 1, False, s), lambda s: s, st)

    n_pairs = n_earlier >> 1
    odd = (n_earlier & 1) == 1
    if alive is None:
        state = lax.fori_loop(0, n_pairs, pair, state)
    else:
        _, state = lax.while_loop(lambda c: (c[0] < n_pairs) & alive(c[1]),
                                  lambda c: (c[0] + 1, pair(c[0], c[1])), (jnp.int32(0), state))
        odd = odd & alive(state)

    @pl.when(odd)
    def _():
        finalize(consume_all(1, n_earlier - 1, False, state))

    @pl.when(jnp.logical_not(odd))
    def _():
        finalize(state)


def _score_buffers(chains, qn):
    return [pltpu.VMEM((2, chains, KEY_TILE, qn), F32), pltpu.VMEM((2, chains, 1, qn), F32)]


def _online_init(n, dv, qn):
    return tuple((jnp.full((1, qn), NEG_INF, F32), jnp.zeros((1, qn), F32), jnp.zeros((dv, qn), F32))
                 for _ in range(n))


def _split_bf16(v, terms):
    out, rest = [], v
    for i in range(terms):
        part = rest.astype(BF16)
        out.append(part)
        if i + 1 < terms:
            rest = rest - part.astype(F32)
    return out


def _top_rows(key, n_top):
    n_rows, n_cols = key.shape

    def bisect(i, tau):
        cand = tau | lax.shift_left(jnp.int32(1), jnp.int32(30) - i)
        cnt = jnp.sum(jnp.where(key >= cand, 1.0, 0.0), axis=0, keepdims=True)
        return jnp.where(cnt >= float(n_top), cand, tau)

    tau = lax.fori_loop(0, 31, bisect, jnp.zeros((1, n_cols), I32))
    above = jnp.where(key > tau, 1.0, 0.0)
    equal = jnp.where(key == tau, 1.0, 0.0)
    need = float(n_top) - jnp.sum(above, axis=0, keepdims=True)
    lower = jnp.where(_iota((n_rows, n_rows), 1) < _iota((n_rows, n_rows), 0), 1.0, 0.0).astype(BF16)
    before = _dot(lower, equal.astype(BF16))
    return above + equal * jnp.where(before < need, 1.0, 0.0)

def _nsa_kernel(q_ref, kc_ref, vc_ref, ks_ref, vs_ref, kw_ref, vw_ref, g_ref, ovt_ref, o_ref,
                bias_scr, imp_scr, part_scr, s_scr, mx_scr, *, n_sel):
    qn = q_ref.shape[2]
    ncp = kc_ref.shape[2]
    ns = ovt_ref.shape[0]
    kt_n = KEY_TILE
    qi = pl.program_id(2)
    q0 = qi * qn
    reps = range(NSA_REP)
    q_head = lambda r: q_ref[0, r * HEAD_DIM:(r + 1) * HEAD_DIM, :]

    gates = g_ref[0]
    sees_cmp = (q0 + _iota((1, qn), 1)) >= CMP_BLOCK - 1

    def compressed_branch(n_cmp):
        kc = kc_ref[0, 0, :n_cmp, :]
        vc_t = vc_ref[0, :, :n_cmp]
        bias_c = jnp.where((_iota((n_cmp, qn), 0) * CMP_STRIDE + (CMP_BLOCK - 1)) <= (q0 + _iota((n_cmp, qn), 1)),
                           0.0, NEG_INF)
        p_sum = None
        for r in reps:
            p = _softmax_cols(_dot(kc, q_head(r)), bias_c, sees_cmp)
            part_scr[r] = gates[3 * r:3 * r + 1] * _dot(vc_t, p.astype(BF16))
            p_sum = p if p_sum is None else p_sum + p
        imp = None
        for term in _split_bf16(p_sum, 3):
            part = _dot(ovt_ref[:, :n_cmp], term)
            imp = part if imp is None else imp + part
        imp_scr[...] = imp

    half = ncp // 2
    upper_visible = q0 + qn - 1 >= half * CMP_STRIDE + CMP_BLOCK - 1
    pl.when(upper_visible)(lambda: compressed_branch(ncp))
    pl.when(jnp.logical_not(upper_visible))(lambda: compressed_branch(half))
    imp_t = imp_scr[...]

    blk = _iota((ns, qn), 0)
    cur = (q0 + _iota((ns, qn), 1)) >> SLC_SHIFT
    forced = (blk == 0) | (blk == cur) | (blk == cur - 1)
    imp_bits = jnp.where(imp_t > 0.0, lax.bitcast_convert_type(imp_t, I32), 0)
    key = jnp.where(forced, jnp.int32(2 ** 31 - 1), jnp.where(blk <= cur, imp_bits, -1))

    picked = _top_rows(key, n_sel)
    bias_scr[...] = jnp.where(picked > 0.5, 0.0, NEG_INF)

    carry = list(_online_init(NSA_REP, HEAD_DIM, qn))
    for diagonal in (True, False):
        kt = qi if diagonal else jnp.maximum(qi - WINDOW // kt_n, 0)
        k0 = pl.multiple_of(kt * kt_n, kt_n)
        kp = k0 + _iota((kt_n, qn), 0)
        tq = q0 + _iota((kt_n, qn), 1)
        inside = (kp <= tq) if diagonal else ((kp > tq - WINDOW) & (kp < q0))
        bias_w = jnp.where(inside, 0.0, NEG_INF)
        k = kw_ref[0, 0, pl.ds(k0, kt_n), :]
        scores = [_dot(k, q_head(r)) for r in reps]
        for r in reps:
            carry[r] = _online_cols(scores[r] + bias_w, vw_ref[0, kt], *carry[r])

    for r in reps:
        part_scr[r] = part_scr[r] + gates[3 * r + 2:3 * r + 3] * _normalized(carry[r])

    blocks_per_tile = kt_n // SLC_BLOCK
    causal = _iota((kt_n, qn), 0) <= _iota((kt_n, qn), 1)
    def prepare(f, diagonal):
        kt = qi if diagonal else jnp.minimum(f, jnp.maximum(qi - 1, 0))
        k0 = pl.multiple_of(kt * kt_n, kt_n)
        rows = [jnp.broadcast_to(bias_scr[pl.ds(kt * blocks_per_tile + i, 1), :], (SLC_BLOCK, qn))
                for i in range(blocks_per_tile)]
        bias = jnp.concatenate(rows, axis=0)
        if diagonal:
            bias = jnp.where(causal, bias, NEG_INF)
        return k0, bias

    def produce(ctx, slot, r):
        k0, bias = ctx
        s = _dot(ks_ref[0, 0, pl.ds(k0, kt_n), :], q_head(r)) + bias
        s_scr[slot, r] = s
        mx_scr[slot, r] = jnp.max(s, axis=0, keepdims=True)

    def consume(slot, f, diagonal, r, state_r):
        return _online_stored(s_scr.at[slot, r], mx_scr[slot, r], vs_ref[0, qi if diagonal else f], *state_r)

    def finalize(state):
        for r in reps:
            o = part_scr[r] + g_ref[0, 3 * r + 1:3 * r + 2, :] * _normalized(state[r])
            o_ref[0, r * HEAD_DIM:(r + 1) * HEAD_DIM, :] = o.astype(o_ref.dtype)

    _sweep(qi, NSA_REP, prepare, produce, consume, finalize, _online_init(NSA_REP, HEAD_DIM, qn))


def _nsa_attention(q_t, k_cmp, v_cmp_t, k_slc, v_slc_t, k_win, v_win_t, gates_t):
    bsz, _, t = q_t.shape
    ncp = k_cmp.shape[2]
    ns = t // SLC_BLOCK
    qn = min(Q_TILE, t)
    grp_w = NSA_REP * HEAD_DIM
    cmp_start = jnp.arange(ncp) * CMP_STRIDE
    slc_start = jnp.arange(ns) * SLC_BLOCK
    real = jnp.arange(ncp) < (t - CMP_BLOCK) // CMP_STRIDE + 1
    overlap_t = ((cmp_start[None, :] < slc_start[:, None] + SLC_BLOCK)
                 & (cmp_start[None, :] + CMP_BLOCK > slc_start[:, None]) & real[None, :]).astype(BF16)
    k_spec = lambda n: pl.BlockSpec((1, 1, n, HEAD_DIM), lambda b, g, i: (b, g, 0, 0))
    v_spec = pl.BlockSpec((1, t // KEY_TILE, HEAD_DIM, KEY_TILE), lambda b, g, i: (b, 0, g, 0))
    return pl.pallas_call(
        functools.partial(_nsa_kernel, n_sel=min(N_SELECT, ns)),
        grid=(bsz, NSA_KV_GROUPS, t // qn),
        in_specs=[pl.BlockSpec((1, grp_w, qn), lambda b, g, i: (b, g, i)),
                  k_spec(ncp),
                  pl.BlockSpec((1, HEAD_DIM, ncp), lambda b, g, i: (b, g, 0)),
                  k_spec(t), v_spec, k_spec(t), v_spec,
                  pl.BlockSpec((1, GATE_ROWS, qn), lambda b, g, i: (b, g, i)),
                  pl.BlockSpec((ns, ncp), lambda b, g, i: (0, 0))],
        out_specs=pl.BlockSpec((1, grp_w, qn), lambda b, g, i: (b, g, i)),
        out_shape=jax.ShapeDtypeStruct((bsz, NSA_HEADS * HEAD_DIM, t), BF16),
        scratch_shapes=[pltpu.VMEM((ns, qn), F32), pltpu.VMEM((ns, qn), F32),
                        pltpu.VMEM((NSA_REP, HEAD_DIM, qn), F32)] + _score_buffers(NSA_REP, qn),
        compiler_params=_params("arbitrary", "arbitrary", "arbitrary"),
        name="nsa_attention",
    )(q_t, k_cmp, v_cmp_t, k_slc, v_slc_t, k_win, v_win_t, gates_t, overlap_t)


SB_HEADS_PER_STEP = 2
SB_DEAD_TAIL = -110.0
SB_SCAN_BLOCK = 128


def _sb_kernel(q_ref, k_ref, v_ref, o_ref, nz_scr):
    qn = q_ref.shape[2]
    kt_n = KEY_TILE
    qi = pl.program_id(2)
    heads = range(SB_HEADS_PER_STEP)
    rows = [slice(h * HEAD_DIM, (h + 1) * HEAD_DIM) for h in heads]
    sub = SB_SCAN_BLOCK
    n_sub = kt_n // sub
    tri = jnp.where(_iota((sub, sub), 1) > _iota((sub, sub), 0), 1.0, 0.0).astype(BF16)
    later2 = jnp.concatenate([tri, tri], axis=1)
    strictly_before = _iota((kt_n, qn), 0) < _iota((kt_n, qn), 1)

    def tile_index(f, diagonal):
        return qi if diagonal else jnp.maximum(qi - 1 - f, 0)

    def prepare(f, diagonal):
        return pl.multiple_of(tile_index(f, diagonal) * kt_n, kt_n)

    def produce(k0, slot, h):
        nz_scr[slot, h] = _dot(k_ref[0, h, pl.ds(k0, kt_n), :], -q_ref[0, rows[h], :])

    def consume(slot, f, diagonal, h, state_h):
        tail, acc = state_h
        nz = nz_scr[slot, h]
        neg_abs = lax.bitcast_convert_type(lax.bitcast_convert_type(nz, jnp.uint32) | jnp.uint32(0x80000000), F32)
        soft = jnp.log(1.0 + jnp.exp(neg_abs))
        log_keep = jnp.minimum(nz, 0.0) - soft
        log_beta = log_keep - nz
        if diagonal:
            log_keep = jnp.where(strictly_before, log_keep, 0.0)
        offs = tail
        parts = [None] * n_sub
        for blk in reversed(range(n_sub)):
            lk = log_keep[blk * sub:(blk + 1) * sub]
            hi = lk.astype(BF16)
            lo = (lk - hi.astype(F32)).astype(BF16)
            within = _dot(later2, jnp.concatenate([hi, lo], axis=0))
            parts[blk] = log_beta[blk * sub:(blk + 1) * sub] + within + offs
            offs = offs + jnp.sum(lk, axis=0, keepdims=True)
        a = jnp.exp(jnp.concatenate(parts, axis=0))
        if diagonal:
            a = jnp.where(strictly_before, a, 0.0)
        return offs, acc + _dot(v_ref[0, tile_index(f, diagonal), rows[h], :], a.astype(BF16))

    def finalize(state):
        for h in heads:
            o_ref[0, rows[h], :] = state[h][1].astype(o_ref.dtype)

    def alive(state):
        tails = [state[h][0] for h in heads]
        return jnp.max(functools.reduce(jnp.maximum, tails)) > SB_DEAD_TAIL

    init = tuple((jnp.zeros((1, qn), F32), jnp.zeros((HEAD_DIM, qn), F32)) for _ in heads)
    _sweep(qi, SB_HEADS_PER_STEP, prepare, produce, consume, finalize, init, alive)


def _sb_attention(q_t, k, v_t):
    bsz, nh, t, _ = k.shape
    qn = min(Q_TILE, t)
    hps = SB_HEADS_PER_STEP
    w = hps * HEAD_DIM
    return pl.pallas_call(
        _sb_kernel,
        grid=(bsz, nh // hps, t // qn),
        in_specs=[pl.BlockSpec((1, w, qn), lambda b, h, i: (b, h, i)),
                  pl.BlockSpec((1, hps, t, HEAD_DIM), lambda b, h, i: (b, h, 0, 0)),
                  pl.BlockSpec((1, t // KEY_TILE, w, KEY_TILE), lambda b, h, i: (b, 0, h, 0))],
        out_specs=pl.BlockSpec((1, w, qn), lambda b, h, i: (b, h, i)),
        out_shape=jax.ShapeDtypeStruct((bsz, nh * HEAD_DIM, t), BF16),
        scratch_shapes=[pltpu.VMEM((2, hps, KEY_TILE, qn), F32)],
        compiler_params=_params("arbitrary", "arbitrary", "arbitrary"),
        name="stickbreak_attention",
    )(q_t, k, v_t)


DIFF_HEADS_PER_STEP = 2


def _diff_kernel(q_ref, k_ref, v_ref, lq1_ref, lk1_ref, lq2_ref, lk2_ref, sub_ref, o_ref, s_scr, mx_scr,
                 *, lambda_init):
    qn = q_ref.shape[2]
    vd = 2 * HEAD_DIM
    kt_n = KEY_TILE
    qi = pl.program_id(2)
    chains = range(2 * DIFF_HEADS_PER_STEP)
    lam = (jnp.exp(jnp.sum(lq1_ref[...] * lk1_ref[...], axis=1, keepdims=True))
           - jnp.exp(jnp.sum(lq2_ref[...] * lk2_ref[...], axis=1, keepdims=True)) + lambda_init)

    causal = _iota((kt_n, qn), 0) <= _iota((kt_n, qn), 1)

    def prepare(f, diagonal):
        kt = qi if diagonal else jnp.minimum(f, jnp.maximum(qi - 1, 0))
        return pl.multiple_of(kt * kt_n, kt_n), diagonal

    def produce(ctx, slot, c):
        k0, diagonal = ctx
        s = _dot(k_ref[0, c, pl.ds(k0, kt_n), :], q_ref[0, c * HEAD_DIM:(c + 1) * HEAD_DIM, :])
        if diagonal:
            s = jnp.where(causal, s, NEG_INF)
        s_scr[slot, c] = s
        mx_scr[slot, c] = jnp.max(s, axis=0, keepdims=True)

    def consume(slot, f, diagonal, c, state_c):
        kt = qi if diagonal else f
        return _online_stored(s_scr.at[slot, c], mx_scr[slot, c],
                              v_ref[0, kt, (c // 2) * vd:(c // 2 + 1) * vd, :], *state_c)

    def finalize(state):
        for h in range(DIFF_HEADS_PER_STEP):
            o = _normalized(state[2 * h]) - lam * _normalized(state[2 * h + 1])
            ms = jnp.mean(o * o, axis=0, keepdims=True)
            o = o * lax.rsqrt(ms + RMS_EPS) * sub_ref[...] * (1.0 - lambda_init)
            o_ref[0, h * vd:(h + 1) * vd, :] = o.astype(o_ref.dtype)

    _sweep(qi, len(chains), prepare, produce, consume, finalize, _online_init(len(chains), vd, qn))


def _diff_attention(q_t, k, v_t, lq1, lk1, lq2, lk2, subln, lambda_init):
    bsz, nh2, t, _ = k.shape
    nh = nh2 // 2
    vd = 2 * HEAD_DIM
    hps = DIFF_HEADS_PER_STEP
    qn = min(Q_TILE, t)
    row = lambda a: a.reshape(1, -1)
    small = lambda n: pl.BlockSpec((1, n), lambda b, h, i: (0, 0))
    return pl.pallas_call(
        functools.partial(_diff_kernel, lambda_init=lambda_init),
        grid=(bsz, nh // hps, t // qn),
        in_specs=[pl.BlockSpec((1, hps * vd, qn), lambda b, h, i: (b, h, i)),
                  pl.BlockSpec((1, 2 * hps, t, HEAD_DIM), lambda b, h, i: (b, h, 0, 0)),
                  pl.BlockSpec((1, t // KEY_TILE, hps * vd, KEY_TILE), lambda b, h, i: (b, 0, h, 0)),
                  small(HEAD_DIM), small(HEAD_DIM), small(HEAD_DIM), small(HEAD_DIM),
                  pl.BlockSpec((vd, 1), lambda b, h, i: (0, 0))],
        out_specs=pl.BlockSpec((1, hps * vd, qn), lambda b, h, i: (b, h, i)),
        out_shape=jax.ShapeDtypeStruct((bsz, nh * vd, t), BF16),
        scratch_shapes=_score_buffers(2 * hps, qn),
        compiler_params=_params("arbitrary", "arbitrary", "arbitrary"),
        name="diff_attention",
    )(q_t, k, v_t, row(lq1), row(lk1), row(lq2), row(lk2), subln.reshape(vd, 1))


def _outproj_kernel(*refs, n_in):
    a_refs, w_refs = refs[:n_in], refs[n_in:2 * n_in]
    x_ref, gate_ref, o_ref = refs[2 * n_in:]
    y = None
    for a_ref, w_ref in zip(a_refs, w_refs):
        part = lax.dot_general(a_ref[0], w_ref[...].astype(BF16), _TN, preferred_element_type=F32)
        y = part if y is None else y + part
    o_ref[0] = x_ref[0] + gate_ref[0] * y


def _out_project(acts_t, weight, x, gate, name):
    bsz, t, d = x.shape
    rows = min(t, PROJ_ROWS)
    k_rows = acts_t[0].shape[1]
    tile = pl.BlockSpec((1, rows, d), lambda b, i: (b, i, 0))
    return pl.pallas_call(
        functools.partial(_outproj_kernel, n_in=len(acts_t)),
        grid=(bsz, t // rows),
        in_specs=([pl.BlockSpec((1, a.shape[1], rows), lambda b, i: (b, 0, i)) for a in acts_t]
                  + [pl.BlockSpec((k_rows, d), lambda b, i, n=n: (n, 0)) for n in range(len(acts_t))]
                  + [tile, pl.BlockSpec((1, 1, d), lambda b, i: (b, 0, 0))]),
        out_specs=tile,
        out_shape=jax.ShapeDtypeStruct((bsz, t, d), F32),
        compiler_params=_params("arbitrary", "arbitrary"),
        name=name,
    )(*acts_t, *([weight] * len(acts_t)), x, gate)


def _ffn_kernel(x_ref, halo_ref, g_ref, sh_ref, sc_ref, gate_ref, wg_ref, wu_ref, cw_ref, cb_ref, wd_ref,
                nf_ref, o_ref, a_scr, *, final_norm):
    ti = pl.program_id(1)
    h = _norm_modulate(x_ref[0], g_ref[...], sh_ref[0], sc_ref[0]).astype(BF16)
    h_prev = _norm_modulate(halo_ref[0], g_ref[...], sh_ref[0], sc_ref[0]).astype(BF16)
    h_both = jnp.concatenate([h_prev, h], axis=0)
    seen = jnp.where(ti > 0, 1.0, 0.0)
    cols = FFN_COLS
    row = _iota((h.shape[0], cols), 0)
    for c in range(a_scr.shape[1] // cols):
        sl = slice(c * cols, (c + 1) * cols)
        g_both = _dot(h_both, wg_ref[0, :, sl].astype(BF16))
        g = g_both[HALO_ROWS:]
        g_prev = g_both[:HALO_ROWS] * seen
        last = g_prev[HALO_ROWS - 1:HALO_ROWS]
        last2 = g_prev[HALO_ROWS - 2:HALO_ROWS - 1]
        g1 = jnp.where(row == 0, last, pltpu.roll(g, 1, 0))
        g2 = jnp.where(row == 0, last2, jnp.where(row == 1, last, pltpu.roll(g, 2, 0)))
        cw = cw_ref[:, sl]
        conv = cw[0:1] * g2 + cw[1:2] * g1 + cw[2:3] * g + cb_ref[:, sl]
        a_scr[:, sl] = (conv * _sigmoid(conv) * _dot(h, wu_ref[0, :, sl].astype(BF16))).astype(BF16)
    y = x_ref[0] + gate_ref[0] * _dot(a_scr[...], wd_ref[...])
    if final_norm:
        ms = jnp.mean(y * y, axis=-1, keepdims=True)
        y = y * lax.rsqrt(ms + RMS_EPS) * nf_ref[...]
    o_ref[0] = y


def _conv_ffn(x, g, shift, scale, gate, layer, w_gate_all, w_up_all, conv_w, conv_b, w_down, norm_f, final_norm,
              name):
    bsz, t, d = x.shape
    f = w_gate_all.shape[2]
    rows = min(t, FFN_ROWS)
    halo_blocks = rows // HALO_ROWS
    vec = pl.BlockSpec((1, 1, d), lambda b, i: (b, 0, 0))
    drow = pl.BlockSpec((1, d), lambda b, i: (0, 0))
    resident = lambda shape: pl.BlockSpec(shape, lambda b, i: (0, 0), pipeline_mode=pl.Buffered(1))
    stacked = pl.BlockSpec((1, d, f), lambda b, i: (layer, 0, 0), pipeline_mode=pl.Buffered(1))
    return pl.pallas_call(
        functools.partial(_ffn_kernel, final_norm=final_norm),
        grid=(bsz, t // rows),
        in_specs=[pl.BlockSpec((1, rows, d), lambda b, i: (b, i, 0)),
                  pl.BlockSpec((1, HALO_ROWS, d), lambda b, i: (b, jnp.maximum(i * halo_blocks - 1, 0), 0)),
                  drow, vec, vec, vec,
                  stacked, stacked, resident((CONV_WIDTH, f)), resident((1, f)),
                  resident((f, d)), drow],
        out_specs=pl.BlockSpec((1, rows, d), lambda b, i: (b, i, 0)),
        out_shape=jax.ShapeDtypeStruct((bsz, t, d), F32),
        scratch_shapes=[pltpu.VMEM((rows, f), BF16)],
        compiler_params=_params("arbitrary", "arbitrary"),
        name=name,
    )(x, x, g.reshape(1, d), shift, scale, gate, w_gate_all, w_up_all, conv_w,
      conv_b.reshape(1, f), w_down.astype(BF16), norm_f.reshape(1, d))


def _layout(widths):
    starts, start = [], 0
    for w in widths:
        starts.append(start)
        start += w
    return starts


def _hybrid_layer(x, mods, tables, norm_g, w_in, pos_k, pos_v, ck_w1, ck_w2, cv_w1, cv_w2, w_out):
    shift, scale, gate = mods
    bsz, t, d = x.shape
    qw, kvw, sbw = NSA_HEADS * HEAD_DIM, NSA_KV_GROUPS * HEAD_DIM, SB_HEADS * HEAD_DIM
    col = _layout((qw,) + (kvw,) * 6 + (3 * NSA_HEADS,) + (sbw,) * 3)
    cols = lambda i, w: w_in[:, col[i]:col[i] + w]
    q_n, kc, vc, ks, vs, kw, vw, gl, q_s, k_s, v_s = (cols(i, w) for i, w in enumerate(
        (qw,) + (kvw,) * 6 + (3 * NSA_HEADS,) + (sbw,) * 3))
    inv_sqrt = HEAD_DIM ** -0.5

    w_tok = jnp.concatenate([ks, kw, kc, vc, k_s], axis=1).astype(BF16)
    ts = _layout((kvw, kvw, kvw, kvw, sbw))
    tok_segs = [TokSeg(ts[0], kvw, True, HEAD_DIM, BF16),
                TokSeg(ts[1], kvw, True, HEAD_DIM, BF16),
                TokSeg(ts[2], kvw, False, 0, F32),
                TokSeg(ts[3], kvw, False, 0, F32),
                TokSeg(ts[4], sbw, False, HEAD_DIM, BF16)]
    per_group = 3 * NSA_REP
    gate_pad = jnp.zeros((d, GATE_ROWS - per_group), w_in.dtype)
    gl_pad = jnp.concatenate([gl[:, :per_group], gate_pad, gl[:, per_group:], gate_pad], axis=1)
    w_feat_t = jnp.concatenate([q_n, q_s, vs, vw, v_s, gl_pad], axis=1).T.astype(BF16)
    fs = _layout((qw, sbw, kvw, kvw, sbw, NSA_KV_GROUPS * GATE_ROWS))
    feat_segs = [FeatSeg(fs[0], qw, True, inv_sqrt * LOG2E, False, False, BF16),
                 FeatSeg(fs[1], sbw, False, inv_sqrt, False, False, BF16),
                 FeatSeg(fs[2], kvw, False, 1.0, False, True, BF16),
                 FeatSeg(fs[3], kvw, False, 1.0, False, True, BF16),
                 FeatSeg(fs[4], sbw, False, 1.0, False, True, BF16),
                 FeatSeg(fs[5], NSA_KV_GROUPS * GATE_ROWS, False, 1.0, True, False, F32)]
    (k_slc, k_win, kc_a, vc_a, k_sb, q_n_t, q_s_t, v_slc_t, v_win_t, v_sb_t, gates_t) = _project(
        x, norm_g, shift, scale, w_tok, w_feat_t, tables, tok_segs, feat_segs, "hybrid_in_proj")

    ncp = t // CMP_STRIDE
    end_rows = jnp.minimum(jnp.arange(ncp) * CMP_STRIDE + CMP_BLOCK - 1, t - 1)
    cos_c, sin_c = tables[0][:, end_rows], tables[1][:, end_rows]
    k_cmp = _compress(kc_a, pos_k, ck_w1, ck_w2, cos_c, sin_c, True, "compress_k")
    v_cmp_t = _compress(vc_a, pos_v, cv_w1, cv_w2, cos_c, sin_c, False, "compress_v")
    o_nsa_t = _nsa_attention(q_n_t, k_cmp, v_cmp_t, k_slc, v_slc_t, k_win, v_win_t, gates_t)
    o_sb_t = _sb_attention(q_s_t, k_sb, v_sb_t)
    return _out_project([o_nsa_t, o_sb_t], w_out, x, gate, "hybrid_out_proj")


def _diff_layer(x, mods, tables, norm_g, w_qkv, lq1, lk1, lq2, lk2, subln, w_out, layer_idx):
    shift, scale, gate = mods
    dw = 2 * DIFF_HEADS * HEAD_DIM
    inv_sqrt = HEAD_DIM ** -0.5
    w_tok = w_qkv[:, dw:2 * dw].astype(BF16)
    w_feat_t = jnp.concatenate([w_qkv[:, :dw], w_qkv[:, 2 * dw:]], axis=1).T.astype(BF16)
    tok_segs = [TokSeg(0, dw, True, HEAD_DIM, BF16)]
    feat_segs = [FeatSeg(0, dw, True, inv_sqrt * LOG2E, False, False, BF16),
                 FeatSeg(dw, dw, False, 1.0, False, True, BF16)]
    k, q_t, v_t = _project(x, norm_g, shift, scale, w_tok, w_feat_t, tables, tok_segs, feat_segs, "diff_in_proj")
    lambda_init = 0.8 - 0.6 * math.exp(-0.3 * layer_idx)
    o_t = _diff_attention(q_t, k, v_t, lq1, lk1, lq2, lk2, subln, lambda_init)
    return _out_project([o_t], w_out, x, gate, "diff_out_proj")


def kernel(x, c, positions, mod_w, mod_b, norm_mix, norm_ffn, ffn_w_gate, ffn_w_up, ffn_conv_w, ffn_conv_b, ffn_w_down, hyb_w_in, nsa_pos_k, nsa_pos_v, nsa_ck_w1, nsa_ck_w2, nsa_cv_w1, nsa_cv_w2, hyb_w_out, diff_w_qkv, diff_lq1, diff_lk1, diff_lq2, diff_lk2, diff_subln, diff_w_out, norm_f):
    bsz, t, d = x.shape
    depth = mod_w.shape[0]
    mod = _adaln_mod(c, mod_w, mod_b)
    tables = _rope_tables(positions)
    for i in range(depth):
        sh_m, sc_m, g_m, sh_f, sc_f, g_f = (mod[i, :, k * d:(k + 1) * d].reshape(bsz, 1, d) for k in range(6))
        j = i // 2
        if i % 2 == 0:
            x = _hybrid_layer(x, (sh_m, sc_m, g_m), tables, norm_mix[i], hyb_w_in[j],
                              nsa_pos_k[j], nsa_pos_v[j], nsa_ck_w1[j], nsa_ck_w2[j], nsa_cv_w1[j],
                              nsa_cv_w2[j], hyb_w_out[j])
        else:
            x = _diff_layer(x, (sh_m, sc_m, g_m), tables, norm_mix[i], diff_w_qkv[j], diff_lq1[j],
                            diff_lk1[j], diff_lq2[j], diff_lk2[j], diff_subln[j], diff_w_out[j], i)
        x = _conv_ffn(x, norm_ffn[i], sh_f, sc_f, g_f, i, ffn_w_gate, ffn_w_up, ffn_conv_w[i],
                      ffn_conv_b[i], ffn_w_down[i], norm_f, i == depth - 1, "conv_ffn_%d" % i)
    return x
```

```python
import functools
import math
from typing import NamedTuple

import jax
import jax.numpy as jnp
from jax import lax
from jax.experimental import pallas as pl
from jax.experimental.pallas import tpu as pltpu

F32 = jnp.float32
BF16 = jnp.bfloat16
I32 = jnp.int32

HEAD_DIM = 64
ROPE_DIM = HEAD_DIM // 4
ROPE_HALF = ROPE_DIM // 2
ROPE_THETA = 500000.0
NSA_HEADS = 8
NSA_KV_GROUPS = 2
NSA_REP = NSA_HEADS // NSA_KV_GROUPS
CMP_BLOCK = 32
CMP_STRIDE = 16
CMP_HIDDEN = 4 * HEAD_DIM
SLC_BLOCK = 64
SLC_SHIFT = 6
N_SELECT = 16
WINDOW = 512
SB_HEADS = 8
DIFF_HEADS = 8
CONV_WIDTH = 3
RMS_EPS = 1e-6
NEG_INF = -1e30
LOG2E = math.log2(math.e)

LANES = 128
Q_TILE = 512
KEY_TILE = 512
GATE_ROWS = 16
PROJ_ROWS = 1024
FFN_ROWS = 512
FFN_COLS = 256
HALO_ROWS = 16
VMEM_LIMIT = 56 * 1024 * 1024

_NT = (((1,), (1,)), ((), ()))
_TN = (((0,), (0,)), ((), ()))


def _params(*sem):
    return pltpu.CompilerParams(dimension_semantics=sem, vmem_limit_bytes=VMEM_LIMIT)


def _sigmoid(v):
    return 1.0 / (1.0 + jnp.exp(-v))


def _iota(shape, axis):
    return lax.broadcasted_iota(I32, shape, axis)


def _dot(a, b):
    return jnp.dot(a, b, preferred_element_type=F32)


def _mod_kernel(c_ref, w_ref, b_ref, o_ref):
    c = c_ref[...]
    cond = c * _sigmoid(c)
    o_ref[0] = jnp.dot(cond, w_ref[0], preferred_element_type=F32,
                       precision=lax.Precision.HIGHEST) + b_ref[0]


def _adaln_mod(c, mod_w, mod_b):
    depth, d, n = mod_w.shape
    bsz = c.shape[0]
    tn = n // 4
    return pl.pallas_call(
        _mod_kernel,
        grid=(depth, n // tn),
        in_specs=[pl.BlockSpec((bsz, d), lambda i, j: (0, 0)),
                  pl.BlockSpec((1, d, tn), lambda i, j: (i, 0, j)),
                  pl.BlockSpec((1, 1, tn), lambda i, j: (i, 0, j))],
        out_specs=pl.BlockSpec((1, bsz, tn), lambda i, j: (i, 0, j)),
        out_shape=jax.ShapeDtypeStruct((depth, bsz, n), F32),
        compiler_params=_params("arbitrary", "arbitrary"),
        name="adaln_mod",
    )(c, mod_w, mod_b.reshape(depth, 1, n))


def _rope_kernel(pos_col_ref, pos_row_ref, inv_row_ref, sgn_row_ref, inv_col_ref,
                 cos_ref, sin_ref, cos_t_ref, sin_t_ref):
    ang = pos_col_ref[0].astype(F32) * inv_row_ref[...]
    cos_ref[0] = jnp.cos(ang)
    sin_ref[0] = jnp.sin(ang) * sgn_row_ref[...]
    ang_t = inv_col_ref[...] * pos_row_ref[0].astype(F32)
    cos_t_ref[0] = jnp.cos(ang_t)
    sin_t_ref[0] = jnp.sin(ang_t)


def _rope_tables(positions):
    bsz, t = positions.shape
    inv = ROPE_THETA ** (-jnp.arange(0, ROPE_DIM, 2, dtype=F32) / ROPE_DIM)
    per_head_inv = jnp.concatenate([inv, inv, jnp.zeros((HEAD_DIM - ROPE_DIM,), F32)])
    per_head_sgn = jnp.concatenate([-jnp.ones((ROPE_HALF,), F32), jnp.ones((ROPE_HALF,), F32),
                                    jnp.zeros((HEAD_DIM - ROPE_DIM,), F32)])
    inv_row = jnp.tile(per_head_inv, LANES // HEAD_DIM)[None, :]
    sgn_row = jnp.tile(per_head_sgn, LANES // HEAD_DIM)[None, :]
    rows = min(t, 1024)
    tab = jax.ShapeDtypeStruct((bsz, t, LANES), F32)
    tab_t = jax.ShapeDtypeStruct((bsz, ROPE_HALF, t), F32)
    row_spec = pl.BlockSpec((1, LANES), lambda b, i: (0, 0))
    return pl.pallas_call(
        _rope_kernel,
        grid=(bsz, t // rows),
        in_specs=[pl.BlockSpec((1, rows, 1), lambda b, i: (b, i, 0)),
                  pl.BlockSpec((1, 1, rows), lambda b, i: (b, 0, i)),
                  row_spec, row_spec,
                  pl.BlockSpec((ROPE_HALF, 1), lambda b, i: (0, 0))],
        out_specs=[pl.BlockSpec((1, rows, LANES), lambda b, i: (b, i, 0))] * 2
        + [pl.BlockSpec((1, ROPE_HALF, rows), lambda b, i: (b, 0, i))] * 2,
        out_shape=[tab, tab, tab_t, tab_t],
        compiler_params=_params("arbitrary", "arbitrary"),
        name="rope_tables",
    )(positions.reshape(bsz, t, 1), positions.reshape(bsz, 1, t), inv_row, sgn_row, inv[:, None])


def _rope_chunk(y, cosv, sinv, first_half):
    ahead = pltpu.roll(y, LANES - ROPE_HALF, 1)
    behind = pltpu.roll(y, ROPE_HALF, 1)
    return y * cosv + jnp.where(first_half, ahead, behind) * sinv


def _rope_rows(y_t, cos_t, sin_t):
    heads = []
    for h in range(y_t.shape[0] // HEAD_DIM):
        blk = y_t[h * HEAD_DIM:(h + 1) * HEAD_DIM]
        x1, x2 = blk[:ROPE_HALF], blk[ROPE_HALF:ROPE_DIM]
        heads += [x1 * cos_t - x2 * sin_t, x2 * cos_t + x1 * sin_t, blk[ROPE_DIM:]]
    return jnp.concatenate(heads, axis=0)


class TokSeg(NamedTuple):
    start: int
    width: int
    rope: bool
    head_width: int
    dtype: object


class FeatSeg(NamedTuple):
    start: int
    rows: int
    rope: bool
    scale: float
    sigmoid: bool
    key_tiled: bool
    dtype: object


def _norm_modulate(x, g, shift, scale):
    ms = jnp.mean(x * x, axis=-1, keepdims=True)
    y = x * lax.rsqrt(ms + RMS_EPS) * g
    return y * (1.0 + scale) + shift


def _proj_kernel(x_ref, g_ref, sh_ref, sc_ref, w_ref, wt_ref, cos_ref, sin_ref, cos_t_ref, sin_t_ref,
                 *out_refs, tok_segs, feat_segs):
    hb = _norm_modulate(x_ref[0], g_ref[...], sh_ref[0], sc_ref[0]).astype(BF16)
    cosv, sinv = cos_ref[0], sin_ref[0]
    first_half = (_iota(cosv.shape, 1) & (HEAD_DIM - 1)) < ROPE_HALF
    tok_refs, feat_refs = out_refs[:len(tok_segs)], out_refs[len(tok_segs):]
    for seg, o_ref in zip(tok_segs, tok_refs):
        y = _dot(hb, w_ref[:, seg.start:seg.start + seg.width])
        for ch in range(seg.width // LANES):
            yc = y[:, ch * LANES:(ch + 1) * LANES]
            if seg.rope:
                yc = _rope_chunk(yc, cosv, sinv, first_half)
            yc = yc.astype(seg.dtype)
            if seg.head_width == 0:
                o_ref[0, :, ch * LANES:(ch + 1) * LANES] = yc
            else:
                o_ref[0, 2 * ch] = yc[:, :HEAD_DIM]
                o_ref[0, 2 * ch + 1] = yc[:, HEAD_DIM:]
    y_t_all = lax.dot_general(wt_ref[...], hb, _NT, preferred_element_type=F32)
    for seg, o_ref in zip(feat_segs, feat_refs):
        y_t = y_t_all[seg.start:seg.start + seg.rows]
        if seg.rope:
            y_t = _rope_rows(y_t, cos_t_ref[0], sin_t_ref[0])
        if seg.scale != 1.0:
            y_t = y_t * seg.scale
        if seg.sigmoid:
            y_t = _sigmoid(y_t)
        y_t = y_t.astype(seg.dtype)
        if seg.key_tiled:
            for ch in range(y_t.shape[1] // KEY_TILE):
                o_ref[0, ch] = y_t[:, ch * KEY_TILE:(ch + 1) * KEY_TILE]
        else:
            o_ref[0] = y_t


def _project(x, g, shift, scale, w_tok, w_feat_t, tables, tok_segs, feat_segs, name):
    bsz, t, d = x.shape
    rows = min(t, PROJ_ROWS)
    cos_tab, sin_tab, cos_t, sin_t = tables
    out_shapes, out_specs = [], []
    for seg in tok_segs:
        if seg.head_width == 0:
            out_shapes.append(jax.ShapeDtypeStruct((bsz, t, seg.width), seg.dtype))
            out_specs.append(pl.BlockSpec((1, rows, seg.width), lambda b, i: (b, i, 0)))
        else:
            nh = seg.width // seg.head_width
            out_shapes.append(jax.ShapeDtypeStruct((bsz, nh, t, seg.head_width), seg.dtype))
            out_specs.append(pl.BlockSpec((1, nh, rows, seg.head_width), lambda b, i: (b, 0, i, 0)))
    for seg in feat_segs:
        if seg.key_tiled:
            out_shapes.append(jax.ShapeDtypeStruct((bsz, t // KEY_TILE, seg.rows, KEY_TILE), seg.dtype))
            out_specs.append(pl.BlockSpec((1, rows // KEY_TILE, seg.rows, KEY_TILE), lambda b, i: (b, i, 0, 0)))
        else:
            out_shapes.append(jax.ShapeDtypeStruct((bsz, seg.rows, t), seg.dtype))
            out_specs.append(pl.BlockSpec((1, seg.rows, rows), lambda b, i: (b, 0, i)))
    vec = pl.BlockSpec((1, 1, d), lambda b, i: (b, 0, 0))
    tab = pl.BlockSpec((1, rows, LANES), lambda b, i: (b, i, 0))
    tab_t = pl.BlockSpec((1, ROPE_HALF, rows), lambda b, i: (b, 0, i))
    return pl.pallas_call(
        functools.partial(_proj_kernel, tok_segs=tuple(tok_segs), feat_segs=tuple(feat_segs)),
        grid=(bsz, t // rows),
        in_specs=[pl.BlockSpec((1, rows, d), lambda b, i: (b, i, 0)),
                  pl.BlockSpec((1, d), lambda b, i: (0, 0)),
                  vec, vec,
                  pl.BlockSpec(w_tok.shape, lambda b, i: (0, 0)),
                  pl.BlockSpec(w_feat_t.shape, lambda b, i: (0, 0)),
                  tab, tab, tab_t, tab_t],
        out_specs=out_specs,
        out_shape=out_shapes,
        compiler_params=_params("arbitrary", "arbitrary"),
        name=name,
    )(x, g.reshape(1, d), shift, scale, w_tok, w_feat_t, cos_tab, sin_tab, cos_t, sin_t)


def _compress_kernel(r_ref, pa_ref, pb_ref, wa_ref, wb_ref, w2_ref, cos_ref, sin_ref, o_ref, *, is_key):
    r = r_ref[0]
    ncp = r.shape[0]
    a = _dot((r + pa_ref[...]).astype(BF16), wa_ref[...])
    b = _dot((r + pb_ref[...]).astype(BF16), wb_ref[...])
    hid = a + pltpu.roll(b, ncp - 1, 0)
    hid = (hid * _sigmoid(hid)).astype(BF16)
    if is_key:
        y = _dot(hid, w2_ref[...])
        first_half = (_iota(y.shape, 1) & (HEAD_DIM - 1)) < ROPE_HALF
        y = _rope_chunk(y, cos_ref[0], sin_ref[0], first_half).astype(o_ref.dtype)
        for g in range(NSA_KV_GROUPS):
            o_ref[0, g] = y[:, g * HEAD_DIM:(g + 1) * HEAD_DIM]
    else:
        o_ref[0] = lax.dot_general(w2_ref[...], hid, _NT, preferred_element_type=F32).astype(o_ref.dtype)


def _compress(kv, pos_emb, w1, w2, cos_c, sin_c, is_key, name):
    bsz, t, _ = kv.shape
    ncp = t // CMP_STRIDE
    kwid = CMP_STRIDE * NSA_KV_GROUPS * HEAD_DIM
    hid_w = NSA_KV_GROUPS * CMP_HIDDEN
    r = kv.reshape(bsz, ncp, kwid)
    per = CMP_BLOCK // CMP_STRIDE
    w1r = w1.reshape(per, CMP_STRIDE, HEAD_DIM, CMP_HIDDEN)
    zeros = jnp.zeros_like(w1r)
    grp0 = jnp.concatenate([w1r, zeros], axis=-1)
    grp1 = jnp.concatenate([zeros, w1r], axis=-1)
    wbig = jnp.stack([grp0, grp1], axis=2).reshape(per, kwid, hid_w).astype(BF16)
    posr = pos_emb.reshape(per, CMP_STRIDE, 1, HEAD_DIM)
    posbig = jnp.broadcast_to(posr, (per, CMP_STRIDE, NSA_KV_GROUPS, HEAD_DIM)).reshape(per, 1, kwid)
    z2 = jnp.zeros_like(w2)
    w2big = jnp.concatenate([jnp.concatenate([w2, z2], axis=1),
                             jnp.concatenate([z2, w2], axis=1)], axis=0).astype(BF16)
    const = lambda shape: pl.BlockSpec(shape, lambda b: (0,) * len(shape))
    tab = pl.BlockSpec((1, ncp, LANES), lambda b: (b, 0, 0))
    if is_key:
        w2_arg = w2big
        out_spec = pl.BlockSpec((1, NSA_KV_GROUPS, ncp, HEAD_DIM), lambda b: (b, 0, 0, 0))
        out_shape = jax.ShapeDtypeStruct((bsz, NSA_KV_GROUPS, ncp, HEAD_DIM), BF16)
    else:
        w2_arg = w2big.T
        out_spec = pl.BlockSpec((1, NSA_KV_GROUPS * HEAD_DIM, ncp), lambda b: (b, 0, 0))
        out_shape = jax.ShapeDtypeStruct((bsz, NSA_KV_GROUPS * HEAD_DIM, ncp), BF16)
    return pl.pallas_call(
        functools.partial(_compress_kernel, is_key=is_key),
        grid=(bsz,),
        in_specs=[pl.BlockSpec((1, ncp, kwid), lambda b: (b, 0, 0)),
                  const((1, kwid)), const((1, kwid)),
                  const((kwid, hid_w)), const((kwid, hid_w)),
                  const(w2_arg.shape), tab, tab],
        out_specs=out_spec,
        out_shape=out_shape,
        compiler_params=_params("arbitrary"),
        name=name,
    )(r, posbig[0], posbig[1], wbig[0], wbig[1], w2_arg, cos_c, sin_c)


def _softmax_cols(s, bias, any_visible):
    sb = s + bias
    e = jnp.exp2(sb - jnp.max(sb, axis=0, keepdims=True))
    l = jnp.sum(e, axis=0, keepdims=True)
    return e * jnp.where(any_visible, 1.0 / l, 0.0)


def _online_step(s, tile_max, v_t, m_old, l_old, acc_old):
    m_new = jnp.maximum(m_old, tile_max)
    alpha = jnp.exp2(m_old - m_new)
    p = jnp.exp2(s - m_new)
    l_new = alpha * l_old + jnp.sum(p, axis=0, keepdims=True)
    acc_new = alpha * acc_old + _dot(v_t, p.astype(BF16))
    return m_new, l_new, acc_new


def _online_cols(s, v_t, m_old, l_old, acc_old):
    return _online_step(s, jnp.max(s, axis=0, keepdims=True), v_t, m_old, l_old, acc_old)


def _online_stored(s_ref, tile_max, v_t, m_old, l_old, acc_old):
    return _online_step(s_ref[...], tile_max, v_t, m_old, l_old, acc_old)


def _normalized(state):
    _, l, acc = state
    return acc * (1.0 / l)


def _sweep(n_earlier, n_chains, prepare, produce, consume, finalize, state, alive=None):
    chains = range(n_chains)

    def produce_all(slot, f, diagonal):
        ctx = prepare(f, diagonal)
        for c in chains:
            produce(ctx, slot, c)

    def consume_all(slot, f, diagonal, st):
        return tuple(consume(slot, f, diagonal, c, st[c]) for c in chains)

    def overlapped(p_slot, p_f, c_slot, c_f, c_diagonal, st):
        ctx = prepare(p_f, False)
        out = []
        for c in chains:
            produce(ctx, p_slot, c)
            out.append(consume(c_slot, c_f, c_diagonal, c, st[c]))
        return tuple(out)

    produce_all(0, 0, True)
    state = overlapped(1, 0, 0, 0, True, state)

    def pair(i, st):
        st = overlapped(0, 2 * i + 1, 1, 2 * i, False, st)
        if alive is None:
            return overlapped(1, 2 * i + 2, 0, 2 * i + 1, False, st)
        produce_all(1, 2 * i + 2, False)
        return lax.cond(alive(st), lambda s: consume_all(0, 2 * i + 1, False, s), lambda s: s, st)

    n_pairs = n_earlier >> 1
    odd = (n_earlier & 1) == 1
    if alive is None:
        state = lax.fori_loop(0, n_pairs, pair, state)
    else:
        _, state = lax.while_loop(lambda c: (c[0] < n_pairs) & alive(c[1]),
                                  lambda c: (c[0] + 1, pair(c[0], c[1])), (jnp.int32(0), state))
        odd = odd & alive(state)

    @pl.when(odd)
    def _():
        finalize(consume_all(1, n_earlier - 1, False, state))

    @pl.when(jnp.logical_not(odd))
    def _():
        finalize(state)


def _score_buffers(chains, qn):
    return [pltpu.VMEM((2, chains, KEY_TILE, qn), F32), pltpu.VMEM((2, chains, 1, qn), F32)]


def _online_init(n, dv, qn):
    return tuple((jnp.full((1, qn), NEG_INF, F32), jnp.zeros((1, qn), F32), jnp.zeros((dv, qn), F32))
                 for _ in range(n))


def _split_bf16(v, terms):
    out, rest = [], v
    for i in range(terms):
        part = rest.astype(BF16)
        out.append(part)
        if i + 1 < terms:
            rest = rest - part.astype(F32)
    return out


def _top_rows(key, n_top):
    n_rows, n_cols = key.shape

    def bisect(i, tau):
        cand = tau | lax.shift_left(jnp.int32(1), jnp.int32(30) - i)
        cnt = jnp.sum(jnp.where(key >= cand, 1.0, 0.0), axis=0, keepdims=True)
        return jnp.where(cnt >= float(n_top), cand, tau)

    tau = lax.fori_loop(0, 31, bisect, jnp.zeros((1, n_cols), I32))
    above = jnp.where(key > tau, 1.0, 0.0)
    equal = jnp.where(key == tau, 1.0, 0.0)
    need = float(n_top) - jnp.sum(above, axis=0, keepdims=True)
    lower = jnp.where(_iota((n_rows, n_rows), 1) < _iota((n_rows, n_rows), 0), 1.0, 0.0).astype(BF16)
    before = _dot(lower, equal.astype(BF16))
    return above + equal * jnp.where(before < need, 1.0, 0.0)

def _nsa_kernel(q_ref, kc_ref, vc_ref, ks_ref, vs_ref, kw_ref, vw_ref, g_ref, ovt_ref, o_ref,
                bias_scr, imp_scr, part_scr, s_scr, mx_scr, *, n_sel):
    qn = q_ref.shape[2]
    ncp = kc_ref.shape[2]
    ns = ovt_ref.shape[0]
    kt_n = KEY_TILE
    qi = pl.program_id(2)
    q0 = qi * qn
    reps = range(NSA_REP)
    q_head = lambda r: q_ref[0, r * HEAD_DIM:(r + 1) * HEAD_DIM, :]

    gates = g_ref[0]
    sees_cmp = (q0 + _iota((1, qn), 1)) >= CMP_BLOCK - 1

    def compressed_branch(n_cmp):
        kc = kc_ref[0, 0, :n_cmp, :]
        vc_t = vc_ref[0, :, :n_cmp]
        bias_c = jnp.where((_iota((n_cmp, qn), 0) * CMP_STRIDE + (CMP_BLOCK - 1)) <= (q0 + _iota((n_cmp, qn), 1)),
                           0.0, NEG_INF)
        p_sum = None
        for r in reps:
            p = _softmax_cols(_dot(kc, q_head(r)), bias_c, sees_cmp)
            part_scr[r] = gates[3 * r:3 * r + 1] * _dot(vc_t, p.astype(BF16))
            p_sum = p if p_sum is None else p_sum + p
        imp = None
        for term in _split_bf16(p_sum, 3):
            part = _dot(ovt_ref[:, :n_cmp], term)
            imp = part if imp is None else imp + part
        imp_scr[...] = imp

    half = ncp // 2
    upper_visible = q0 + qn - 1 >= half * CMP_STRIDE + CMP_BLOCK - 1
    pl.when(upper_visible)(lambda: compressed_branch(ncp))
    pl.when(jnp.logical_not(upper_visible))(lambda: compressed_branch(half))
    imp_t = imp_scr[...]

    blk = _iota((ns, qn), 0)
    cur = (q0 + _iota((ns, qn), 1)) >> SLC_SHIFT
    forced = (blk == 0) | (blk == cur) | (blk == cur - 1)
    imp_bits = jnp.where(imp_t > 0.0, lax.bitcast_convert_type(imp_t, I32), 0)
    key = jnp.where(forced, jnp.int32(2 ** 31 - 1), jnp.where(blk <= cur, imp_bits, -1))

    picked = _top_rows(key, n_sel)
    bias_scr[...] = jnp.where(picked > 0.5, 0.0, NEG_INF)

    carry = list(_online_init(NSA_REP, HEAD_DIM, qn))
    for diagonal in (True, False):
        kt = qi if diagonal else jnp.maximum(qi - WINDOW // kt_n, 0)
        k0 = pl.multiple_of(kt * kt_n, kt_n)
        kp = k0 + _iota((kt_n, qn), 0)
        tq = q0 + _iota((kt_n, qn), 1)
        inside = (kp <= tq) if diagonal else ((kp > tq - WINDOW) & (kp < q0))
        bias_w = jnp.where(inside, 0.0, NEG_INF)
        k = kw_ref[0, 0, pl.ds(k0, kt_n), :]
        scores = [_dot(k, q_head(r)) for r in reps]
        for r in reps:
            carry[r] = _online_cols(scores[r] + bias_w, vw_ref[0, kt], *carry[r])

    for r in reps:
        part_scr[r] = part_scr[r] + gates[3 * r + 2:3 * r + 3] * _normalized(carry[r])

    blocks_per_tile = kt_n // SLC_BLOCK
    causal = _iota((kt_n, qn), 0) <= _iota((kt_n, qn), 1)
    def prepare(f, diagonal):
        kt = qi if diagonal else jnp.minimum(f, jnp.maximum(qi - 1, 0))
        k0 = pl.multiple_of(kt * kt_n, kt_n)
        rows = [jnp.broadcast_to(bias_scr[pl.ds(kt * blocks_per_tile + i, 1), :], (SLC_BLOCK, qn))
                for i in range(blocks_per_tile)]
        bias = jnp.concatenate(rows, axis=0)
        if diagonal:
            bias = jnp.where(causal, bias, NEG_INF)
        return k0, bias

    def produce(ctx, slot, r):
        k0, bias = ctx
        s = _dot(ks_ref[0, 0, pl.ds(k0, kt_n), :], q_head(r)) + bias
        s_scr[slot, r] = s
        mx_scr[slot, r] = jnp.max(s, axis=0, keepdims=True)

    def consume(slot, f, diagonal, r, state_r):
        return _online_stored(s_scr.at[slot, r], mx_scr[slot, r], vs_ref[0, qi if diagonal else f], *state_r)

    def finalize(state):
        for r in reps:
            o = part_scr[r] + g_ref[0, 3 * r + 1:3 * r + 2, :] * _normalized(state[r])
            o_ref[0, r * HEAD_DIM:(r + 1) * HEAD_DIM, :] = o.astype(o_ref.dtype)

    _sweep(qi, NSA_REP, prepare, produce, consume, finalize, _online_init(NSA_REP, HEAD_DIM, qn))


def _nsa_attention(q_t, k_cmp, v_cmp_t, k_slc, v_slc_t, k_win, v_win_t, gates_t):
    bsz, _, t = q_t.shape
    ncp = k_cmp.shape[2]
    ns = t // SLC_BLOCK
    qn = min(Q_TILE, t)
    grp_w = NSA_REP * HEAD_DIM
    cmp_start = jnp.arange(ncp) * CMP_STRIDE
    slc_start = jnp.arange(ns) * SLC_BLOCK
    real = jnp.arange(ncp) < (t - CMP_BLOCK) // CMP_STRIDE + 1
    overlap_t = ((cmp_start[None, :] < slc_start[:, None] + SLC_BLOCK)
                 & (cmp_start[None, :] + CMP_BLOCK > slc_start[:, None]) & real[None, :]).astype(BF16)
    k_spec = lambda n: pl.BlockSpec((1, 1, n, HEAD_DIM), lambda b, g, i: (b, g, 0, 0))
    v_spec = pl.BlockSpec((1, t // KEY_TILE, HEAD_DIM, KEY_TILE), lambda b, g, i: (b, 0, g, 0))
    return pl.pallas_call(
        functools.partial(_nsa_kernel, n_sel=min(N_SELECT, ns)),
        grid=(bsz, NSA_KV_GROUPS, t // qn),
        in_specs=[pl.BlockSpec((1, grp_w, qn), lambda b, g, i: (b, g, i)),
                  k_spec(ncp),
                  pl.BlockSpec((1, HEAD_DIM, ncp), lambda b, g, i: (b, g, 0)),
                  k_spec(t), v_spec, k_spec(t), v_spec,
                  pl.BlockSpec((1, GATE_ROWS, qn), lambda b, g, i: (b, g, i)),
                  pl.BlockSpec((ns, ncp), lambda b, g, i: (0, 0))],
        out_specs=pl.BlockSpec((1, grp_w, qn), lambda b, g, i: (b, g, i)),
        out_shape=jax.ShapeDtypeStruct((bsz, NSA_HEADS * HEAD_DIM, t), BF16),
        scratch_shapes=[pltpu.VMEM((ns, qn), F32), pltpu.VMEM((ns, qn), F32),
                        pltpu.VMEM((NSA_REP, HEAD_DIM, qn), F32)] + _score_buffers(NSA_REP, qn),
        compiler_params=_params("arbitrary", "arbitrary", "arbitrary"),
        name="nsa_attention",
    )(q_t, k_cmp, v_cmp_t, k_slc, v_slc_t, k_win, v_win_t, gates_t, overlap_t)


SB_HEADS_PER_STEP = 2
SB_DEAD_TAIL = -110.0
SB_SCAN_BLOCK = 128


def _sb_kernel(q_ref, k_ref, v_ref, o_ref, nz_scr):
    qn = q_ref.shape[2]
    kt_n = KEY_TILE
    qi = pl.program_id(2)
    heads = range(SB_HEADS_PER_STEP)
    rows = [slice(h * HEAD_DIM, (h + 1) * HEAD_DIM) for h in heads]
    sub = SB_SCAN_BLOCK
    n_sub = kt_n // sub
    tri = jnp.where(_iota((sub, sub), 1) > _iota((sub, sub), 0), 1.0, 0.0).astype(BF16)
    later2 = jnp.concatenate([tri, tri], axis=1)
    strictly_before = _iota((kt_n, qn), 0) < _iota((kt_n, qn), 1)

    def tile_index(f, diagonal):
        return qi if diagonal else jnp.maximum(qi - 1 - f, 0)

    def prepare(f, diagonal):
        return pl.multiple_of(tile_index(f, diagonal) * kt_n, kt_n)

    def produce(k0, slot, h):
        nz_scr[slot, h] = _dot(k_ref[0, h, pl.ds(k0, kt_n), :], -q_ref[0, rows[h], :])

    def consume(slot, f, diagonal, h, state_h):
        tail, acc = state_h
        nz = nz_scr[slot, h]
        neg_abs = lax.bitcast_convert_type(lax.bitcast_convert_type(nz, jnp.uint32) | jnp.uint32(0x80000000), F32)
        soft = jnp.log(1.0 + jnp.exp(neg_abs))
        log_keep = jnp.minimum(nz, 0.0) - soft
        log_beta = log_keep - nz
        if diagonal:
            log_keep = jnp.where(strictly_before, log_keep, 0.0)
        offs = tail
        parts = [None] * n_sub
        for blk in reversed(range(n_sub)):
            lk = log_keep[blk * sub:(blk + 1) * sub]
            hi = lk.astype(BF16)
            lo = (lk - hi.astype(F32)).astype(BF16)
            within = _dot(later2, jnp.concatenate([hi, lo], axis=0))
            parts[blk] = log_beta[blk * sub:(blk + 1) * sub] + within + offs
            offs = offs + jnp.sum(lk, axis=0, keepdims=True)
        a = jnp.exp(jnp.concatenate(parts, axis=0))
        if diagonal:
            a = jnp.where(strictly_before, a, 0.0)
        return offs, acc + _dot(v_ref[0, tile_index(f, diagonal), rows[h], :], a.astype(BF16))

    def finalize(state):
        for h in heads:
            o_ref[0, rows[h], :] = state[h][1].astype(o_ref.dtype)

    def alive(state):
        tails = [state[h][0] for h in heads]
        return jnp.max(functools.reduce(jnp.maximum, tails)) > SB_DEAD_TAIL

    init = tuple((jnp.zeros((1, qn), F32), jnp.zeros((HEAD_DIM, qn), F32)) for _ in heads)
    _sweep(qi, SB_HEADS_PER_STEP, prepare, produce, consume, finalize, init, alive)


def _sb_attention(q_t, k, v_t):
    bsz, nh, t, _ = k.shape
    qn = min(Q_TILE, t)
    hps = SB_HEADS_PER_STEP
    w = hps * HEAD_DIM
    return pl.pallas_call(
        _sb_kernel,
        grid=(bsz, nh // hps, t // qn),
        in_specs=[pl.BlockSpec((1, w, qn), lambda b, h, i: (b, h, i)),
                  pl.BlockSpec((1, hps, t, HEAD_DIM), lambda b, h, i: (b, h, 0, 0)),
                  pl.BlockSpec((1, t // KEY_TILE, w, KEY_TILE), lambda b, h, i: (b, 0, h, 0))],
        out_specs=pl.BlockSpec((1, w, qn), lambda b, h, i: (b, h, i)),
        out_shape=jax.ShapeDtypeStruct((bsz, nh * HEAD_DIM, t), BF16),
        scratch_shapes=[pltpu.VMEM((2, hps, KEY_TILE, qn), F32)],
        compiler_params=_params("arbitrary", "arbitrary", "arbitrary"),
        name="stickbreak_attention",
    )(q_t, k, v_t)


DIFF_HEADS_PER_STEP = 2


def _diff_kernel(q_ref, k_ref, v_ref, lq1_ref, lk1_ref, lq2_ref, lk2_ref, sub_ref, o_ref, s_scr, mx_scr,
                 *, lambda_init):
    qn = q_ref.shape[2]
    vd = 2 * HEAD_DIM
    kt_n = KEY_TILE
    qi = pl.program_id(2)
    chains = range(2 * DIFF_HEADS_PER_STEP)
    lam = (jnp.exp(jnp.sum(lq1_ref[...] * lk1_ref[...], axis=1, keepdims=True))
           - jnp.exp(jnp.sum(lq2_ref[...] * lk2_ref[...], axis=1, keepdims=True)) + lambda_init)

    causal = _iota((kt_n, qn), 0) <= _iota((kt_n, qn), 1)

    def prepare(f, diagonal):
        kt = qi if diagonal else jnp.minimum(f, jnp.maximum(qi - 1, 0))
        return pl.multiple_of(kt * kt_n, kt_n), diagonal

    def produce(ctx, slot, c):
        k0, diagonal = ctx
        s = _dot(k_ref[0, c, pl.ds(k0, kt_n), :], q_ref[0, c * HEAD_DIM:(c + 1) * HEAD_DIM, :])
        if diagonal:
            s = jnp.where(causal, s, NEG_INF)
        s_scr[slot, c] = s
        mx_scr[slot, c] = jnp.max(s, axis=0, keepdims=True)

    def consume(slot, f, diagonal, c, state_c):
        kt = qi if diagonal else f
        return _online_stored(s_scr.at[slot, c], mx_scr[slot, c],
                              v_ref[0, kt, (c // 2) * vd:(c // 2 + 1) * vd, :], *state_c)

    def finalize(state):
        for h in range(DIFF_HEADS_PER_STEP):
            o = _normalized(state[2 * h]) - lam * _normalized(state[2 * h + 1])
            ms = jnp.mean(o * o, axis=0, keepdims=True)
            o = o * lax.rsqrt(ms + RMS_EPS) * sub_ref[...] * (1.0 - lambda_init)
            o_ref[0, h * vd:(h + 1) * vd, :] = o.astype(o_ref.dtype)

    _sweep(qi, len(chains), prepare, produce, consume, finalize, _online_init(len(chains), vd, qn))


def _diff_attention(q_t, k, v_t, lq1, lk1, lq2, lk2, subln, lambda_init):
    bsz, nh2, t, _ = k.shape
    nh = nh2 // 2
    vd = 2 * HEAD_DIM
    hps = DIFF_HEADS_PER_STEP
    qn = min(Q_TILE, t)
    row = lambda a: a.reshape(1, -1)
    small = lambda n: pl.BlockSpec((1, n), lambda b, h, i: (0, 0))
    return pl.pallas_call(
        functools.partial(_diff_kernel, lambda_init=lambda_init),
        grid=(bsz, nh // hps, t // qn),
        in_specs=[pl.BlockSpec((1, hps * vd, qn), lambda b, h, i: (b, h, i)),
                  pl.BlockSpec((1, 2 * hps, t, HEAD_DIM), lambda b, h, i: (b, h, 0, 0)),
                  pl.BlockSpec((1, t // KEY_TILE, hps * vd, KEY_TILE), lambda b, h, i: (b, 0, h, 0)),
                  small(HEAD_DIM), small(HEAD_DIM), small(HEAD_DIM), small(HEAD_DIM),
                  pl.BlockSpec((vd, 1), lambda b, h, i: (0, 0))],
        out_specs=pl.BlockSpec((1, hps * vd, qn), lambda b, h, i: (b, h, i)),
        out_shape=jax.ShapeDtypeStruct((bsz, nh * vd, t), BF16),
        scratch_shapes=_score_buffers(2 * hps, qn),
        compiler_params=_params("arbitrary", "arbitrary", "arbitrary"),
        name="diff_attention",
    )(q_t, k, v_t, row(lq1), row(lk1), row(lq2), row(lk2), subln.reshape(vd, 1))


def _ffn_kernel(*refs, final_norm, n_mix):
    a_refs, ah_refs, wo_refs = refs[:n_mix], refs[n_mix:2 * n_mix], refs[2 * n_mix:3 * n_mix]
    (gm_ref, x_ref, halo_ref, g_ref, sh_ref, sc_ref, gate_ref, wg_ref, wu_ref, cw_ref, cb_ref, wd_ref,
     nf_ref, o_ref, a_scr) = refs[3 * n_mix:]
    ti = pl.program_id(1)
    y_mix = None
    for a_ref, ah_ref, wo_ref in zip(a_refs, ah_refs, wo_refs):
        both = jnp.concatenate([ah_ref[0], a_ref[0]], axis=1)
        part = lax.dot_general(both, wo_ref[...].astype(BF16), _TN, preferred_element_type=F32)
        y_mix = part if y_mix is None else y_mix + part
    x_tile = x_ref[0] + gm_ref[0] * y_mix[LANES:]
    x_prev = halo_ref[0] + gm_ref[0] * y_mix[LANES - HALO_ROWS:LANES]
    h = _norm_modulate(x_tile, g_ref[...], sh_ref[0], sc_ref[0]).astype(BF16)
    h_prev = _norm_modulate(x_prev, g_ref[...], sh_ref[0], sc_ref[0]).astype(BF16)
    h_both = jnp.concatenate([h_prev, h], axis=0)
    seen = jnp.where(ti > 0, 1.0, 0.0)
    cols = FFN_COLS
    row = _iota((h.shape[0], cols), 0)
    for c in range(a_scr.shape[1] // cols):
        sl = slice(c * cols, (c + 1) * cols)
        g_both = _dot(h_both, wg_ref[0, :, sl].astype(BF16))
        g = g_both[HALO_ROWS:]
        g_prev = g_both[:HALO_ROWS] * seen
        last = g_prev[HALO_ROWS - 1:HALO_ROWS]
        last2 = g_prev[HALO_ROWS - 2:HALO_ROWS - 1]
        g1 = jnp.where(row == 0, last, pltpu.roll(g, 1, 0))
        g2 = jnp.where(row == 0, last2, jnp.where(row == 1, last, pltpu.roll(g, 2, 0)))
        cw = cw_ref[:, sl]
        conv = cw[0:1] * g2 + cw[1:2] * g1 + cw[2:3] * g + cb_ref[:, sl]
        a_scr[:, sl] = (conv * _sigmoid(conv) * _dot(h, wu_ref[0, :, sl].astype(BF16))).astype(BF16)
    y = x_tile + gate_ref[0] * _dot(a_scr[...], wd_ref[...])
    if final_norm:
        ms = jnp.mean(y * y, axis=-1, keepdims=True)
        y = y * lax.rsqrt(ms + RMS_EPS) * nf_ref[...]
    o_ref[0] = y


def _mix_ffn(acts_t, w_out, gate_mix, x, g, shift, scale, gate, layer, w_gate_all, w_up_all, conv_w, conv_b,
             w_down, norm_f, final_norm, name):
    bsz, t, d = x.shape
    f = w_gate_all.shape[2]
    rows = min(t, FFN_ROWS)
    halo_blocks = rows // HALO_ROWS
    lane_blocks = rows // LANES
    k_rows = acts_t[0].shape[1]
    n_mix = len(acts_t)
    mix_specs = ([pl.BlockSpec((1, k_rows, rows), lambda b, i: (b, 0, i)) for _ in acts_t]
                 + [pl.BlockSpec((1, k_rows, LANES), lambda b, i: (b, 0, jnp.maximum(i * lane_blocks - 1, 0)))
                    for _ in acts_t]
                 + [pl.BlockSpec((k_rows, d), lambda b, i, n=n: (n, 0), pipeline_mode=pl.Buffered(1))
                    for n in range(n_mix)])
    vec = pl.BlockSpec((1, 1, d), lambda b, i: (b, 0, 0))
    drow = pl.BlockSpec((1, d), lambda b, i: (0, 0))
    resident = lambda shape: pl.BlockSpec(shape, lambda b, i: (0, 0), pipeline_mode=pl.Buffered(1))
    stacked = pl.BlockSpec((1, d, f), lambda b, i: (layer, 0, 0), pipeline_mode=pl.Buffered(1))
    return pl.pallas_call(
        functools.partial(_ffn_kernel, final_norm=final_norm, n_mix=n_mix),
        grid=(bsz, t // rows),
        in_specs=mix_specs + [vec,
                  pl.BlockSpec((1, rows, d), lambda b, i: (b, i, 0)),
                  pl.BlockSpec((1, HALO_ROWS, d), lambda b, i: (b, jnp.maximum(i * halo_blocks - 1, 0), 0)),
                  drow, vec, vec, vec,
                  stacked, stacked, resident((CONV_WIDTH, f)), resident((1, f)),
                  resident((f, d)), drow],
        out_specs=pl.BlockSpec((1, rows, d), lambda b, i: (b, i, 0)),
        out_shape=jax.ShapeDtypeStruct((bsz, t, d), F32),
        scratch_shapes=[pltpu.VMEM((rows, f), BF16)],
        compiler_params=_params("arbitrary", "arbitrary"),
        name=name,
    )(*acts_t, *acts_t, *([w_out] * n_mix), gate_mix, x, x, g.reshape(1, d), shift, scale, gate, w_gate_all,
      w_up_all, conv_w, conv_b.reshape(1, f), w_down.astype(BF16), norm_f.reshape(1, d))


def _layout(widths):
    starts, start = [], 0
    for w in widths:
        starts.append(start)
        start += w
    return starts


def _hybrid_layer(x, mods, tables, norm_g, w_in, pos_k, pos_v, ck_w1, ck_w2, cv_w1, cv_w2):
    shift, scale = mods
    bsz, t, d = x.shape
    qw, kvw, sbw = NSA_HEADS * HEAD_DIM, NSA_KV_GROUPS * HEAD_DIM, SB_HEADS * HEAD_DIM
    col = _layout((qw,) + (kvw,) * 6 + (3 * NSA_HEADS,) + (sbw,) * 3)
    cols = lambda i, w: w_in[:, col[i]:col[i] + w]
    q_n, kc, vc, ks, vs, kw, vw, gl, q_s, k_s, v_s = (cols(i, w) for i, w in enumerate(
        (qw,) + (kvw,) * 6 + (3 * NSA_HEADS,) + (sbw,) * 3))
    inv_sqrt = HEAD_DIM ** -0.5

    w_tok = jnp.concatenate([ks, kw, kc, vc, k_s], axis=1).astype(BF16)
    ts = _layout((kvw, kvw, kvw, kvw, sbw))
    tok_segs = [TokSeg(ts[0], kvw, True, HEAD_DIM, BF16),
                TokSeg(ts[1], kvw, True, HEAD_DIM, BF16),
                TokSeg(ts[2], kvw, False, 0, F32),
                TokSeg(ts[3], kvw, False, 0, F32),
                TokSeg(ts[4], sbw, False, HEAD_DIM, BF16)]
    per_group = 3 * NSA_REP
    gate_pad = jnp.zeros((d, GATE_ROWS - per_group), w_in.dtype)
    gl_pad = jnp.concatenate([gl[:, :per_group], gate_pad, gl[:, per_group:], gate_pad], axis=1)
    w_feat_t = jnp.concatenate([q_n, q_s, vs, vw, v_s, gl_pad], axis=1).T.astype(BF16)
    fs = _layout((qw, sbw, kvw, kvw, sbw, NSA_KV_GROUPS * GATE_ROWS))
    feat_segs = [FeatSeg(fs[0], qw, True, inv_sqrt * LOG2E, False, False, BF16),
                 FeatSeg(fs[1], sbw, False, inv_sqrt, False, False, BF16),
                 FeatSeg(fs[2], kvw, False, 1.0, False, True, BF16),
                 FeatSeg(fs[3], kvw, False, 1.0, False, True, BF16),
                 FeatSeg(fs[4], sbw, False, 1.0, False, True, BF16),
                 FeatSeg(fs[5], NSA_KV_GROUPS * GATE_ROWS, False, 1.0, True, False, F32)]
    (k_slc, k_win, kc_a, vc_a, k_sb, q_n_t, q_s_t, v_slc_t, v_win_t, v_sb_t, gates_t) = _project(
        x, norm_g, shift, scale, w_tok, w_feat_t, tables, tok_segs, feat_segs, "hybrid_in_proj")

    ncp = t // CMP_STRIDE
    end_rows = jnp.minimum(jnp.arange(ncp) * CMP_STRIDE + CMP_BLOCK - 1, t - 1)
    cos_c, sin_c = tables[0][:, end_rows], tables[1][:, end_rows]
    k_cmp = _compress(kc_a, pos_k, ck_w1, ck_w2, cos_c, sin_c, True, "compress_k")
    v_cmp_t = _compress(vc_a, pos_v, cv_w1, cv_w2, cos_c, sin_c, False, "compress_v")
    o_nsa_t = _nsa_attention(q_n_t, k_cmp, v_cmp_t, k_slc, v_slc_t, k_win, v_win_t, gates_t)
    o_sb_t = _sb_attention(q_s_t, k_sb, v_sb_t)
    return [o_nsa_t, o_sb_t]


def _diff_layer(x, mods, tables, norm_g, w_qkv, lq1, lk1, lq2, lk2, subln, layer_idx):
    shift, scale = mods
    dw = 2 * DIFF_HEADS * HEAD_DIM
    inv_sqrt = HEAD_DIM ** -0.5
    w_tok = w_qkv[:, dw:2 * dw].astype(BF16)
    w_feat_t = jnp.concatenate([w_qkv[:, :dw], w_qkv[:, 2 * dw:]], axis=1).T.astype(BF16)
    tok_segs = [TokSeg(0, dw, True, HEAD_DIM, BF16)]
    feat_segs = [FeatSeg(0, dw, True, inv_sqrt * LOG2E, False, False, BF16),
                 FeatSeg(dw, dw, False, 1.0, False, True, BF16)]
    k, q_t, v_t = _project(x, norm_g, shift, scale, w_tok, w_feat_t, tables, tok_segs, feat_segs, "diff_in_proj")
    lambda_init = 0.8 - 0.6 * math.exp(-0.3 * layer_idx)
    o_t = _diff_attention(q_t, k, v_t, lq1, lk1, lq2, lk2, subln, lambda_init)
    return [o_t]


def kernel(x, c, positions, mod_w, mod_b, norm_mix, norm_ffn, ffn_w_gate, ffn_w_up, ffn_conv_w, ffn_conv_b, ffn_w_down, hyb_w_in, nsa_pos_k, nsa_pos_v, nsa_ck_w1, nsa_ck_w2, nsa_cv_w1, nsa_cv_w2, hyb_w_out, diff_w_qkv, diff_lq1, diff_lk1, diff_lq2, diff_lk2, diff_subln, diff_w_out, norm_f):
    bsz, t, d = x.shape
    depth = mod_w.shape[0]
    mod = _adaln_mod(c, mod_w, mod_b)
    tables = _rope_tables(positions)
    for i in range(depth):
        sh_m, sc_m, g_m, sh_f, sc_f, g_f = (mod[i, :, k * d:(k + 1) * d].reshape(bsz, 1, d) for k in range(6))
        j = i // 2
        if i % 2 == 0:
            acts_t = _hybrid_layer(x, (sh_m, sc_m), tables, norm_mix[i], hyb_w_in[j],
                                   nsa_pos_k[j], nsa_pos_v[j], nsa_ck_w1[j], nsa_ck_w2[j], nsa_cv_w1[j],
                                   nsa_cv_w2[j])
            w_out = hyb_w_out[j]
        else:
            acts_t = _diff_layer(x, (sh_m, sc_m), tables, norm_mix[i], diff_w_qkv[j], diff_lq1[j],
                                 diff_lk1[j], diff_lq2[j], diff_lk2[j], diff_subln[j], i)
            w_out = diff_w_out[j]
        x = _mix_ffn(acts_t, w_out, g_m, x, norm_ffn[i], sh_f, sc_f, g_f, i, ffn_w_gate, ffn_w_up, ffn_conv_w[i],
                     ffn_conv_b[i], ffn_w_down[i], norm_f, i == depth - 1, "mix_ffn_%d" % i)
    return x
```
